```python
import math
import jax, jax.numpy as jnp
from jax import lax
import numpy as np

D_MODEL = 2048
BATCH = 16
SEQ = 256
DEPTH = 2
DEC_BATCH = 2
DEC_SEQ = 1024
PAST_LEN = 512

GRID_W = 64
D_ATT = 1024
D_HY = 1024
D_MIX = D_ATT + D_HY
HEAD_DIM = 128
N_HEADS = D_ATT // HEAD_DIM
N_KV_HEADS = 4
GROUP = N_HEADS // N_KV_HEADS
D_KV = N_KV_HEADS * HEAD_DIM
Q_BLOCK = 128
ROPE_THETA = 10000.0
ROPE_PAIRS_AXIS = HEAD_DIM // 4
SHORT_CONV = 3
POS_BANDS = 16
POS_EMB = 1 + 2 * POS_BANDS
FILT_HID = 64
DECAY_TARGET = 1e-2
FAST_DECAY_PCT = 0.3
SLOW_DECAY_PCT = 1.5
DECAY_SHIFT = 0.05
EPS = 1e-6
SPLIT_Q = D_ATT
SPLIT_K = SPLIT_Q + D_KV
SPLIT_V = SPLIT_K + D_KV
SPLIT_GA = SPLIT_V + D_ATT
SPLIT_HY = SPLIT_GA + 3 * D_HY
D_IN = SPLIT_HY + D_HY

kernel_name = "hymba_attn_hyena_prefix_dit_step"


def rmsnorm(x, g):
    x32 = x.astype(jnp.float32)
    y = x32 * lax.rsqrt(jnp.mean(x32 * x32, axis=-1, keepdims=True) + EPS)
    return (y * g.astype(jnp.float32)).astype(x.dtype)


def rope_tables(n_tokens):
    rows = n_tokens // GRID_W
    t = jnp.arange(rows * GRID_W)
    row = (t // GRID_W).astype(jnp.float32)
    col = (t % GRID_W).astype(jnp.float32)
    inv_freq = ROPE_THETA ** (-jnp.arange(ROPE_PAIRS_AXIS, dtype=jnp.float32) / ROPE_PAIRS_AXIS)
    ang = jnp.concatenate([row[:, None] * inv_freq, col[:, None] * inv_freq], axis=-1)
    return jnp.cos(ang), jnp.sin(ang)


def apply_rope(x, cos, sin):
    xr = x.reshape(x.shape[:-1] + (HEAD_DIM // 2, 2))
    x0, x1 = xr[..., 0], xr[..., 1]
    c = cos[None, :, None, :].astype(x.dtype)
    s = sin[None, :, None, :].astype(x.dtype)
    out = jnp.stack([x0 * c - x1 * s, x0 * s + x1 * c], axis=-1)
    return out.reshape(x.shape)


def attend(q, k, v):
    B, Lq = q.shape[0], q.shape[1]
    nb = Lq // Q_BLOCK
    scale = 1.0 / math.sqrt(HEAD_DIM)
    qb = q.reshape(B, nb, Q_BLOCK, N_KV_HEADS, GROUP, HEAD_DIM).transpose(1, 0, 2, 3, 4, 5)

    def one_block(qblk):
        s = jnp.einsum('bqkgd,bskd->bkgqs', qblk, k).astype(jnp.float32) * scale
        p = jax.nn.softmax(s, axis=-1).astype(v.dtype)
        return jnp.einsum('bkgqs,bskd->bqkgd', p, v)

    o = lax.map(one_block, qb)
    return o.transpose(1, 0, 2, 3, 4, 5).reshape(B, Lq, D_ATT)


def short_conv(u, w, b):
    up = jnp.pad(u, ((0, 0), (1, 1), (0, 0)))
    return w[0] * up[:, :-2] + w[1] * up[:, 1:-1] + w[2] * up[:, 2:] + b


def implicit_filter(L, f_w1, f_b1, f_w2, f_b2, f_w3, f_freq):
    f32 = jnp.float32
    tpos = jnp.arange(L, dtype=f32)
    t_norm = tpos / max(L - 1, 1)
    w = 2.0 * math.pi * tpos / L
    bands = jnp.linspace(1e-4, POS_BANDS - 1, POS_BANDS, dtype=f32)
    z = jnp.concatenate([t_norm[:, None], jnp.cos(w[:, None] * bands), -jnp.sin(w[:, None] * bands)], axis=-1)
    freq = f_freq.astype(f32)
    hdn = jnp.sin(freq * (z @ f_w1.astype(f32) + f_b1.astype(f32)))
    hdn = jnp.sin(freq * (hdn @ f_w2.astype(f32) + f_b2.astype(f32)))
    h = hdn @ f_w3.astype(f32)
    max_decay = math.log(DECAY_TARGET) / FAST_DECAY_PCT
    min_decay = math.log(DECAY_TARGET) / SLOW_DECAY_PCT
    deltas = jnp.abs(jnp.linspace(min_decay, max_decay, D_HY, dtype=f32))
    deltas = jnp.concatenate([deltas, deltas])
    h = h * (jnp.exp(-t_norm[:, None] * deltas) + DECAY_SHIFT)
    h_f, h_b = h[:, :D_HY], h[:, D_HY:]
    return jnp.concatenate([h_f[:1] + h_b[:1], h_f[1:], jnp.zeros((1, D_HY), f32), h_b[1:][::-1]], axis=0)


def hyena(u_in, conv_w, conv_b, f_w1, f_b1, f_w2, f_b2, f_w3, f_freq, hy_bias):
    L = u_in.shape[1]
    u = short_conv(u_in, conv_w, conv_b)
    x0, x1, vv = jnp.split(u, 3, axis=-1)
    z = vv * x1
    filt = implicit_filter(L, f_w1, f_b1, f_w2, f_b2, f_w3, f_freq)
    zf = jnp.fft.rfft(z.astype(jnp.float32), n=2 * L, axis=1)
    hf = jnp.fft.rfft(filt, axis=0)
    y = jnp.fft.irfft(zf * hf[None], n=2 * L, axis=1)[:, :L]
    y = y.astype(z.dtype) + hy_bias * z
    return x0 * y


def mixer_layer(x, mod, rope, ctx_kv, norm_g, w_in, q_g, k_g, conv_w, conv_b,
                f_w1, f_b1, f_w2, f_b2, f_w3, f_freq, hy_bias, w_out):
    B, L, _ = x.shape
    shift, scale, gate = jnp.split(mod, 3, axis=-1)
    h = rmsnorm(x, norm_g) * (1.0 + scale) + shift
    proj = h @ w_in
    q, k, v, g_att, hy_in, g_hy = jnp.split(proj, [SPLIT_Q, SPLIT_K, SPLIT_V, SPLIT_GA, SPLIT_HY], axis=-1)
    q = rmsnorm(q.reshape(B, L, N_HEADS, HEAD_DIM), q_g)
    k = rmsnorm(k.reshape(B, L, N_KV_HEADS, HEAD_DIM), k_g)
    v = v.reshape(B, L, N_KV_HEADS, HEAD_DIM)
    if rope is None:
        kv_out = (k, v)
        k_all, v_all = k, v
    else:
        cos, sin = rope
        q = apply_rope(q, cos, sin)
        k = apply_rope(k, cos, sin)
        k_all = jnp.concatenate([k, ctx_kv[0]], axis=1)
        v_all = jnp.concatenate([v, ctx_kv[1]], axis=1)
        kv_out = None
    att = attend(q, k_all, v_all) * jax.nn.silu(g_att)
    hy = hyena(hy_in, conv_w, conv_b, f_w1, f_b1, f_w2, f_b2, f_w3, f_freq, hy_bias) * jax.nn.silu(g_hy)
    out = jnp.concatenate([att, hy], axis=-1) @ w_out
    return x + gate * out, kv_out


def setup_inputs(seed: int = 0) -> dict:
    key = jax.random.key(seed)
    ks = jax.random.split(key, 26)

    def nrm(k, shape, s):
        return jax.random.normal(k, shape, jnp.float32) * s

    return {
        "x_prompt": nrm(ks[0], (BATCH, SEQ, D_MODEL), 1.0),
        "x_sample": nrm(ks[1], (DEC_BATCH, DEC_SEQ, D_MODEL), 1.0),
        "cache_k": nrm(ks[2], (DEC_BATCH, DEPTH, PAST_LEN, N_KV_HEADS, HEAD_DIM), 1.0),
        "cache_v": nrm(ks[3], (DEC_BATCH, DEPTH, PAST_LEN, N_KV_HEADS, HEAD_DIM), 1.0),
        "c": nrm(ks[4], (DEC_BATCH, D_MODEL), 1.0),
        "c_ctx": nrm(ks[5], (D_MODEL,), 1.0),
        "norm_g": 1.0 + nrm(ks[6], (DEPTH, D_MODEL), 0.1),
        "w_ada": nrm(ks[7], (DEPTH, D_MODEL, 3 * D_MODEL), 0.02),
        "b_ada": nrm(ks[8], (DEPTH, 3 * D_MODEL), 0.01),
        "w_in": nrm(ks[9], (DEPTH, D_MODEL, D_IN), D_MODEL ** -0.5),
        "q_norm_g": 1.0 + nrm(ks[10], (DEPTH, HEAD_DIM), 0.1),
        "k_norm_g": 1.0 + nrm(ks[11], (DEPTH, HEAD_DIM), 0.1),
        "conv_w": nrm(ks[12], (DEPTH, SHORT_CONV, 3 * D_HY), 0.5),
        "conv_b": nrm(ks[13], (DEPTH, 3 * D_HY), 0.02),
        "filt_w1": nrm(ks[14], (DEPTH, POS_EMB, FILT_HID), POS_EMB ** -0.5),
        "filt_b1": nrm(ks[15], (DEPTH, FILT_HID), 0.02),
        "filt_w2": nrm(ks[16], (DEPTH, FILT_HID, FILT_HID), FILT_HID ** -0.5),
        "filt_b2": nrm(ks[17], (DEPTH, FILT_HID), 0.02),
        "filt_w3": nrm(ks[18], (DEPTH, FILT_HID, 2 * D_HY), 0.02),
        "filt_freq": 1.0 + nrm(ks[19], (DEPTH, FILT_HID), 0.1),
        "hy_bias": nrm(ks[20], (DEPTH, D_HY), 0.1),
        "w_out": nrm(ks[21], (DEPTH, D_MIX, D_MODEL), D_MIX ** -0.5),
        "final_norm_g": 1.0 + nrm(ks[22], (D_MODEL,), 0.1),
    }


def reference(x_prompt, x_sample, cache_k, cache_v, c, c_ctx, norm_g, w_ada, b_ada, w_in,
              q_norm_g, k_norm_g, conv_w, conv_b, filt_w1, filt_b1, filt_w2, filt_b2,
              filt_w3, filt_freq, hy_bias, w_out, final_norm_g):
    rope = rope_tables(x_sample.shape[1])
    ctx = x_prompt
    lat = x_sample
    ks_out = []
    vs_out = []
    for l in range(DEPTH):
        lw = (norm_g[l], w_in[l], q_norm_g[l], k_norm_g[l], conv_w[l], conv_b[l],
              filt_w1[l], filt_b1[l], filt_w2[l], filt_b2[l], filt_w3[l], filt_freq[l],
              hy_bias[l], w_out[l])
        mod_ctx = (jax.nn.silu(c_ctx) @ w_ada[l] + b_ada[l])[None, None, :]
        mod_lat = (jax.nn.silu(c) @ w_ada[l] + b_ada[l])[:, None, :]
        ctx, kv = mixer_layer(ctx, mod_ctx, None, None, *lw)
        ks_out.append(kv[0])
        vs_out.append(kv[1])
        lat, _ = mixer_layer(lat, mod_lat, rope, (cache_k[:, l], cache_v[:, l]), *lw)
    y_prompt = rmsnorm(ctx, final_norm_g)
    y_sample = rmsnorm(lat, final_norm_g)
    new_k = jnp.stack(ks_out, axis=1)
    new_v = jnp.stack(vs_out, axis=1)
    return (y_prompt, y_sample, new_k, new_v)
```

```python
import functools
import math

import numpy as np
import jax
import jax.numpy as jnp
from jax import lax
from jax.experimental import pallas as pl
from jax.experimental.pallas import tpu as pltpu

D_MODEL = 2048
BATCH = 16
SEQ = 256
DEPTH = 2
DEC_BATCH = 2
DEC_SEQ = 1024
PAST_LEN = 512
GRID_W = 64
D_ATT = 1024
D_HY = 1024
HEAD_DIM = 128
N_HEADS = 8
N_KV_HEADS = 4
GROUP = 2
D_KV = 512
ROPE_THETA = 10000.0
POS_BANDS = 16
POS_EMB = 33
FILT_HID = 64
DECAY_TARGET = 1e-2
FAST_DECAY_PCT = 0.3
SLOW_DECAY_PCT = 1.5
DECAY_SHIFT = 0.05
EPS = 1e-6
D_IN = 7168

COL_Q, COL_K, COL_V, COL_GA, COL_X0, COL_X1, COL_VV, COL_GH = 0, 1024, 1536, 2048, 3072, 4096, 5120, 6144

F32 = jnp.float32
BF16 = jnp.bfloat16

VMEM_LIMIT_BYTES = 56 * 1024 * 1024
FEAT_PAD = 128


def _params(*sem):
    return pltpu.CompilerParams(dimension_semantics=sem, vmem_limit_bytes=VMEM_LIMIT_BYTES)


def _silu(x):
    return x * (1.0 / (1.0 + jnp.exp(-x)))


def _rms(x, g):
    return x * lax.rsqrt(jnp.mean(x * x, axis=-1, keepdims=True) + EPS) * g


MOD_ROWS = 8
MOD_TN = 512


def _mod_kernel(c_ref, w_ref, b_ref, o_ref):
    s = _silu(c_ref[...]).astype(BF16)
    o_ref[0] = jnp.dot(s, w_ref[0].astype(BF16), preferred_element_type=F32) + b_ref[0]


def _modulation(cvec, w_ada, b_ada):
    n = 3 * D_MODEL
    return pl.pallas_call(
        _mod_kernel,
        grid=(DEPTH, n // MOD_TN),
        in_specs=[
            pl.BlockSpec((MOD_ROWS, D_MODEL), lambda l, j: (0, 0)),
            pl.BlockSpec((1, D_MODEL, MOD_TN), lambda l, j: (l, 0, j)),
            pl.BlockSpec((1, 1, MOD_TN), lambda l, j: (l, 0, j)),
        ],
        out_specs=pl.BlockSpec((1, MOD_ROWS, MOD_TN), lambda l, j: (l, 0, j)),
        out_shape=jax.ShapeDtypeStruct((DEPTH, MOD_ROWS, n), F32),
        compiler_params=_params("arbitrary", "arbitrary"),
        name="adaln_mod",
    )(cvec, w_ada, b_ada.reshape(DEPTH, 1, n))


INPROJ_TM = 512
INPROJ_TN = 512


def _inproj_kernel(x_ref, g_ref, shift_ref, scale_ref, w_ref, o_ref, h_scr):
    @pl.when(pl.program_id(1) == 0)
    def _():
        h = _rms(x_ref[...], g_ref[...]) * (1.0 + scale_ref[0]) + shift_ref[0]
        h_scr[...] = h.astype(BF16)

    o_ref[...] = jnp.dot(h_scr[...], w_ref[...], preferred_element_type=F32)


def _in_proj(x, norm_g, mod3, w_bf16, row0, tiles_per_row):
    m = x.shape[0]
    mod_row = lambda i: row0 + i // tiles_per_row
    return pl.pallas_call(
        _inproj_kernel,
        grid=(m // INPROJ_TM, D_IN // INPROJ_TN),
        in_specs=[
            pl.BlockSpec((INPROJ_TM, D_MODEL), lambda i, j: (i, 0)),
            pl.BlockSpec((1, D_MODEL), lambda i, j: (0, 0)),
            pl.BlockSpec((1, 1, D_MODEL), lambda i, j: (mod_row(i), 0, 0)),
            pl.BlockSpec((1, 1, D_MODEL), lambda i, j: (mod_row(i), 0, 1)),
            pl.BlockSpec((D_MODEL, INPROJ_TN), lambda i, j: (0, j)),
        ],
        out_specs=pl.BlockSpec((INPROJ_TM, INPROJ_TN), lambda i, j: (i, j)),
        out_shape=jax.ShapeDtypeStruct((m, D_IN), F32),
        scratch_shapes=[pltpu.VMEM((INPROJ_TM, D_MODEL), BF16)],
        compiler_params=_params("arbitrary", "arbitrary"),
        name="in_proj",
    )(x, norm_g, mod3, mod3, w_bf16)


ATT_SCALE = 1.0 / math.sqrt(HEAD_DIM)


def _head(ref_or_val, h):
    return ref_or_val[:, h * HEAD_DIM:(h + 1) * HEAD_DIM]


def _softmax_pv(q2, k_bf, v_bf):
    s = lax.dot_general(q2, k_bf, (((1,), (1,)), ((), ())), preferred_element_type=F32) * ATT_SCALE
    p = jnp.exp(s - jnp.max(s, axis=-1, keepdims=True))
    denom = jnp.sum(p, axis=-1, keepdims=True)
    o = jnp.dot(p.astype(BF16), v_bf, preferred_element_type=F32)
    return o * (1.0 / denom)


def _rope(x, cos2, sin_a, sin_b):
    nxt = pltpu.roll(x, HEAD_DIM - 1, axis=1)
    prv = pltpu.roll(x, 1, axis=1)
    return x * cos2 + nxt * sin_a + prv * sin_b


def _attn_ctx_kernel(q_ref, kv_ref, g_ref, qg_ref, kg_ref, att_ref, ko_ref, vo_ref):
    rows = q_ref.shape[0]
    for h in range(N_KV_HEADS):
        kn = _rms(_head(kv_ref, h), kg_ref[...])
        vh = _head(kv_ref, N_KV_HEADS + h)
        ko_ref[0, :, h * HEAD_DIM:(h + 1) * HEAD_DIM] = kn
        vo_ref[0, :, h * HEAD_DIM:(h + 1) * HEAD_DIM] = vh
        q2 = jnp.concatenate(
            [_rms(_head(q_ref, GROUP * h + g), qg_ref[...]) for g in range(GROUP)], axis=0
        ).astype(BF16)
        o = _softmax_pv(q2, kn.astype(BF16), vh.astype(BF16))
        for g in range(GROUP):
            hq = GROUP * h + g
            gate = _silu(_head(g_ref, hq))
            att_ref[:, hq * HEAD_DIM:(hq + 1) * HEAD_DIM] = (o[g * rows:(g + 1) * rows] * gate).astype(BF16)


def _attn_ctx(proj, q_g, k_g):
    m = proj.shape[0]
    nb = m // SEQ
    return pl.pallas_call(
        _attn_ctx_kernel,
        grid=(nb,),
        in_specs=[
            pl.BlockSpec((SEQ, D_ATT), lambda b: (b, COL_Q // D_ATT)),
            pl.BlockSpec((SEQ, 2 * D_KV), lambda b: (b, COL_K // (2 * D_KV))),
            pl.BlockSpec((SEQ, D_ATT), lambda b: (b, COL_GA // D_ATT)),
            pl.BlockSpec((1, HEAD_DIM), lambda b: (0, 0)),
            pl.BlockSpec((1, HEAD_DIM), lambda b: (0, 0)),
        ],
        out_specs=[
            pl.BlockSpec((SEQ, D_ATT), lambda b: (b, 0)),
            pl.BlockSpec((1, SEQ, D_KV), lambda b: (b, 0, 0)),
            pl.BlockSpec((1, SEQ, D_KV), lambda b: (b, 0, 0)),
        ],
        out_shape=[
            jax.ShapeDtypeStruct((m, D_ATT), BF16),
            jax.ShapeDtypeStruct((nb, SEQ, D_KV), F32),
            jax.ShapeDtypeStruct((nb, SEQ, D_KV), F32),
        ],
        compiler_params=_params("arbitrary"),
        name="attn_ctx",
    )(proj, proj, proj, q_g, k_g)


LAT_TQ = 256
LAT_KEYS = DEC_SEQ + PAST_LEN


def _attn_lat_kernel(q_ref, kv_ref, g_ref, ck_ref, cv_ref, qg_ref, kg_ref,
                     qc_ref, qsa_ref, qsb_ref, kc_ref, ksa_ref, ksb_ref,
                     att_ref, k_scr, v_scr):
    @pl.when(pl.program_id(1) == 0)
    def _():
        for h in range(N_KV_HEADS):
            kn = _rope(_rms(_head(kv_ref, h), kg_ref[...]), kc_ref[...], ksa_ref[...], ksb_ref[...])
            k_scr[0:DEC_SEQ, h * HEAD_DIM:(h + 1) * HEAD_DIM] = kn.astype(BF16)
            v_scr[0:DEC_SEQ, h * HEAD_DIM:(h + 1) * HEAD_DIM] = _head(kv_ref, N_KV_HEADS + h).astype(BF16)
        k_scr[DEC_SEQ:LAT_KEYS, :] = ck_ref[0, 0].astype(BF16)
        v_scr[DEC_SEQ:LAT_KEYS, :] = cv_ref[0, 0].astype(BF16)

    rows = q_ref.shape[0]
    for h in range(N_KV_HEADS):
        q2 = jnp.concatenate(
            [_rope(_rms(_head(q_ref, GROUP * h + g), qg_ref[...]), qc_ref[...], qsa_ref[...], qsb_ref[...])
             for g in range(GROUP)], axis=0).astype(BF16)
        o = _softmax_pv(q2, _head(k_scr, h), _head(v_scr, h))
        for g in range(GROUP):
            hq = GROUP * h + g
            gate = _silu(_head(g_ref, hq))
            att_ref[:, hq * HEAD_DIM:(hq + 1) * HEAD_DIM] = (o[g * rows:(g + 1) * rows] * gate).astype(BF16)


def _attn_lat(proj, cache_k4, cache_v4, layer, q_g, k_g, rope_tabs):
    m = proj.shape[0]
    nq = DEC_SEQ // LAT_TQ
    cos2, sin_a, sin_b = rope_tabs
    qtab = pl.BlockSpec((LAT_TQ, HEAD_DIM), lambda b, i: (i, 0))
    ktab = pl.BlockSpec((DEC_SEQ, HEAD_DIM), lambda b, i: (0, 0))
    cache = pl.BlockSpec((1, 1, PAST_LEN, D_KV), lambda b, i: (b, layer, 0, 0))
    vec = pl.BlockSpec((1, HEAD_DIM), lambda b, i: (0, 0))
    return pl.pallas_call(
        _attn_lat_kernel,
        grid=(DEC_BATCH, nq),
        in_specs=[
            pl.BlockSpec((LAT_TQ, D_ATT), lambda b, i: (b * nq + i, COL_Q // D_ATT)),
            pl.BlockSpec((DEC_SEQ, 2 * D_KV), lambda b, i: (b, COL_K // (2 * D_KV))),
            pl.BlockSpec((LAT_TQ, D_ATT), lambda b, i: (b * nq + i, COL_GA // D_ATT)),
            cache, cache, vec, vec, qtab, qtab, qtab, ktab, ktab, ktab,
        ],
        out_specs=pl.BlockSpec((LAT_TQ, D_ATT), lambda b, i: (b * nq + i, 0)),
        out_shape=jax.ShapeDtypeStruct((m, D_ATT), BF16),
        scratch_shapes=[pltpu.VMEM((LAT_KEYS, D_KV), BF16), pltpu.VMEM((LAT_KEYS, D_KV), BF16)],
        compiler_params=_params("arbitrary", "arbitrary"),
        name="attn_lat",
    )(proj, proj, proj, cache_k4, cache_v4, q_g, k_g, cos2, sin_a, sin_b, cos2, sin_a, sin_b)


def _dft_matrix(L):
    k = np.arange(L, dtype=np.int64)[:, None]
    t = np.arange(L, dtype=np.int64)[None, :]
    ang = 2.0 * np.pi * ((k * t) % (2 * L)).astype(np.float64) / (2 * L)
    top = np.cos(ang)
    bot = np.sin(ang)
    bot[0, :] = np.where(np.arange(L) % 2 == 0, 1.0, -1.0)
    return np.concatenate([top, bot], axis=0).astype(np.float32)


def _hi_lo(a):
    hi = a.astype(BF16)
    lo = (a - hi.astype(F32)).astype(BF16)
    return hi, lo


def _split_dot(a_hi, a_lo, b):
    b_hi = b.astype(BF16)
    b_lo = (b - b_hi.astype(F32)).astype(BF16)
    dot = functools.partial(jnp.dot, preferred_element_type=F32)
    return dot(a_hi, b_hi) + (dot(a_lo, b_hi) + dot(a_hi, b_lo))


def _filter_kernel(L, z_ref, w1_ref, b1_ref, w2_ref, b2_ref, fr_ref, w3f_ref, w3b_ref, dec_ref,
                   fhi_ref, flo_ref, o_ref, hdn_scr):
    hp = lax.Precision.HIGHEST

    @pl.when(pl.program_id(1) == 0)
    def _():
        fr = fr_ref[0]
        h1 = jnp.sin(fr * (jnp.dot(z_ref[...], w1_ref[0], precision=hp, preferred_element_type=F32) + b1_ref[0]))
        hdn_scr[...] = jnp.sin(fr * (jnp.dot(h1, w2_ref[0], precision=hp, preferred_element_type=F32) + b2_ref[0]))

    hdn = hdn_scr[...]
    dec = dec_ref[...]
    h_f = jnp.dot(hdn, w3f_ref[0], precision=hp, preferred_element_type=F32) * dec
    h_b = jnp.dot(hdn, w3b_ref[0], precision=hp, preferred_element_type=F32) * dec
    t1 = _split_dot(fhi_ref[...], flo_ref[...], h_f + h_b)
    t2 = _split_dot(fhi_ref[L:2 * L, :], flo_ref[L:2 * L, :], h_f - h_b)
    row = lax.broadcasted_iota(jnp.int32, (L, 1), 0)
    first = row == 0
    wk = jnp.where(first, 1.0 / (2 * L), 2.0 / (2 * L))
    ha = t1[0:L]
    o_ref[0, 0] = ha * wk
    o_ref[0, 1] = jnp.where(first, 0.0, t2 * wk)
    o_ref[0, 2] = jnp.where(first, t1[L:2 * L], ha) * wk


FILT_CB = 256


def _filter_spectra(L, w1, b1, w2, b2, freq, w3, fhi, flo):
    tpos = np.arange(L, dtype=np.float64)
    t_norm = tpos / max(L - 1, 1)
    w = 2.0 * math.pi * tpos / L
    bands = np.linspace(1e-4, POS_BANDS - 1, POS_BANDS)
    z = np.concatenate([t_norm[:, None], np.cos(w[:, None] * bands), -np.sin(w[:, None] * bands)], axis=-1)
    z = np.pad(z, ((0, 0), (0, FEAT_PAD - POS_EMB))).astype(np.float32)
    max_decay = math.log(DECAY_TARGET) / FAST_DECAY_PCT
    min_decay = math.log(DECAY_TARGET) / SLOW_DECAY_PCT
    deltas = np.abs(np.linspace(min_decay, max_decay, D_HY))
    dec = (np.exp(-t_norm[:, None] * deltas) + DECAY_SHIFT).astype(np.float32)

    pad_h = FEAT_PAD - FILT_HID
    w1p = jnp.pad(w1, ((0, 0), (0, FEAT_PAD - POS_EMB), (0, pad_h)))
    w2p = jnp.pad(w2, ((0, 0), (0, pad_h), (0, pad_h)))
    w3p = jnp.pad(w3, ((0, 0), (0, pad_h), (0, 0)))
    vec = lambda v: jnp.pad(v, ((0, 0), (0, pad_h))).reshape(DEPTH, 1, FEAT_PAD)
    ncb = D_HY // FILT_CB
    small = lambda shape: pl.BlockSpec((1,) + shape, lambda l, c: (l, 0, 0))
    return pl.pallas_call(
        functools.partial(_filter_kernel, L),
        grid=(DEPTH, ncb),
        in_specs=[
            pl.BlockSpec((L, FEAT_PAD), lambda l, c: (0, 0)),
            small((FEAT_PAD, FEAT_PAD)), small((1, FEAT_PAD)),
            small((FEAT_PAD, FEAT_PAD)), small((1, FEAT_PAD)), small((1, FEAT_PAD)),
            pl.BlockSpec((1, FEAT_PAD, FILT_CB), lambda l, c: (l, 0, c)),
            pl.BlockSpec((1, FEAT_PAD, FILT_CB), lambda l, c: (l, 0, ncb + c)),
            pl.BlockSpec((L, FILT_CB), lambda l, c: (0, c)),
            pl.BlockSpec((2 * L, L), lambda l, c: (0, 0)),
            pl.BlockSpec((2 * L, L), lambda l, c: (0, 0)),
        ],
        out_specs=pl.BlockSpec((1, 3, L, FILT_CB), lambda l, c: (l, 0, 0, c)),
        out_shape=jax.ShapeDtypeStruct((DEPTH, 3, L, D_HY), F32),
        scratch_shapes=[pltpu.VMEM((L, FEAT_PAD), F32)],
        compiler_params=_params("arbitrary", "arbitrary"),
        name=f"hyena_filter_{L}",
    )(jnp.asarray(z), w1p, vec(b1), w2p, vec(b2), vec(freq), w3p, w3p, jnp.asarray(dec), fhi, flo)


def _hyena_kernel(L, x0_ref, x1_ref, vv_ref, g_ref, cw0_ref, cw1_ref, cwv_ref, cb0_ref, cb1_ref, cbv_ref,
                  bias_ref, coef_ref, f_ref, ft_ref, o_ref):
    row = lax.broadcasted_iota(jnp.int32, (L, 1), 0)
    is_first = row == 0
    is_last = row == L - 1

    def sconv(x_ref, w_ref, b_ref):
        x = x_ref[...]
        prev = jnp.where(is_first, 0.0, pltpu.roll(x, 1, axis=0))
        nxt = jnp.where(is_last, 0.0, pltpu.roll(x, L - 1, axis=0))
        return w_ref[0:1, :] * prev + w_ref[1:2, :] * x + w_ref[2:3, :] * nxt + b_ref[...]

    z = sconv(vv_ref, cwv_ref, cbv_ref) * sconv(x1_ref, cw1_ref, cb1_ref)
    ab = jnp.dot(f_ref[...], z.astype(BF16), preferred_element_type=F32)
    a, b = ab[0:L], ab[L:2 * L]
    g1, g2, g3 = coef_ref[0], coef_ref[1], coef_ref[2]
    pq = jnp.concatenate([a * g1 - b * g2, a * g2 + b * g3], axis=0).astype(BF16)
    y = jnp.dot(ft_ref[...], pq, preferred_element_type=F32) + bias_ref[...] * z
    o_ref[...] = (sconv(x0_ref, cw0_ref, cb0_ref) * y * _silu(g_ref[...])).astype(BF16)


def _hyena(proj, L, cb, conv_w, conv_b, hy_bias, coef, f_bf, ft_bf):
    m = proj.shape[0]
    nb = m // L
    ncb = D_HY // cb
    col = lambda off: (lambda c, b: (b, off // cb + c))
    cw = lambda part: pl.BlockSpec((3, cb), lambda c, b: (0, part * ncb + c))
    cbias = lambda part: pl.BlockSpec((1, cb), lambda c, b: (0, part * ncb + c))
    return pl.pallas_call(
        functools.partial(_hyena_kernel, L),
        grid=(ncb, nb),
        in_specs=[
            pl.BlockSpec((L, cb), col(COL_X0)),
            pl.BlockSpec((L, cb), col(COL_X1)),
            pl.BlockSpec((L, cb), col(COL_VV)),
            pl.BlockSpec((L, cb), col(COL_GH)),
            cw(0), cw(1), cw(2), cbias(0), cbias(1), cbias(2),
            pl.BlockSpec((1, cb), lambda c, b: (0, c)),
            pl.BlockSpec((3, L, cb), lambda c, b: (0, 0, c)),
            pl.BlockSpec((2 * L, L), lambda c, b: (0, 0)),
            pl.BlockSpec((L, 2 * L), lambda c, b: (0, 0)),
        ],
        out_specs=pl.BlockSpec((L, cb), lambda c, b: (b, c)),
        out_shape=jax.ShapeDtypeStruct((m, D_HY), BF16),
        compiler_params=_params("arbitrary", "arbitrary"),
        name=f"hyena_{L}",
    )(proj, proj, proj, proj, conv_w, conv_w, conv_w, conv_b, conv_b, conv_b, hy_bias, coef, f_bf, ft_bf)


OUTPROJ_TM = 256


def _outproj_kernel(final, att_ref, hy_ref, wa_ref, wh_ref, x_ref, gate_ref, fg_ref, o_ref):
    out = jnp.dot(att_ref[...], wa_ref[...], preferred_element_type=F32)
    out = out + jnp.dot(hy_ref[...], wh_ref[...], preferred_element_type=F32)
    y = x_ref[...] + gate_ref[0] * out
    if final:
        y = _rms(y, fg_ref[...])
    o_ref[...] = y


def _out_proj(att, hy, w_bf16, x, mod3, final_g, row0, tiles_per_row, final):
    m = x.shape[0]
    return pl.pallas_call(
        functools.partial(_outproj_kernel, final),
        grid=(m // OUTPROJ_TM,),
        in_specs=[
            pl.BlockSpec((OUTPROJ_TM, D_ATT), lambda i: (i, 0)),
            pl.BlockSpec((OUTPROJ_TM, D_HY), lambda i: (i, 0)),
            pl.BlockSpec((D_ATT, D_MODEL), lambda i: (0, 0)),
            pl.BlockSpec((D_HY, D_MODEL), lambda i: (1, 0)),
            pl.BlockSpec((OUTPROJ_TM, D_MODEL), lambda i: (i, 0)),
            pl.BlockSpec((1, 1, D_MODEL), lambda i: (row0 + i // tiles_per_row, 0, 2)),
            pl.BlockSpec((1, D_MODEL), lambda i: (0, 0)),
        ],
        out_specs=pl.BlockSpec((OUTPROJ_TM, D_MODEL), lambda i: (i, 0)),
        out_shape=jax.ShapeDtypeStruct((m, D_MODEL), F32),
        compiler_params=_params("arbitrary"),
        name="out_proj_final" if final else "out_proj",
    )(att, hy, w_bf16, w_bf16, x, mod3, final_g)


def _rope_tables():
    t = np.arange(DEC_SEQ)
    row = (t // GRID_W).astype(np.float64)
    col = (t % GRID_W).astype(np.float64)
    pairs = HEAD_DIM // 4
    inv_freq = ROPE_THETA ** (-np.arange(pairs, dtype=np.float64) / pairs)
    ang = np.concatenate([row[:, None] * inv_freq, col[:, None] * inv_freq], axis=-1)
    cos = np.repeat(np.cos(ang), 2, axis=-1).astype(np.float32)
    sin = np.repeat(np.sin(ang), 2, axis=-1).astype(np.float32)
    even = (np.arange(HEAD_DIM) % 2 == 0)[None, :]
    sin_a = np.where(even, -sin, 0.0).astype(np.float32)
    sin_b = np.where(even, 0.0, sin).astype(np.float32)
    return jnp.asarray(cos), jnp.asarray(sin_a), jnp.asarray(sin_b)


def kernel(x_prompt, x_sample, cache_k, cache_v, c, c_ctx, norm_g, w_ada, b_ada, w_in, q_norm_g, k_norm_g,
           conv_w, conv_b, filt_w1, filt_b1, filt_w2, filt_b2, filt_w3, filt_freq, hy_bias, w_out, final_norm_g):
    ctx = x_prompt.reshape(BATCH * SEQ, D_MODEL)
    lat = x_sample.reshape(DEC_BATCH * DEC_SEQ, D_MODEL)
    cache_k4 = cache_k.reshape(DEC_BATCH, DEPTH, PAST_LEN, D_KV)
    cache_v4 = cache_v.reshape(DEC_BATCH, DEPTH, PAST_LEN, D_KV)

    cvec = jnp.concatenate([c_ctx[None, :], c, jnp.zeros((MOD_ROWS - 1 - DEC_BATCH, D_MODEL), F32)], axis=0)
    mod = _modulation(cvec, w_ada, b_ada)

    rope_tabs = _rope_tables()
    dft = {}
    coefs = {}
    for L in (SEQ, DEC_SEQ):
        f_np = _dft_matrix(L)
        fhi, flo = _hi_lo(jnp.asarray(f_np))
        dft[L] = (fhi, jnp.asarray(np.ascontiguousarray(f_np.T)).astype(BF16))
        coefs[L] = _filter_spectra(L, filt_w1, filt_b1, filt_w2, filt_b2, filt_freq, filt_w3, fhi, flo)

    w_in_bf = w_in.astype(BF16)
    w_out_bf = w_out.astype(BF16)
    final_g = final_norm_g.reshape(1, D_MODEL)

    ks, vs = [], []
    for l in range(DEPTH):
        mod3 = mod[l].reshape(MOD_ROWS, 1, 3 * D_MODEL)
        ng = norm_g[l].reshape(1, D_MODEL)
        qg = q_norm_g[l].reshape(1, HEAD_DIM)
        kg = k_norm_g[l].reshape(1, HEAD_DIM)
        cwl, cbl = conv_w[l], conv_b[l].reshape(1, 3 * D_HY)
        hb = hy_bias[l].reshape(1, D_HY)
        final = l == DEPTH - 1
        big = 1 << 30

        proj_c = _in_proj(ctx, ng, mod3, w_in_bf[l], 0, big)
        proj_l = _in_proj(lat, ng, mod3, w_in_bf[l], 1, DEC_SEQ // INPROJ_TM)

        att_c, k_new, v_new = _attn_ctx(proj_c, qg, kg)
        ks.append(k_new)
        vs.append(v_new)
        att_l = _attn_lat(proj_l, cache_k4, cache_v4, l, qg, kg, rope_tabs)

        hy_c = _hyena(proj_c, SEQ, D_HY, cwl, cbl, hb, coefs[SEQ][l], *dft[SEQ])
        hy_l = _hyena(proj_l, DEC_SEQ, 256, cwl, cbl, hb, coefs[DEC_SEQ][l], *dft[DEC_SEQ])

        ctx = _out_proj(att_c, hy_c, w_out_bf[l], ctx, mod3, final_g, 0, big, final)
        lat = _out_proj(att_l, hy_l, w_out_bf[l], lat, mod3, final_g, 1, DEC_SEQ // OUTPROJ_TM, final)

    y_prompt = ctx.reshape(BATCH, SEQ, D_MODEL)
    y_sample = lat.reshape(DEC_BATCH, DEC_SEQ, D_MODEL)
    new_k = jnp.stack(ks, axis=1).reshape(BATCH, DEPTH, SEQ, N_KV_HEADS, HEAD_DIM)
    new_v = jnp.stack(vs, axis=1).reshape(BATCH, DEPTH, SEQ, N_KV_HEADS, HEAD_DIM)
    return (y_prompt, y_sample, new_k, new_v)
```

```python
import functools
import math

import numpy as np
import jax
import jax.numpy as jnp
from jax import lax
from jax.experimental import pallas as pl
from jax.experimental.pallas import tpu as pltpu

D_MODEL = 2048
BATCH = 16
SEQ = 256
DEPTH = 2
DEC_BATCH = 2
DEC_SEQ = 1024
PAST_LEN = 512
GRID_W = 64
D_ATT = 1024
D_HY = 1024
HEAD_DIM = 128
N_HEADS = 8
N_KV_HEADS = 4
GROUP = 2
D_KV = 512
ROPE_THETA = 10000.0
POS_BANDS = 16
POS_EMB = 33
FILT_HID = 64
DECAY_TARGET = 1e-2
FAST_DECAY_PCT = 0.3
SLOW_DECAY_PCT = 1.5
DECAY_SHIFT = 0.05
EPS = 1e-6
D_IN = 7168

COL_Q, COL_K, COL_V, COL_GA, COL_X0, COL_X1, COL_VV, COL_GH = 0, 1024, 1536, 2048, 3072, 4096, 5120, 6144

F32 = jnp.float32
BF16 = jnp.bfloat16

VMEM_LIMIT_BYTES = 56 * 1024 * 1024
FEAT_PAD = 128


def _params(*sem):
    return pltpu.CompilerParams(dimension_semantics=sem, vmem_limit_bytes=VMEM_LIMIT_BYTES)


def _silu(x):
    return x * (1.0 / (1.0 + jnp.exp(-x)))


def _rms(x, g):
    return x * lax.rsqrt(jnp.mean(x * x, axis=-1, keepdims=True) + EPS) * g


MOD_ROWS = 8
MOD_TN = 512


def _mod_kernel(c_ref, w_ref, b_ref, o_ref):
    s = _silu(c_ref[...]).astype(BF16)
    o_ref[0] = jnp.dot(s, w_ref[0].astype(BF16), preferred_element_type=F32) + b_ref[0]


def _modulation(cvec, w_ada, b_ada):
    n = 3 * D_MODEL
    return pl.pallas_call(
        _mod_kernel,
        grid=(DEPTH, n // MOD_TN),
        in_specs=[
            pl.BlockSpec((MOD_ROWS, D_MODEL), lambda l, j: (0, 0)),
            pl.BlockSpec((1, D_MODEL, MOD_TN), lambda l, j: (l, 0, j)),
            pl.BlockSpec((1, 1, MOD_TN), lambda l, j: (l, 0, j)),
        ],
        out_specs=pl.BlockSpec((1, MOD_ROWS, MOD_TN), lambda l, j: (l, 0, j)),
        out_shape=jax.ShapeDtypeStruct((DEPTH, MOD_ROWS, n), F32),
        compiler_params=_params("arbitrary", "arbitrary"),
        name="adaln_mod",
    )(cvec, w_ada, b_ada.reshape(DEPTH, 1, n))


M_CTX = BATCH * SEQ
M_LAT = DEC_BATCH * DEC_SEQ
M_ALL = M_CTX + M_LAT
NORM_TM = 512
INPROJ_TM = 2048
INPROJ_TN = 512


def _norm_kernel(n_ctx_tiles, xc_ref, xl_ref, g_ref, shift_ref, scale_ref, h_ref):
    def emit(x_ref):
        h = _rms(x_ref[...], g_ref[...]) * (1.0 + scale_ref[0]) + shift_ref[0]
        h_ref[...] = h.astype(BF16)

    i = pl.program_id(0)
    pl.when(i < n_ctx_tiles)(lambda: emit(xc_ref))
    pl.when(i >= n_ctx_tiles)(lambda: emit(xl_ref))


def _norm_mod(x_ctx, x_lat, norm_g, mod3):
    nc = M_CTX // NORM_TM
    per_batch = DEC_SEQ // NORM_TM
    mod_row = lambda i: jnp.where(i < nc, 0, 1 + (i - nc) // per_batch)
    return pl.pallas_call(
        functools.partial(_norm_kernel, nc),
        grid=(M_ALL // NORM_TM,),
        in_specs=[
            pl.BlockSpec((NORM_TM, D_MODEL), lambda i: (jnp.minimum(i, nc - 1), 0)),
            pl.BlockSpec((NORM_TM, D_MODEL), lambda i: (jnp.maximum(i - nc, 0), 0)),
            pl.BlockSpec((1, D_MODEL), lambda i: (0, 0)),
            pl.BlockSpec((1, 1, D_MODEL), lambda i: (mod_row(i), 0, 0)),
            pl.BlockSpec((1, 1, D_MODEL), lambda i: (mod_row(i), 0, 1)),
        ],
        out_specs=pl.BlockSpec((NORM_TM, D_MODEL), lambda i: (i, 0)),
        out_shape=jax.ShapeDtypeStruct((M_ALL, D_MODEL), BF16),
        compiler_params=_params("arbitrary"),
        name="norm_mod",
    )(x_ctx, x_lat, norm_g, mod3, mod3)


def _inproj_kernel(h_ref, w_ref, o_ref, w_scr):
    i = pl.program_id(1)

    @pl.when(i == 0)
    def _():
        w_scr[...] = w_ref[0].astype(BF16)

    rows = pl.ds(pl.multiple_of(i * INPROJ_TM, INPROJ_TM), INPROJ_TM)
    o_ref[...] = jnp.dot(h_ref[rows, :], w_scr[...], preferred_element_type=F32)


def _in_proj(h, w_in, layer):
    return pl.pallas_call(
        _inproj_kernel,
        grid=(D_IN // INPROJ_TN, M_ALL // INPROJ_TM),
        in_specs=[
            pl.BlockSpec((M_ALL, D_MODEL), lambda j, i: (0, 0), pipeline_mode=pl.Buffered(1)),
            pl.BlockSpec((1, D_MODEL, INPROJ_TN), lambda j, i: (layer, 0, j)),
        ],
        out_specs=pl.BlockSpec((INPROJ_TM, INPROJ_TN), lambda j, i: (i, j)),
        out_shape=jax.ShapeDtypeStruct((M_ALL, D_IN), F32),
        scratch_shapes=[pltpu.VMEM((D_MODEL, INPROJ_TN), BF16)],
        compiler_params=_params("arbitrary", "arbitrary"),
        name="in_proj",
    )(h, w_in)


ATT_SCALE = 1.0 / math.sqrt(HEAD_DIM)


def _head(ref_or_val, h):
    return ref_or_val[:, h * HEAD_DIM:(h + 1) * HEAD_DIM]


def _softmax_pv(q2, k_bf, v_bf):
    s = lax.dot_general(q2, k_bf, (((1,), (1,)), ((), ())), preferred_element_type=F32) * ATT_SCALE
    p = jnp.exp(s - jnp.max(s, axis=-1, keepdims=True))
    denom = jnp.sum(p, axis=-1, keepdims=True)
    o = jnp.dot(p.astype(BF16), v_bf, preferred_element_type=F32)
    return o * (1.0 / denom)


def _rope(x, cos2, sin_a, sin_b):
    nxt = pltpu.roll(x, HEAD_DIM - 1, axis=1)
    prv = pltpu.roll(x, 1, axis=1)
    return x * cos2 + nxt * sin_a + prv * sin_b


def _attn_ctx_kernel(q_ref, kv_ref, g_ref, qg_ref, kg_ref, *rest):
    att_ref, ko_ref, vo_ref = rest[-3:]
    rows = q_ref.shape[0]
    for h in range(N_KV_HEADS):
        kn = _rms(_head(kv_ref, h), kg_ref[...])
        vh = _head(kv_ref, N_KV_HEADS + h)
        ko_ref[0, 0, :, h, :] = kn
        vo_ref[0, 0, :, h, :] = vh
        q2 = jnp.concatenate(
            [_rms(_head(q_ref, GROUP * h + g), qg_ref[...]) for g in range(GROUP)], axis=0
        ).astype(BF16)
        o = _softmax_pv(q2, kn.astype(BF16), vh.astype(BF16))
        for g in range(GROUP):
            hq = GROUP * h + g
            gate = _silu(_head(g_ref, hq))
            att_ref[:, hq * HEAD_DIM:(hq + 1) * HEAD_DIM] = (o[g * rows:(g + 1) * rows] * gate).astype(BF16)


def _attn_ctx(proj, q_g, k_g, layer, kv_prev):
    kv_shape = jax.ShapeDtypeStruct((BATCH, DEPTH, SEQ, N_KV_HEADS, HEAD_DIM), F32)
    kv_spec = pl.BlockSpec((1, 1, SEQ, N_KV_HEADS, HEAD_DIM), lambda b: (b, layer, 0, 0, 0))
    in_specs = [
        pl.BlockSpec((SEQ, D_ATT), lambda b: (b, COL_Q // D_ATT)),
        pl.BlockSpec((SEQ, 2 * D_KV), lambda b: (b, COL_K // (2 * D_KV))),
        pl.BlockSpec((SEQ, D_ATT), lambda b: (b, COL_GA // D_ATT)),
        pl.BlockSpec((1, HEAD_DIM), lambda b: (0, 0)),
        pl.BlockSpec((1, HEAD_DIM), lambda b: (0, 0)),
    ]
    args = [proj, proj, proj, q_g, k_g]
    aliases = {}
    if kv_prev is not None:
        in_specs += [pl.BlockSpec(memory_space=pl.ANY)] * 2
        args += list(kv_prev)
        aliases = {5: 1, 6: 2}
    return pl.pallas_call(
        _attn_ctx_kernel,
        grid=(BATCH,),
        in_specs=in_specs,
        out_specs=[pl.BlockSpec((SEQ, D_ATT), lambda b: (b, 0)), kv_spec, kv_spec],
        out_shape=[jax.ShapeDtypeStruct((M_CTX, D_ATT), BF16), kv_shape, kv_shape],
        input_output_aliases=aliases,
        compiler_params=_params("arbitrary"),
        name="attn_ctx",
    )(*args)


LAT_TQ = 256
LAT_KEYS = DEC_SEQ + PAST_LEN


def _attn_lat_kernel(q_ref, kv_ref, g_ref, ck_ref, cv_ref, qg_ref, kg_ref,
                     qc_ref, qsa_ref, qsb_ref, kc_ref, ksa_ref, ksb_ref,
                     att_ref, k_scr, v_scr):
    @pl.when(pl.program_id(1) == 0)
    def _():
        for h in range(N_KV_HEADS):
            kn = _rope(_rms(_head(kv_ref, h), kg_ref[...]), kc_ref[...], ksa_ref[...], ksb_ref[...])
            k_scr[0:DEC_SEQ, h * HEAD_DIM:(h + 1) * HEAD_DIM] = kn.astype(BF16)
            v_scr[0:DEC_SEQ, h * HEAD_DIM:(h + 1) * HEAD_DIM] = _head(kv_ref, N_KV_HEADS + h).astype(BF16)
        k_scr[DEC_SEQ:LAT_KEYS, :] = ck_ref[0, 0].astype(BF16)
        v_scr[DEC_SEQ:LAT_KEYS, :] = cv_ref[0, 0].astype(BF16)

    rows = q_ref.shape[0]
    for h in range(N_KV_HEADS):
        q2 = jnp.concatenate(
            [_rope(_rms(_head(q_ref, GROUP * h + g), qg_ref[...]), qc_ref[...], qsa_ref[...], qsb_ref[...])
             for g in range(GROUP)], axis=0).astype(BF16)
        o = _softmax_pv(q2, _head(k_scr, h), _head(v_scr, h))
        for g in range(GROUP):
            hq = GROUP * h + g
            gate = _silu(_head(g_ref, hq))
            att_ref[:, hq * HEAD_DIM:(hq + 1) * HEAD_DIM] = (o[g * rows:(g + 1) * rows] * gate).astype(BF16)


def _attn_lat(proj, cache_k4, cache_v4, layer, q_g, k_g, rope_tabs):
    nq = DEC_SEQ // LAT_TQ
    q0 = M_CTX // LAT_TQ
    kv0 = M_CTX // DEC_SEQ
    cos2, sin_a, sin_b = rope_tabs
    qtab = pl.BlockSpec((LAT_TQ, HEAD_DIM), lambda b, i: (i, 0))
    ktab = pl.BlockSpec((DEC_SEQ, HEAD_DIM), lambda b, i: (0, 0))
    cache = pl.BlockSpec((1, 1, PAST_LEN, D_KV), lambda b, i: (b, layer, 0, 0))
    vec = pl.BlockSpec((1, HEAD_DIM), lambda b, i: (0, 0))
    return pl.pallas_call(
        _attn_lat_kernel,
        grid=(DEC_BATCH, nq),
        in_specs=[
            pl.BlockSpec((LAT_TQ, D_ATT), lambda b, i: (q0 + b * nq + i, COL_Q // D_ATT)),
            pl.BlockSpec((DEC_SEQ, 2 * D_KV), lambda b, i: (kv0 + b, COL_K // (2 * D_KV))),
            pl.BlockSpec((LAT_TQ, D_ATT), lambda b, i: (q0 + b * nq + i, COL_GA // D_ATT)),
            cache, cache, vec, vec, qtab, qtab, qtab, ktab, ktab, ktab,
        ],
        out_specs=pl.BlockSpec((LAT_TQ, D_ATT), lambda b, i: (b * nq + i, 0)),
        out_shape=jax.ShapeDtypeStruct((M_LAT, D_ATT), BF16),
        scratch_shapes=[pltpu.VMEM((LAT_KEYS, D_KV), BF16), pltpu.VMEM((LAT_KEYS, D_KV), BF16)],
        compiler_params=_params("arbitrary", "arbitrary"),
        name="attn_lat",
    )(proj, proj, proj, cache_k4, cache_v4, q_g, k_g, cos2, sin_a, sin_b, cos2, sin_a, sin_b)


def _dft_matrix(L):
    k = np.arange(L, dtype=np.int64)[:, None]
    t = np.arange(L, dtype=np.int64)[None, :]
    ang = 2.0 * np.pi * ((k * t) % (2 * L)).astype(np.float64) / (2 * L)
    top = np.cos(ang)
    bot = np.sin(ang)
    bot[0, :] = np.where(np.arange(L) % 2 == 0, 1.0, -1.0)
    return np.concatenate([top, bot], axis=0).astype(np.float32)


def _hi_lo(a):
    hi = a.astype(BF16)
    lo = (a - hi.astype(F32)).astype(BF16)
    return hi, lo


def _split_dot(a_hi, a_lo, b):
    b_hi = b.astype(BF16)
    b_lo = (b - b_hi.astype(F32)).astype(BF16)
    dot = functools.partial(jnp.dot, preferred_element_type=F32)
    return dot(a_hi, b_hi) + (dot(a_lo, b_hi) + dot(a_hi, b_lo))


def _filter_kernel(L, z_ref, w1_ref, b1_ref, w2_ref, b2_ref, fr_ref, w3f_ref, w3b_ref, dec_ref,
                   fhi_ref, flo_ref, o_ref, hdn_scr):
    hp = lax.Precision.HIGHEST

    @pl.when(pl.program_id(1) == 0)
    def _():
        fr = fr_ref[0]
        h1 = jnp.sin(fr * (jnp.dot(z_ref[...], w1_ref[0], precision=hp, preferred_element_type=F32) + b1_ref[0]))
        hdn_scr[...] = jnp.sin(fr * (jnp.dot(h1, w2_ref[0], precision=hp, preferred_element_type=F32) + b2_ref[0]))

    hdn = hdn_scr[...]
    dec = dec_ref[...]
    h_f = jnp.dot(hdn, w3f_ref[0], precision=hp, preferred_element_type=F32) * dec
    h_b = jnp.dot(hdn, w3b_ref[0], precision=hp, preferred_element_type=F32) * dec
    t1 = _split_dot(fhi_ref[...], flo_ref[...], h_f + h_b)
    t2 = _split_dot(fhi_ref[L:2 * L, :], flo_ref[L:2 * L, :], h_f - h_b)
    row = lax.broadcasted_iota(jnp.int32, (L, 1), 0)
    first = row == 0
    wk = jnp.where(first, 1.0 / (2 * L), 2.0 / (2 * L))
    ha = t1[0:L]
    o_ref[0, 0] = ha * wk
    o_ref[0, 1] = jnp.where(first, 0.0, t2 * wk)
    o_ref[0, 2] = jnp.where(first, t1[L:2 * L], ha) * wk


FILT_CB = 256


def _filter_spectra(L, w1, b1, w2, b2, freq, w3, fhi, flo):
    tpos = np.arange(L, dtype=np.float64)
    t_norm = tpos / max(L - 1, 1)
    w = 2.0 * math.pi * tpos / L
    bands = np.linspace(1e-4, POS_BANDS - 1, POS_BANDS)
    z = np.concatenate([t_norm[:, None], np.cos(w[:, None] * bands), -np.sin(w[:, None] * bands)], axis=-1)
    z = np.pad(z, ((0, 0), (0, FEAT_PAD - POS_EMB))).astype(np.float32)
    max_decay = math.log(DECAY_TARGET) / FAST_DECAY_PCT
    min_decay = math.log(DECAY_TARGET) / SLOW_DECAY_PCT
    deltas = np.abs(np.linspace(min_decay, max_decay, D_HY))
    dec = (np.exp(-t_norm[:, None] * deltas) + DECAY_SHIFT).astype(np.float32)

    pad_h = FEAT_PAD - FILT_HID
    w1p = jnp.pad(w1, ((0, 0), (0, FEAT_PAD - POS_EMB), (0, pad_h)))
    w2p = jnp.pad(w2, ((0, 0), (0, pad_h), (0, pad_h)))
    w3p = jnp.pad(w3, ((0, 0), (0, pad_h), (0, 0)))
    vec = lambda v: jnp.pad(v, ((0, 0), (0, pad_h))).reshape(DEPTH, 1, FEAT_PAD)
    ncb = D_HY // FILT_CB
    small = lambda shape: pl.BlockSpec((1,) + shape, lambda l, c: (l, 0, 0))
    return pl.pallas_call(
        functools.partial(_filter_kernel, L),
        grid=(DEPTH, ncb),
        in_specs=[
            pl.BlockSpec((L, FEAT_PAD), lambda l, c: (0, 0)),
            small((FEAT_PAD, FEAT_PAD)), small((1, FEAT_PAD)),
            small((FEAT_PAD, FEAT_PAD)), small((1, FEAT_PAD)), small((1, FEAT_PAD)),
            pl.BlockSpec((1, FEAT_PAD, FILT_CB), lambda l, c: (l, 0, c)),
            pl.BlockSpec((1, FEAT_PAD, FILT_CB), lambda l, c: (l, 0, ncb + c)),
            pl.BlockSpec((L, FILT_CB), lambda l, c: (0, c)),
            pl.BlockSpec((2 * L, L), lambda l, c: (0, 0)),
            pl.BlockSpec((2 * L, L), lambda l, c: (0, 0)),
        ],
        out_specs=pl.BlockSpec((1, 3, L, FILT_CB), lambda l, c: (l, 0, 0, c)),
        out_shape=jax.ShapeDtypeStruct((DEPTH, 3, L, D_HY), F32),
        scratch_shapes=[pltpu.VMEM((L, FEAT_PAD), F32)],
        compiler_params=_params("arbitrary", "arbitrary"),
        name=f"hyena_filter_{L}",
    )(jnp.asarray(z), w1p, vec(b1), w2p, vec(b2), vec(freq), w3p, w3p, jnp.asarray(dec), fhi, flo)


def _hyena_kernel(L, x0_ref, x1_ref, vv_ref, g_ref, cw0_ref, cw1_ref, cwv_ref, cb0_ref, cb1_ref, cbv_ref,
                  bias_ref, coef_ref, f_ref, ft_ref, o_ref):
    row = lax.broadcasted_iota(jnp.int32, (L, 1), 0)
    is_first = row == 0
    is_last = row == L - 1

    def sconv(x_ref, w_ref, b_ref):
        x = x_ref[...]
        prev = jnp.where(is_first, 0.0, pltpu.roll(x, 1, axis=0))
        nxt = jnp.where(is_last, 0.0, pltpu.roll(x, L - 1, axis=0))
        return w_ref[0, 0:1, :] * prev + w_ref[0, 1:2, :] * x + w_ref[0, 2:3, :] * nxt + b_ref[0]

    z = sconv(vv_ref, cwv_ref, cbv_ref) * sconv(x1_ref, cw1_ref, cb1_ref)
    ab = jnp.dot(f_ref[...], z.astype(BF16), preferred_element_type=F32)
    a, b = ab[0:L], ab[L:2 * L]
    g1, g2, g3 = coef_ref[0, 0], coef_ref[0, 1], coef_ref[0, 2]
    pq = jnp.concatenate([a * g1 - b * g2, a * g2 + b * g3], axis=0).astype(BF16)
    y = jnp.dot(ft_ref[...], pq, preferred_element_type=F32) + bias_ref[0] * z
    o_ref[...] = (sconv(x0_ref, cw0_ref, cb0_ref) * y * _silu(g_ref[...])).astype(BF16)


def _hyena(proj, L, nb, row0, cb, layer, conv_w, conv_b, hy_bias, coef, f_bf, ft_bf):
    m = nb * L
    ncb = D_HY // cb
    col = lambda off: (lambda c, b: (row0 + b, off // cb + c))
    cw = lambda part: pl.BlockSpec((1, 3, cb), lambda c, b: (layer, 0, part * ncb + c))
    cbias = lambda part: pl.BlockSpec((1, 1, cb), lambda c, b: (layer, 0, part * ncb + c))
    return pl.pallas_call(
        functools.partial(_hyena_kernel, L),
        grid=(ncb, nb),
        in_specs=[
            pl.BlockSpec((L, cb), col(COL_X0)),
            pl.BlockSpec((L, cb), col(COL_X1)),
            pl.BlockSpec((L, cb), col(COL_VV)),
            pl.BlockSpec((L, cb), col(COL_GH)),
            cw(0), cw(1), cw(2), cbias(0), cbias(1), cbias(2),
            pl.BlockSpec((1, 1, cb), lambda c, b: (layer, 0, c)),
            pl.BlockSpec((1, 3, L, cb), lambda c, b: (layer, 0, 0, c)),
            pl.BlockSpec((2 * L, L), lambda c, b: (0, 0)),
            pl.BlockSpec((L, 2 * L), lambda c, b: (0, 0)),
        ],
        out_specs=pl.BlockSpec((L, cb), lambda c, b: (b, c)),
        out_shape=jax.ShapeDtypeStruct((m, D_HY), BF16),
        compiler_params=_params("arbitrary", "arbitrary"),
        name=f"hyena_{L}",
    )(proj, proj, proj, proj, conv_w, conv_w, conv_w, conv_b, conv_b, conv_b, hy_bias, coef, f_bf, ft_bf)


OUTPROJ_TM = 256


def _outproj_kernel(final, att_ref, hy_ref, w_ref, x_ref, gate_ref, fg_ref, o_ref, w_scr):
    @pl.when(pl.program_id(0) == 0)
    def _():
        w_scr[...] = w_ref[0].astype(BF16)

    out = jnp.dot(att_ref[...], w_scr[0:D_ATT, :], preferred_element_type=F32)
    out = out + jnp.dot(hy_ref[...], w_scr[D_ATT:D_ATT + D_HY, :], preferred_element_type=F32)
    y = x_ref[...] + gate_ref[0] * out
    if final:
        y = _rms(y, fg_ref[...])
    o_ref[...] = y


def _out_proj(att, hy, w_out, layer, x, mod3, final_g, row0, tiles_per_row, final):
    m = x.shape[0]
    return pl.pallas_call(
        functools.partial(_outproj_kernel, final),
        grid=(m // OUTPROJ_TM,),
        in_specs=[
            pl.BlockSpec((OUTPROJ_TM, D_ATT), lambda i: (i, 0)),
            pl.BlockSpec((OUTPROJ_TM, D_HY), lambda i: (i, 0)),
            pl.BlockSpec((1, D_ATT + D_HY, D_MODEL), lambda i: (layer, 0, 0), pipeline_mode=pl.Buffered(1)),
            pl.BlockSpec((OUTPROJ_TM, D_MODEL), lambda i: (i, 0)),
            pl.BlockSpec((1, 1, D_MODEL), lambda i: (row0 + i // tiles_per_row, 0, 2)),
            pl.BlockSpec((1, D_MODEL), lambda i: (0, 0)),
        ],
        out_specs=pl.BlockSpec((OUTPROJ_TM, D_MODEL), lambda i: (i, 0)),
        out_shape=jax.ShapeDtypeStruct((m, D_MODEL), F32),
        scratch_shapes=[pltpu.VMEM((D_ATT + D_HY, D_MODEL), BF16)],
        compiler_params=_params("arbitrary"),
        name="out_proj_final" if final else "out_proj",
    )(att, hy, w_out, x, mod3, final_g)


def _rope_tables():
    t = np.arange(DEC_SEQ)
    row = (t // GRID_W).astype(np.float64)
    col = (t % GRID_W).astype(np.float64)
    pairs = HEAD_DIM // 4
    inv_freq = ROPE_THETA ** (-np.arange(pairs, dtype=np.float64) / pairs)
    ang = np.concatenate([row[:, None] * inv_freq, col[:, None] * inv_freq], axis=-1)
    cos = np.repeat(np.cos(ang), 2, axis=-1).astype(np.float32)
    sin = np.repeat(np.sin(ang), 2, axis=-1).astype(np.float32)
    even = (np.arange(HEAD_DIM) % 2 == 0)[None, :]
    sin_a = np.where(even, -sin, 0.0).astype(np.float32)
    sin_b = np.where(even, 0.0, sin).astype(np.float32)
    return jnp.asarray(cos), jnp.asarray(sin_a), jnp.asarray(sin_b)


def kernel(x_prompt, x_sample, cache_k, cache_v, c, c_ctx, norm_g, w_ada, b_ada, w_in, q_norm_g, k_norm_g,
           conv_w, conv_b, filt_w1, filt_b1, filt_w2, filt_b2, filt_w3, filt_freq, hy_bias, w_out, final_norm_g):
    ctx = x_prompt.reshape(BATCH * SEQ, D_MODEL)
    lat = x_sample.reshape(DEC_BATCH * DEC_SEQ, D_MODEL)
    cache_k4 = cache_k.reshape(DEC_BATCH, DEPTH, PAST_LEN, D_KV)
    cache_v4 = cache_v.reshape(DEC_BATCH, DEPTH, PAST_LEN, D_KV)

    cvec = jnp.concatenate([c_ctx[None, :], c, jnp.zeros((MOD_ROWS - 1 - DEC_BATCH, D_MODEL), F32)], axis=0)
    mod = _modulation(cvec, w_ada, b_ada)

    rope_tabs = _rope_tables()
    dft = {}
    coefs = {}
    for L in (SEQ, DEC_SEQ):
        f_np = _dft_matrix(L)
        fhi, flo = _hi_lo(jnp.asarray(f_np))
        dft[L] = (fhi, jnp.asarray(np.ascontiguousarray(f_np.T)).astype(BF16))
        coefs[L] = _filter_spectra(L, filt_w1, filt_b1, filt_w2, filt_b2, filt_freq, filt_w3, fhi, flo)

    final_g = final_norm_g.reshape(1, D_MODEL)
    conv_b3 = conv_b.reshape(DEPTH, 1, 3 * D_HY)
    hy_bias3 = hy_bias.reshape(DEPTH, 1, D_HY)

    kv_out = None
    for l in range(DEPTH):
        mod3 = mod[l].reshape(MOD_ROWS, 1, 3 * D_MODEL)
        ng = norm_g[l].reshape(1, D_MODEL)
        qg = q_norm_g[l].reshape(1, HEAD_DIM)
        kg = k_norm_g[l].reshape(1, HEAD_DIM)
        final = l == DEPTH - 1
        big = 1 << 30

        h = _norm_mod(ctx, lat, ng, mod3)
        proj = _in_proj(h, w_in, l)

        att_c, new_k, new_v = _attn_ctx(proj, qg, kg, l, kv_out)
        kv_out = (new_k, new_v)
        att_l = _attn_lat(proj, cache_k4, cache_v4, l, qg, kg, rope_tabs)

        hy_c = _hyena(proj, SEQ, BATCH, 0, D_HY, l, conv_w, conv_b3, hy_bias3, coefs[SEQ], *dft[SEQ])
        hy_l = _hyena(proj, DEC_SEQ, DEC_BATCH, M_CTX // DEC_SEQ, 256, l, conv_w, conv_b3, hy_bias3,
                      coefs[DEC_SEQ], *dft[DEC_SEQ])

        ctx = _out_proj(att_c, hy_c, w_out, l, ctx, mod3, final_g, 0, big, final)
        lat = _out_proj(att_l, hy_l, w_out, l, lat, mod3, final_g, 1, DEC_SEQ // OUTPROJ_TM, final)

    y_prompt = ctx.reshape(BATCH, SEQ, D_MODEL)
    y_sample = lat.reshape(DEC_BATCH, DEC_SEQ, D_MODEL)
    return (y_prompt, y_sample, kv_out[0], kv_out[1])
```

```python
import functools
import math

import numpy as np
import jax
import jax.numpy as jnp
from jax import lax
from jax.experimental import pallas as pl
from jax.experimental.pallas import tpu as pltpu

D_MODEL = 2048
BATCH = 16
SEQ = 256
DEPTH = 2
DEC_BATCH = 2
DEC_SEQ = 1024
PAST_LEN = 512
GRID_W = 64
D_ATT = 1024
D_HY = 1024
HEAD_DIM = 128
N_HEADS = 8
N_KV_HEADS = 4
GROUP = 2
D_KV = 512
ROPE_THETA = 10000.0
POS_BANDS = 16
POS_EMB = 33
FILT_HID = 64
DECAY_TARGET = 1e-2
FAST_DECAY_PCT = 0.3
SLOW_DECAY_PCT = 1.5
DECAY_SHIFT = 0.05
EPS = 1e-6
D_IN = 7168

COL_Q, COL_K, COL_V, COL_GA, COL_X0, COL_X1, COL_VV, COL_GH = 0, 1024, 1536, 2048, 3072, 4096, 5120, 6144

F32 = jnp.float32
BF16 = jnp.bfloat16

VMEM_LIMIT_BYTES = 56 * 1024 * 1024
FEAT_PAD = 128


def _params(*sem):
    return pltpu.CompilerParams(dimension_semantics=sem, vmem_limit_bytes=VMEM_LIMIT_BYTES)


def _silu(x):
    return x * (1.0 / (1.0 + jnp.exp(-x)))


def _rms(x, g):
    return x * lax.rsqrt(jnp.mean(x * x, axis=-1, keepdims=True) + EPS) * g


MOD_ROWS = 8
MOD_TN = 512


def _mod_kernel(c_ref, w_ref, b_ref, o_ref):
    s = _silu(c_ref[...]).astype(BF16)
    o_ref[0] = jnp.dot(s, w_ref[0].astype(BF16), preferred_element_type=F32) + b_ref[0]


def _modulation(cvec, w_ada, b_ada):
    n = 3 * D_MODEL
    return pl.pallas_call(
        _mod_kernel,
        grid=(DEPTH, n // MOD_TN),
        in_specs=[
            pl.BlockSpec((MOD_ROWS, D_MODEL), lambda l, j: (0, 0)),
            pl.BlockSpec((1, D_MODEL, MOD_TN), lambda l, j: (l, 0, j)),
            pl.BlockSpec((1, 1, MOD_TN), lambda l, j: (l, 0, j)),
        ],
        out_specs=pl.BlockSpec((1, MOD_ROWS, MOD_TN), lambda l, j: (l, 0, j)),
        out_shape=jax.ShapeDtypeStruct((DEPTH, MOD_ROWS, n), F32),
        compiler_params=_params("arbitrary", "arbitrary"),
        name="adaln_mod",
    )(cvec, w_ada, b_ada.reshape(DEPTH, 1, n))


M_CTX = BATCH * SEQ
M_LAT = DEC_BATCH * DEC_SEQ
M_ALL = M_CTX + M_LAT
NORM_TM = 512
INPROJ_TM = 2048
INPROJ_TN = 512


ROW_CHUNK = 32


def _modulated_norm_rows(x_ref, h_ref, g, shift, scale):
    gain = g * (1.0 + scale)

    def body(r, _):
        rows = pl.ds(pl.multiple_of(r * ROW_CHUNK, ROW_CHUNK), ROW_CHUNK)
        x = x_ref[rows, :]
        inv = lax.rsqrt(jnp.mean(x * x, axis=-1, keepdims=True) + EPS)
        h_ref[rows, :] = (x * inv * gain + shift).astype(BF16)
        return 0

    lax.fori_loop(0, x_ref.shape[0] // ROW_CHUNK, body, 0)


MOD_SHIFT, MOD_SCALE, MOD_GATE = 0, 1, 2


def _mod_spec(tm, layer, part):
    nc = M_CTX // tm
    per_batch = DEC_SEQ // tm
    row = lambda i: jnp.where(i < nc, 0, 1 + (i - nc) // per_batch)
    return pl.BlockSpec((1, 1, 1, D_MODEL), lambda i: (layer, row(i), 0, part))


def _two_stream_specs(tm, width):
    nc = M_CTX // tm
    return (pl.BlockSpec((tm, width), lambda i: (jnp.minimum(i, nc - 1), 0)),
            pl.BlockSpec((tm, width), lambda i: (jnp.maximum(i - nc, 0), 0)))


def _norm_kernel(n_ctx_tiles, xc_ref, xl_ref, g_ref, shift_ref, scale_ref, h_ref):
    def emit(x_ref):
        _modulated_norm_rows(x_ref, h_ref, g_ref[0], shift_ref[0, 0], scale_ref[0, 0])

    i = pl.program_id(0)
    pl.when(i < n_ctx_tiles)(lambda: emit(xc_ref))
    pl.when(i >= n_ctx_tiles)(lambda: emit(xl_ref))


def _norm_mod(x_ctx, x_lat, norm_g3, mod4, layer):
    xc_spec, xl_spec = _two_stream_specs(NORM_TM, D_MODEL)
    return pl.pallas_call(
        functools.partial(_norm_kernel, M_CTX // NORM_TM),
        grid=(M_ALL // NORM_TM,),
        in_specs=[
            xc_spec, xl_spec,
            pl.BlockSpec((1, 1, D_MODEL), lambda i: (layer, 0, 0)),
            _mod_spec(NORM_TM, layer, MOD_SHIFT),
            _mod_spec(NORM_TM, layer, MOD_SCALE),
        ],
        out_specs=pl.BlockSpec((NORM_TM, D_MODEL), lambda i: (i, 0)),
        out_shape=jax.ShapeDtypeStruct((M_ALL, D_MODEL), BF16),
        compiler_params=_params("arbitrary"),
        name="norm_mod",
    )(x_ctx, x_lat, norm_g3, mod4, mod4)


def _inproj_kernel(h_ref, w_ref, o_ref, w_scr):
    i = pl.program_id(1)

    @pl.when(i == 0)
    def _():
        w_scr[...] = w_ref[0].astype(BF16)

    rows = pl.ds(pl.multiple_of(i * INPROJ_TM, INPROJ_TM), INPROJ_TM)
    o_ref[...] = jnp.dot(h_ref[rows, :], w_scr[...], preferred_element_type=F32)


def _in_proj(h, w_in, layer):
    return pl.pallas_call(
        _inproj_kernel,
        grid=(D_IN // INPROJ_TN, M_ALL // INPROJ_TM),
        in_specs=[
            pl.BlockSpec((M_ALL, D_MODEL), lambda j, i: (0, 0), pipeline_mode=pl.Buffered(1)),
            pl.BlockSpec((1, D_MODEL, INPROJ_TN), lambda j, i: (layer, 0, j)),
        ],
        out_specs=pl.BlockSpec((INPROJ_TM, INPROJ_TN), lambda j, i: (i, j)),
        out_shape=jax.ShapeDtypeStruct((M_ALL, D_IN), F32),
        scratch_shapes=[pltpu.VMEM((D_MODEL, INPROJ_TN), BF16)],
        compiler_params=_params("arbitrary", "arbitrary"),
        name="in_proj",
    )(h, w_in)


ATT_SCALE = 1.0 / math.sqrt(HEAD_DIM)


def _head(ref_or_val, h):
    return ref_or_val[:, h * HEAD_DIM:(h + 1) * HEAD_DIM]


def _softmax_pv(q2, k_bf, v_bf):
    s = lax.dot_general(q2, k_bf, (((1,), (1,)), ((), ())), preferred_element_type=F32) * ATT_SCALE
    p = jnp.exp(s - jnp.max(s, axis=-1, keepdims=True))
    denom = jnp.sum(p, axis=-1, keepdims=True)
    o = jnp.dot(p.astype(BF16), v_bf, preferred_element_type=F32)
    return o * (1.0 / denom)


def _rope(x, cos2, sin_a, sin_b):
    nxt = pltpu.roll(x, HEAD_DIM - 1, axis=1)
    prv = pltpu.roll(x, 1, axis=1)
    return x * cos2 + nxt * sin_a + prv * sin_b


def _attn_ctx_kernel(q_ref, kv_ref, g_ref, qg_ref, kg_ref, *rest):
    att_ref, ko_ref, vo_ref = rest[-3:]
    rows = q_ref.shape[0]
    for h in range(N_KV_HEADS):
        kn = _rms(_head(kv_ref, h), kg_ref[...])
        vh = _head(kv_ref, N_KV_HEADS + h)
        ko_ref[0, 0, :, h, :] = kn
        vo_ref[0, 0, :, h, :] = vh
        q2 = jnp.concatenate(
            [_rms(_head(q_ref, GROUP * h + g), qg_ref[...]) for g in range(GROUP)], axis=0
        ).astype(BF16)
        o = _softmax_pv(q2, kn.astype(BF16), vh.astype(BF16))
        for g in range(GROUP):
            hq = GROUP * h + g
            gate = _silu(_head(g_ref, hq))
            att_ref[:, hq * HEAD_DIM:(hq + 1) * HEAD_DIM] = (o[g * rows:(g + 1) * rows] * gate).astype(BF16)


def _attn_ctx(proj, q_g, k_g, layer, kv_prev):
    kv_shape = jax.ShapeDtypeStruct((BATCH, DEPTH, SEQ, N_KV_HEADS, HEAD_DIM), F32)
    kv_spec = pl.BlockSpec((1, 1, SEQ, N_KV_HEADS, HEAD_DIM), lambda b: (b, layer, 0, 0, 0))
    in_specs = [
        pl.BlockSpec((SEQ, D_ATT), lambda b: (b, COL_Q // D_ATT)),
        pl.BlockSpec((SEQ, 2 * D_KV), lambda b: (b, COL_K // (2 * D_KV))),
        pl.BlockSpec((SEQ, D_ATT), lambda b: (b, COL_GA // D_ATT)),
        pl.BlockSpec((1, HEAD_DIM), lambda b: (0, 0)),
        pl.BlockSpec((1, HEAD_DIM), lambda b: (0, 0)),
    ]
    args = [proj, proj, proj, q_g, k_g]
    aliases = {}
    if kv_prev is not None:
        in_specs += [pl.BlockSpec(memory_space=pl.ANY)] * 2
        args += list(kv_prev)
        aliases = {5: 1, 6: 2}
    return pl.pallas_call(
        _attn_ctx_kernel,
        grid=(BATCH,),
        in_specs=in_specs,
        out_specs=[pl.BlockSpec((SEQ, D_ATT), lambda b: (b, 0)), kv_spec, kv_spec],
        out_shape=[jax.ShapeDtypeStruct((M_ALL, D_ATT), BF16), kv_shape, kv_shape],
        input_output_aliases=aliases,
        compiler_params=_params("arbitrary"),
        name="attn_ctx",
    )(*args)


LAT_TQ = 256
LAT_KEYS = DEC_SEQ + PAST_LEN


def _attn_lat_kernel(q_ref, kv_ref, g_ref, ck_ref, cv_ref, qg_ref, kg_ref,
                     qc_ref, qsa_ref, qsb_ref, kc_ref, ksa_ref, ksb_ref,
                     att_in_ref, att_ref, k_scr, v_scr):
    del att_in_ref
    @pl.when(pl.program_id(1) == 0)
    def _():
        for h in range(N_KV_HEADS):
            kn = _rope(_rms(_head(kv_ref, h), kg_ref[...]), kc_ref[...], ksa_ref[...], ksb_ref[...])
            k_scr[0:DEC_SEQ, h * HEAD_DIM:(h + 1) * HEAD_DIM] = kn.astype(BF16)
            v_scr[0:DEC_SEQ, h * HEAD_DIM:(h + 1) * HEAD_DIM] = _head(kv_ref, N_KV_HEADS + h).astype(BF16)
        k_scr[DEC_SEQ:LAT_KEYS, :] = ck_ref[0, 0].astype(BF16)
        v_scr[DEC_SEQ:LAT_KEYS, :] = cv_ref[0, 0].astype(BF16)

    rows = q_ref.shape[0]
    for h in range(N_KV_HEADS):
        q2 = jnp.concatenate(
            [_rope(_rms(_head(q_ref, GROUP * h + g), qg_ref[...]), qc_ref[...], qsa_ref[...], qsb_ref[...])
             for g in range(GROUP)], axis=0).astype(BF16)
        o = _softmax_pv(q2, _head(k_scr, h), _head(v_scr, h))
        for g in range(GROUP):
            hq = GROUP * h + g
            gate = _silu(_head(g_ref, hq))
            att_ref[:, hq * HEAD_DIM:(hq + 1) * HEAD_DIM] = (o[g * rows:(g + 1) * rows] * gate).astype(BF16)


def _attn_lat(proj, cache_k4, cache_v4, layer, q_g, k_g, rope_tabs, att):
    nq = DEC_SEQ // LAT_TQ
    q0 = M_CTX // LAT_TQ
    kv0 = M_CTX // DEC_SEQ
    cos2, sin_a, sin_b = rope_tabs
    qtab = pl.BlockSpec((LAT_TQ, HEAD_DIM), lambda b, i: (i, 0))
    ktab = pl.BlockSpec((DEC_SEQ, HEAD_DIM), lambda b, i: (0, 0))
    cache = pl.BlockSpec((1, 1, PAST_LEN, D_KV), lambda b, i: (b, layer, 0, 0))
    vec = pl.BlockSpec((1, HEAD_DIM), lambda b, i: (0, 0))
    return pl.pallas_call(
        _attn_lat_kernel,
        grid=(DEC_BATCH, nq),
        in_specs=[
            pl.BlockSpec((LAT_TQ, D_ATT), lambda b, i: (q0 + b * nq + i, COL_Q // D_ATT)),
            pl.BlockSpec((DEC_SEQ, 2 * D_KV), lambda b, i: (kv0 + b, COL_K // (2 * D_KV))),
            pl.BlockSpec((LAT_TQ, D_ATT), lambda b, i: (q0 + b * nq + i, COL_GA // D_ATT)),
            cache, cache, vec, vec, qtab, qtab, qtab, ktab, ktab, ktab,
            pl.BlockSpec(memory_space=pl.ANY),
        ],
        out_specs=pl.BlockSpec((LAT_TQ, D_ATT), lambda b, i: (q0 + b * nq + i, 0)),
        out_shape=jax.ShapeDtypeStruct((M_ALL, D_ATT), BF16),
        input_output_aliases={13: 0},
        scratch_shapes=[pltpu.VMEM((LAT_KEYS, D_KV), BF16), pltpu.VMEM((LAT_KEYS, D_KV), BF16)],
        compiler_params=_params("arbitrary", "arbitrary"),
        name="attn_lat",
    )(proj, proj, proj, cache_k4, cache_v4, q_g, k_g, cos2, sin_a, sin_b, cos2, sin_a, sin_b, att)


def _dft_matrix(L):
    k = np.arange(L, dtype=np.int64)[:, None]
    t = np.arange(L, dtype=np.int64)[None, :]
    ang = 2.0 * np.pi * ((k * t) % (2 * L)).astype(np.float64) / (2 * L)
    top = np.cos(ang)
    bot = np.sin(ang)
    bot[0, :] = np.where(np.arange(L) % 2 == 0, 1.0, -1.0)
    return np.concatenate([top, bot], axis=0).astype(np.float32)


def _filter_kernel(L, z_ref, w1_ref, b1_ref, w2_ref, b2_ref, fr_ref, w3f_ref, w3b_ref, dec_ref,
                   f_ref, o_ref, hdn_scr):
    hp = lax.Precision.HIGHEST

    @pl.when(pl.program_id(1) == 0)
    def _():
        fr = fr_ref[0]
        h1 = jnp.sin(fr * (jnp.dot(z_ref[...], w1_ref[0], precision=hp, preferred_element_type=F32) + b1_ref[0]))
        hdn_scr[...] = jnp.sin(fr * (jnp.dot(h1, w2_ref[0], precision=hp, preferred_element_type=F32) + b2_ref[0]))

    hdn = hdn_scr[...]
    dec = dec_ref[...]
    h_f = jnp.dot(hdn, w3f_ref[0], precision=hp, preferred_element_type=F32) * dec
    h_b = jnp.dot(hdn, w3b_ref[0], precision=hp, preferred_element_type=F32) * dec
    hs = h_f + h_b
    ha = jnp.dot(f_ref[0:L, :], hs.astype(BF16), preferred_element_type=F32)
    hb = jnp.dot(f_ref[L:2 * L, :], (h_f - h_b).astype(BF16), preferred_element_type=F32)
    row = lax.broadcasted_iota(jnp.int32, (L, 1), 0)
    first = row == 0
    nyquist = jnp.sum(jnp.where(row % 2 == 0, hs, -hs), axis=0, keepdims=True)
    wk = jnp.where(first, 1.0 / (2 * L), 2.0 / (2 * L))
    o_ref[0, 0] = ha * wk
    o_ref[0, 1] = jnp.where(first, 0.0, hb * wk)
    o_ref[0, 2] = jnp.where(first, nyquist, ha) * wk


FILT_CB = 256


def _filter_spectra(L, w1, b1, w2, b2, freq, w3, f_bf):
    tpos = np.arange(L, dtype=np.float64)
    t_norm = tpos / max(L - 1, 1)
    w = 2.0 * math.pi * tpos / L
    bands = np.linspace(1e-4, POS_BANDS - 1, POS_BANDS)
    z = np.concatenate([t_norm[:, None], np.cos(w[:, None] * bands), -np.sin(w[:, None] * bands)], axis=-1)
    z = np.pad(z, ((0, 0), (0, FEAT_PAD - POS_EMB))).astype(np.float32)
    max_decay = math.log(DECAY_TARGET) / FAST_DECAY_PCT
    min_decay = math.log(DECAY_TARGET) / SLOW_DECAY_PCT
    deltas = np.abs(np.linspace(min_decay, max_decay, D_HY))
    dec = (np.exp(-t_norm[:, None] * deltas) + DECAY_SHIFT).astype(np.float32)

    pad_h = FEAT_PAD - FILT_HID
    w1p = jnp.pad(w1, ((0, 0), (0, FEAT_PAD - POS_EMB), (0, pad_h)))
    w2p = jnp.pad(w2, ((0, 0), (0, pad_h), (0, pad_h)))
    w3p = jnp.pad(w3, ((0, 0), (0, pad_h), (0, 0)))
    vec = lambda v: jnp.pad(v, ((0, 0), (0, pad_h))).reshape(DEPTH, 1, FEAT_PAD)
    ncb = D_HY // FILT_CB
    small = lambda shape: pl.BlockSpec((1,) + shape, lambda l, c: (l, 0, 0))
    return pl.pallas_call(
        functools.partial(_filter_kernel, L),
        grid=(DEPTH, ncb),
        in_specs=[
            pl.BlockSpec((L, FEAT_PAD), lambda l, c: (0, 0)),
            small((FEAT_PAD, FEAT_PAD)), small((1, FEAT_PAD)),
            small((FEAT_PAD, FEAT_PAD)), small((1, FEAT_PAD)), small((1, FEAT_PAD)),
            pl.BlockSpec((1, FEAT_PAD, FILT_CB), lambda l, c: (l, 0, c)),
            pl.BlockSpec((1, FEAT_PAD, FILT_CB), lambda l, c: (l, 0, ncb + c)),
            pl.BlockSpec((L, FILT_CB), lambda l, c: (0, c)),
            pl.BlockSpec((2 * L, L), lambda l, c: (0, 0)),
        ],
        out_specs=pl.BlockSpec((1, 3, L, FILT_CB), lambda l, c: (l, 0, 0, c)),
        out_shape=jax.ShapeDtypeStruct((DEPTH, 3, L, D_HY), F32),
        scratch_shapes=[pltpu.VMEM((L, FEAT_PAD), F32)],
        compiler_params=_params("arbitrary", "arbitrary"),
        name=f"hyena_filter_{L}",
    )(jnp.asarray(z), w1p, vec(b1), w2p, vec(b2), vec(freq), w3p, w3p, jnp.asarray(dec), f_bf)


def _hyena_kernel(L, x0_ref, x1_ref, vv_ref, g_ref, cw0_ref, cw1_ref, cwv_ref, cb0_ref, cb1_ref, cbv_ref,
                  bias_ref, coef_ref, f_ref, ft_ref, *rest):
    o_ref = rest[-1]
    row = lax.broadcasted_iota(jnp.int32, (L, 1), 0)
    is_first = row == 0
    is_last = row == L - 1

    def sconv(x_ref, w_ref, b_ref):
        x = x_ref[...]
        prev = jnp.where(is_first, 0.0, pltpu.roll(x, 1, axis=0))
        nxt = jnp.where(is_last, 0.0, pltpu.roll(x, L - 1, axis=0))
        return w_ref[0, 0:1, :] * prev + w_ref[0, 1:2, :] * x + w_ref[0, 2:3, :] * nxt + b_ref[0]

    z = sconv(vv_ref, cwv_ref, cbv_ref) * sconv(x1_ref, cw1_ref, cb1_ref)
    ab = jnp.dot(f_ref[...], z.astype(BF16), preferred_element_type=F32)
    a, b = ab[0:L], ab[L:2 * L]
    g1, g2, g3 = coef_ref[0, 0], coef_ref[0, 1], coef_ref[0, 2]
    pq = jnp.concatenate([a * g1 - b * g2, a * g2 + b * g3], axis=0).astype(BF16)
    y = jnp.dot(ft_ref[...], pq, preferred_element_type=F32) + bias_ref[0] * z
    o_ref[...] = (sconv(x0_ref, cw0_ref, cb0_ref) * y * _silu(g_ref[...])).astype(BF16)


def _hyena(proj, L, nb, row0, cb, layer, conv_w, conv_b, hy_bias, coef, f_bf, ft_bf, hy_prev=None):
    ncb = D_HY // cb
    extra_specs = [] if hy_prev is None else [pl.BlockSpec(memory_space=pl.ANY)]
    extra_args = [] if hy_prev is None else [hy_prev]
    aliases = {} if hy_prev is None else {14: 0}
    col = lambda off: (lambda c, b: (row0 + b, off // cb + c))
    cw = lambda part: pl.BlockSpec((1, 3, cb), lambda c, b: (layer, 0, part * ncb + c))
    cbias = lambda part: pl.BlockSpec((1, 1, cb), lambda c, b: (layer, 0, part * ncb + c))
    return pl.pallas_call(
        functools.partial(_hyena_kernel, L),
        grid=(ncb, nb),
        in_specs=[
            pl.BlockSpec((L, cb), col(COL_X0)),
            pl.BlockSpec((L, cb), col(COL_X1)),
            pl.BlockSpec((L, cb), col(COL_VV)),
            pl.BlockSpec((L, cb), col(COL_GH)),
            cw(0), cw(1), cw(2), cbias(0), cbias(1), cbias(2),
            pl.BlockSpec((1, 1, cb), lambda c, b: (layer, 0, c)),
            pl.BlockSpec((1, 3, L, cb), lambda c, b: (layer, 0, 0, c)),
            pl.BlockSpec((2 * L, L), lambda c, b: (0, 0)),
            pl.BlockSpec((L, 2 * L), lambda c, b: (0, 0)),
            *extra_specs,
        ],
        out_specs=pl.BlockSpec((L, cb), lambda c, b: (row0 + b, c)),
        out_shape=jax.ShapeDtypeStruct((M_ALL, D_HY), BF16),
        input_output_aliases=aliases,
        compiler_params=_params("arbitrary", "arbitrary"),
        name=f"hyena_{L}",
    )(proj, proj, proj, proj, conv_w, conv_w, conv_w, conv_b, conv_b, conv_b, hy_bias, coef, f_bf, ft_bf,
      *extra_args)


OUTPROJ_TM = 256


def _outproj_kernel(split_in, final, n_ctx_tiles, *refs):
    it = iter(refs)
    att_ref, hy_ref, w_ref = next(it), next(it), next(it)
    x_refs = (next(it), next(it)) if split_in else (next(it),)
    gate_ref, g_ref = next(it), next(it)
    shift_ref, scale_ref = (None, None) if final else (next(it), next(it))
    out_refs = (next(it), next(it))
    w_scr, acc = next(it), next(it)
    i = pl.program_id(0)

    @pl.when(i == 0)
    def _():
        w_scr[...] = w_ref[0].astype(BF16)

    acc[...] = (jnp.dot(att_ref[...], w_scr[0:D_ATT, :], preferred_element_type=F32)
                + jnp.dot(hy_ref[...], w_scr[D_ATT:D_ATT + D_HY, :], preferred_element_type=F32))

    def epilogue(x_ref, y_ref, h_ref):
        gate = gate_ref[0, 0]
        g = g_ref[0]
        if not final:
            gain = g * (1.0 + scale_ref[0, 0])
            shift = shift_ref[0, 0]

        def body(r, _):
            rows = pl.ds(pl.multiple_of(r * ROW_CHUNK, ROW_CHUNK), ROW_CHUNK)
            y = x_ref[rows, :] + gate * acc[rows, :]
            inv = lax.rsqrt(jnp.mean(y * y, axis=-1, keepdims=True) + EPS)
            if final:
                y_ref[rows, :] = y * inv * g
            else:
                y_ref[rows, :] = y
                h_ref[rows, :] = (y * inv * gain + shift).astype(BF16)
            return 0

        lax.fori_loop(0, OUTPROJ_TM // ROW_CHUNK, body, 0)

    if final:
        pl.when(i < n_ctx_tiles)(lambda: epilogue(x_refs[0], out_refs[0], None))
        pl.when(i >= n_ctx_tiles)(lambda: epilogue(x_refs[-1], out_refs[1], None))
    elif split_in:
        pl.when(i < n_ctx_tiles)(lambda: epilogue(x_refs[0], out_refs[0], out_refs[1]))
        pl.when(i >= n_ctx_tiles)(lambda: epilogue(x_refs[1], out_refs[0], out_refs[1]))
    else:
        epilogue(x_refs[0], out_refs[0], out_refs[1])


def _out_proj(att, hy, w_out, layer, x_parts, mod4, gains3, final):
    tm = OUTPROJ_TM
    split_in = len(x_parts) == 2
    row_tile = pl.BlockSpec((tm, D_MODEL), lambda i: (i, 0))
    x_specs = list(_two_stream_specs(tm, D_MODEL)) if split_in else [row_tile]
    in_specs = [
        pl.BlockSpec((tm, D_ATT), lambda i: (i, 0)),
        pl.BlockSpec((tm, D_HY), lambda i: (i, 0)),
        pl.BlockSpec((1, D_ATT + D_HY, D_MODEL), lambda i: (layer, 0, 0), pipeline_mode=pl.Buffered(1)),
        *x_specs,
        _mod_spec(tm, layer, MOD_GATE),
    ]
    args = [att, hy, w_out, *x_parts, mod4]
    if final:
        in_specs.append(pl.BlockSpec((1, 1, D_MODEL), lambda i: (0, 0, 0)))
        args.append(gains3)
        out_specs = list(_two_stream_specs(tm, D_MODEL))
        out_shape = [jax.ShapeDtypeStruct((M_CTX, D_MODEL), F32), jax.ShapeDtypeStruct((M_LAT, D_MODEL), F32)]
    else:
        in_specs += [pl.BlockSpec((1, 1, D_MODEL), lambda i: (layer + 1, 0, 0)),
                     _mod_spec(tm, layer + 1, MOD_SHIFT), _mod_spec(tm, layer + 1, MOD_SCALE)]
        args += [gains3, mod4, mod4]
        out_specs = [row_tile, row_tile]
        out_shape = [jax.ShapeDtypeStruct((M_ALL, D_MODEL), F32), jax.ShapeDtypeStruct((M_ALL, D_MODEL), BF16)]
    return pl.pallas_call(
        functools.partial(_outproj_kernel, split_in, final, M_CTX // tm),
        grid=(M_ALL // tm,),
        in_specs=in_specs,
        out_specs=out_specs,
        out_shape=out_shape,
        scratch_shapes=[pltpu.VMEM((D_ATT + D_HY, D_MODEL), BF16), pltpu.VMEM((tm, D_MODEL), F32)],
        compiler_params=_params("arbitrary"),
        name="out_proj_final" if final else "out_proj",
    )(*args)


def _rope_tables():
    t = np.arange(DEC_SEQ)
    row = (t // GRID_W).astype(np.float64)
    col = (t % GRID_W).astype(np.float64)
    pairs = HEAD_DIM // 4
    inv_freq = ROPE_THETA ** (-np.arange(pairs, dtype=np.float64) / pairs)
    ang = np.concatenate([row[:, None] * inv_freq, col[:, None] * inv_freq], axis=-1)
    cos = np.repeat(np.cos(ang), 2, axis=-1).astype(np.float32)
    sin = np.repeat(np.sin(ang), 2, axis=-1).astype(np.float32)
    even = (np.arange(HEAD_DIM) % 2 == 0)[None, :]
    sin_a = np.where(even, -sin, 0.0).astype(np.float32)
    sin_b = np.where(even, 0.0, sin).astype(np.float32)
    return jnp.asarray(cos), jnp.asarray(sin_a), jnp.asarray(sin_b)


def kernel(x_prompt, x_sample, cache_k, cache_v, c, c_ctx, norm_g, w_ada, b_ada, w_in, q_norm_g, k_norm_g,
           conv_w, conv_b, filt_w1, filt_b1, filt_w2, filt_b2, filt_w3, filt_freq, hy_bias, w_out, final_norm_g):
    ctx = x_prompt.reshape(BATCH * SEQ, D_MODEL)
    lat = x_sample.reshape(DEC_BATCH * DEC_SEQ, D_MODEL)
    cache_k4 = cache_k.reshape(DEC_BATCH, DEPTH, PAST_LEN, D_KV)
    cache_v4 = cache_v.reshape(DEC_BATCH, DEPTH, PAST_LEN, D_KV)

    cvec = jnp.concatenate([c_ctx[None, :], c, jnp.zeros((MOD_ROWS - 1 - DEC_BATCH, D_MODEL), F32)], axis=0)
    mod = _modulation(cvec, w_ada, b_ada)

    rope_tabs = _rope_tables()
    dft = {}
    coefs = {}
    for L in (SEQ, DEC_SEQ):
        f_np = _dft_matrix(L)
        f_bf = jnp.asarray(f_np).astype(BF16)
        dft[L] = (f_bf, jnp.asarray(np.ascontiguousarray(f_np.T)).astype(BF16))
        coefs[L] = _filter_spectra(L, filt_w1, filt_b1, filt_w2, filt_b2, filt_freq, filt_w3, f_bf)

    mod4 = mod.reshape(DEPTH, MOD_ROWS, 1, 3 * D_MODEL)
    norm_g3 = norm_g.reshape(DEPTH, 1, D_MODEL)
    final_g3 = final_norm_g.reshape(1, 1, D_MODEL)
    conv_b3 = conv_b.reshape(DEPTH, 1, 3 * D_HY)
    hy_bias3 = hy_bias.reshape(DEPTH, 1, D_HY)

    x_parts = (ctx, lat)
    h = _norm_mod(ctx, lat, norm_g3, mod4, 0)
    kv_out = None
    for l in range(DEPTH):
        qg = q_norm_g[l].reshape(1, HEAD_DIM)
        kg = k_norm_g[l].reshape(1, HEAD_DIM)
        final = l == DEPTH - 1

        proj = _in_proj(h, w_in, l)

        att, new_k, new_v = _attn_ctx(proj, qg, kg, l, kv_out)
        kv_out = (new_k, new_v)
        att = _attn_lat(proj, cache_k4, cache_v4, l, qg, kg, rope_tabs, att)

        hy = _hyena(proj, SEQ, BATCH, 0, D_HY, l, conv_w, conv_b3, hy_bias3, coefs[SEQ], *dft[SEQ])
        hy = _hyena(proj, DEC_SEQ, DEC_BATCH, M_CTX // DEC_SEQ, 256, l, conv_w, conv_b3, hy_bias3,
                    coefs[DEC_SEQ], *dft[DEC_SEQ], hy_prev=hy)

        if final:
            y_ctx, y_lat = _out_proj(att, hy, w_out, l, x_parts, mod4, final_g3, True)
        else:
            y, h = _out_proj(att, hy, w_out, l, x_parts, mod4, norm_g3, False)
            x_parts = (y,)

    y_prompt = y_ctx.reshape(BATCH, SEQ, D_MODEL)
    y_sample = y_lat.reshape(DEC_BATCH, DEC_SEQ, D_MODEL)
    return (y_prompt, y_sample, kv_out[0], kv_out[1])
```

```python
import functools
import math

import numpy as np
import jax
import jax.numpy as jnp
from jax import lax
from jax.experimental import pallas as pl
from jax.experimental.pallas import tpu as pltpu

D_MODEL = 2048
BATCH = 16
SEQ = 256
DEPTH = 2
DEC_BATCH = 2
DEC_SEQ = 1024
PAST_LEN = 512
GRID_W = 64
D_ATT = 1024
D_HY = 1024
HEAD_DIM = 128
N_HEADS = 8
N_KV_HEADS = 4
GROUP = 2
D_KV = 512
ROPE_THETA = 10000.0
POS_BANDS = 16
POS_EMB = 33
FILT_HID = 64
DECAY_TARGET = 1e-2
FAST_DECAY_PCT = 0.3
SLOW_DECAY_PCT = 1.5
DECAY_SHIFT = 0.05
EPS = 1e-6
D_IN = 7168

COL_Q, COL_K, COL_V, COL_GA, COL_X0, COL_X1, COL_VV, COL_GH = 0, 1024, 1536, 2048, 3072, 4096, 5120, 6144

F32 = jnp.float32
BF16 = jnp.bfloat16

VMEM_LIMIT_BYTES = 56 * 1024 * 1024
FEAT_PAD = 128


def _params(*sem):
    return pltpu.CompilerParams(dimension_semantics=sem, vmem_limit_bytes=VMEM_LIMIT_BYTES)


def _silu(x):
    return x * (1.0 / (1.0 + jnp.exp(-x)))


def _rms(x, g):
    return x * lax.rsqrt(jnp.mean(x * x, axis=-1, keepdims=True) + EPS) * g


MOD_ROWS = 8
MOD_TN = 512


def _mod_kernel(c_ref, w_ref, b_ref, o_ref):
    s = _silu(c_ref[...]).astype(BF16)
    o_ref[0] = jnp.dot(s, w_ref[0].astype(BF16), preferred_element_type=F32) + b_ref[0]


def _modulation(cvec, w_ada, b_ada):
    n = 3 * D_MODEL
    return pl.pallas_call(
        _mod_kernel,
        grid=(DEPTH, n // MOD_TN),
        in_specs=[
            pl.BlockSpec((MOD_ROWS, D_MODEL), lambda l, j: (0, 0)),
            pl.BlockSpec((1, D_MODEL, MOD_TN), lambda l, j: (l, 0, j)),
            pl.BlockSpec((1, 1, MOD_TN), lambda l, j: (l, 0, j)),
        ],
        out_specs=pl.BlockSpec((1, MOD_ROWS, MOD_TN), lambda l, j: (l, 0, j)),
        out_shape=jax.ShapeDtypeStruct((DEPTH, MOD_ROWS, n), F32),
        compiler_params=_params("arbitrary", "arbitrary"),
        name="adaln_mod",
    )(cvec, w_ada, b_ada.reshape(DEPTH, 1, n))


M_CTX = BATCH * SEQ
M_LAT = DEC_BATCH * DEC_SEQ
M_ALL = M_CTX + M_LAT
NORM_TM = 512
INPROJ_TM = 2048
INPROJ_TN = 512


ROW_CHUNK = 32


def _modulated_norm_rows(x_ref, h_ref, g, shift, scale):
    gain = g * (1.0 + scale)

    def body(r, _):
        rows = pl.ds(pl.multiple_of(r * ROW_CHUNK, ROW_CHUNK), ROW_CHUNK)
        x = x_ref[rows, :]
        inv = lax.rsqrt(jnp.mean(x * x, axis=-1, keepdims=True) + EPS)
        h_ref[rows, :] = (x * inv * gain + shift).astype(BF16)
        return 0

    lax.fori_loop(0, x_ref.shape[0] // ROW_CHUNK, body, 0, unroll=4)


MOD_SHIFT, MOD_SCALE, MOD_GATE = 0, 1, 2


def _mod_spec(tm, layer, part, tile0=0):
    nc = M_CTX // tm
    per_batch = DEC_SEQ // tm
    row = lambda t: jnp.where(t < nc, 0, 1 + (t - nc) // per_batch)
    return pl.BlockSpec((1, 1, 1, D_MODEL), lambda i: (layer, row(tile0 + i), 0, part))


def _two_stream_specs(tm, width):
    nc = M_CTX // tm
    return (pl.BlockSpec((tm, width), lambda i: (jnp.minimum(i, nc - 1), 0)),
            pl.BlockSpec((tm, width), lambda i: (jnp.maximum(i - nc, 0), 0)))


def _norm_kernel(n_ctx_tiles, xc_ref, xl_ref, g_ref, shift_ref, scale_ref, h_ref):
    def emit(x_ref):
        _modulated_norm_rows(x_ref, h_ref, g_ref[0], shift_ref[0, 0], scale_ref[0, 0])

    i = pl.program_id(0)
    pl.when(i < n_ctx_tiles)(lambda: emit(xc_ref))
    pl.when(i >= n_ctx_tiles)(lambda: emit(xl_ref))


def _norm_mod(x_ctx, x_lat, norm_g3, mod4, layer):
    xc_spec, xl_spec = _two_stream_specs(NORM_TM, D_MODEL)
    return pl.pallas_call(
        functools.partial(_norm_kernel, M_CTX // NORM_TM),
        grid=(M_ALL // NORM_TM,),
        in_specs=[
            xc_spec, xl_spec,
            pl.BlockSpec((1, 1, D_MODEL), lambda i: (layer, 0, 0)),
            _mod_spec(NORM_TM, layer, MOD_SHIFT),
            _mod_spec(NORM_TM, layer, MOD_SCALE),
        ],
        out_specs=pl.BlockSpec((NORM_TM, D_MODEL), lambda i: (i, 0)),
        out_shape=jax.ShapeDtypeStruct((M_ALL, D_MODEL), BF16),
        compiler_params=_params("arbitrary"),
        name="norm_mod",
    )(x_ctx, x_lat, norm_g3, mod4, mod4)


def _inproj_kernel(h_ref, w_ref, o_ref, w_scr):
    i = pl.program_id(1)

    @pl.when(i == 0)
    def _():
        w_scr[...] = w_ref[0].astype(BF16)

    rows = pl.ds(pl.multiple_of(i * INPROJ_TM, INPROJ_TM), INPROJ_TM)
    o_ref[...] = jnp.dot(h_ref[rows, :], w_scr[...], preferred_element_type=F32)


def _in_proj(h, w_in, layer):
    return pl.pallas_call(
        _inproj_kernel,
        grid=(D_IN // INPROJ_TN, M_ALL // INPROJ_TM),
        in_specs=[
            pl.BlockSpec((M_ALL, D_MODEL), lambda j, i: (0, 0), pipeline_mode=pl.Buffered(1)),
            pl.BlockSpec((1, D_MODEL, INPROJ_TN), lambda j, i: (layer, 0, j)),
        ],
        out_specs=pl.BlockSpec((INPROJ_TM, INPROJ_TN), lambda j, i: (i, j)),
        out_shape=jax.ShapeDtypeStruct((M_ALL, D_IN), F32),
        scratch_shapes=[pltpu.VMEM((D_MODEL, INPROJ_TN), BF16)],
        compiler_params=_params("arbitrary", "arbitrary"),
        name="in_proj",
    )(h, w_in)


ATT_SCALE = 1.0 / math.sqrt(HEAD_DIM)


def _head(ref_or_val, h):
    return ref_or_val[:, h * HEAD_DIM:(h + 1) * HEAD_DIM]


def _softmax_pv(q2, k_bf, v_bf):
    s = lax.dot_general(q2, k_bf, (((1,), (1,)), ((), ())), preferred_element_type=F32) * ATT_SCALE
    p = jnp.exp(s - jnp.max(s, axis=-1, keepdims=True))
    denom = jnp.sum(p, axis=-1, keepdims=True)
    o = jnp.dot(p.astype(BF16), v_bf, preferred_element_type=F32)
    return o * (1.0 / denom)


def _rope(x, cos2, sin_a, sin_b):
    nxt = pltpu.roll(x, HEAD_DIM - 1, axis=1)
    prv = pltpu.roll(x, 1, axis=1)
    return x * cos2 + nxt * sin_a + prv * sin_b


def _attn_ctx_kernel(q_ref, kv_ref, g_ref, qg_ref, kg_ref, *rest):
    att_ref, ko_ref, vo_ref = rest[-3:]
    rows = q_ref.shape[0]
    for h in range(N_KV_HEADS):
        kn = _rms(_head(kv_ref, h), kg_ref[...])
        vh = _head(kv_ref, N_KV_HEADS + h)
        ko_ref[0, 0, :, h, :] = kn
        vo_ref[0, 0, :, h, :] = vh
        q2 = jnp.concatenate(
            [_rms(_head(q_ref, GROUP * h + g), qg_ref[...]) for g in range(GROUP)], axis=0
        ).astype(BF16)
        o = _softmax_pv(q2, kn.astype(BF16), vh.astype(BF16))
        for g in range(GROUP):
            hq = GROUP * h + g
            gate = _silu(_head(g_ref, hq))
            att_ref[:, hq * HEAD_DIM:(hq + 1) * HEAD_DIM] = (o[g * rows:(g + 1) * rows] * gate).astype(BF16)


def _attn_ctx(proj, q_g, k_g, layer, kv_prev):
    kv_shape = jax.ShapeDtypeStruct((BATCH, DEPTH, SEQ, N_KV_HEADS, HEAD_DIM), F32)
    kv_spec = pl.BlockSpec((1, 1, SEQ, N_KV_HEADS, HEAD_DIM), lambda b: (b, layer, 0, 0, 0))
    in_specs = [
        pl.BlockSpec((SEQ, D_ATT), lambda b: (b, COL_Q // D_ATT)),
        pl.BlockSpec((SEQ, 2 * D_KV), lambda b: (b, COL_K // (2 * D_KV))),
        pl.BlockSpec((SEQ, D_ATT), lambda b: (b, COL_GA // D_ATT)),
        pl.BlockSpec((1, HEAD_DIM), lambda b: (0, 0)),
        pl.BlockSpec((1, HEAD_DIM), lambda b: (0, 0)),
    ]
    args = [proj, proj, proj, q_g, k_g]
    aliases = {}
    if kv_prev is not None:
        in_specs += [pl.BlockSpec(memory_space=pl.ANY)] * 2
        args += list(kv_prev)
        aliases = {5: 1, 6: 2}
    return pl.pallas_call(
        _attn_ctx_kernel,
        grid=(BATCH,),
        in_specs=in_specs,
        out_specs=[pl.BlockSpec((SEQ, D_ATT), lambda b: (b, 0)), kv_spec, kv_spec],
        out_shape=[jax.ShapeDtypeStruct((M_ALL, D_ATT), BF16), kv_shape, kv_shape],
        input_output_aliases=aliases,
        compiler_params=_params("arbitrary"),
        name="attn_ctx",
    )(*args)


LAT_TQ = 256
LAT_KEYS = DEC_SEQ + PAST_LEN


def _attn_lat_kernel(q_ref, kv_ref, g_ref, ck_ref, cv_ref, qg_ref, kg_ref,
                     qc_ref, qsa_ref, qsb_ref, kc_ref, ksa_ref, ksb_ref,
                     att_in_ref, att_ref, k_scr, v_scr):
    del att_in_ref
    @pl.when(pl.program_id(1) == 0)
    def _():
        for h in range(N_KV_HEADS):
            kn = _rope(_rms(_head(kv_ref, h), kg_ref[...]), kc_ref[...], ksa_ref[...], ksb_ref[...])
            k_scr[0:DEC_SEQ, h * HEAD_DIM:(h + 1) * HEAD_DIM] = kn.astype(BF16)
            v_scr[0:DEC_SEQ, h * HEAD_DIM:(h + 1) * HEAD_DIM] = _head(kv_ref, N_KV_HEADS + h).astype(BF16)
        k_scr[DEC_SEQ:LAT_KEYS, :] = ck_ref[0, 0].astype(BF16)
        v_scr[DEC_SEQ:LAT_KEYS, :] = cv_ref[0, 0].astype(BF16)

    rows = q_ref.shape[0]
    for h in range(N_KV_HEADS):
        q2 = jnp.concatenate(
            [_rope(_rms(_head(q_ref, GROUP * h + g), qg_ref[...]), qc_ref[...], qsa_ref[...], qsb_ref[...])
             for g in range(GROUP)], axis=0).astype(BF16)
        o = _softmax_pv(q2, _head(k_scr, h), _head(v_scr, h))
        for g in range(GROUP):
            hq = GROUP * h + g
            gate = _silu(_head(g_ref, hq))
            att_ref[:, hq * HEAD_DIM:(hq + 1) * HEAD_DIM] = (o[g * rows:(g + 1) * rows] * gate).astype(BF16)


def _attn_lat(proj, cache_k4, cache_v4, layer, q_g, k_g, rope_tabs, att):
    nq = DEC_SEQ // LAT_TQ
    q0 = M_CTX // LAT_TQ
    kv0 = M_CTX // DEC_SEQ
    cos2, sin_a, sin_b = rope_tabs
    qtab = pl.BlockSpec((LAT_TQ, HEAD_DIM), lambda b, i: (i, 0))
    ktab = pl.BlockSpec((DEC_SEQ, HEAD_DIM), lambda b, i: (0, 0))
    cache = pl.BlockSpec((1, 1, PAST_LEN, D_KV), lambda b, i: (b, layer, 0, 0))
    vec = pl.BlockSpec((1, HEAD_DIM), lambda b, i: (0, 0))
    return pl.pallas_call(
        _attn_lat_kernel,
        grid=(DEC_BATCH, nq),
        in_specs=[
            pl.BlockSpec((LAT_TQ, D_ATT), lambda b, i: (q0 + b * nq + i, COL_Q // D_ATT)),
            pl.BlockSpec((DEC_SEQ, 2 * D_KV), lambda b, i: (kv0 + b, COL_K // (2 * D_KV))),
            pl.BlockSpec((LAT_TQ, D_ATT), lambda b, i: (q0 + b * nq + i, COL_GA // D_ATT)),
            cache, cache, vec, vec, qtab, qtab, qtab, ktab, ktab, ktab,
            pl.BlockSpec(memory_space=pl.ANY),
        ],
        out_specs=pl.BlockSpec((LAT_TQ, D_ATT), lambda b, i: (q0 + b * nq + i, 0)),
        out_shape=jax.ShapeDtypeStruct((M_ALL, D_ATT), BF16),
        input_output_aliases={13: 0},
        scratch_shapes=[pltpu.VMEM((LAT_KEYS, D_KV), BF16), pltpu.VMEM((LAT_KEYS, D_KV), BF16)],
        compiler_params=_params("arbitrary", "arbitrary"),
        name="attn_lat",
    )(proj, proj, proj, cache_k4, cache_v4, q_g, k_g, cos2, sin_a, sin_b, cos2, sin_a, sin_b, att)


def _dft_matrix(L):
    k = np.arange(L, dtype=np.int64)[:, None]
    t = np.arange(L, dtype=np.int64)[None, :]
    ang = 2.0 * np.pi * ((k * t) % (2 * L)).astype(np.float64) / (2 * L)
    top = np.cos(ang)
    bot = np.sin(ang)
    bot[0, :] = np.where(np.arange(L) % 2 == 0, 1.0, -1.0)
    return np.concatenate([top, bot], axis=0).astype(np.float32)


def _filter_kernel(L, z_ref, w1_ref, b1_ref, w2_ref, b2_ref, fr_ref, w3f_ref, w3b_ref, dec_ref,
                   f_ref, o_ref, hdn_scr):
    hp = lax.Precision.HIGHEST

    @pl.when(pl.program_id(1) == 0)
    def _():
        fr = fr_ref[0]
        h1 = jnp.sin(fr * (jnp.dot(z_ref[...], w1_ref[0], precision=hp, preferred_element_type=F32) + b1_ref[0]))
        hdn_scr[...] = jnp.sin(fr * (jnp.dot(h1, w2_ref[0], precision=hp, preferred_element_type=F32) + b2_ref[0]))

    hdn = hdn_scr[...]
    dec = dec_ref[...]
    h_f = jnp.dot(hdn, w3f_ref[0], precision=hp, preferred_element_type=F32) * dec
    h_b = jnp.dot(hdn, w3b_ref[0], precision=hp, preferred_element_type=F32) * dec
    hs = h_f + h_b
    ha = jnp.dot(f_ref[0:L, :], hs.astype(BF16), preferred_element_type=F32)
    hb = jnp.dot(f_ref[L:2 * L, :], (h_f - h_b).astype(BF16), preferred_element_type=F32)
    row = lax.broadcasted_iota(jnp.int32, (L, 1), 0)
    first = row == 0
    nyquist = jnp.sum(jnp.where(row % 2 == 0, hs, -hs), axis=0, keepdims=True)
    wk = jnp.where(first, 1.0 / (2 * L), 2.0 / (2 * L))
    o_ref[0, 0] = ha * wk
    o_ref[0, 1] = jnp.where(first, 0.0, hb * wk)
    o_ref[0, 2] = jnp.where(first, nyquist, ha) * wk


FILT_CB = 256


def _filter_spectra(L, w1, b1, w2, b2, freq, w3, f_bf):
    tpos = np.arange(L, dtype=np.float64)
    t_norm = tpos / max(L - 1, 1)
    w = 2.0 * math.pi * tpos / L
    bands = np.linspace(1e-4, POS_BANDS - 1, POS_BANDS)
    z = np.concatenate([t_norm[:, None], np.cos(w[:, None] * bands), -np.sin(w[:, None] * bands)], axis=-1)
    z = np.pad(z, ((0, 0), (0, FEAT_PAD - POS_EMB))).astype(np.float32)
    max_decay = math.log(DECAY_TARGET) / FAST_DECAY_PCT
    min_decay = math.log(DECAY_TARGET) / SLOW_DECAY_PCT
    deltas = np.abs(np.linspace(min_decay, max_decay, D_HY))
    dec = (np.exp(-t_norm[:, None] * deltas) + DECAY_SHIFT).astype(np.float32)

    pad_h = FEAT_PAD - FILT_HID
    w1p = jnp.pad(w1, ((0, 0), (0, FEAT_PAD - POS_EMB), (0, pad_h)))
    w2p = jnp.pad(w2, ((0, 0), (0, pad_h), (0, pad_h)))
    w3p = jnp.pad(w3, ((0, 0), (0, pad_h), (0, 0)))
    vec = lambda v: jnp.pad(v, ((0, 0), (0, pad_h))).reshape(DEPTH, 1, FEAT_PAD)
    ncb = D_HY // FILT_CB
    small = lambda shape: pl.BlockSpec((1,) + shape, lambda l, c: (l, 0, 0))
    return pl.pallas_call(
        functools.partial(_filter_kernel, L),
        grid=(DEPTH, ncb),
        in_specs=[
            pl.BlockSpec((L, FEAT_PAD), lambda l, c: (0, 0)),
            small((FEAT_PAD, FEAT_PAD)), small((1, FEAT_PAD)),
            small((FEAT_PAD, FEAT_PAD)), small((1, FEAT_PAD)), small((1, FEAT_PAD)),
            pl.BlockSpec((1, FEAT_PAD, FILT_CB), lambda l, c: (l, 0, c)),
            pl.BlockSpec((1, FEAT_PAD, FILT_CB), lambda l, c: (l, 0, ncb + c)),
            pl.BlockSpec((L, FILT_CB), lambda l, c: (0, c)),
            pl.BlockSpec((2 * L, L), lambda l, c: (0, 0)),
        ],
        out_specs=pl.BlockSpec((1, 3, L, FILT_CB), lambda l, c: (l, 0, 0, c)),
        out_shape=jax.ShapeDtypeStruct((DEPTH, 3, L, D_HY), F32),
        scratch_shapes=[pltpu.VMEM((L, FEAT_PAD), F32)],
        compiler_params=_params("arbitrary", "arbitrary"),
        name=f"hyena_filter_{L}",
    )(jnp.asarray(z), w1p, vec(b1), w2p, vec(b2), vec(freq), w3p, w3p, jnp.asarray(dec), f_bf)


def _hyena_kernel(L, x0_ref, x1_ref, vv_ref, g_ref, cw0_ref, cw1_ref, cwv_ref, cb0_ref, cb1_ref, cbv_ref,
                  bias_ref, coef_ref, f_ref, ft_ref, *rest):
    o_ref = rest[-1]
    row = lax.broadcasted_iota(jnp.int32, (L, 1), 0)
    is_first = row == 0
    is_last = row == L - 1

    def sconv(x_ref, w_ref, b_ref):
        x = x_ref[...]
        prev = jnp.where(is_first, 0.0, pltpu.roll(x, 1, axis=0))
        nxt = jnp.where(is_last, 0.0, pltpu.roll(x, L - 1, axis=0))
        return w_ref[0, 0:1, :] * prev + w_ref[0, 1:2, :] * x + w_ref[0, 2:3, :] * nxt + b_ref[0]

    z = sconv(vv_ref, cwv_ref, cbv_ref) * sconv(x1_ref, cw1_ref, cb1_ref)
    ab = jnp.dot(f_ref[...], z.astype(BF16), preferred_element_type=F32)
    a, b = ab[0:L], ab[L:2 * L]
    g1, g2, g3 = coef_ref[0, 0], coef_ref[0, 1], coef_ref[0, 2]
    pq = jnp.concatenate([a * g1 - b * g2, a * g2 + b * g3], axis=0).astype(BF16)
    y = jnp.dot(ft_ref[...], pq, preferred_element_type=F32) + bias_ref[0] * z
    o_ref[...] = (sconv(x0_ref, cw0_ref, cb0_ref) * y * _silu(g_ref[...])).astype(BF16)


def _hyena(proj, L, nb, row0, cb, layer, conv_w, conv_b, hy_bias, coef, f_bf, ft_bf, hy_prev=None):
    ncb = D_HY // cb
    extra_specs = [] if hy_prev is None else [pl.BlockSpec(memory_space=pl.ANY)]
    extra_args = [] if hy_prev is None else [hy_prev]
    aliases = {} if hy_prev is None else {14: 0}
    col = lambda off: (lambda c, b: (row0 + b, off // cb + c))
    cw = lambda part: pl.BlockSpec((1, 3, cb), lambda c, b: (layer, 0, part * ncb + c))
    cbias = lambda part: pl.BlockSpec((1, 1, cb), lambda c, b: (layer, 0, part * ncb + c))
    return pl.pallas_call(
        functools.partial(_hyena_kernel, L),
        grid=(ncb, nb),
        in_specs=[
            pl.BlockSpec((L, cb), col(COL_X0)),
            pl.BlockSpec((L, cb), col(COL_X1)),
            pl.BlockSpec((L, cb), col(COL_VV)),
            pl.BlockSpec((L, cb), col(COL_GH)),
            cw(0), cw(1), cw(2), cbias(0), cbias(1), cbias(2),
            pl.BlockSpec((1, 1, cb), lambda c, b: (layer, 0, c)),
            pl.BlockSpec((1, 3, L, cb), lambda c, b: (layer, 0, 0, c)),
            pl.BlockSpec((2 * L, L), lambda c, b: (0, 0)),
            pl.BlockSpec((L, 2 * L), lambda c, b: (0, 0)),
            *extra_specs,
        ],
        out_specs=pl.BlockSpec((L, cb), lambda c, b: (row0 + b, c)),
        out_shape=jax.ShapeDtypeStruct((M_ALL, D_HY), BF16),
        input_output_aliases=aliases,
        compiler_params=_params("arbitrary", "arbitrary"),
        name=f"hyena_{L}",
    )(proj, proj, proj, proj, conv_w, conv_w, conv_w, conv_b, conv_b, conv_b, hy_bias, coef, f_bf, ft_bf,
      *extra_args)


OUTPROJ_TM = 256


def _outproj_kernel(split_in, final, n_ctx_tiles, *refs):
    it = iter(refs)
    att_ref, hy_ref, w_ref = next(it), next(it), next(it)
    x_refs = (next(it), next(it)) if split_in else (next(it),)
    gate_ref, g_ref = next(it), next(it)
    shift_ref, scale_ref = (None, None) if final else (next(it), next(it))
    out_refs = (next(it),) if final else (next(it), next(it))
    w_scr = next(it)
    i = pl.program_id(0)

    @pl.when(i == 0)
    def _():
        w_scr[...] = w_ref[0].astype(BF16)

    out = (jnp.dot(att_ref[...], w_scr[0:D_ATT, :], preferred_element_type=F32)
           + jnp.dot(hy_ref[...], w_scr[D_ATT:D_ATT + D_HY, :], preferred_element_type=F32))
    if split_in:
        x = jnp.where(i < n_ctx_tiles, x_refs[0][...], x_refs[1][...])
    else:
        x = x_refs[0][...]
    y = x + gate_ref[0, 0] * out
    inv = lax.rsqrt(jnp.mean(y * y, axis=-1, keepdims=True) + EPS)
    if final:
        out_refs[0][...] = y * inv * g_ref[0]
    else:
        out_refs[0][...] = y
        gain = g_ref[0] * (1.0 + scale_ref[0, 0])
        out_refs[1][...] = (y * inv * gain + shift_ref[0, 0]).astype(BF16)


def _out_proj(att, hy, w_out, layer, x_parts, mod4, gains3, final, tile0=0, n_tiles=M_ALL // OUTPROJ_TM):
    tm = OUTPROJ_TM
    split_in = len(x_parts) == 2
    assert not split_in or (tile0 == 0 and n_tiles == M_ALL // tm)
    row_tile = lambda width: pl.BlockSpec((tm, width), lambda i: (tile0 + i, 0))
    out_tile = pl.BlockSpec((tm, D_MODEL), lambda i: (i, 0))
    x_specs = list(_two_stream_specs(tm, D_MODEL)) if split_in else [row_tile(D_MODEL)]
    in_specs = [
        row_tile(D_ATT),
        row_tile(D_HY),
        pl.BlockSpec((1, D_ATT + D_HY, D_MODEL), lambda i: (layer, 0, 0), pipeline_mode=pl.Buffered(1)),
        *x_specs,
        _mod_spec(tm, layer, MOD_GATE, tile0),
    ]
    args = [att, hy, w_out, *x_parts, mod4]
    if final:
        in_specs.append(pl.BlockSpec((1, 1, D_MODEL), lambda i: (0, 0, 0)))
        args.append(gains3)
        out_specs = [out_tile]
        out_shape = [jax.ShapeDtypeStruct((n_tiles * tm, D_MODEL), F32)]
    else:
        in_specs += [pl.BlockSpec((1, 1, D_MODEL), lambda i: (layer + 1, 0, 0)),
                     _mod_spec(tm, layer + 1, MOD_SHIFT, tile0), _mod_spec(tm, layer + 1, MOD_SCALE, tile0)]
        args += [gains3, mod4, mod4]
        out_specs = [out_tile, out_tile]
        out_shape = [jax.ShapeDtypeStruct((n_tiles * tm, D_MODEL), F32),
                     jax.ShapeDtypeStruct((n_tiles * tm, D_MODEL), BF16)]
    return pl.pallas_call(
        functools.partial(_outproj_kernel, split_in, final, M_CTX // tm),
        grid=(n_tiles,),
        in_specs=in_specs,
        out_specs=out_specs,
        out_shape=out_shape,
        scratch_shapes=[pltpu.VMEM((D_ATT + D_HY, D_MODEL), BF16)],
        compiler_params=_params("arbitrary"),
        name="out_proj_final" if final else "out_proj",
    )(*args)


def _rope_tables():
    t = np.arange(DEC_SEQ)
    row = (t // GRID_W).astype(np.float64)
    col = (t % GRID_W).astype(np.float64)
    pairs = HEAD_DIM // 4
    inv_freq = ROPE_THETA ** (-np.arange(pairs, dtype=np.float64) / pairs)
    ang = np.concatenate([row[:, None] * inv_freq, col[:, None] * inv_freq], axis=-1)
    cos = np.repeat(np.cos(ang), 2, axis=-1).astype(np.float32)
    sin = np.repeat(np.sin(ang), 2, axis=-1).astype(np.float32)
    even = (np.arange(HEAD_DIM) % 2 == 0)[None, :]
    sin_a = np.where(even, -sin, 0.0).astype(np.float32)
    sin_b = np.where(even, 0.0, sin).astype(np.float32)
    return jnp.asarray(cos), jnp.asarray(sin_a), jnp.asarray(sin_b)


def kernel(x_prompt, x_sample, cache_k, cache_v, c, c_ctx, norm_g, w_ada, b_ada, w_in, q_norm_g, k_norm_g,
           conv_w, conv_b, filt_w1, filt_b1, filt_w2, filt_b2, filt_w3, filt_freq, hy_bias, w_out, final_norm_g):
    ctx = x_prompt.reshape(BATCH * SEQ, D_MODEL)
    lat = x_sample.reshape(DEC_BATCH * DEC_SEQ, D_MODEL)
    cache_k4 = cache_k.reshape(DEC_BATCH, DEPTH, PAST_LEN, D_KV)
    cache_v4 = cache_v.reshape(DEC_BATCH, DEPTH, PAST_LEN, D_KV)

    cvec = jnp.concatenate([c_ctx[None, :], c, jnp.zeros((MOD_ROWS - 1 - DEC_BATCH, D_MODEL), F32)], axis=0)
    mod = _modulation(cvec, w_ada, b_ada)

    rope_tabs = _rope_tables()
    dft = {}
    coefs = {}
    for L in (SEQ, DEC_SEQ):
        f_np = _dft_matrix(L)
        f_bf = jnp.asarray(f_np).astype(BF16)
        dft[L] = (f_bf, jnp.asarray(np.ascontiguousarray(f_np.T)).astype(BF16))
        coefs[L] = _filter_spectra(L, filt_w1, filt_b1, filt_w2, filt_b2, filt_freq, filt_w3, f_bf)

    mod4 = mod.reshape(DEPTH, MOD_ROWS, 1, 3 * D_MODEL)
    norm_g3 = norm_g.reshape(DEPTH, 1, D_MODEL)
    final_g3 = final_norm_g.reshape(1, 1, D_MODEL)
    conv_b3 = conv_b.reshape(DEPTH, 1, 3 * D_HY)
    hy_bias3 = hy_bias.reshape(DEPTH, 1, D_HY)

    x_parts = (ctx, lat)
    h = _norm_mod(ctx, lat, norm_g3, mod4, 0)
    kv_out = None
    for l in range(DEPTH):
        qg = q_norm_g[l].reshape(1, HEAD_DIM)
        kg = k_norm_g[l].reshape(1, HEAD_DIM)
        final = l == DEPTH - 1

        proj = _in_proj(h, w_in, l)

        att, new_k, new_v = _attn_ctx(proj, qg, kg, l, kv_out)
        kv_out = (new_k, new_v)
        att = _attn_lat(proj, cache_k4, cache_v4, l, qg, kg, rope_tabs, att)

        hy = _hyena(proj, SEQ, BATCH, 0, D_HY, l, conv_w, conv_b3, hy_bias3, coefs[SEQ], *dft[SEQ])
        hy = _hyena(proj, DEC_SEQ, DEC_BATCH, M_CTX // DEC_SEQ, 256, l, conv_w, conv_b3, hy_bias3,
                    coefs[DEC_SEQ], *dft[DEC_SEQ], hy_prev=hy)

        if final:
            assert len(x_parts) == 1
            nc = M_CTX // OUTPROJ_TM
            (y_ctx,) = _out_proj(att, hy, w_out, l, x_parts, mod4, final_g3, True, 0, nc)
            (y_lat,) = _out_proj(att, hy, w_out, l, x_parts, mod4, final_g3, True, nc, M_LAT // OUTPROJ_TM)
        else:
            y, h = _out_proj(att, hy, w_out, l, x_parts, mod4, norm_g3, False)
            x_parts = (y,)

    y_prompt = y_ctx.reshape(BATCH, SEQ, D_MODEL)
    y_sample = y_lat.reshape(DEC_BATCH, DEC_SEQ, D_MODEL)
    return (y_prompt, y_sample, kv_out[0], kv_out[1])
```

```python
import functools
import math

import numpy as np
import jax
import jax.numpy as jnp
from jax import lax
from jax.experimental import pallas as pl
from jax.experimental.pallas import tpu as pltpu

D_MODEL = 2048
BATCH = 16
SEQ = 256
DEPTH = 2
DEC_BATCH = 2
DEC_SEQ = 1024
PAST_LEN = 512
GRID_W = 64
D_ATT = 1024
D_HY = 1024
HEAD_DIM = 128
N_HEADS = 8
N_KV_HEADS = 4
GROUP = 2
D_KV = 512
ROPE_THETA = 10000.0
POS_BANDS = 16
POS_EMB = 33
FILT_HID = 64
DECAY_TARGET = 1e-2
FAST_DECAY_PCT = 0.3
SLOW_DECAY_PCT = 1.5
DECAY_SHIFT = 0.05
EPS = 1e-6
D_IN = 7168

COL_Q, COL_K, COL_V, COL_GA, COL_X0, COL_X1, COL_VV, COL_GH = 0, 1024, 1536, 2048, 3072, 4096, 5120, 6144

F32 = jnp.float32
BF16 = jnp.bfloat16

VMEM_LIMIT_BYTES = 56 * 1024 * 1024
FEAT_PAD = 128


def _params(*sem):
    return pltpu.CompilerParams(dimension_semantics=sem, vmem_limit_bytes=VMEM_LIMIT_BYTES)


def _silu(x):
    return x * (1.0 / (1.0 + jnp.exp(-x)))


def _rms(x, g):
    return x * lax.rsqrt(jnp.mean(x * x, axis=-1, keepdims=True) + EPS) * g


MOD_ROWS = 8
MOD_TN = 512


def _mod_kernel(c_ref, w_ref, b_ref, o_ref):
    s = _silu(c_ref[...]).astype(BF16)
    o_ref[0] = jnp.dot(s, w_ref[0].astype(BF16), preferred_element_type=F32) + b_ref[0]


def _modulation(cvec, w_ada, b_ada):
    n = 3 * D_MODEL
    return pl.pallas_call(
        _mod_kernel,
        grid=(DEPTH, n // MOD_TN),
        in_specs=[
            pl.BlockSpec((MOD_ROWS, D_MODEL), lambda l, j: (0, 0)),
            pl.BlockSpec((1, D_MODEL, MOD_TN), lambda l, j: (l, 0, j)),
            pl.BlockSpec((1, 1, MOD_TN), lambda l, j: (l, 0, j)),
        ],
        out_specs=pl.BlockSpec((1, MOD_ROWS, MOD_TN), lambda l, j: (l, 0, j)),
        out_shape=jax.ShapeDtypeStruct((DEPTH, MOD_ROWS, n), F32),
        compiler_params=_params("arbitrary", "arbitrary"),
        name="adaln_mod",
    )(cvec, w_ada, b_ada.reshape(DEPTH, 1, n))


M_CTX = BATCH * SEQ
M_LAT = DEC_BATCH * DEC_SEQ
M_ALL = M_CTX + M_LAT
NORM_TM = 512
INPROJ_TM = 2048
INPROJ_TN = 512


ROW_CHUNK = 32


def _modulated_norm_rows(x_ref, h_ref, g, shift, scale):
    gain = g * (1.0 + scale)

    def body(r, _):
        rows = pl.ds(pl.multiple_of(r * ROW_CHUNK, ROW_CHUNK), ROW_CHUNK)
        x = x_ref[rows, :]
        inv = lax.rsqrt(jnp.mean(x * x, axis=-1, keepdims=True) + EPS)
        h_ref[rows, :] = (x * inv * gain + shift).astype(BF16)
        return 0

    lax.fori_loop(0, x_ref.shape[0] // ROW_CHUNK, body, 0, unroll=4)


MOD_SHIFT, MOD_SCALE, MOD_GATE = 0, 1, 2


def _mod_spec(tm, layer, part, tile0=0):
    nc = M_CTX // tm
    per_batch = DEC_SEQ // tm
    row = lambda t: jnp.where(t < nc, 0, 1 + (t - nc) // per_batch)
    return pl.BlockSpec((1, 1, 1, D_MODEL), lambda i: (layer, row(tile0 + i), 0, part))


def _two_stream_specs(tm, width):
    nc = M_CTX // tm
    return (pl.BlockSpec((tm, width), lambda i: (jnp.minimum(i, nc - 1), 0)),
            pl.BlockSpec((tm, width), lambda i: (jnp.maximum(i - nc, 0), 0)))


def _norm_kernel(n_ctx_tiles, xc_ref, xl_ref, g_ref, shift_ref, scale_ref, h_ref):
    def emit(x_ref):
        _modulated_norm_rows(x_ref, h_ref, g_ref[0], shift_ref[0, 0], scale_ref[0, 0])

    i = pl.program_id(0)
    pl.when(i < n_ctx_tiles)(lambda: emit(xc_ref))
    pl.when(i >= n_ctx_tiles)(lambda: emit(xl_ref))


def _norm_mod(x_ctx, x_lat, norm_g3, mod4, layer):
    xc_spec, xl_spec = _two_stream_specs(NORM_TM, D_MODEL)
    return pl.pallas_call(
        functools.partial(_norm_kernel, M_CTX // NORM_TM),
        grid=(M_ALL // NORM_TM,),
        in_specs=[
            xc_spec, xl_spec,
            pl.BlockSpec((1, 1, D_MODEL), lambda i: (layer, 0, 0)),
            _mod_spec(NORM_TM, layer, MOD_SHIFT),
            _mod_spec(NORM_TM, layer, MOD_SCALE),
        ],
        out_specs=pl.BlockSpec((NORM_TM, D_MODEL), lambda i: (i, 0)),
        out_shape=jax.ShapeDtypeStruct((M_ALL, D_MODEL), BF16),
        compiler_params=_params("arbitrary"),
        name="norm_mod",
    )(x_ctx, x_lat, norm_g3, mod4, mod4)


def _inproj_kernel(h_ref, w_ref, o_ref, w_scr):
    i = pl.program_id(1)

    @pl.when(i == 0)
    def _():
        w_scr[...] = w_ref[0].astype(BF16)

    rows = pl.ds(pl.multiple_of(i * INPROJ_TM, INPROJ_TM), INPROJ_TM)
    o_ref[...] = jnp.dot(h_ref[rows, :], w_scr[...], preferred_element_type=F32)


def _in_proj(h, w_in, layer):
    return pl.pallas_call(
        _inproj_kernel,
        grid=(D_IN // INPROJ_TN, M_ALL // INPROJ_TM),
        in_specs=[
            pl.BlockSpec((M_ALL, D_MODEL), lambda j, i: (0, 0), pipeline_mode=pl.Buffered(1)),
            pl.BlockSpec((1, D_MODEL, INPROJ_TN), lambda j, i: (layer, 0, j)),
        ],
        out_specs=pl.BlockSpec((INPROJ_TM, INPROJ_TN), lambda j, i: (i, j)),
        out_shape=jax.ShapeDtypeStruct((M_ALL, D_IN), F32),
        scratch_shapes=[pltpu.VMEM((D_MODEL, INPROJ_TN), BF16)],
        compiler_params=_params("arbitrary", "arbitrary"),
        name="in_proj",
    )(h, w_in)


ATT_SCALE = 1.0 / math.sqrt(HEAD_DIM)


def _head(ref_or_val, h):
    return ref_or_val[:, h * HEAD_DIM:(h + 1) * HEAD_DIM]


Q_PRESCALE = ATT_SCALE * math.log2(math.e)


def _softmax_pv(q2, k_bf, v_ext):
    s = lax.dot_general(q2, k_bf, (((1,), (1,)), ((), ())), preferred_element_type=F32)
    p = jnp.exp2(s - jnp.max(s, axis=-1, keepdims=True)).astype(BF16)
    o = jnp.dot(p, v_ext, preferred_element_type=F32)
    return o[:, 0:HEAD_DIM] / o[:, HEAD_DIM:2 * HEAD_DIM]


def _rope(x, cos2, sin_a, sin_b):
    nxt = pltpu.roll(x, HEAD_DIM - 1, axis=1)
    prv = pltpu.roll(x, 1, axis=1)
    return x * cos2 + nxt * sin_a + prv * sin_b


CTX_TQ = SEQ
LAT_TQ = 128


def _attn_kernel(rope, n_cache, emit_kv, n_aliased, *refs):
    it = iter(refs)
    q_ref, kv_ref, g_ref, qg_ref, kg_ref = (next(it) for _ in range(5))
    ck_ref, cv_ref = (next(it), next(it)) if n_cache else (None, None)
    q_tabs = tuple(next(it) for _ in range(3)) if rope else None
    k_tabs = tuple(next(it) for _ in range(3)) if rope else None
    for _ in range(n_aliased):
        next(it)
    att_ref = next(it)
    ko_ref, vo_ref = (next(it), next(it)) if emit_kv else (None, None)
    k_scr, v_scr = next(it), next(it)
    n_new = kv_ref.shape[0]

    def prepare_keys_values():
        for h in range(N_KV_HEADS):
            kn = _rms(_head(kv_ref, h), kg_ref[...])
            vh = _head(kv_ref, N_KV_HEADS + h)
            if emit_kv:
                ko_ref[0, 0, :, h, :] = kn
                vo_ref[0, 0, :, h, :] = vh
            if rope:
                kn = _rope(kn, *(t[...] for t in k_tabs))
            k_scr[0:n_new, h * HEAD_DIM:(h + 1) * HEAD_DIM] = kn.astype(BF16)
            v0 = 2 * h * HEAD_DIM
            v_scr[0:n_new, v0:v0 + HEAD_DIM] = vh.astype(BF16)
            if n_cache:
                v_scr[n_new:n_new + n_cache, v0:v0 + HEAD_DIM] = _head(cv_ref[0, 0], h).astype(BF16)
            v_scr[:, v0 + HEAD_DIM:v0 + 2 * HEAD_DIM] = jnp.ones((n_new + n_cache, HEAD_DIM), BF16)
        if n_cache:
            k_scr[n_new:n_new + n_cache, :] = ck_ref[0, 0].astype(BF16)

    if q_ref.shape[0] == n_new:
        prepare_keys_values()
    else:
        pl.when(pl.program_id(1) == 0)(prepare_keys_values)

    def query(hq):
        x = _rms(_head(q_ref, hq), qg_ref[...])
        if rope:
            x = _rope(x, *(t[...] for t in q_tabs))
        return x * Q_PRESCALE

    rows = q_ref.shape[0]
    for h in range(N_KV_HEADS):
        q2 = jnp.concatenate([query(GROUP * h + g) for g in range(GROUP)], axis=0).astype(BF16)
        o = _softmax_pv(q2, _head(k_scr, h), v_scr[:, 2 * h * HEAD_DIM:2 * (h + 1) * HEAD_DIM])
        for g in range(GROUP):
            hq = GROUP * h + g
            gate = _silu(_head(g_ref, hq))
            att_ref[:, hq * HEAD_DIM:(hq + 1) * HEAD_DIM] = (o[g * rows:(g + 1) * rows] * gate).astype(BF16)


def _attention(proj, q_g, k_g, layer, seq, tq, n_batch, row0, *, cache=None, rope_tabs=None, att_prev=None,
               kv_prev=None, emit_kv=False):
    nq = seq // tq
    q0, kv0 = row0 // tq, row0 // seq
    n_cache = 0 if cache is None else cache[0].shape[2]
    vec = pl.BlockSpec((1, HEAD_DIM), lambda b, i: (0, 0))
    in_specs = [
        pl.BlockSpec((tq, D_ATT), lambda b, i: (q0 + b * nq + i, COL_Q // D_ATT)),
        pl.BlockSpec((seq, 2 * D_KV), lambda b, i: (kv0 + b, COL_K // (2 * D_KV))),
        pl.BlockSpec((tq, D_ATT), lambda b, i: (q0 + b * nq + i, COL_GA // D_ATT)),
        vec, vec,
    ]
    args = [proj, proj, proj, q_g, k_g]
    if cache is not None:
        in_specs += [pl.BlockSpec((1, 1, n_cache, D_KV), lambda b, i: (b, layer, 0, 0))] * 2
        args += list(cache)
    if rope_tabs is not None:
        in_specs += [pl.BlockSpec((tq, HEAD_DIM), lambda b, i: (i, 0))] * 3
        in_specs += [pl.BlockSpec((seq, HEAD_DIM), lambda b, i: (0, 0))] * 3
        args += list(rope_tabs) * 2
    aliased = ([] if att_prev is None else [att_prev]) + ([] if kv_prev is None else list(kv_prev))
    out_first = 0 if att_prev is not None else 1
    aliases = {len(args) + n: out_first + n for n in range(len(aliased))}
    in_specs += [pl.BlockSpec(memory_space=pl.ANY)] * len(aliased)
    args += aliased
    out_specs = [pl.BlockSpec((tq, D_ATT), lambda b, i: (q0 + b * nq + i, 0))]
    out_shape = [jax.ShapeDtypeStruct((M_ALL, D_ATT), BF16)]
    if emit_kv:
        out_specs += [pl.BlockSpec((1, 1, seq, N_KV_HEADS, HEAD_DIM), lambda b, i: (b, layer, 0, 0, 0))] * 2
        out_shape += [jax.ShapeDtypeStruct((n_batch, DEPTH, seq, N_KV_HEADS, HEAD_DIM), F32)] * 2
    return pl.pallas_call(
        functools.partial(_attn_kernel, rope_tabs is not None, n_cache, emit_kv, len(aliased)),
        grid=(n_batch, nq),
        in_specs=in_specs,
        out_specs=out_specs,
        out_shape=out_shape,
        input_output_aliases=aliases,
        scratch_shapes=[pltpu.VMEM((seq + n_cache, D_KV), BF16), pltpu.VMEM((seq + n_cache, 2 * D_KV), BF16)],
        compiler_params=_params("arbitrary", "arbitrary"),
        name=f"attention_{seq}",
    )(*args)


def _dft_matrix(L):
    k = np.arange(L, dtype=np.int64)[:, None]
    t = np.arange(L, dtype=np.int64)[None, :]
    ang = 2.0 * np.pi * ((k * t) % (2 * L)).astype(np.float64) / (2 * L)
    top = np.cos(ang)
    bot = np.sin(ang)
    bot[0, :] = np.where(np.arange(L) % 2 == 0, 1.0, -1.0)
    return np.concatenate([top, bot], axis=0).astype(np.float32)


def _filter_kernel(L, z_ref, w1_ref, b1_ref, w2_ref, b2_ref, fr_ref, w3f_ref, w3b_ref, dec_ref,
                   f_ref, o_ref, hdn_scr):
    hp = lax.Precision.HIGHEST

    @pl.when(pl.program_id(1) == 0)
    def _():
        fr = fr_ref[0]
        h1 = jnp.sin(fr * (jnp.dot(z_ref[...], w1_ref[0], precision=hp, preferred_element_type=F32) + b1_ref[0]))
        hdn_scr[...] = jnp.sin(fr * (jnp.dot(h1, w2_ref[0], precision=hp, preferred_element_type=F32) + b2_ref[0]))

    hdn = hdn_scr[...]
    dec = dec_ref[...]
    h_f = jnp.dot(hdn, w3f_ref[0], precision=hp, preferred_element_type=F32) * dec
    h_b = jnp.dot(hdn, w3b_ref[0], precision=hp, preferred_element_type=F32) * dec
    hs = h_f + h_b
    ha = jnp.dot(f_ref[0:L, :], hs.astype(BF16), preferred_element_type=F32)
    hb = jnp.dot(f_ref[L:2 * L, :], (h_f - h_b).astype(BF16), preferred_element_type=F32)
    row = lax.broadcasted_iota(jnp.int32, (L, 1), 0)
    first = row == 0
    nyquist = jnp.sum(jnp.where(row % 2 == 0, hs, -hs), axis=0, keepdims=True)
    wk = jnp.where(first, 1.0 / (2 * L), 2.0 / (2 * L))
    o_ref[0, 0] = ha * wk
    o_ref[0, 1] = jnp.where(first, 0.0, hb * wk)
    o_ref[0, 2] = jnp.where(first, nyquist, ha) * wk


FILT_CB = 256


def _filter_spectra(L, w1, b1, w2, b2, freq, w3, f_bf):
    tpos = np.arange(L, dtype=np.float64)
    t_norm = tpos / max(L - 1, 1)
    w = 2.0 * math.pi * tpos / L
    bands = np.linspace(1e-4, POS_BANDS - 1, POS_BANDS)
    z = np.concatenate([t_norm[:, None], np.cos(w[:, None] * bands), -np.sin(w[:, None] * bands)], axis=-1)
    z = np.pad(z, ((0, 0), (0, FEAT_PAD - POS_EMB))).astype(np.float32)
    max_decay = math.log(DECAY_TARGET) / FAST_DECAY_PCT
    min_decay = math.log(DECAY_TARGET) / SLOW_DECAY_PCT
    deltas = np.abs(np.linspace(min_decay, max_decay, D_HY))
    dec = (np.exp(-t_norm[:, None] * deltas) + DECAY_SHIFT).astype(np.float32)

    pad_h = FEAT_PAD - FILT_HID
    w1p = jnp.pad(w1, ((0, 0), (0, FEAT_PAD - POS_EMB), (0, pad_h)))
    w2p = jnp.pad(w2, ((0, 0), (0, pad_h), (0, pad_h)))
    w3p = jnp.pad(w3, ((0, 0), (0, pad_h), (0, 0)))
    vec = lambda v: jnp.pad(v, ((0, 0), (0, pad_h))).reshape(DEPTH, 1, FEAT_PAD)
    ncb = D_HY // FILT_CB
    small = lambda shape: pl.BlockSpec((1,) + shape, lambda l, c: (l, 0, 0))
    return pl.pallas_call(
        functools.partial(_filter_kernel, L),
        grid=(DEPTH, ncb),
        in_specs=[
            pl.BlockSpec((L, FEAT_PAD), lambda l, c: (0, 0)),
            small((FEAT_PAD, FEAT_PAD)), small((1, FEAT_PAD)),
            small((FEAT_PAD, FEAT_PAD)), small((1, FEAT_PAD)), small((1, FEAT_PAD)),
            pl.BlockSpec((1, FEAT_PAD, FILT_CB), lambda l, c: (l, 0, c)),
            pl.BlockSpec((1, FEAT_PAD, FILT_CB), lambda l, c: (l, 0, ncb + c)),
            pl.BlockSpec((L, FILT_CB), lambda l, c: (0, c)),
            pl.BlockSpec((2 * L, L), lambda l, c: (0, 0)),
        ],
        out_specs=pl.BlockSpec((1, 3, L, FILT_CB), lambda l, c: (l, 0, 0, c)),
        out_shape=jax.ShapeDtypeStruct((DEPTH, 3, L, D_HY), F32),
        scratch_shapes=[pltpu.VMEM((L, FEAT_PAD), F32)],
        compiler_params=_params("arbitrary", "arbitrary"),
        name=f"hyena_filter_{L}",
    )(jnp.asarray(z), w1p, vec(b1), w2p, vec(b2), vec(freq), w3p, w3p, jnp.asarray(dec), f_bf)


def _hyena_kernel(L, x0_ref, x1_ref, vv_ref, g_ref, cw0_ref, cw1_ref, cwv_ref, cb0_ref, cb1_ref, cbv_ref,
                  bias_ref, coef_ref, f_ref, ft_ref, *rest):
    o_ref = rest[-1]
    row = lax.broadcasted_iota(jnp.int32, (L, 1), 0)
    is_first = row == 0
    is_last = row == L - 1

    def sconv(x_ref, w_ref, b_ref):
        x = x_ref[...]
        prev = jnp.where(is_first, 0.0, pltpu.roll(x, 1, axis=0))
        nxt = jnp.where(is_last, 0.0, pltpu.roll(x, L - 1, axis=0))
        return w_ref[0, 0:1, :] * prev + w_ref[0, 1:2, :] * x + w_ref[0, 2:3, :] * nxt + b_ref[0]

    z = sconv(vv_ref, cwv_ref, cbv_ref) * sconv(x1_ref, cw1_ref, cb1_ref)
    ab = jnp.dot(f_ref[...], z.astype(BF16), preferred_element_type=F32)
    a, b = ab[0:L], ab[L:2 * L]
    g1, g2, g3 = coef_ref[0, 0], coef_ref[0, 1], coef_ref[0, 2]
    pq = jnp.concatenate([a * g1 - b * g2, a * g2 + b * g3], axis=0).astype(BF16)
    y = jnp.dot(ft_ref[...], pq, preferred_element_type=F32) + bias_ref[0] * z
    o_ref[...] = (sconv(x0_ref, cw0_ref, cb0_ref) * y * _silu(g_ref[...])).astype(BF16)


def _hyena(proj, L, nb, row0, cb, layer, conv_w, conv_b, hy_bias, coef, f_bf, ft_bf, hy_prev=None):
    ncb = D_HY // cb
    extra_specs = [] if hy_prev is None else [pl.BlockSpec(memory_space=pl.ANY)]
    extra_args = [] if hy_prev is None else [hy_prev]
    aliases = {} if hy_prev is None else {14: 0}
    col = lambda off: (lambda c, b: (row0 + b, off // cb + c))
    cw = lambda part: pl.BlockSpec((1, 3, cb), lambda c, b: (layer, 0, part * ncb + c))
    cbias = lambda part: pl.BlockSpec((1, 1, cb), lambda c, b: (layer, 0, part * ncb + c))
    return pl.pallas_call(
        functools.partial(_hyena_kernel, L),
        grid=(ncb, nb),
        in_specs=[
            pl.BlockSpec((L, cb), col(COL_X0)),
            pl.BlockSpec((L, cb), col(COL_X1)),
            pl.BlockSpec((L, cb), col(COL_VV)),
            pl.BlockSpec((L, cb), col(COL_GH)),
            cw(0), cw(1), cw(2), cbias(0), cbias(1), cbias(2),
            pl.BlockSpec((1, 1, cb), lambda c, b: (layer, 0, c)),
            pl.BlockSpec((1, 3, L, cb), lambda c, b: (layer, 0, 0, c)),
            pl.BlockSpec((2 * L, L), lambda c, b: (0, 0)),
            pl.BlockSpec((L, 2 * L), lambda c, b: (0, 0)),
            *extra_specs,
        ],
        out_specs=pl.BlockSpec((L, cb), lambda c, b: (row0 + b, c)),
        out_shape=jax.ShapeDtypeStruct((M_ALL, D_HY), BF16),
        input_output_aliases=aliases,
        compiler_params=_params("arbitrary", "arbitrary"),
        name=f"hyena_{L}",
    )(proj, proj, proj, proj, conv_w, conv_w, conv_w, conv_b, conv_b, conv_b, hy_bias, coef, f_bf, ft_bf,
      *extra_args)


OUTPROJ_TM = 256


def _outproj_kernel(split_in, final, n_ctx_tiles, *refs):
    it = iter(refs)
    att_ref, hy_ref, w_ref = next(it), next(it), next(it)
    x_refs = (next(it), next(it)) if split_in else (next(it),)
    gate_ref, g_ref = next(it), next(it)
    shift_ref, scale_ref = (None, None) if final else (next(it), next(it))
    out_refs = (next(it),) if final else (next(it), next(it))
    w_scr = next(it)
    i = pl.program_id(0)

    @pl.when(i == 0)
    def _():
        w_scr[...] = w_ref[0].astype(BF16)

    out = (jnp.dot(att_ref[...], w_scr[0:D_ATT, :], preferred_element_type=F32)
           + jnp.dot(hy_ref[...], w_scr[D_ATT:D_ATT + D_HY, :], preferred_element_type=F32))
    if split_in:
        x = jnp.where(i < n_ctx_tiles, x_refs[0][...], x_refs[1][...])
    else:
        x = x_refs[0][...]
    y = x + gate_ref[0, 0] * out
    inv = lax.rsqrt(jnp.mean(y * y, axis=-1, keepdims=True) + EPS)
    if final:
        out_refs[0][...] = y * inv * g_ref[0]
    else:
        out_refs[0][...] = y
        gain = g_ref[0] * (1.0 + scale_ref[0, 0])
        out_refs[1][...] = (y * inv * gain + shift_ref[0, 0]).astype(BF16)


def _out_proj(att, hy, w_out, layer, x_parts, mod4, gains3, final, tile0=0, n_tiles=M_ALL // OUTPROJ_TM):
    tm = OUTPROJ_TM
    split_in = len(x_parts) == 2
    assert not split_in or (tile0 == 0 and n_tiles == M_ALL // tm)
    row_tile = lambda width: pl.BlockSpec((tm, width), lambda i: (tile0 + i, 0))
    out_tile = pl.BlockSpec((tm, D_MODEL), lambda i: (i, 0))
    x_specs = list(_two_stream_specs(tm, D_MODEL)) if split_in else [row_tile(D_MODEL)]
    in_specs = [
        row_tile(D_ATT),
        row_tile(D_HY),
        pl.BlockSpec((1, D_ATT + D_HY, D_MODEL), lambda i: (layer, 0, 0), pipeline_mode=pl.Buffered(1)),
        *x_specs,
        _mod_spec(tm, layer, MOD_GATE, tile0),
    ]
    args = [att, hy, w_out, *x_parts, mod4]
    if final:
        in_specs.append(pl.BlockSpec((1, 1, D_MODEL), lambda i: (0, 0, 0)))
        args.append(gains3)
        out_specs = [out_tile]
        out_shape = [jax.ShapeDtypeStruct((n_tiles * tm, D_MODEL), F32)]
    else:
        in_specs += [pl.BlockSpec((1, 1, D_MODEL), lambda i: (layer + 1, 0, 0)),
                     _mod_spec(tm, layer + 1, MOD_SHIFT, tile0), _mod_spec(tm, layer + 1, MOD_SCALE, tile0)]
        args += [gains3, mod4, mod4]
        out_specs = [out_tile, out_tile]
        out_shape = [jax.ShapeDtypeStruct((n_tiles * tm, D_MODEL), F32),
                     jax.ShapeDtypeStruct((n_tiles * tm, D_MODEL), BF16)]
    return pl.pallas_call(
        functools.partial(_outproj_kernel, split_in, final, M_CTX // tm),
        grid=(n_tiles,),
        in_specs=in_specs,
        out_specs=out_specs,
        out_shape=out_shape,
        scratch_shapes=[pltpu.VMEM((D_ATT + D_HY, D_MODEL), BF16)],
        compiler_params=_params("arbitrary"),
        name="out_proj_final" if final else "out_proj",
    )(*args)


def _rope_tables():
    t = np.arange(DEC_SEQ)
    row = (t // GRID_W).astype(np.float64)
    col = (t % GRID_W).astype(np.float64)
    pairs = HEAD_DIM // 4
    inv_freq = ROPE_THETA ** (-np.arange(pairs, dtype=np.float64) / pairs)
    ang = np.concatenate([row[:, None] * inv_freq, col[:, None] * inv_freq], axis=-1)
    cos = np.repeat(np.cos(ang), 2, axis=-1).astype(np.float32)
    sin = np.repeat(np.sin(ang), 2, axis=-1).astype(np.float32)
    even = (np.arange(HEAD_DIM) % 2 == 0)[None, :]
    sin_a = np.where(even, -sin, 0.0).astype(np.float32)
    sin_b = np.where(even, 0.0, sin).astype(np.float32)
    return jnp.asarray(cos), jnp.asarray(sin_a), jnp.asarray(sin_b)


def kernel(x_prompt, x_sample, cache_k, cache_v, c, c_ctx, norm_g, w_ada, b_ada, w_in, q_norm_g, k_norm_g,
           conv_w, conv_b, filt_w1, filt_b1, filt_w2, filt_b2, filt_w3, filt_freq, hy_bias, w_out, final_norm_g):
    ctx = x_prompt.reshape(BATCH * SEQ, D_MODEL)
    lat = x_sample.reshape(DEC_BATCH * DEC_SEQ, D_MODEL)
    cache_k4 = cache_k.reshape(DEC_BATCH, DEPTH, PAST_LEN, D_KV)
    cache_v4 = cache_v.reshape(DEC_BATCH, DEPTH, PAST_LEN, D_KV)

    cvec = jnp.concatenate([c_ctx[None, :], c, jnp.zeros((MOD_ROWS - 1 - DEC_BATCH, D_MODEL), F32)], axis=0)
    mod = _modulation(cvec, w_ada, b_ada)

    rope_tabs = _rope_tables()
    dft = {}
    coefs = {}
    for L in (SEQ, DEC_SEQ):
        f_np = _dft_matrix(L)
        f_bf = jnp.asarray(f_np).astype(BF16)
        dft[L] = (f_bf, jnp.asarray(np.ascontiguousarray(f_np.T)).astype(BF16))
        coefs[L] = _filter_spectra(L, filt_w1, filt_b1, filt_w2, filt_b2, filt_freq, filt_w3, f_bf)

    mod4 = mod.reshape(DEPTH, MOD_ROWS, 1, 3 * D_MODEL)
    norm_g3 = norm_g.reshape(DEPTH, 1, D_MODEL)
    final_g3 = final_norm_g.reshape(1, 1, D_MODEL)
    conv_b3 = conv_b.reshape(DEPTH, 1, 3 * D_HY)
    hy_bias3 = hy_bias.reshape(DEPTH, 1, D_HY)

    x_parts = (ctx, lat)
    h = _norm_mod(ctx, lat, norm_g3, mod4, 0)
    kv_out = None
    for l in range(DEPTH):
        qg = q_norm_g[l].reshape(1, HEAD_DIM)
        kg = k_norm_g[l].reshape(1, HEAD_DIM)
        final = l == DEPTH - 1

        proj = _in_proj(h, w_in, l)

        att, new_k, new_v = _attention(proj, qg, kg, l, SEQ, CTX_TQ, BATCH, 0, kv_prev=kv_out, emit_kv=True)
        kv_out = (new_k, new_v)
        (att,) = _attention(proj, qg, kg, l, DEC_SEQ, LAT_TQ, DEC_BATCH, M_CTX, cache=(cache_k4, cache_v4),
                            rope_tabs=rope_tabs, att_prev=att)

        hy = _hyena(proj, SEQ, BATCH, 0, D_HY, l, conv_w, conv_b3, hy_bias3, coefs[SEQ], *dft[SEQ])
        hy = _hyena(proj, DEC_SEQ, DEC_BATCH, M_CTX // DEC_SEQ, 256, l, conv_w, conv_b3, hy_bias3,
                    coefs[DEC_SEQ], *dft[DEC_SEQ], hy_prev=hy)

        if final:
            assert len(x_parts) == 1
            nc = M_CTX // OUTPROJ_TM
            (y_ctx,) = _out_proj(att, hy, w_out, l, x_parts, mod4, final_g3, True, 0, nc)
            (y_lat,) = _out_proj(att, hy, w_out, l, x_parts, mod4, final_g3, True, nc, M_LAT // OUTPROJ_TM)
        else:
            y, h = _out_proj(att, hy, w_out, l, x_parts, mod4, norm_g3, False)
            x_parts = (y,)

    y_prompt = y_ctx.reshape(BATCH, SEQ, D_MODEL)
    y_sample = y_lat.reshape(DEC_BATCH, DEC_SEQ, D_MODEL)
    return (y_prompt, y_sample, kv_out[0], kv_out[1])
```

```python
import functools
import math

import numpy as np
import jax
import jax.numpy as jnp
from jax import lax
from jax.experimental import pallas as pl
from jax.experimental.pallas import tpu as pltpu

D_MODEL = 2048
BATCH = 16
SEQ = 256
DEPTH = 2
DEC_BATCH = 2
DEC_SEQ = 1024
PAST_LEN = 512
GRID_W = 64
D_ATT = 1024
D_HY = 1024
HEAD_DIM = 128
N_HEADS = 8
N_KV_HEADS = 4
GROUP = 2
D_KV = 512
ROPE_THETA = 10000.0
POS_BANDS = 16
POS_EMB = 33
FILT_HID = 64
DECAY_TARGET = 1e-2
FAST_DECAY_PCT = 0.3
SLOW_DECAY_PCT = 1.5
DECAY_SHIFT = 0.05
EPS = 1e-6
D_IN = 7168

COL_Q, COL_K, COL_V, COL_GA, COL_X0, COL_X1, COL_VV, COL_GH = 0, 1024, 1536, 2048, 3072, 4096, 5120, 6144

F32 = jnp.float32
BF16 = jnp.bfloat16

VMEM_LIMIT_BYTES = 56 * 1024 * 1024
FEAT_PAD = 128


def _params(*sem):
    return pltpu.CompilerParams(dimension_semantics=sem, vmem_limit_bytes=VMEM_LIMIT_BYTES)


def _silu(x):
    return x * (1.0 / (1.0 + jnp.exp(-x)))


def _rms(x, g):
    return x * lax.rsqrt(jnp.mean(x * x, axis=-1, keepdims=True) + EPS) * g


MOD_ROWS = 8
MOD_TN = 512


def _mod_kernel(c_ref, w_ref, b_ref, o_ref):
    s = _silu(c_ref[...]).astype(BF16)
    o_ref[0] = jnp.dot(s, w_ref[0].astype(BF16), preferred_element_type=F32) + b_ref[0]


def _modulation(cvec, w_ada, b_ada):
    n = 3 * D_MODEL
    return pl.pallas_call(
        _mod_kernel,
        grid=(DEPTH, n // MOD_TN),
        in_specs=[
            pl.BlockSpec((MOD_ROWS, D_MODEL), lambda l, j: (0, 0)),
            pl.BlockSpec((1, D_MODEL, MOD_TN), lambda l, j: (l, 0, j)),
            pl.BlockSpec((1, 1, MOD_TN), lambda l, j: (l, 0, j)),
        ],
        out_specs=pl.BlockSpec((1, MOD_ROWS, MOD_TN), lambda l, j: (l, 0, j)),
        out_shape=jax.ShapeDtypeStruct((DEPTH, MOD_ROWS, n), F32),
        compiler_params=_params("arbitrary", "arbitrary"),
        name="adaln_mod",
    )(cvec, w_ada, b_ada.reshape(DEPTH, 1, n))


M_CTX = BATCH * SEQ
M_LAT = DEC_BATCH * DEC_SEQ
M_ALL = M_CTX + M_LAT
NORM_TM = 512
INPROJ_TM = 2048
INPROJ_TN = 512


ROW_CHUNK = 32


def _modulated_norm_rows(x_ref, h_ref, g, shift, scale):
    gain = g * (1.0 + scale)

    def body(r, _):
        rows = pl.ds(pl.multiple_of(r * ROW_CHUNK, ROW_CHUNK), ROW_CHUNK)
        x = x_ref[rows, :]
        inv = lax.rsqrt(jnp.mean(x * x, axis=-1, keepdims=True) + EPS)
        h_ref[rows, :] = (x * inv * gain + shift).astype(BF16)
        return 0

    lax.fori_loop(0, x_ref.shape[0] // ROW_CHUNK, body, 0, unroll=4)


MOD_SHIFT, MOD_SCALE, MOD_GATE = 0, 1, 2


def _mod_spec(tm, layer, part, tile0=0):
    nc = M_CTX // tm
    per_batch = DEC_SEQ // tm
    row = lambda t: jnp.where(t < nc, 0, 1 + (t - nc) // per_batch)
    return pl.BlockSpec((1, 1, 1, D_MODEL), lambda i: (layer, row(tile0 + i), 0, part))


def _two_stream_specs(tm, width):
    nc = M_CTX // tm
    return (pl.BlockSpec((tm, width), lambda i: (jnp.minimum(i, nc - 1), 0)),
            pl.BlockSpec((tm, width), lambda i: (jnp.maximum(i - nc, 0), 0)))


def _norm_kernel(n_ctx_tiles, xc_ref, xl_ref, g_ref, shift_ref, scale_ref, h_ref):
    def emit(x_ref):
        _modulated_norm_rows(x_ref, h_ref, g_ref[0], shift_ref[0, 0], scale_ref[0, 0])

    i = pl.program_id(0)
    pl.when(i < n_ctx_tiles)(lambda: emit(xc_ref))
    pl.when(i >= n_ctx_tiles)(lambda: emit(xl_ref))


def _norm_mod(x_ctx, x_lat, norm_g3, mod4, layer):
    xc_spec, xl_spec = _two_stream_specs(NORM_TM, D_MODEL)
    return pl.pallas_call(
        functools.partial(_norm_kernel, M_CTX // NORM_TM),
        grid=(M_ALL // NORM_TM,),
        in_specs=[
            xc_spec, xl_spec,
            pl.BlockSpec((1, 1, D_MODEL), lambda i: (layer, 0, 0)),
            _mod_spec(NORM_TM, layer, MOD_SHIFT),
            _mod_spec(NORM_TM, layer, MOD_SCALE),
        ],
        out_specs=pl.BlockSpec((NORM_TM, D_MODEL), lambda i: (i, 0)),
        out_shape=jax.ShapeDtypeStruct((M_ALL, D_MODEL), BF16),
        compiler_params=_params("arbitrary"),
        name="norm_mod",
    )(x_ctx, x_lat, norm_g3, mod4, mod4)


def _inproj_kernel(h_ref, w_ref, o_ref, w_scr):
    i = pl.program_id(1)

    @pl.when(i == 0)
    def _():
        w_scr[...] = w_ref[0].astype(BF16)

    rows = pl.ds(pl.multiple_of(i * INPROJ_TM, INPROJ_TM), INPROJ_TM)
    o_ref[...] = jnp.dot(h_ref[rows, :], w_scr[...], preferred_element_type=F32)


def _in_proj(h, w_in, layer):
    return pl.pallas_call(
        _inproj_kernel,
        grid=(D_IN // INPROJ_TN, M_ALL // INPROJ_TM),
        in_specs=[
            pl.BlockSpec((M_ALL, D_MODEL), lambda j, i: (0, 0), pipeline_mode=pl.Buffered(1)),
            pl.BlockSpec((1, D_MODEL, INPROJ_TN), lambda j, i: (layer, 0, j)),
        ],
        out_specs=pl.BlockSpec((INPROJ_TM, INPROJ_TN), lambda j, i: (i, j)),
        out_shape=jax.ShapeDtypeStruct((M_ALL, D_IN), F32),
        scratch_shapes=[pltpu.VMEM((D_MODEL, INPROJ_TN), BF16)],
        compiler_params=_params("arbitrary", "arbitrary"),
        name="in_proj",
    )(h, w_in)


ATT_SCALE = 1.0 / math.sqrt(HEAD_DIM)


def _head(ref_or_val, h):
    return ref_or_val[:, h * HEAD_DIM:(h + 1) * HEAD_DIM]


Q_PRESCALE = ATT_SCALE * math.log2(math.e)


def _softmax_pv(q2, k_bf, v_ext):
    s = lax.dot_general(q2, k_bf, (((1,), (1,)), ((), ())), preferred_element_type=F32)
    p = jnp.exp2(s - jnp.max(s, axis=-1, keepdims=True)).astype(BF16)
    o = jnp.dot(p, v_ext, preferred_element_type=F32)
    return o[:, 0:HEAD_DIM] / o[:, HEAD_DIM:2 * HEAD_DIM]


def _rope(x, cos2, sin_a, sin_b):
    nxt = pltpu.roll(x, HEAD_DIM - 1, axis=1)
    prv = pltpu.roll(x, 1, axis=1)
    return x * cos2 + nxt * sin_a + prv * sin_b


CTX_TQ = SEQ
LAT_TQ = 128


def _attn_kernel(rope, n_cache, emit_kv, n_aliased, *refs):
    it = iter(refs)
    q_ref, kv_ref, g_ref, qg_ref, kg_ref = (next(it) for _ in range(5))
    ck_ref, cv_ref = (next(it), next(it)) if n_cache else (None, None)
    q_tabs = tuple(next(it) for _ in range(3)) if rope else None
    k_tabs = tuple(next(it) for _ in range(3)) if rope else None
    for _ in range(n_aliased):
        next(it)
    att_ref = next(it)
    ko_ref, vo_ref = (next(it), next(it)) if emit_kv else (None, None)
    k_scr, v_scr = next(it), next(it)
    n_new = kv_ref.shape[0]

    def prepare_keys_values():
        for h in range(N_KV_HEADS):
            kn = _rms(_head(kv_ref, h), kg_ref[...])
            vh = _head(kv_ref, N_KV_HEADS + h)
            if emit_kv:
                ko_ref[0, 0, :, h, :] = kn
                vo_ref[0, 0, :, h, :] = vh
            if rope:
                kn = _rope(kn, *(t[...] for t in k_tabs))
            k_scr[0:n_new, h * HEAD_DIM:(h + 1) * HEAD_DIM] = kn.astype(BF16)
            v0 = 2 * h * HEAD_DIM
            v_scr[0:n_new, v0:v0 + HEAD_DIM] = vh.astype(BF16)
            if n_cache:
                v_scr[n_new:n_new + n_cache, v0:v0 + HEAD_DIM] = _head(cv_ref[0, 0], h).astype(BF16)
            v_scr[:, v0 + HEAD_DIM:v0 + 2 * HEAD_DIM] = jnp.ones((n_new + n_cache, HEAD_DIM), BF16)
        if n_cache:
            k_scr[n_new:n_new + n_cache, :] = ck_ref[0, 0].astype(BF16)

    if q_ref.shape[0] == n_new:
        prepare_keys_values()
    else:
        pl.when(pl.program_id(1) == 0)(prepare_keys_values)

    def query(hq):
        x = _rms(_head(q_ref, hq), qg_ref[...])
        if rope:
            x = _rope(x, *(t[...] for t in q_tabs))
        return x * Q_PRESCALE

    rows = q_ref.shape[0]
    for h in range(N_KV_HEADS):
        q2 = jnp.concatenate([query(GROUP * h + g) for g in range(GROUP)], axis=0).astype(BF16)
        o = _softmax_pv(q2, _head(k_scr, h), v_scr[:, 2 * h * HEAD_DIM:2 * (h + 1) * HEAD_DIM])
        for g in range(GROUP):
            hq = GROUP * h + g
            gate = _silu(_head(g_ref, hq))
            att_ref[:, hq * HEAD_DIM:(hq + 1) * HEAD_DIM] = (o[g * rows:(g + 1) * rows] * gate).astype(BF16)


def _attention(proj, q_g, k_g, layer, seq, tq, n_batch, row0, *, cache=None, rope_tabs=None, att_prev=None,
               kv_prev=None, emit_kv=False):
    nq = seq // tq
    q0, kv0 = row0 // tq, row0 // seq
    n_cache = 0 if cache is None else cache[0].shape[2]
    vec = pl.BlockSpec((1, HEAD_DIM), lambda b, i: (0, 0))
    in_specs = [
        pl.BlockSpec((tq, D_ATT), lambda b, i: (q0 + b * nq + i, COL_Q // D_ATT)),
        pl.BlockSpec((seq, 2 * D_KV), lambda b, i: (kv0 + b, COL_K // (2 * D_KV))),
        pl.BlockSpec((tq, D_ATT), lambda b, i: (q0 + b * nq + i, COL_GA // D_ATT)),
        vec, vec,
    ]
    args = [proj, proj, proj, q_g, k_g]
    if cache is not None:
        in_specs += [pl.BlockSpec((1, 1, n_cache, D_KV), lambda b, i: (b, layer, 0, 0))] * 2
        args += list(cache)
    if rope_tabs is not None:
        in_specs += [pl.BlockSpec((tq, HEAD_DIM), lambda b, i: (i, 0))] * 3
        in_specs += [pl.BlockSpec((seq, HEAD_DIM), lambda b, i: (0, 0))] * 3
        args += list(rope_tabs) * 2
    aliased = ([] if att_prev is None else [att_prev]) + ([] if kv_prev is None else list(kv_prev))
    out_first = 0 if att_prev is not None else 1
    aliases = {len(args) + n: out_first + n for n in range(len(aliased))}
    in_specs += [pl.BlockSpec(memory_space=pl.ANY)] * len(aliased)
    args += aliased
    out_specs = [pl.BlockSpec((tq, D_ATT), lambda b, i: (q0 + b * nq + i, 0))]
    out_shape = [jax.ShapeDtypeStruct((M_ALL, D_ATT), BF16)]
    if emit_kv:
        out_specs += [pl.BlockSpec((1, 1, seq, N_KV_HEADS, HEAD_DIM), lambda b, i: (b, layer, 0, 0, 0))] * 2
        out_shape += [jax.ShapeDtypeStruct((n_batch, DEPTH, seq, N_KV_HEADS, HEAD_DIM), F32)] * 2
    return pl.pallas_call(
        functools.partial(_attn_kernel, rope_tabs is not None, n_cache, emit_kv, len(aliased)),
        grid=(n_batch, nq),
        in_specs=in_specs,
        out_specs=out_specs,
        out_shape=out_shape,
        input_output_aliases=aliases,
        scratch_shapes=[pltpu.VMEM((seq + n_cache, D_KV), BF16), pltpu.VMEM((seq + n_cache, 2 * D_KV), BF16)],
        compiler_params=_params("arbitrary", "arbitrary"),
        name=f"attention_{seq}",
    )(*args)


def _dft_matrix(L):
    k = np.arange(L, dtype=np.int64)[:, None]
    t = np.arange(L, dtype=np.int64)[None, :]
    ang = 2.0 * np.pi * ((k * t) % (2 * L)).astype(np.float64) / (2 * L)
    top = np.cos(ang)
    bot = np.sin(ang)
    bot[0, :] = np.where(np.arange(L) % 2 == 0, 1.0, -1.0)
    return np.concatenate([top, bot], axis=0).astype(np.float32)


def _filter_kernel(L, z_ref, w1_ref, b1_ref, w2_ref, b2_ref, fr_ref, w3f_ref, w3b_ref, dec_ref,
                   f_ref, o_ref, hdn_scr):
    hp = lax.Precision.HIGHEST

    @pl.when(pl.program_id(1) == 0)
    def _():
        fr = fr_ref[0]
        h1 = jnp.sin(fr * (jnp.dot(z_ref[...], w1_ref[0], precision=hp, preferred_element_type=F32) + b1_ref[0]))
        hdn_scr[...] = jnp.sin(fr * (jnp.dot(h1, w2_ref[0], precision=hp, preferred_element_type=F32) + b2_ref[0]))

    hdn = hdn_scr[...]
    dec = dec_ref[...]
    hdn_bf = hdn.astype(BF16)
    h_f = jnp.dot(hdn_bf, w3f_ref[0].astype(BF16), preferred_element_type=F32) * dec
    h_b = jnp.dot(hdn_bf, w3b_ref[0].astype(BF16), preferred_element_type=F32) * dec
    hs = h_f + h_b
    ha = jnp.dot(f_ref[0:L, :], hs.astype(BF16), preferred_element_type=F32)
    hb = jnp.dot(f_ref[L:2 * L, :], (h_f - h_b).astype(BF16), preferred_element_type=F32)
    row = lax.broadcasted_iota(jnp.int32, (L, 1), 0)
    first = row == 0
    nyquist = jnp.sum(jnp.where(row % 2 == 0, hs, -hs), axis=0, keepdims=True)
    wk = jnp.where(first, 1.0 / (2 * L), 2.0 / (2 * L))
    o_ref[0, 0] = ha * wk
    o_ref[0, 1] = jnp.where(first, 0.0, hb * wk)
    o_ref[0, 2] = jnp.where(first, nyquist, ha) * wk


FILT_CB = 512


def _filter_spectra(L, w1, b1, w2, b2, freq, w3, f_bf):
    tpos = np.arange(L, dtype=np.float64)
    t_norm = tpos / max(L - 1, 1)
    w = 2.0 * math.pi * tpos / L
    bands = np.linspace(1e-4, POS_BANDS - 1, POS_BANDS)
    z = np.concatenate([t_norm[:, None], np.cos(w[:, None] * bands), -np.sin(w[:, None] * bands)], axis=-1)
    z = np.pad(z, ((0, 0), (0, FEAT_PAD - POS_EMB))).astype(np.float32)
    max_decay = math.log(DECAY_TARGET) / FAST_DECAY_PCT
    min_decay = math.log(DECAY_TARGET) / SLOW_DECAY_PCT
    deltas = np.abs(np.linspace(min_decay, max_decay, D_HY))
    dec = (np.exp(-t_norm[:, None] * deltas) + DECAY_SHIFT).astype(np.float32)

    pad_h = FEAT_PAD - FILT_HID
    w1p = jnp.pad(w1, ((0, 0), (0, FEAT_PAD - POS_EMB), (0, pad_h)))
    w2p = jnp.pad(w2, ((0, 0), (0, pad_h), (0, pad_h)))
    w3p = jnp.pad(w3, ((0, 0), (0, pad_h), (0, 0)))
    vec = lambda v: jnp.pad(v, ((0, 0), (0, pad_h))).reshape(DEPTH, 1, FEAT_PAD)
    ncb = D_HY // FILT_CB
    small = lambda shape: pl.BlockSpec((1,) + shape, lambda l, c: (l, 0, 0))
    return pl.pallas_call(
        functools.partial(_filter_kernel, L),
        grid=(DEPTH, ncb),
        in_specs=[
            pl.BlockSpec((L, FEAT_PAD), lambda l, c: (0, 0)),
            small((FEAT_PAD, FEAT_PAD)), small((1, FEAT_PAD)),
            small((FEAT_PAD, FEAT_PAD)), small((1, FEAT_PAD)), small((1, FEAT_PAD)),
            pl.BlockSpec((1, FEAT_PAD, FILT_CB), lambda l, c: (l, 0, c)),
            pl.BlockSpec((1, FEAT_PAD, FILT_CB), lambda l, c: (l, 0, ncb + c)),
            pl.BlockSpec((L, FILT_CB), lambda l, c: (0, c)),
            pl.BlockSpec((2 * L, L), lambda l, c: (0, 0)),
        ],
        out_specs=pl.BlockSpec((1, 3, L, FILT_CB), lambda l, c: (l, 0, 0, c)),
        out_shape=jax.ShapeDtypeStruct((DEPTH, 3, L, D_HY), F32),
        scratch_shapes=[pltpu.VMEM((L, FEAT_PAD), F32)],
        compiler_params=_params("arbitrary", "arbitrary"),
        name=f"hyena_filter_{L}",
    )(jnp.asarray(z), w1p, vec(b1), w2p, vec(b2), vec(freq), w3p, w3p, jnp.asarray(dec), f_bf)


def _hyena_kernel(L, x0_ref, x1_ref, vv_ref, g_ref, cw0_ref, cw1_ref, cwv_ref, cb0_ref, cb1_ref,
                  cbv_ref, bias_ref, coef_ref, f_ref, ft_ref, *rest):
    o_ref = rest[-1]
    row = lax.broadcasted_iota(jnp.int32, (L, 1), 0)
    is_first = row == 0
    is_last = row == L - 1

    def sconv(x_ref, w_ref, b_ref):
        x = x_ref[...]
        prev = jnp.where(is_first, 0.0, pltpu.roll(x, 1, axis=0))
        nxt = jnp.where(is_last, 0.0, pltpu.roll(x, L - 1, axis=0))
        return w_ref[0, 0:1, :] * prev + w_ref[0, 1:2, :] * x + w_ref[0, 2:3, :] * nxt + b_ref[0]

    z = sconv(vv_ref, cwv_ref, cbv_ref) * sconv(x1_ref, cw1_ref, cb1_ref)
    ab = jnp.dot(f_ref[...], z.astype(BF16), preferred_element_type=F32)
    a, b = ab[0:L], ab[L:2 * L]
    g1, g2, g3 = coef_ref[0, 0], coef_ref[0, 1], coef_ref[0, 2]
    pq = jnp.concatenate([a * g1 - b * g2, a * g2 + b * g3], axis=0).astype(BF16)
    y = jnp.dot(ft_ref[...], pq, preferred_element_type=F32) + bias_ref[0] * z
    o_ref[...] = (sconv(x0_ref, cw0_ref, cb0_ref) * y * _silu(g_ref[...])).astype(BF16)


def _hyena(proj, L, nb, row0, cb, layer, conv_w, conv_b, hy_bias, coef, f_bf, ft_bf, hy_prev=None):
    ncb = D_HY // cb
    extra_specs = [] if hy_prev is None else [pl.BlockSpec(memory_space=pl.ANY)]
    extra_args = [] if hy_prev is None else [hy_prev]
    aliases = {} if hy_prev is None else {14: 0}
    col = lambda off: (lambda c, b: (row0 + b, off // cb + c))
    cw = lambda part: pl.BlockSpec((1, 3, cb), lambda c, b: (layer, 0, part * ncb + c))
    cbias = lambda part: pl.BlockSpec((1, 1, cb), lambda c, b: (layer, 0, part * ncb + c))
    once = pl.Buffered(1)
    return pl.pallas_call(
        functools.partial(_hyena_kernel, L),
        grid=(ncb, nb),
        in_specs=[
            pl.BlockSpec((L, cb), col(COL_X0)),
            pl.BlockSpec((L, cb), col(COL_X1)),
            pl.BlockSpec((L, cb), col(COL_VV)),
            pl.BlockSpec((L, cb), col(COL_GH)),
            cw(0), cw(1), cw(2), cbias(0), cbias(1), cbias(2),
            pl.BlockSpec((1, 1, cb), lambda c, b: (layer, 0, c)),
            pl.BlockSpec((1, 3, L, cb), lambda c, b: (layer, 0, 0, c)),
            pl.BlockSpec((2 * L, L), lambda c, b: (0, 0), pipeline_mode=once),
            pl.BlockSpec((L, 2 * L), lambda c, b: (0, 0), pipeline_mode=once),
            *extra_specs,
        ],
        out_specs=pl.BlockSpec((L, cb), lambda c, b: (row0 + b, c)),
        out_shape=jax.ShapeDtypeStruct((M_ALL, D_HY), BF16),
        input_output_aliases=aliases,
        compiler_params=_params("arbitrary", "arbitrary"),
        name=f"hyena_{L}",
    )(proj, proj, proj, proj, conv_w, conv_w, conv_w, conv_b, conv_b, conv_b, hy_bias, coef, f_bf, ft_bf,
      *extra_args)


OUTPROJ_TM = 256


def _outproj_kernel(split_in, final, n_ctx_tiles, *refs):
    it = iter(refs)
    att_ref, hy_ref, w_ref = next(it), next(it), next(it)
    x_refs = (next(it), next(it)) if split_in else (next(it),)
    gate_ref, g_ref = next(it), next(it)
    shift_ref, scale_ref = (None, None) if final else (next(it), next(it))
    out_refs = (next(it),) if final else (next(it), next(it))
    w_scr = next(it)
    i = pl.program_id(0)

    @pl.when(i == 0)
    def _():
        w_scr[...] = w_ref[0].astype(BF16)

    out = (jnp.dot(att_ref[...], w_scr[0:D_ATT, :], preferred_element_type=F32)
           + jnp.dot(hy_ref[...], w_scr[D_ATT:D_ATT + D_HY, :], preferred_element_type=F32))
    if split_in:
        x = jnp.where(i < n_ctx_tiles, x_refs[0][...], x_refs[1][...])
    else:
        x = x_refs[0][...]
    y = x + gate_ref[0, 0] * out
    inv = lax.rsqrt(jnp.mean(y * y, axis=-1, keepdims=True) + EPS)
    if final:
        out_refs[0][...] = y * inv * g_ref[0]
    else:
        out_refs[0][...] = y
        gain = g_ref[0] * (1.0 + scale_ref[0, 0])
        out_refs[1][...] = (y * inv * gain + shift_ref[0, 0]).astype(BF16)


def _out_proj(att, hy, w_out, layer, x_parts, mod4, gains3, final, tile0=0, n_tiles=M_ALL // OUTPROJ_TM):
    tm = OUTPROJ_TM
    split_in = len(x_parts) == 2
    assert not split_in or (tile0 == 0 and n_tiles == M_ALL // tm)
    row_tile = lambda width: pl.BlockSpec((tm, width), lambda i: (tile0 + i, 0))
    out_tile = pl.BlockSpec((tm, D_MODEL), lambda i: (i, 0))
    x_specs = list(_two_stream_specs(tm, D_MODEL)) if split_in else [row_tile(D_MODEL)]
    in_specs = [
        row_tile(D_ATT),
        row_tile(D_HY),
        pl.BlockSpec((1, D_ATT + D_HY, D_MODEL), lambda i: (layer, 0, 0), pipeline_mode=pl.Buffered(1)),
        *x_specs,
        _mod_spec(tm, layer, MOD_GATE, tile0),
    ]
    args = [att, hy, w_out, *x_parts, mod4]
    if final:
        in_specs.append(pl.BlockSpec((1, 1, D_MODEL), lambda i: (0, 0, 0)))
        args.append(gains3)
        out_specs = [out_tile]
        out_shape = [jax.ShapeDtypeStruct((n_tiles * tm, D_MODEL), F32)]
    else:
        in_specs += [pl.BlockSpec((1, 1, D_MODEL), lambda i: (layer + 1, 0, 0)),
                     _mod_spec(tm, layer + 1, MOD_SHIFT, tile0), _mod_spec(tm, layer + 1, MOD_SCALE, tile0)]
        args += [gains3, mod4, mod4]
        out_specs = [out_tile, out_tile]
        out_shape = [jax.ShapeDtypeStruct((n_tiles * tm, D_MODEL), F32),
                     jax.ShapeDtypeStruct((n_tiles * tm, D_MODEL), BF16)]
    return pl.pallas_call(
        functools.partial(_outproj_kernel, split_in, final, M_CTX // tm),
        grid=(n_tiles,),
        in_specs=in_specs,
        out_specs=out_specs,
        out_shape=out_shape,
        scratch_shapes=[pltpu.VMEM((D_ATT + D_HY, D_MODEL), BF16)],
        compiler_params=_params("arbitrary"),
        name="out_proj_final" if final else "out_proj",
    )(*args)


def _rope_tables():
    t = np.arange(DEC_SEQ)
    row = (t // GRID_W).astype(np.float64)
    col = (t % GRID_W).astype(np.float64)
    pairs = HEAD_DIM // 4
    inv_freq = ROPE_THETA ** (-np.arange(pairs, dtype=np.float64) / pairs)
    ang = np.concatenate([row[:, None] * inv_freq, col[:, None] * inv_freq], axis=-1)
    cos = np.repeat(np.cos(ang), 2, axis=-1).astype(np.float32)
    sin = np.repeat(np.sin(ang), 2, axis=-1).astype(np.float32)
    even = (np.arange(HEAD_DIM) % 2 == 0)[None, :]
    sin_a = np.where(even, -sin, 0.0).astype(np.float32)
    sin_b = np.where(even, 0.0, sin).astype(np.float32)
    return jnp.asarray(cos), jnp.asarray(sin_a), jnp.asarray(sin_b)


def kernel(x_prompt, x_sample, cache_k, cache_v, c, c_ctx, norm_g, w_ada, b_ada, w_in, q_norm_g, k_norm_g,
           conv_w, conv_b, filt_w1, filt_b1, filt_w2, filt_b2, filt_w3, filt_freq, hy_bias, w_out, final_norm_g):
    ctx = x_prompt.reshape(BATCH * SEQ, D_MODEL)
    lat = x_sample.reshape(DEC_BATCH * DEC_SEQ, D_MODEL)
    cache_k4 = cache_k.reshape(DEC_BATCH, DEPTH, PAST_LEN, D_KV)
    cache_v4 = cache_v.reshape(DEC_BATCH, DEPTH, PAST_LEN, D_KV)

    cvec = jnp.concatenate([c_ctx[None, :], c, jnp.zeros((MOD_ROWS - 1 - DEC_BATCH, D_MODEL), F32)], axis=0)
    mod = _modulation(cvec, w_ada, b_ada)

    rope_tabs = _rope_tables()
    dft = {}
    coefs = {}
    for L in (SEQ, DEC_SEQ):
        f_np = _dft_matrix(L)
        f_bf = jnp.asarray(f_np).astype(BF16)
        dft[L] = (f_bf, jnp.asarray(np.ascontiguousarray(f_np.T)).astype(BF16))
        coefs[L] = _filter_spectra(L, filt_w1, filt_b1, filt_w2, filt_b2, filt_freq, filt_w3, f_bf)

    mod4 = mod.reshape(DEPTH, MOD_ROWS, 1, 3 * D_MODEL)
    norm_g3 = norm_g.reshape(DEPTH, 1, D_MODEL)
    final_g3 = final_norm_g.reshape(1, 1, D_MODEL)
    conv_b3 = conv_b.reshape(DEPTH, 1, 3 * D_HY)
    hy_bias3 = hy_bias.reshape(DEPTH, 1, D_HY)

    x_parts = (ctx, lat)
    h = _norm_mod(ctx, lat, norm_g3, mod4, 0)
    kv_out = None
    for l in range(DEPTH):
        qg = q_norm_g[l].reshape(1, HEAD_DIM)
        kg = k_norm_g[l].reshape(1, HEAD_DIM)
        final = l == DEPTH - 1

        proj = _in_proj(h, w_in, l)

        att, new_k, new_v = _attention(proj, qg, kg, l, SEQ, CTX_TQ, BATCH, 0, kv_prev=kv_out, emit_kv=True)
        kv_out = (new_k, new_v)
        (att,) = _attention(proj, qg, kg, l, DEC_SEQ, LAT_TQ, DEC_BATCH, M_CTX, cache=(cache_k4, cache_v4),
                            rope_tabs=rope_tabs, att_prev=att)

        hy = _hyena(proj, SEQ, BATCH, 0, D_HY, l, conv_w, conv_b3, hy_bias3, coefs[SEQ], *dft[SEQ])
        hy = _hyena(proj, DEC_SEQ, DEC_BATCH, M_CTX // DEC_SEQ, 512, l, conv_w, conv_b3, hy_bias3,
                    coefs[DEC_SEQ], *dft[DEC_SEQ], hy_prev=hy)

        if final:
            assert len(x_parts) == 1
            nc = M_CTX // OUTPROJ_TM
            (y_ctx,) = _out_proj(att, hy, w_out, l, x_parts, mod4, final_g3, True, 0, nc)
            (y_lat,) = _out_proj(att, hy, w_out, l, x_parts, mod4, final_g3, True, nc, M_LAT // OUTPROJ_TM)
        else:
            y, h = _out_proj(att, hy, w_out, l, x_parts, mod4, norm_g3, False)
            x_parts = (y,)

    y_prompt = y_ctx.reshape(BATCH, SEQ, D_MODEL)
    y_sample = y_lat.reshape(DEC_BATCH, DEC_SEQ, D_MODEL)
    return (y_prompt, y_sample, kv_out[0], kv_out[1])
```

```python
import functools
import math

import numpy as np
import jax
import jax.numpy as jnp
from jax import lax
from jax.experimental import pallas as pl
from jax.experimental.pallas import tpu as pltpu

D_MODEL = 2048
BATCH = 16
SEQ = 256
DEPTH = 2
DEC_BATCH = 2
DEC_SEQ = 1024
PAST_LEN = 512
GRID_W = 64
D_ATT = 1024
D_HY = 1024
HEAD_DIM = 128
N_HEADS = 8
N_KV_HEADS = 4
GROUP = 2
D_KV = 512
ROPE_THETA = 10000.0
POS_BANDS = 16
POS_EMB = 33
FILT_HID = 64
DECAY_TARGET = 1e-2
FAST_DECAY_PCT = 0.3
SLOW_DECAY_PCT = 1.5
DECAY_SHIFT = 0.05
EPS = 1e-6
D_IN = 7168

COL_Q, COL_K, COL_V, COL_GA, COL_X0, COL_X1, COL_VV, COL_GH = 0, 1024, 1536, 2048, 3072, 4096, 5120, 6144

F32 = jnp.float32
BF16 = jnp.bfloat16

VMEM_LIMIT_BYTES = 56 * 1024 * 1024
FEAT_PAD = 128

def _params(*sem):
    return pltpu.CompilerParams(dimension_semantics=sem, vmem_limit_bytes=VMEM_LIMIT_BYTES)


def _silu(x):
    half = 0.5 * x
    return half + half * jnp.tanh(half)


def _rms(x, g):
    return x * lax.rsqrt(jnp.mean(x * x, axis=-1, keepdims=True) + EPS) * g


MOD_ROWS = 8
MOD_TN = 1024


def _mod_kernel(c_ref, w_ref, b_ref, o_ref):
    s = _silu(c_ref[...]).astype(BF16)
    o_ref[0] = jnp.dot(s, w_ref[0].astype(BF16), preferred_element_type=F32) + b_ref[0]


def _modulation(cvec, w_ada, b_ada):
    n = 3 * D_MODEL
    return pl.pallas_call(
        _mod_kernel,
        grid=(DEPTH, n // MOD_TN),
        in_specs=[
            pl.BlockSpec((MOD_ROWS, D_MODEL), lambda l, j: (0, 0)),
            pl.BlockSpec((1, D_MODEL, MOD_TN), lambda l, j: (l, 0, j)),
            pl.BlockSpec((1, 1, MOD_TN), lambda l, j: (l, 0, j)),
        ],
        out_specs=pl.BlockSpec((1, MOD_ROWS, MOD_TN), lambda l, j: (l, 0, j)),
        out_shape=jax.ShapeDtypeStruct((DEPTH, MOD_ROWS, n), F32),
        compiler_params=_params("arbitrary", "arbitrary"),
        name="adaln_mod",
    )(cvec, w_ada, b_ada.reshape(DEPTH, 1, n))


M_CTX = BATCH * SEQ
M_LAT = DEC_BATCH * DEC_SEQ
M_ALL = M_CTX + M_LAT
NORM_TM = 1024
INPROJ_TM = 3072
INPROJ_TN = 512


ROW_CHUNK = 32


def _modulated_norm_rows(x_ref, h_ref, g, shift, scale):
    gain = g * (1.0 + scale)

    def body(r, _):
        rows = pl.ds(pl.multiple_of(r * ROW_CHUNK, ROW_CHUNK), ROW_CHUNK)
        x = x_ref[rows, :]
        inv = lax.rsqrt(jnp.mean(x * x, axis=-1, keepdims=True) + EPS)
        h_ref[rows, :] = (x * inv * gain + shift).astype(BF16)
        return 0

    lax.fori_loop(0, x_ref.shape[0] // ROW_CHUNK, body, 0, unroll=4)


MOD_SHIFT, MOD_SCALE, MOD_GATE = 0, 1, 2


def _mod_spec(tm, layer, part, tile0=0):
    nc = M_CTX // tm
    per_batch = DEC_SEQ // tm
    row = lambda t: jnp.where(t < nc, 0, 1 + (t - nc) // per_batch)
    return pl.BlockSpec((1, 1, 1, D_MODEL), lambda i: (layer, row(tile0 + i), 0, part))


def _two_stream_specs(tm, width):
    nc = M_CTX // tm
    return (pl.BlockSpec((tm, width), lambda i: (jnp.minimum(i, nc - 1), 0)),
            pl.BlockSpec((tm, width), lambda i: (jnp.maximum(i - nc, 0), 0)))


def _norm_kernel(n_ctx_tiles, xc_ref, xl_ref, g_ref, shift_ref, scale_ref, h_ref):
    def emit(x_ref):
        _modulated_norm_rows(x_ref, h_ref, g_ref[0], shift_ref[0, 0], scale_ref[0, 0])

    i = pl.program_id(0)
    pl.when(i < n_ctx_tiles)(lambda: emit(xc_ref))
    pl.when(i >= n_ctx_tiles)(lambda: emit(xl_ref))


def _norm_mod(x_ctx, x_lat, norm_g3, mod4, layer):
    xc_spec, xl_spec = _two_stream_specs(NORM_TM, D_MODEL)
    return pl.pallas_call(
        functools.partial(_norm_kernel, M_CTX // NORM_TM),
        grid=(M_ALL // NORM_TM,),
        in_specs=[
            xc_spec, xl_spec,
            pl.BlockSpec((1, 1, D_MODEL), lambda i: (layer, 0, 0)),
            _mod_spec(NORM_TM, layer, MOD_SHIFT),
            _mod_spec(NORM_TM, layer, MOD_SCALE),
        ],
        out_specs=pl.BlockSpec((NORM_TM, D_MODEL), lambda i: (i, 0)),
        out_shape=jax.ShapeDtypeStruct((M_ALL, D_MODEL), BF16),
        compiler_params=_params("arbitrary"),
        name="norm_mod",
    )(x_ctx, x_lat, norm_g3, mod4, mod4)


def _inproj_kernel(h_ref, w_ref, o_ref, w_scr):
    i = pl.program_id(1)

    @pl.when(i == 0)
    def _():
        w_scr[...] = w_ref[0].astype(BF16)

    rows = pl.ds(pl.multiple_of(i * INPROJ_TM, INPROJ_TM), INPROJ_TM)
    o_ref[...] = jnp.dot(h_ref[rows, :], w_scr[...], preferred_element_type=F32)


def _in_proj(h, w_in, layer):
    return pl.pallas_call(
        _inproj_kernel,
        grid=(D_IN // INPROJ_TN, M_ALL // INPROJ_TM),
        in_specs=[
            pl.BlockSpec((M_ALL, D_MODEL), lambda j, i: (0, 0), pipeline_mode=pl.Buffered(1)),
            pl.BlockSpec((1, D_MODEL, INPROJ_TN), lambda j, i: (layer, 0, j)),
        ],
        out_specs=pl.BlockSpec((INPROJ_TM, INPROJ_TN), lambda j, i: (i, j)),
        out_shape=jax.ShapeDtypeStruct((M_ALL, D_IN), F32),
        scratch_shapes=[pltpu.VMEM((D_MODEL, INPROJ_TN), BF16)],
        compiler_params=_params("arbitrary", "arbitrary"),
        name="in_proj",
    )(h, w_in)


ATT_SCALE = 1.0 / math.sqrt(HEAD_DIM)


def _head(ref_or_val, h):
    return ref_or_val[:, h * HEAD_DIM:(h + 1) * HEAD_DIM]


Q_PRESCALE = ATT_SCALE * math.log2(math.e)


def _softmax_pv(q2, k_bf, v_ext):
    s = lax.dot_general(q2, k_bf, (((1,), (1,)), ((), ())), preferred_element_type=F32)
    p = jnp.exp2(s - jnp.max(s, axis=-1, keepdims=True)).astype(BF16)
    o = jnp.dot(p, v_ext, preferred_element_type=F32)
    return o[:, 0:HEAD_DIM] / o[:, HEAD_DIM:2 * HEAD_DIM]


def _rope(x, cos2, sin_a, sin_b):
    nxt = pltpu.roll(x, HEAD_DIM - 1, axis=1)
    prv = pltpu.roll(x, 1, axis=1)
    return x * cos2 + nxt * sin_a + prv * sin_b


CTX_TQ = SEQ
LAT_TQ = 128


def _attn_kernel(rope, n_cache, emit_kv, n_aliased, *refs):
    it = iter(refs)
    q_ref, kv_ref, g_ref, qg_ref, kg_ref = (next(it) for _ in range(5))
    ck_ref, cv_ref = (next(it), next(it)) if n_cache else (None, None)
    q_tabs = tuple(next(it) for _ in range(3)) if rope else None
    k_tabs = tuple(next(it) for _ in range(3)) if rope else None
    for _ in range(n_aliased):
        next(it)
    att_ref = next(it)
    ko_ref, vo_ref = (next(it), next(it)) if emit_kv else (None, None)
    k_scr, v_scr = next(it), next(it)
    n_new = kv_ref.shape[0]

    def prepare_keys_values():
        for h in range(N_KV_HEADS):
            kn = _rms(_head(kv_ref, h), kg_ref[...])
            vh = _head(kv_ref, N_KV_HEADS + h)
            if emit_kv:
                ko_ref[0, 0, :, h, :] = kn
                vo_ref[0, 0, :, h, :] = vh
            if rope:
                kn = _rope(kn, *(t[...] for t in k_tabs))
            k_scr[0:n_new, h * HEAD_DIM:(h + 1) * HEAD_DIM] = kn.astype(BF16)
            v0 = 2 * h * HEAD_DIM
            v_scr[0:n_new, v0:v0 + HEAD_DIM] = vh.astype(BF16)
            if n_cache:
                v_scr[n_new:n_new + n_cache, v0:v0 + HEAD_DIM] = _head(cv_ref[0, 0], h).astype(BF16)
            v_scr[:, v0 + HEAD_DIM:v0 + 2 * HEAD_DIM] = jnp.ones((n_new + n_cache, HEAD_DIM), BF16)
        if n_cache:
            k_scr[n_new:n_new + n_cache, :] = ck_ref[0, 0].astype(BF16)

    if q_ref.shape[0] == n_new:
        prepare_keys_values()
    else:
        pl.when(pl.program_id(1) == 0)(prepare_keys_values)

    def query(hq):
        x = _rms(_head(q_ref, hq), qg_ref[...])
        if rope:
            x = _rope(x, *(t[...] for t in q_tabs))
        return x * Q_PRESCALE

    rows = q_ref.shape[0]
    for h in range(N_KV_HEADS):
        q2 = jnp.concatenate([query(GROUP * h + g) for g in range(GROUP)], axis=0).astype(BF16)
        o = _softmax_pv(q2, _head(k_scr, h), v_scr[:, 2 * h * HEAD_DIM:2 * (h + 1) * HEAD_DIM])
        for g in range(GROUP):
            hq = GROUP * h + g
            gate = _silu(_head(g_ref, hq))
            att_ref[:, hq * HEAD_DIM:(hq + 1) * HEAD_DIM] = (o[g * rows:(g + 1) * rows] * gate).astype(BF16)


def _attention(proj, q_g, k_g, layer, seq, tq, n_batch, row0, *, cache=None, rope_tabs=None, att_prev=None,
               kv_prev=None, emit_kv=False):
    nq = seq // tq
    q0, kv0 = row0 // tq, row0 // seq
    n_cache = 0 if cache is None else cache[0].shape[2]
    vec = pl.BlockSpec((1, HEAD_DIM), lambda b, i: (0, 0))
    in_specs = [
        pl.BlockSpec((tq, D_ATT), lambda b, i: (q0 + b * nq + i, COL_Q // D_ATT)),
        pl.BlockSpec((seq, 2 * D_KV), lambda b, i: (kv0 + b, COL_K // (2 * D_KV))),
        pl.BlockSpec((tq, D_ATT), lambda b, i: (q0 + b * nq + i, COL_GA // D_ATT)),
        vec, vec,
    ]
    args = [proj, proj, proj, q_g, k_g]
    if cache is not None:
        in_specs += [pl.BlockSpec((1, 1, n_cache, D_KV), lambda b, i: (b, layer, 0, 0))] * 2
        args += list(cache)
    if rope_tabs is not None:
        in_specs += [pl.BlockSpec((tq, HEAD_DIM), lambda b, i: (i, 0))] * 3
        in_specs += [pl.BlockSpec((seq, HEAD_DIM), lambda b, i: (0, 0))] * 3
        args += list(rope_tabs) * 2
    aliased = ([] if att_prev is None else [att_prev]) + ([] if kv_prev is None else list(kv_prev))
    out_first = 0 if att_prev is not None else 1
    aliases = {len(args) + n: out_first + n for n in range(len(aliased))}
    in_specs += [pl.BlockSpec(memory_space=pl.ANY)] * len(aliased)
    args += aliased
    out_specs = [pl.BlockSpec((tq, D_ATT), lambda b, i: (q0 + b * nq + i, 0))]
    out_shape = [jax.ShapeDtypeStruct((M_ALL, D_ATT), BF16)]
    if emit_kv:
        out_specs += [pl.BlockSpec((1, 1, seq, N_KV_HEADS, HEAD_DIM), lambda b, i: (b, layer, 0, 0, 0))] * 2
        out_shape += [jax.ShapeDtypeStruct((n_batch, DEPTH, seq, N_KV_HEADS, HEAD_DIM), F32)] * 2
    return pl.pallas_call(
        functools.partial(_attn_kernel, rope_tabs is not None, n_cache, emit_kv, len(aliased)),
        grid=(n_batch, nq),
        in_specs=in_specs,
        out_specs=out_specs,
        out_shape=out_shape,
        input_output_aliases=aliases,
        scratch_shapes=[pltpu.VMEM((seq + n_cache, D_KV), BF16), pltpu.VMEM((seq + n_cache, 2 * D_KV), BF16)],
        compiler_params=_params("arbitrary", "arbitrary"),
        name=f"attention_{seq}",
    )(*args)


def _dft_matrix(L):
    k = np.arange(L, dtype=np.int64)[:, None]
    t = np.arange(L, dtype=np.int64)[None, :]
    ang = 2.0 * np.pi * ((k * t) % (2 * L)).astype(np.float64) / (2 * L)
    top = np.cos(ang)
    bot = np.sin(ang)
    bot[0, :] = np.where(np.arange(L) % 2 == 0, 1.0, -1.0)
    return np.concatenate([top, bot], axis=0).astype(np.float32)


def _filter_kernel(L, z_ref, w1_ref, b1_ref, w2_ref, b2_ref, fr_ref, w3f_ref, w3b_ref, dec_ref,
                   f_ref, o_ref, hdn_scr):
    hp = lax.Precision.HIGHEST

    @pl.when(pl.program_id(1) == 0)
    def _():
        fr = fr_ref[0]
        h1 = jnp.sin(fr * (jnp.dot(z_ref[...], w1_ref[0], precision=hp, preferred_element_type=F32) + b1_ref[0]))
        hdn_scr[...] = jnp.sin(fr * (jnp.dot(h1, w2_ref[0], precision=hp, preferred_element_type=F32) + b2_ref[0]))

    hdn = hdn_scr[...]
    dec = dec_ref[...]
    hdn_bf = hdn.astype(BF16)
    h_f = jnp.dot(hdn_bf, w3f_ref[0].astype(BF16), preferred_element_type=F32) * dec
    h_b = jnp.dot(hdn_bf, w3b_ref[0].astype(BF16), preferred_element_type=F32) * dec
    hs = h_f + h_b
    ha = jnp.dot(f_ref[0:L, :], hs.astype(BF16), preferred_element_type=F32)
    hb = jnp.dot(f_ref[L:2 * L, :], (h_f - h_b).astype(BF16), preferred_element_type=F32)
    row = lax.broadcasted_iota(jnp.int32, (L, 1), 0)
    first = row == 0
    nyquist = jnp.sum(jnp.where(row % 2 == 0, hs, -hs), axis=0, keepdims=True)
    wk = jnp.where(first, 1.0 / (2 * L), 2.0 / (2 * L))
    o_ref[0, 0] = ha * wk
    o_ref[0, 1] = jnp.where(first, 0.0, hb * wk)
    o_ref[0, 2] = jnp.where(first, nyquist, ha) * wk


FILT_CB = 512


def _filter_spectra(L, w1, b1, w2, b2, freq, w3, f_bf):
    tpos = np.arange(L, dtype=np.float64)
    t_norm = tpos / max(L - 1, 1)
    w = 2.0 * math.pi * tpos / L
    bands = np.linspace(1e-4, POS_BANDS - 1, POS_BANDS)
    z = np.concatenate([t_norm[:, None], np.cos(w[:, None] * bands), -np.sin(w[:, None] * bands)], axis=-1)
    z = np.pad(z, ((0, 0), (0, FEAT_PAD - POS_EMB))).astype(np.float32)
    max_decay = math.log(DECAY_TARGET) / FAST_DECAY_PCT
    min_decay = math.log(DECAY_TARGET) / SLOW_DECAY_PCT
    deltas = np.abs(np.linspace(min_decay, max_decay, D_HY))
    dec = (np.exp(-t_norm[:, None] * deltas) + DECAY_SHIFT).astype(np.float32)

    pad_h = FEAT_PAD - FILT_HID
    w1p = jnp.pad(w1, ((0, 0), (0, FEAT_PAD - POS_EMB), (0, pad_h)))
    w2p = jnp.pad(w2, ((0, 0), (0, pad_h), (0, pad_h)))
    w3p = jnp.pad(w3, ((0, 0), (0, pad_h), (0, 0)))
    vec = lambda v: jnp.pad(v, ((0, 0), (0, pad_h))).reshape(DEPTH, 1, FEAT_PAD)
    ncb = D_HY // FILT_CB
    small = lambda shape: pl.BlockSpec((1,) + shape, lambda l, c: (l, 0, 0))
    return pl.pallas_call(
        functools.partial(_filter_kernel, L),
        grid=(DEPTH, ncb),
        in_specs=[
            pl.BlockSpec((L, FEAT_PAD), lambda l, c: (0, 0)),
            small((FEAT_PAD, FEAT_PAD)), small((1, FEAT_PAD)),
            small((FEAT_PAD, FEAT_PAD)), small((1, FEAT_PAD)), small((1, FEAT_PAD)),
            pl.BlockSpec((1, FEAT_PAD, FILT_CB), lambda l, c: (l, 0, c)),
            pl.BlockSpec((1, FEAT_PAD, FILT_CB), lambda l, c: (l, 0, ncb + c)),
            pl.BlockSpec((L, FILT_CB), lambda l, c: (0, c)),
            pl.BlockSpec((2 * L, L), lambda l, c: (0, 0)),
        ],
        out_specs=pl.BlockSpec((1, 3, L, FILT_CB), lambda l, c: (l, 0, 0, c)),
        out_shape=jax.ShapeDtypeStruct((DEPTH, 3, L, D_HY), F32),
        scratch_shapes=[pltpu.VMEM((L, FEAT_PAD), F32)],
        compiler_params=_params("arbitrary", "arbitrary"),
        name=f"hyena_filter_{L}",
    )(jnp.asarray(z), w1p, vec(b1), w2p, vec(b2), vec(freq), w3p, w3p, jnp.asarray(dec), f_bf)


def _hyena_kernel(L, x0_ref, x1_ref, vv_ref, g_ref, cw0_ref, cw1_ref, cwv_ref, cb0_ref, cb1_ref,
                  cbv_ref, bias_ref, coef_ref, f_ref, ft_ref, *rest):
    o_ref = rest[-1]
    row = lax.broadcasted_iota(jnp.int32, (L, 1), 0)
    is_first = row == 0
    is_last = row == L - 1

    def sconv(x_ref, w_ref, b_ref):
        x = x_ref[...]
        prev = jnp.where(is_first, 0.0, pltpu.roll(x, 1, axis=0))
        nxt = jnp.where(is_last, 0.0, pltpu.roll(x, L - 1, axis=0))
        return w_ref[0, 0:1, :] * prev + w_ref[0, 1:2, :] * x + w_ref[0, 2:3, :] * nxt + b_ref[0]

    z = sconv(vv_ref, cwv_ref, cbv_ref) * sconv(x1_ref, cw1_ref, cb1_ref)
    ab = jnp.dot(f_ref[...], z.astype(BF16), preferred_element_type=F32)
    a, b = ab[0:L], ab[L:2 * L]
    g1, g2, g3 = coef_ref[0, 0], coef_ref[0, 1], coef_ref[0, 2]
    pq = jnp.concatenate([a * g1 - b * g2, a * g2 + b * g3], axis=0).astype(BF16)
    y = jnp.dot(ft_ref[...], pq, preferred_element_type=F32) + bias_ref[0] * z
    o_ref[...] = (sconv(x0_ref, cw0_ref, cb0_ref) * y * _silu(g_ref[...])).astype(BF16)


def _hyena(proj, L, nb, row0, cb, layer, conv_w, conv_b, hy_bias, coef, f_bf, ft_bf, hy_prev=None):
    ncb = D_HY // cb
    extra_specs = [] if hy_prev is None else [pl.BlockSpec(memory_space=pl.ANY)]
    extra_args = [] if hy_prev is None else [hy_prev]
    aliases = {} if hy_prev is None else {14: 0}
    col = lambda off: (lambda c, b: (row0 + b, off // cb + c))
    cw = lambda part: pl.BlockSpec((1, 3, cb), lambda c, b: (layer, 0, part * ncb + c))
    cbias = lambda part: pl.BlockSpec((1, 1, cb), lambda c, b: (layer, 0, part * ncb + c))
    once = pl.Buffered(1)
    return pl.pallas_call(
        functools.partial(_hyena_kernel, L),
        grid=(ncb, nb),
        in_specs=[
            pl.BlockSpec((L, cb), col(COL_X0)),
            pl.BlockSpec((L, cb), col(COL_X1)),
            pl.BlockSpec((L, cb), col(COL_VV)),
            pl.BlockSpec((L, cb), col(COL_GH)),
            cw(0), cw(1), cw(2), cbias(0), cbias(1), cbias(2),
            pl.BlockSpec((1, 1, cb), lambda c, b: (layer, 0, c)),
            pl.BlockSpec((1, 3, L, cb), lambda c, b: (layer, 0, 0, c)),
            pl.BlockSpec((2 * L, L), lambda c, b: (0, 0), pipeline_mode=once),
            pl.BlockSpec((L, 2 * L), lambda c, b: (0, 0), pipeline_mode=once),
            *extra_specs,
        ],
        out_specs=pl.BlockSpec((L, cb), lambda c, b: (row0 + b, c)),
        out_shape=jax.ShapeDtypeStruct((M_ALL, D_HY), BF16),
        input_output_aliases=aliases,
        compiler_params=_params("arbitrary", "arbitrary"),
        name=f"hyena_{L}",
    )(proj, proj, proj, proj, conv_w, conv_w, conv_w, conv_b, conv_b, conv_b, hy_bias, coef, f_bf, ft_bf,
      *extra_args)


OUTPROJ_TM = 256
OUTPROJ_TM_FINAL = 512


def _outproj_kernel(split_in, final, n_ctx_tiles, *refs):
    it = iter(refs)
    att_ref, hy_ref, w_ref = next(it), next(it), next(it)
    x_refs = (next(it), next(it)) if split_in else (next(it),)
    gate_ref, g_ref = next(it), next(it)
    shift_ref, scale_ref = (None, None) if final else (next(it), next(it))
    out_refs = (next(it),) if final else (next(it), next(it))
    w_scr = next(it)
    i = pl.program_id(0)

    @pl.when(i == 0)
    def _():
        w_scr[...] = w_ref[0].astype(BF16)

    out = (jnp.dot(att_ref[...], w_scr[0:D_ATT, :], preferred_element_type=F32)
           + jnp.dot(hy_ref[...], w_scr[D_ATT:D_ATT + D_HY, :], preferred_element_type=F32))
    if split_in:
        x = jnp.where(i < n_ctx_tiles, x_refs[0][...], x_refs[1][...])
    else:
        x = x_refs[0][...]
    y = x + gate_ref[0, 0] * out
    inv = lax.rsqrt(jnp.mean(y * y, axis=-1, keepdims=True) + EPS)
    if final:
        out_refs[0][...] = y * inv * g_ref[0]
    else:
        out_refs[0][...] = y
        gain = g_ref[0] * (1.0 + scale_ref[0, 0])
        out_refs[1][...] = (y * inv * gain + shift_ref[0, 0]).astype(BF16)


def _out_proj(att, hy, w_out, layer, x_parts, mod4, gains3, final, tm, tile0=0, n_tiles=None):
    n_tiles = M_ALL // tm if n_tiles is None else n_tiles
    split_in = len(x_parts) == 2
    assert not split_in or (tile0 == 0 and n_tiles == M_ALL // tm)
    row_tile = lambda width: pl.BlockSpec((tm, width), lambda i: (tile0 + i, 0))
    out_tile = pl.BlockSpec((tm, D_MODEL), lambda i: (i, 0))
    x_specs = list(_two_stream_specs(tm, D_MODEL)) if split_in else [row_tile(D_MODEL)]
    in_specs = [
        row_tile(D_ATT),
        row_tile(D_HY),
        pl.BlockSpec((1, D_ATT + D_HY, D_MODEL), lambda i: (layer, 0, 0), pipeline_mode=pl.Buffered(1)),
        *x_specs,
        _mod_spec(tm, layer, MOD_GATE, tile0),
    ]
    args = [att, hy, w_out, *x_parts, mod4]
    if final:
        in_specs.append(pl.BlockSpec((1, 1, D_MODEL), lambda i: (0, 0, 0)))
        args.append(gains3)
        out_specs = [out_tile]
        out_shape = [jax.ShapeDtypeStruct((n_tiles * tm, D_MODEL), F32)]
    else:
        in_specs += [pl.BlockSpec((1, 1, D_MODEL), lambda i: (layer + 1, 0, 0)),
                     _mod_spec(tm, layer + 1, MOD_SHIFT, tile0), _mod_spec(tm, layer + 1, MOD_SCALE, tile0)]
        args += [gains3, mod4, mod4]
        out_specs = [out_tile, out_tile]
        out_shape = [jax.ShapeDtypeStruct((n_tiles * tm, D_MODEL), F32),
                     jax.ShapeDtypeStruct((n_tiles * tm, D_MODEL), BF16)]
    return pl.pallas_call(
        functools.partial(_outproj_kernel, split_in, final, M_CTX // tm),
        grid=(n_tiles,),
        in_specs=in_specs,
        out_specs=out_specs,
        out_shape=out_shape,
        scratch_shapes=[pltpu.VMEM((D_ATT + D_HY, D_MODEL), BF16)],
        compiler_params=_params("arbitrary"),
        name="out_proj_final" if final else "out_proj",
    )(*args)


def _rope_tables():
    t = np.arange(DEC_SEQ)
    row = (t // GRID_W).astype(np.float64)
    col = (t % GRID_W).astype(np.float64)
    pairs = HEAD_DIM // 4
    inv_freq = ROPE_THETA ** (-np.arange(pairs, dtype=np.float64) / pairs)
    ang = np.concatenate([row[:, None] * inv_freq, col[:, None] * inv_freq], axis=-1)
    cos = np.repeat(np.cos(ang), 2, axis=-1).astype(np.float32)
    sin = np.repeat(np.sin(ang), 2, axis=-1).astype(np.float32)
    even = (np.arange(HEAD_DIM) % 2 == 0)[None, :]
    sin_a = np.where(even, -sin, 0.0).astype(np.float32)
    sin_b = np.where(even, 0.0, sin).astype(np.float32)
    return jnp.asarray(cos), jnp.asarray(sin_a), jnp.asarray(sin_b)


def kernel(x_prompt, x_sample, cache_k, cache_v, c, c_ctx, norm_g, w_ada, b_ada, w_in, q_norm_g, k_norm_g,
           conv_w, conv_b, filt_w1, filt_b1, filt_w2, filt_b2, filt_w3, filt_freq, hy_bias, w_out, final_norm_g):
    ctx = x_prompt.reshape(BATCH * SEQ, D_MODEL)
    lat = x_sample.reshape(DEC_BATCH * DEC_SEQ, D_MODEL)
    cache_k4 = cache_k.reshape(DEC_BATCH, DEPTH, PAST_LEN, D_KV)
    cache_v4 = cache_v.reshape(DEC_BATCH, DEPTH, PAST_LEN, D_KV)

    cvec = jnp.concatenate([c_ctx[None, :], c, jnp.zeros((MOD_ROWS - 1 - DEC_BATCH, D_MODEL), F32)], axis=0)
    mod = _modulation(cvec, w_ada, b_ada)

    rope_tabs = _rope_tables()
    dft = {}
    coefs = {}
    for L in (SEQ, DEC_SEQ):
        f_np = _dft_matrix(L)
        f_bf = jnp.asarray(f_np).astype(BF16)
        dft[L] = (f_bf, jnp.asarray(np.ascontiguousarray(f_np.T)).astype(BF16))
        coefs[L] = _filter_spectra(L, filt_w1, filt_b1, filt_w2, filt_b2, filt_freq, filt_w3, f_bf)

    mod4 = mod.reshape(DEPTH, MOD_ROWS, 1, 3 * D_MODEL)
    norm_g3 = norm_g.reshape(DEPTH, 1, D_MODEL)
    final_g3 = final_norm_g.reshape(1, 1, D_MODEL)
    conv_b3 = conv_b.reshape(DEPTH, 1, 3 * D_HY)
    hy_bias3 = hy_bias.reshape(DEPTH, 1, D_HY)

    x_parts = (ctx, lat)
    h = _norm_mod(ctx, lat, norm_g3, mod4, 0)
    kv_out = None
    for l in range(DEPTH):
        qg = q_norm_g[l].reshape(1, HEAD_DIM)
        kg = k_norm_g[l].reshape(1, HEAD_DIM)
        final = l == DEPTH - 1

        proj = _in_proj(h, w_in, l)

        att, new_k, new_v = _attention(proj, qg, kg, l, SEQ, CTX_TQ, BATCH, 0, kv_prev=kv_out, emit_kv=True)
        kv_out = (new_k, new_v)
        (att,) = _attention(proj, qg, kg, l, DEC_SEQ, LAT_TQ, DEC_BATCH, M_CTX, cache=(cache_k4, cache_v4),
                            rope_tabs=rope_tabs, att_prev=att)

        hy = _hyena(proj, SEQ, BATCH, 0, D_HY, l, conv_w, conv_b3, hy_bias3, coefs[SEQ], *dft[SEQ])
        hy = _hyena(proj, DEC_SEQ, DEC_BATCH, M_CTX // DEC_SEQ, 512, l, conv_w, conv_b3, hy_bias3,
                    coefs[DEC_SEQ], *dft[DEC_SEQ], hy_prev=hy)

        if final:
            assert len(x_parts) == 1
            tm = OUTPROJ_TM_FINAL
            nc = M_CTX // tm
            (y_ctx,) = _out_proj(att, hy, w_out, l, x_parts, mod4, final_g3, True, tm, 0, nc)
            (y_lat,) = _out_proj(att, hy, w_out, l, x_parts, mod4, final_g3, True, tm, nc, M_LAT // tm)
        else:
            y, h = _out_proj(att, hy, w_out, l, x_parts, mod4, norm_g3, False, OUTPROJ_TM)
            x_parts = (y,)

    y_prompt = y_ctx.reshape(BATCH, SEQ, D_MODEL)
    y_sample = y_lat.reshape(DEC_BATCH, DEC_SEQ, D_MODEL)
    return (y_prompt, y_sample, kv_out[0], kv_out[1])
```

```python
import functools
import math

import numpy as np
import jax
import jax.numpy as jnp
from jax import lax
from jax.experimental import pallas as pl
from jax.experimental.pallas import tpu as pltpu

D_MODEL = 2048
BATCH = 16
SEQ = 256
DEPTH = 2
DEC_BATCH = 2
DEC_SEQ = 1024
PAST_LEN = 512
GRID_W = 64
D_ATT = 1024
D_HY = 1024
HEAD_DIM = 128
N_HEADS = 8
N_KV_HEADS = 4
GROUP = 2
D_KV = 512
ROPE_THETA = 10000.0
POS_BANDS = 16
POS_EMB = 33
FILT_HID = 64
DECAY_TARGET = 1e-2
FAST_DECAY_PCT = 0.3
SLOW_DECAY_PCT = 1.5
DECAY_SHIFT = 0.05
EPS = 1e-6
D_IN = 7168

COL_Q, COL_K, COL_V, COL_GA, COL_X0, COL_X1, COL_VV, COL_GH = 0, 1024, 1536, 2048, 3072, 4096, 5120, 6144

F32 = jnp.float32
BF16 = jnp.bfloat16

VMEM_LIMIT_BYTES = 56 * 1024 * 1024
FEAT_PAD = 128
def _params(*sem):
    return pltpu.CompilerParams(dimension_semantics=sem, vmem_limit_bytes=VMEM_LIMIT_BYTES)


def _silu(x):
    half = 0.5 * x
    return half + half * jnp.tanh(half)


def _rms(x, g):
    return x * lax.rsqrt(jnp.mean(x * x, axis=-1, keepdims=True) + EPS) * g


MOD_ROWS = 8
MOD_TN = 1024


def _mod_kernel(c_ref, w_ref, b_ref, o_ref):
    s = _silu(c_ref[...]).astype(BF16)
    o_ref[0] = jnp.dot(s, w_ref[0].astype(BF16), preferred_element_type=F32) + b_ref[0]


def _modulation(cvec, w_ada, b_ada):
    n = 3 * D_MODEL
    return pl.pallas_call(
        _mod_kernel,
        grid=(DEPTH, n // MOD_TN),
        in_specs=[
            pl.BlockSpec((MOD_ROWS, D_MODEL), lambda l, j: (0, 0)),
            pl.BlockSpec((1, D_MODEL, MOD_TN), lambda l, j: (l, 0, j)),
            pl.BlockSpec((1, 1, MOD_TN), lambda l, j: (l, 0, j)),
        ],
        out_specs=pl.BlockSpec((1, MOD_ROWS, MOD_TN), lambda l, j: (l, 0, j)),
        out_shape=jax.ShapeDtypeStruct((DEPTH, MOD_ROWS, n), F32),
        compiler_params=_params("arbitrary", "arbitrary"),
        name="adaln_mod",
    )(cvec, w_ada, b_ada.reshape(DEPTH, 1, n))


M_CTX = BATCH * SEQ
M_LAT = DEC_BATCH * DEC_SEQ
M_ALL = M_CTX + M_LAT
NORM_TM = 1024
INPROJ_TM = 3072
INPROJ_TN = 512


ROW_CHUNK = 32


def _modulated_norm_rows(x_ref, h_ref, g, shift, scale):
    gain = g * (1.0 + scale)

    def body(r, _):
        rows = pl.ds(pl.multiple_of(r * ROW_CHUNK, ROW_CHUNK), ROW_CHUNK)
        x = x_ref[rows, :]
        inv = lax.rsqrt(jnp.mean(x * x, axis=-1, keepdims=True) + EPS)
        h_ref[rows, :] = (x * inv * gain + shift).astype(BF16)
        return 0

    lax.fori_loop(0, x_ref.shape[0] // ROW_CHUNK, body, 0, unroll=4)


MOD_SHIFT, MOD_SCALE, MOD_GATE = 0, 1, 2


def _mod_spec(tm, layer, part, tile0=0):
    nc = M_CTX // tm
    per_batch = DEC_SEQ // tm
    row = lambda t: jnp.where(t < nc, 0, 1 + (t - nc) // per_batch)
    return pl.BlockSpec((1, 1, 1, D_MODEL), lambda i: (layer, row(tile0 + i), 0, part))


def _two_stream_specs(tm, width):
    nc = M_CTX // tm
    return (pl.BlockSpec((tm, width), lambda i: (jnp.minimum(i, nc - 1), 0)),
            pl.BlockSpec((tm, width), lambda i: (jnp.maximum(i - nc, 0), 0)))


def _norm_kernel(n_ctx_tiles, xc_ref, xl_ref, g_ref, shift_ref, scale_ref, h_ref):
    def emit(x_ref):
        _modulated_norm_rows(x_ref, h_ref, g_ref[0], shift_ref[0, 0], scale_ref[0, 0])

    i = pl.program_id(0)
    pl.when(i < n_ctx_tiles)(lambda: emit(xc_ref))
    pl.when(i >= n_ctx_tiles)(lambda: emit(xl_ref))


def _norm_mod(x_ctx, x_lat, norm_g3, mod4, layer):
    xc_spec, xl_spec = _two_stream_specs(NORM_TM, D_MODEL)
    return pl.pallas_call(
        functools.partial(_norm_kernel, M_CTX // NORM_TM),
        grid=(M_ALL // NORM_TM,),
        in_specs=[
            xc_spec, xl_spec,
            pl.BlockSpec((1, 1, D_MODEL), lambda i: (layer, 0, 0)),
            _mod_spec(NORM_TM, layer, MOD_SHIFT),
            _mod_spec(NORM_TM, layer, MOD_SCALE),
        ],
        out_specs=pl.BlockSpec((NORM_TM, D_MODEL), lambda i: (i, 0)),
        out_shape=jax.ShapeDtypeStruct((M_ALL, D_MODEL), BF16),
        compiler_params=_params("arbitrary"),
        name="norm_mod",
    )(x_ctx, x_lat, norm_g3, mod4, mod4)


def _inproj_kernel(h_ref, w_ref, o_ref, w_scr):
    i = pl.program_id(1)

    @pl.when(i == 0)
    def _():
        w_scr[...] = w_ref[0].astype(BF16)

    rows = pl.ds(pl.multiple_of(i * INPROJ_TM, INPROJ_TM), INPROJ_TM)
    o_ref[...] = jnp.dot(h_ref[rows, :], w_scr[...], preferred_element_type=F32)


def _in_proj(h, w_in, layer):
    return pl.pallas_call(
        _inproj_kernel,
        grid=(D_IN // INPROJ_TN, M_ALL // INPROJ_TM),
        in_specs=[
            pl.BlockSpec((M_ALL, D_MODEL), lambda j, i: (0, 0), pipeline_mode=pl.Buffered(1)),
            pl.BlockSpec((1, D_MODEL, INPROJ_TN), lambda j, i: (layer, 0, j)),
        ],
        out_specs=pl.BlockSpec((INPROJ_TM, INPROJ_TN), lambda j, i: (i, j)),
        out_shape=jax.ShapeDtypeStruct((M_ALL, D_IN), F32),
        scratch_shapes=[pltpu.VMEM((D_MODEL, INPROJ_TN), BF16)],
        compiler_params=_params("arbitrary", "arbitrary"),
        name="in_proj",
    )(h, w_in)


ATT_SCALE = 1.0 / math.sqrt(HEAD_DIM)


def _head(ref_or_val, h):
    return ref_or_val[:, h * HEAD_DIM:(h + 1) * HEAD_DIM]


Q_PRESCALE = ATT_SCALE * math.log2(math.e)


def _softmax_pv(q2, k_bf, v_ext):
    s = lax.dot_general(q2, k_bf, (((1,), (1,)), ((), ())), preferred_element_type=F32)
    p = jnp.exp2(s - jnp.max(s, axis=-1, keepdims=True)).astype(BF16)
    o = jnp.dot(p, v_ext, preferred_element_type=F32)
    return o[:, 0:HEAD_DIM] / o[:, HEAD_DIM:2 * HEAD_DIM]


def _rope(x, cos2, sin_a, sin_b):
    nxt = pltpu.roll(x, HEAD_DIM - 1, axis=1)
    prv = pltpu.roll(x, 1, axis=1)
    return x * cos2 + nxt * sin_a + prv * sin_b


CTX_TQ = SEQ
LAT_TQ = 128


def _attn_kernel(rope, n_cache, emit_kv_layer, n_aliased, *refs):
    emit_kv = emit_kv_layer is not None
    it = iter(refs)
    q_ref, kv_ref, g_ref, qg_ref, kg_ref = (next(it) for _ in range(5))
    ck_ref, cv_ref = (next(it), next(it)) if n_cache else (None, None)
    q_tabs = tuple(next(it) for _ in range(3)) if rope else None
    k_tabs = tuple(next(it) for _ in range(3)) if rope else None
    for _ in range(n_aliased):
        next(it)
    att_ref = next(it)
    ko_hbm, vo_hbm = (next(it), next(it)) if emit_kv else (None, None)
    k_scr, v_scr = next(it), next(it)
    kf_scr, kv_sems = (next(it), next(it)) if emit_kv else (None, None)
    n_new = kv_ref.shape[0]
    assert not emit_kv or q_ref.shape[0] == n_new

    def kv_copy(h, is_value):
        b = pl.program_id(0)
        if is_value:
            src = kv_ref.at[:, pl.ds((N_KV_HEADS + h) * HEAD_DIM, HEAD_DIM)]
            return pltpu.make_async_copy(src, vo_hbm.at[b, emit_kv_layer, :, h, :], kv_sems.at[N_KV_HEADS + h])
        src = kf_scr.at[:, pl.ds(h * HEAD_DIM, HEAD_DIM)]
        return pltpu.make_async_copy(src, ko_hbm.at[b, emit_kv_layer, :, h, :], kv_sems.at[h])

    def prepare_keys_values():
        if emit_kv:
            for h in range(N_KV_HEADS):
                kv_copy(h, True).start()
        for h in range(N_KV_HEADS):
            kn = _rms(_head(kv_ref, h), kg_ref[...])
            vh = _head(kv_ref, N_KV_HEADS + h)
            if emit_kv:
                kf_scr[:, h * HEAD_DIM:(h + 1) * HEAD_DIM] = kn
            if rope:
                kn = _rope(kn, *(t[...] for t in k_tabs))
            k_scr[0:n_new, h * HEAD_DIM:(h + 1) * HEAD_DIM] = kn.astype(BF16)
            v0 = 2 * h * HEAD_DIM
            v_scr[0:n_new, v0:v0 + HEAD_DIM] = vh.astype(BF16)
            if n_cache:
                v_scr[n_new:n_new + n_cache, v0:v0 + HEAD_DIM] = _head(cv_ref[0, 0], h).astype(BF16)
            v_scr[:, v0 + HEAD_DIM:v0 + 2 * HEAD_DIM] = jnp.ones((n_new + n_cache, HEAD_DIM), BF16)
        if n_cache:
            k_scr[n_new:n_new + n_cache, :] = ck_ref[0, 0].astype(BF16)
        if emit_kv:
            for h in range(N_KV_HEADS):
                kv_copy(h, False).start()

    if q_ref.shape[0] == n_new:
        prepare_keys_values()
    else:
        pl.when(pl.program_id(1) == 0)(prepare_keys_values)

    def query(hq):
        x = _rms(_head(q_ref, hq), qg_ref[...])
        if rope:
            x = _rope(x, *(t[...] for t in q_tabs))
        return x * Q_PRESCALE

    rows = q_ref.shape[0]
    for h in range(N_KV_HEADS):
        q2 = jnp.concatenate([query(GROUP * h + g) for g in range(GROUP)], axis=0).astype(BF16)
        o = _softmax_pv(q2, _head(k_scr, h), v_scr[:, 2 * h * HEAD_DIM:2 * (h + 1) * HEAD_DIM])
        for g in range(GROUP):
            hq = GROUP * h + g
            gate = _silu(_head(g_ref, hq))
            att_ref[:, hq * HEAD_DIM:(hq + 1) * HEAD_DIM] = (o[g * rows:(g + 1) * rows] * gate).astype(BF16)

    if emit_kv:
        for h in range(N_KV_HEADS):
            kv_copy(h, False).wait()
            kv_copy(h, True).wait()


def _attention(proj, q_g, k_g, layer, seq, tq, n_batch, row0, *, cache=None, rope_tabs=None, att_prev=None,
               kv_prev=None, emit_kv=False):
    nq = seq // tq
    q0, kv0 = row0 // tq, row0 // seq
    n_cache = 0 if cache is None else cache[0].shape[2]
    vec = pl.BlockSpec((1, HEAD_DIM), lambda b, i: (0, 0))
    in_specs = [
        pl.BlockSpec((tq, D_ATT), lambda b, i: (q0 + b * nq + i, COL_Q // D_ATT)),
        pl.BlockSpec((seq, 2 * D_KV), lambda b, i: (kv0 + b, COL_K // (2 * D_KV))),
        pl.BlockSpec((tq, D_ATT), lambda b, i: (q0 + b * nq + i, COL_GA // D_ATT)),
        vec, vec,
    ]
    args = [proj, proj, proj, q_g, k_g]
    if cache is not None:
        in_specs += [pl.BlockSpec((1, 1, n_cache, D_KV), lambda b, i: (b, layer, 0, 0))] * 2
        args += list(cache)
    if rope_tabs is not None:
        in_specs += [pl.BlockSpec((tq, HEAD_DIM), lambda b, i: (i, 0))] * 3
        in_specs += [pl.BlockSpec((seq, HEAD_DIM), lambda b, i: (0, 0))] * 3
        args += list(rope_tabs) * 2
    aliased = ([] if att_prev is None else [att_prev]) + ([] if kv_prev is None else list(kv_prev))
    out_first = 0 if att_prev is not None else 1
    aliases = {len(args) + n: out_first + n for n in range(len(aliased))}
    in_specs += [pl.BlockSpec(memory_space=pl.ANY)] * len(aliased)
    args += aliased
    out_specs = [pl.BlockSpec((tq, D_ATT), lambda b, i: (q0 + b * nq + i, 0))]
    out_shape = [jax.ShapeDtypeStruct((M_ALL, D_ATT), BF16)]
    scratch = [pltpu.VMEM((seq + n_cache, D_KV), BF16), pltpu.VMEM((seq + n_cache, 2 * D_KV), BF16)]
    if emit_kv:
        out_specs += [pl.BlockSpec(memory_space=pl.ANY)] * 2
        out_shape += [jax.ShapeDtypeStruct((n_batch, DEPTH, seq, N_KV_HEADS, HEAD_DIM), F32)] * 2
        scratch += [pltpu.VMEM((seq, D_KV), F32), pltpu.SemaphoreType.DMA((2 * N_KV_HEADS,))]
    return pl.pallas_call(
        functools.partial(_attn_kernel, rope_tabs is not None, n_cache, layer if emit_kv else None, len(aliased)),
        grid=(n_batch, nq),
        in_specs=in_specs,
        out_specs=out_specs,
        out_shape=out_shape,
        input_output_aliases=aliases,
        scratch_shapes=scratch,
        compiler_params=_params("arbitrary", "arbitrary"),
        name=f"attention_{seq}",
    )(*args)


def _dft_matrix(L):
    k = np.arange(L, dtype=np.int64)[:, None]
    t = np.arange(L, dtype=np.int64)[None, :]
    ang = 2.0 * np.pi * ((k * t) % (2 * L)).astype(np.float64) / (2 * L)
    top = np.cos(ang)
    bot = np.sin(ang)
    bot[0, :] = np.where(np.arange(L) % 2 == 0, 1.0, -1.0)
    return np.concatenate([top, bot], axis=0).astype(np.float32)


def _filter_kernel(L, z_ref, w1_ref, b1_ref, w2_ref, b2_ref, fr_ref, w3f_ref, w3b_ref, dec_ref,
                   f_ref, o_ref, hdn_scr):
    hp = lax.Precision.HIGHEST

    @pl.when(pl.program_id(1) == 0)
    def _():
        fr = fr_ref[0]
        h1 = jnp.sin(fr * (jnp.dot(z_ref[...], w1_ref[0], precision=hp, preferred_element_type=F32) + b1_ref[0]))
        hdn_scr[...] = jnp.sin(fr * (jnp.dot(h1, w2_ref[0], precision=hp, preferred_element_type=F32) + b2_ref[0]))

    hdn = hdn_scr[...]
    dec = dec_ref[...]
    hdn_bf = hdn.astype(BF16)
    h_f = jnp.dot(hdn_bf, w3f_ref[0].astype(BF16), preferred_element_type=F32) * dec
    h_b = jnp.dot(hdn_bf, w3b_ref[0].astype(BF16), preferred_element_type=F32) * dec
    hs = h_f + h_b
    ha = jnp.dot(f_ref[0:L, :], hs.astype(BF16), preferred_element_type=F32)
    hb = jnp.dot(f_ref[L:2 * L, :], (h_f - h_b).astype(BF16), preferred_element_type=F32)
    row = lax.broadcasted_iota(jnp.int32, (L, 1), 0)
    first = row == 0
    nyquist = jnp.sum(jnp.where(row % 2 == 0, hs, -hs), axis=0, keepdims=True)
    wk = jnp.where(first, 1.0 / (2 * L), 2.0 / (2 * L))
    o_ref[0, 0] = ha * wk
    o_ref[0, 1] = jnp.where(first, 0.0, hb * wk)
    o_ref[0, 2] = jnp.where(first, nyquist, ha) * wk


FILT_CB = 512


def _filter_spectra(L, w1, b1, w2, b2, freq, w3, f_bf):
    tpos = np.arange(L, dtype=np.float64)
    t_norm = tpos / max(L - 1, 1)
    w = 2.0 * math.pi * tpos / L
    bands = np.linspace(1e-4, POS_BANDS - 1, POS_BANDS)
    z = np.concatenate([t_norm[:, None], np.cos(w[:, None] * bands), -np.sin(w[:, None] * bands)], axis=-1)
    z = np.pad(z, ((0, 0), (0, FEAT_PAD - POS_EMB))).astype(np.float32)
    max_decay = math.log(DECAY_TARGET) / FAST_DECAY_PCT
    min_decay = math.log(DECAY_TARGET) / SLOW_DECAY_PCT
    deltas = np.abs(np.linspace(min_decay, max_decay, D_HY))
    dec = (np.exp(-t_norm[:, None] * deltas) + DECAY_SHIFT).astype(np.float32)

    pad_h = FEAT_PAD - FILT_HID
    w1p = jnp.pad(w1, ((0, 0), (0, FEAT_PAD - POS_EMB), (0, pad_h)))
    w2p = jnp.pad(w2, ((0, 0), (0, pad_h), (0, pad_h)))
    w3p = jnp.pad(w3, ((0, 0), (0, pad_h), (0, 0)))
    vec = lambda v: jnp.pad(v, ((0, 0), (0, pad_h))).reshape(DEPTH, 1, FEAT_PAD)
    ncb = D_HY // FILT_CB
    small = lambda shape: pl.BlockSpec((1,) + shape, lambda l, c: (l, 0, 0))
    return pl.pallas_call(
        functools.partial(_filter_kernel, L),
        grid=(DEPTH, ncb),
        in_specs=[
            pl.BlockSpec((L, FEAT_PAD), lambda l, c: (0, 0)),
            small((FEAT_PAD, FEAT_PAD)), small((1, FEAT_PAD)),
            small((FEAT_PAD, FEAT_PAD)), small((1, FEAT_PAD)), small((1, FEAT_PAD)),
            pl.BlockSpec((1, FEAT_PAD, FILT_CB), lambda l, c: (l, 0, c)),
            pl.BlockSpec((1, FEAT_PAD, FILT_CB), lambda l, c: (l, 0, ncb + c)),
            pl.BlockSpec((L, FILT_CB), lambda l, c: (0, c)),
            pl.BlockSpec((2 * L, L), lambda l, c: (0, 0)),
        ],
        out_specs=pl.BlockSpec((1, 3, L, FILT_CB), lambda l, c: (l, 0, 0, c)),
        out_shape=jax.ShapeDtypeStruct((DEPTH, 3, L, D_HY), F32),
        scratch_shapes=[pltpu.VMEM((L, FEAT_PAD), F32)],
        compiler_params=_params("arbitrary", "arbitrary"),
        name=f"hyena_filter_{L}",
    )(jnp.asarray(z), w1p, vec(b1), w2p, vec(b2), vec(freq), w3p, w3p, jnp.asarray(dec), f_bf)


def _hyena_kernel(L, x0_ref, x1_ref, vv_ref, g_ref, cw0_ref, cw1_ref, cwv_ref, cb0_ref, cb1_ref,
                  cbv_ref, bias_ref, coef_ref, f_ref, ft_ref, *rest):
    o_ref = rest[-1]
    row = lax.broadcasted_iota(jnp.int32, (L, 1), 0)
    is_first = row == 0
    is_last = row == L - 1

    def sconv(x_ref, w_ref, b_ref):
        x = x_ref[...]
        prev = jnp.where(is_first, 0.0, pltpu.roll(x, 1, axis=0))
        nxt = jnp.where(is_last, 0.0, pltpu.roll(x, L - 1, axis=0))
        return w_ref[0, 0:1, :] * prev + w_ref[0, 1:2, :] * x + w_ref[0, 2:3, :] * nxt + b_ref[0]

    z = sconv(vv_ref, cwv_ref, cbv_ref) * sconv(x1_ref, cw1_ref, cb1_ref)
    ab = jnp.dot(f_ref[...], z.astype(BF16), preferred_element_type=F32)
    a, b = ab[0:L], ab[L:2 * L]
    g1, g2, g3 = coef_ref[0, 0], coef_ref[0, 1], coef_ref[0, 2]
    pq = jnp.concatenate([a * g1 - b * g2, a * g2 + b * g3], axis=0).astype(BF16)
    y = jnp.dot(ft_ref[...], pq, preferred_element_type=F32) + bias_ref[0] * z
    o_ref[...] = (sconv(x0_ref, cw0_ref, cb0_ref) * y * _silu(g_ref[...])).astype(BF16)


def _hyena(proj, L, nb, row0, cb, layer, conv_w, conv_b, hy_bias, coef, f_bf, ft_bf, hy_prev=None):
    ncb = D_HY // cb
    extra_specs = [] if hy_prev is None else [pl.BlockSpec(memory_space=pl.ANY)]
    extra_args = [] if hy_prev is None else [hy_prev]
    aliases = {} if hy_prev is None else {14: 0}
    col = lambda off: (lambda c, b: (row0 + b, off // cb + c))
    cw = lambda part: pl.BlockSpec((1, 3, cb), lambda c, b: (layer, 0, part * ncb + c))
    cbias = lambda part: pl.BlockSpec((1, 1, cb), lambda c, b: (layer, 0, part * ncb + c))
    once = pl.Buffered(1)
    return pl.pallas_call(
        functools.partial(_hyena_kernel, L),
        grid=(ncb, nb),
        in_specs=[
            pl.BlockSpec((L, cb), col(COL_X0)),
            pl.BlockSpec((L, cb), col(COL_X1)),
            pl.BlockSpec((L, cb), col(COL_VV)),
            pl.BlockSpec((L, cb), col(COL_GH)),
            cw(0), cw(1), cw(2), cbias(0), cbias(1), cbias(2),
            pl.BlockSpec((1, 1, cb), lambda c, b: (layer, 0, c)),
            pl.BlockSpec((1, 3, L, cb), lambda c, b: (layer, 0, 0, c)),
            pl.BlockSpec((2 * L, L), lambda c, b: (0, 0), pipeline_mode=once),
            pl.BlockSpec((L, 2 * L), lambda c, b: (0, 0), pipeline_mode=once),
            *extra_specs,
        ],
        out_specs=pl.BlockSpec((L, cb), lambda c, b: (row0 + b, c)),
        out_shape=jax.ShapeDtypeStruct((M_ALL, D_HY), BF16),
        input_output_aliases=aliases,
        compiler_params=_params("arbitrary", "arbitrary"),
        name=f"hyena_{L}",
    )(proj, proj, proj, proj, conv_w, conv_w, conv_w, conv_b, conv_b, conv_b, hy_bias, coef, f_bf, ft_bf,
      *extra_args)


OUTPROJ_TM = 256
OUTPROJ_TM_FINAL = 512


def _outproj_kernel(split_in, final, n_ctx_tiles, *refs):
    it = iter(refs)
    att_ref, hy_ref, w_ref = next(it), next(it), next(it)
    x_refs = (next(it), next(it)) if split_in else (next(it),)
    gate_ref, g_ref = next(it), next(it)
    shift_ref, scale_ref = (None, None) if final else (next(it), next(it))
    out_refs = (next(it),) if final else (next(it), next(it))
    w_scr = next(it)
    i = pl.program_id(0)

    @pl.when(i == 0)
    def _():
        w_scr[...] = w_ref[0].astype(BF16)

    out = (jnp.dot(att_ref[...], w_scr[0:D_ATT, :], preferred_element_type=F32)
           + jnp.dot(hy_ref[...], w_scr[D_ATT:D_ATT + D_HY, :], preferred_element_type=F32))
    if split_in:
        x = jnp.where(i < n_ctx_tiles, x_refs[0][...], x_refs[1][...])
    else:
        x = x_refs[0][...]
    y = x + gate_ref[0, 0] * out
    inv = lax.rsqrt(jnp.mean(y * y, axis=-1, keepdims=True) + EPS)
    if final:
        out_refs[0][...] = y * inv * g_ref[0]
    else:
        out_refs[0][...] = y
        gain = g_ref[0] * (1.0 + scale_ref[0, 0])
        out_refs[1][...] = (y * inv * gain + shift_ref[0, 0]).astype(BF16)


def _out_proj(att, hy, w_out, layer, x_parts, mod4, gains3, final, tm, tile0=0, n_tiles=None):
    n_tiles = M_ALL // tm if n_tiles is None else n_tiles
    split_in = len(x_parts) == 2
    assert not split_in or (tile0 == 0 and n_tiles == M_ALL // tm)
    row_tile = lambda width: pl.BlockSpec((tm, width), lambda i: (tile0 + i, 0))
    out_tile = pl.BlockSpec((tm, D_MODEL), lambda i: (i, 0))
    x_specs = list(_two_stream_specs(tm, D_MODEL)) if split_in else [row_tile(D_MODEL)]
    in_specs = [
        row_tile(D_ATT),
        row_tile(D_HY),
        pl.BlockSpec((1, D_ATT + D_HY, D_MODEL), lambda i: (layer, 0, 0), pipeline_mode=pl.Buffered(1)),
        *x_specs,
        _mod_spec(tm, layer, MOD_GATE, tile0),
    ]
    args = [att, hy, w_out, *x_parts, mod4]
    if final:
        in_specs.append(pl.BlockSpec((1, 1, D_MODEL), lambda i: (0, 0, 0)))
        args.append(gains3)
        out_specs = [out_tile]
        out_shape = [jax.ShapeDtypeStruct((n_tiles * tm, D_MODEL), F32)]
    else:
        in_specs += [pl.BlockSpec((1, 1, D_MODEL), lambda i: (layer + 1, 0, 0)),
                     _mod_spec(tm, layer + 1, MOD_SHIFT, tile0), _mod_spec(tm, layer + 1, MOD_SCALE, tile0)]
        args += [gains3, mod4, mod4]
        out_specs = [out_tile, out_tile]
        out_shape = [jax.ShapeDtypeStruct((n_tiles * tm, D_MODEL), F32),
                     jax.ShapeDtypeStruct((n_tiles * tm, D_MODEL), BF16)]
    return pl.pallas_call(
        functools.partial(_outproj_kernel, split_in, final, M_CTX // tm),
        grid=(n_tiles,),
        in_specs=in_specs,
        out_specs=out_specs,
        out_shape=out_shape,
        scratch_shapes=[pltpu.VMEM((D_ATT + D_HY, D_MODEL), BF16)],
        compiler_params=_params("arbitrary"),
        name="out_proj_final" if final else "out_proj",
    )(*args)


def _rope_tables():
    t = np.arange(DEC_SEQ)
    row = (t // GRID_W).astype(np.float64)
    col = (t % GRID_W).astype(np.float64)
    pairs = HEAD_DIM // 4
    inv_freq = ROPE_THETA ** (-np.arange(pairs, dtype=np.float64) / pairs)
    ang = np.concatenate([row[:, None] * inv_freq, col[:, None] * inv_freq], axis=-1)
    cos = np.repeat(np.cos(ang), 2, axis=-1).astype(np.float32)
    sin = np.repeat(np.sin(ang), 2, axis=-1).astype(np.float32)
    even = (np.arange(HEAD_DIM) % 2 == 0)[None, :]
    sin_a = np.where(even, -sin, 0.0).astype(np.float32)
    sin_b = np.where(even, 0.0, sin).astype(np.float32)
    return jnp.asarray(cos), jnp.asarray(sin_a), jnp.asarray(sin_b)


def kernel(x_prompt, x_sample, cache_k, cache_v, c, c_ctx, norm_g, w_ada, b_ada, w_in, q_norm_g, k_norm_g,
           conv_w, conv_b, filt_w1, filt_b1, filt_w2, filt_b2, filt_w3, filt_freq, hy_bias, w_out, final_norm_g):
    ctx = x_prompt.reshape(BATCH * SEQ, D_MODEL)
    lat = x_sample.reshape(DEC_BATCH * DEC_SEQ, D_MODEL)
    cache_k4 = cache_k.reshape(DEC_BATCH, DEPTH, PAST_LEN, D_KV)
    cache_v4 = cache_v.reshape(DEC_BATCH, DEPTH, PAST_LEN, D_KV)

    cvec = jnp.concatenate([c_ctx[None, :], c, jnp.zeros((MOD_ROWS - 1 - DEC_BATCH, D_MODEL), F32)], axis=0)
    mod = _modulation(cvec, w_ada, b_ada)

    rope_tabs = _rope_tables()
    dft = {}
    coefs = {}
    for L in (SEQ, DEC_SEQ):
        f_np = _dft_matrix(L)
        f_bf = jnp.asarray(f_np).astype(BF16)
        dft[L] = (f_bf, jnp.asarray(np.ascontiguousarray(f_np.T)).astype(BF16))
        coefs[L] = _filter_spectra(L, filt_w1, filt_b1, filt_w2, filt_b2, filt_freq, filt_w3, f_bf)

    mod4 = mod.reshape(DEPTH, MOD_ROWS, 1, 3 * D_MODEL)
    norm_g3 = norm_g.reshape(DEPTH, 1, D_MODEL)
    final_g3 = final_norm_g.reshape(1, 1, D_MODEL)
    conv_b3 = conv_b.reshape(DEPTH, 1, 3 * D_HY)
    hy_bias3 = hy_bias.reshape(DEPTH, 1, D_HY)

    x_parts = (ctx, lat)
    h = _norm_mod(ctx, lat, norm_g3, mod4, 0)
    kv_out = None
    for l in range(DEPTH):
        qg = q_norm_g[l].reshape(1, HEAD_DIM)
        kg = k_norm_g[l].reshape(1, HEAD_DIM)
        final = l == DEPTH - 1

        proj = _in_proj(h, w_in, l)

        att, new_k, new_v = _attention(proj, qg, kg, l, SEQ, CTX_TQ, BATCH, 0, kv_prev=kv_out, emit_kv=True)
        kv_out = (new_k, new_v)
        (att,) = _attention(proj, qg, kg, l, DEC_SEQ, LAT_TQ, DEC_BATCH, M_CTX, cache=(cache_k4, cache_v4),
                            rope_tabs=rope_tabs, att_prev=att)

        hy = _hyena(proj, SEQ, BATCH, 0, D_HY, l, conv_w, conv_b3, hy_bias3, coefs[SEQ], *dft[SEQ])
        hy = _hyena(proj, DEC_SEQ, DEC_BATCH, M_CTX // DEC_SEQ, 512, l, conv_w, conv_b3, hy_bias3,
                    coefs[DEC_SEQ], *dft[DEC_SEQ], hy_prev=hy)

        if final:
            assert len(x_parts) == 1
            tm = OUTPROJ_TM_FINAL
            nc = M_CTX // tm
            (y_ctx,) = _out_proj(att, hy, w_out, l, x_parts, mod4, final_g3, True, tm, 0, nc)
            (y_lat,) = _out_proj(att, hy, w_out, l, x_parts, mod4, final_g3, True, tm, nc, M_LAT // tm)
        else:
            y, h = _out_proj(att, hy, w_out, l, x_parts, mod4, norm_g3, False, OUTPROJ_TM)
            x_parts = (y,)

    y_prompt = y_ctx.reshape(BATCH, SEQ, D_MODEL)
    y_sample = y_lat.reshape(DEC_BATCH, DEC_SEQ, D_MODEL)
    return (y_prompt, y_sample, kv_out[0], kv_out[1])
```

```python
import functools
import math

import numpy as np
import jax
import jax.numpy as jnp
from jax import lax
from jax.experimental import pallas as pl
from jax.experimental.pallas import tpu as pltpu

D_MODEL = 2048
BATCH = 16
SEQ = 256
DEPTH = 2
DEC_BATCH = 2
DEC_SEQ = 1024
PAST_LEN = 512
GRID_W = 64
D_ATT = 1024
D_HY = 1024
HEAD_DIM = 128
N_HEADS = 8
N_KV_HEADS = 4
GROUP = 2
D_KV = 512
ROPE_THETA = 10000.0
POS_BANDS = 16
POS_EMB = 33
FILT_HID = 64
DECAY_TARGET = 1e-2
FAST_DECAY_PCT = 0.3
SLOW_DECAY_PCT = 1.5
DECAY_SHIFT = 0.05
EPS = 1e-6
D_IN = 7168

COL_Q, COL_K, COL_V, COL_GA, COL_X0, COL_X1, COL_VV, COL_GH = 0, 1024, 1536, 2048, 3072, 4096, 5120, 6144

F32 = jnp.float32
BF16 = jnp.bfloat16

VMEM_LIMIT_BYTES = 56 * 1024 * 1024
FEAT_PAD = 128
def _params(*sem):
    return pltpu.CompilerParams(dimension_semantics=sem, vmem_limit_bytes=VMEM_LIMIT_BYTES)


def _silu(x):
    half = 0.5 * x
    return half + half * jnp.tanh(half)


def _rms(x, g):
    return x * lax.rsqrt(jnp.mean(x * x, axis=-1, keepdims=True) + EPS) * g


MOD_ROWS = 8
MOD_TN = 1024


def _mod_kernel(c_ref, w_ref, b_ref, o_ref):
    s = _silu(c_ref[...]).astype(BF16)
    o_ref[0] = jnp.dot(s, w_ref[0].astype(BF16), preferred_element_type=F32) + b_ref[0]


def _modulation(cvec, w_ada, b_ada):
    n = 3 * D_MODEL
    return pl.pallas_call(
        _mod_kernel,
        grid=(DEPTH, n // MOD_TN),
        in_specs=[
            pl.BlockSpec((MOD_ROWS, D_MODEL), lambda l, j: (0, 0)),
            pl.BlockSpec((1, D_MODEL, MOD_TN), lambda l, j: (l, 0, j)),
            pl.BlockSpec((1, 1, MOD_TN), lambda l, j: (l, 0, j)),
        ],
        out_specs=pl.BlockSpec((1, MOD_ROWS, MOD_TN), lambda l, j: (l, 0, j)),
        out_shape=jax.ShapeDtypeStruct((DEPTH, MOD_ROWS, n), F32),
        compiler_params=_params("arbitrary", "arbitrary"),
        name="adaln_mod",
    )(cvec, w_ada, b_ada.reshape(DEPTH, 1, n))


M_CTX = BATCH * SEQ
M_LAT = DEC_BATCH * DEC_SEQ
M_ALL = M_CTX + M_LAT
NORM_TM = 1024
INPROJ_TM = 3072
INPROJ_TN = 512


ROW_CHUNK = 32


def _modulated_norm_rows(x_ref, h_ref, g, shift, scale):
    gain = g * (1.0 + scale)

    def body(r, _):
        rows = pl.ds(pl.multiple_of(r * ROW_CHUNK, ROW_CHUNK), ROW_CHUNK)
        x = x_ref[rows, :]
        inv = lax.rsqrt(jnp.mean(x * x, axis=-1, keepdims=True) + EPS)
        h_ref[rows, :] = (x * inv * gain + shift).astype(BF16)
        return 0

    lax.fori_loop(0, x_ref.shape[0] // ROW_CHUNK, body, 0, unroll=4)


MOD_SHIFT, MOD_SCALE, MOD_GATE = 0, 1, 2


def _mod_spec(tm, layer, part, tile0=0):
    nc = M_CTX // tm
    per_batch = DEC_SEQ // tm
    row = lambda t: jnp.where(t < nc, 0, 1 + (t - nc) // per_batch)
    return pl.BlockSpec((1, 1, 1, D_MODEL), lambda i: (layer, row(tile0 + i), 0, part))


def _two_stream_specs(tm, width):
    nc = M_CTX // tm
    return (pl.BlockSpec((tm, width), lambda i: (jnp.minimum(i, nc - 1), 0)),
            pl.BlockSpec((tm, width), lambda i: (jnp.maximum(i - nc, 0), 0)))


def _norm_kernel(n_ctx_tiles, xc_ref, xl_ref, g_ref, shift_ref, scale_ref, h_ref):
    def emit(x_ref):
        _modulated_norm_rows(x_ref, h_ref, g_ref[0], shift_ref[0, 0], scale_ref[0, 0])

    i = pl.program_id(0)
    pl.when(i < n_ctx_tiles)(lambda: emit(xc_ref))
    pl.when(i >= n_ctx_tiles)(lambda: emit(xl_ref))


def _norm_mod(x_ctx, x_lat, norm_g3, mod4, layer):
    xc_spec, xl_spec = _two_stream_specs(NORM_TM, D_MODEL)
    return pl.pallas_call(
        functools.partial(_norm_kernel, M_CTX // NORM_TM),
        grid=(M_ALL // NORM_TM,),
        in_specs=[
            xc_spec, xl_spec,
            pl.BlockSpec((1, 1, D_MODEL), lambda i: (layer, 0, 0)),
            _mod_spec(NORM_TM, layer, MOD_SHIFT),
            _mod_spec(NORM_TM, layer, MOD_SCALE),
        ],
        out_specs=pl.BlockSpec((NORM_TM, D_MODEL), lambda i: (i, 0)),
        out_shape=jax.ShapeDtypeStruct((M_ALL, D_MODEL), BF16),
        compiler_params=_params("arbitrary"),
        name="norm_mod",
    )(x_ctx, x_lat, norm_g3, mod4, mod4)


def _inproj_kernel(h_ref, w_ref, o_ref, w_scr):
    i = pl.program_id(1)

    @pl.when(i == 0)
    def _():
        w_scr[...] = w_ref[0].astype(BF16)

    rows = pl.ds(pl.multiple_of(i * INPROJ_TM, INPROJ_TM), INPROJ_TM)
    o_ref[...] = jnp.dot(h_ref[rows, :], w_scr[...], preferred_element_type=F32)


def _in_proj(h, w_in, layer):
    return pl.pallas_call(
        _inproj_kernel,
        grid=(D_IN // INPROJ_TN, M_ALL // INPROJ_TM),
        in_specs=[
            pl.BlockSpec((M_ALL, D_MODEL), lambda j, i: (0, 0), pipeline_mode=pl.Buffered(1)),
            pl.BlockSpec((1, D_MODEL, INPROJ_TN), lambda j, i: (layer, 0, j)),
        ],
        out_specs=pl.BlockSpec((INPROJ_TM, INPROJ_TN), lambda j, i: (i, j)),
        out_shape=jax.ShapeDtypeStruct((M_ALL, D_IN), F32),
        scratch_shapes=[pltpu.VMEM((D_MODEL, INPROJ_TN), BF16)],
        compiler_params=_params("arbitrary", "arbitrary"),
        name="in_proj",
    )(h, w_in)


ATT_SCALE = 1.0 / math.sqrt(HEAD_DIM)


def _head(ref_or_val, h):
    return ref_or_val[:, h * HEAD_DIM:(h + 1) * HEAD_DIM]


Q_PRESCALE = ATT_SCALE * math.log2(math.e)


def _softmax_pv(q2, k_bf, v_ext):
    s = lax.dot_general(q2, k_bf, (((1,), (1,)), ((), ())), preferred_element_type=F32)
    p = jnp.exp2(s - jnp.max(s, axis=-1, keepdims=True)).astype(BF16)
    o = jnp.dot(p, v_ext, preferred_element_type=F32)
    return o[:, 0:HEAD_DIM] / o[:, HEAD_DIM:2 * HEAD_DIM]


def _rope(x, cos2, sin_a, sin_b):
    nxt = pltpu.roll(x, HEAD_DIM - 1, axis=1)
    prv = pltpu.roll(x, 1, axis=1)
    return x * cos2 + nxt * sin_a + prv * sin_b


CTX_TQ = SEQ
LAT_TQ = 128


def _attn_kernel(rope, n_cache, emit_kv_layer, n_aliased, nsb, *refs):
    emit_kv = emit_kv_layer is not None
    it = iter(refs)
    q_all, kv_all, g_all, qg_ref, kg_ref = (next(it) for _ in range(5))
    ck_ref, cv_ref = (next(it), next(it)) if n_cache else (None, None)
    q_tabs = tuple(next(it) for _ in range(3)) if rope else None
    k_tabs = tuple(next(it) for _ in range(3)) if rope else None
    for _ in range(n_aliased):
        next(it)
    att_all = next(it)
    ko_hbm, vo_hbm = (next(it), next(it)) if emit_kv else (None, None)
    k_all, v_all = next(it), next(it)
    kf_all, kv_sems = (next(it), next(it)) if emit_kv else (None, None)
    n_new = kv_all.shape[0] // nsb
    single_tile = q_all.shape[0] == kv_all.shape[0]
    assert single_tile or nsb == 1
    assert not emit_kv or single_tile

    def part(ref, sb, rows_per_seq):
        return ref if nsb == 1 else ref.at[pl.ds(sb * rows_per_seq, rows_per_seq)]

    def kv_copy(sb, h, is_value):
        b = pl.program_id(0) * nsb + sb
        if is_value:
            src = part(kv_all, sb, n_new).at[:, pl.ds((N_KV_HEADS + h) * HEAD_DIM, HEAD_DIM)]
            return pltpu.make_async_copy(src, vo_hbm.at[b, emit_kv_layer, :, h, :],
                                         kv_sems.at[sb, N_KV_HEADS + h])
        src = part(kf_all, sb, n_new).at[:, pl.ds(h * HEAD_DIM, HEAD_DIM)]
        return pltpu.make_async_copy(src, ko_hbm.at[b, emit_kv_layer, :, h, :], kv_sems.at[sb, h])

    def prepare_keys_values(sb):
        kv_ref, k_scr, v_scr = part(kv_all, sb, n_new), part(k_all, sb, n_new + n_cache), part(v_all, sb, n_new + n_cache)
        kf_scr = part(kf_all, sb, n_new) if emit_kv else None
        for h in range(N_KV_HEADS):
            kn = _rms(_head(kv_ref, h), kg_ref[...])
            vh = _head(kv_ref, N_KV_HEADS + h)
            if emit_kv:
                kf_scr[:, h * HEAD_DIM:(h + 1) * HEAD_DIM] = kn
            if rope:
                kn = _rope(kn, *(t[...] for t in k_tabs))
            k_scr[0:n_new, h * HEAD_DIM:(h + 1) * HEAD_DIM] = kn.astype(BF16)
            v0 = 2 * h * HEAD_DIM
            v_scr[0:n_new, v0:v0 + HEAD_DIM] = vh.astype(BF16)
            if n_cache:
                v_scr[n_new:n_new + n_cache, v0:v0 + HEAD_DIM] = _head(cv_ref[0, 0], h).astype(BF16)
            v_scr[:, v0 + HEAD_DIM:v0 + 2 * HEAD_DIM] = jnp.ones((n_new + n_cache, HEAD_DIM), BF16)
        if n_cache:
            k_scr[n_new:n_new + n_cache, :] = ck_ref[0, 0].astype(BF16)

    def attend(sb):
        rows = q_all.shape[0] // nsb
        q_ref, g_ref, att_ref = part(q_all, sb, rows), part(g_all, sb, rows), part(att_all, sb, rows)
        k_scr, v_scr = part(k_all, sb, n_new + n_cache), part(v_all, sb, n_new + n_cache)

        def query(hq):
            x = _rms(_head(q_ref, hq), qg_ref[...])
            if rope:
                x = _rope(x, *(t[...] for t in q_tabs))
            return x * Q_PRESCALE

        for h in range(N_KV_HEADS):
            q2 = jnp.concatenate([query(GROUP * h + g) for g in range(GROUP)], axis=0).astype(BF16)
            o = _softmax_pv(q2, _head(k_scr, h), v_scr[:, 2 * h * HEAD_DIM:2 * (h + 1) * HEAD_DIM])
            for g in range(GROUP):
                hq = GROUP * h + g
                gate = _silu(_head(g_ref, hq))
                att_ref[:, hq * HEAD_DIM:(hq + 1) * HEAD_DIM] = (o[g * rows:(g + 1) * rows] * gate).astype(BF16)

    every = [(sb, h) for sb in range(nsb) for h in range(N_KV_HEADS)]
    if single_tile:
        if emit_kv:
            for sb, h in every:
                kv_copy(sb, h, True).start()
        for sb in range(nsb):
            prepare_keys_values(sb)
        if emit_kv:
            for sb, h in every:
                kv_copy(sb, h, False).start()
    else:
        pl.when(pl.program_id(1) == 0)(lambda: prepare_keys_values(0))

    for sb in range(nsb):
        attend(sb)

    if emit_kv:
        for sb, h in every:
            kv_copy(sb, h, False).wait()
            kv_copy(sb, h, True).wait()


def _attention(proj, q_g, k_g, layer, seq, tq, n_batch, row0, *, nsb=1, cache=None, rope_tabs=None,
               att_prev=None, kv_prev=None, emit_kv=False):
    nq = seq // tq
    assert nsb == 1 or (nq == 1 and cache is None)
    tq, kv_rows = nsb * tq, nsb * seq
    q0, kv0 = row0 // tq, row0 // kv_rows
    n_cache = 0 if cache is None else cache[0].shape[2]
    vec = pl.BlockSpec((1, HEAD_DIM), lambda b, i: (0, 0))
    in_specs = [
        pl.BlockSpec((tq, D_ATT), lambda b, i: (q0 + b * nq + i, COL_Q // D_ATT)),
        pl.BlockSpec((kv_rows, 2 * D_KV), lambda b, i: (kv0 + b, COL_K // (2 * D_KV))),
        pl.BlockSpec((tq, D_ATT), lambda b, i: (q0 + b * nq + i, COL_GA // D_ATT)),
        vec, vec,
    ]
    args = [proj, proj, proj, q_g, k_g]
    if cache is not None:
        in_specs += [pl.BlockSpec((1, 1, n_cache, D_KV), lambda b, i: (b, layer, 0, 0))] * 2
        args += list(cache)
    if rope_tabs is not None:
        in_specs += [pl.BlockSpec((tq, HEAD_DIM), lambda b, i: (i, 0))] * 3
        in_specs += [pl.BlockSpec((seq, HEAD_DIM), lambda b, i: (0, 0))] * 3
        args += list(rope_tabs) * 2
    aliased = ([] if att_prev is None else [att_prev]) + ([] if kv_prev is None else list(kv_prev))
    out_first = 0 if att_prev is not None else 1
    aliases = {len(args) + n: out_first + n for n in range(len(aliased))}
    in_specs += [pl.BlockSpec(memory_space=pl.ANY)] * len(aliased)
    args += aliased
    out_specs = [pl.BlockSpec((tq, D_ATT), lambda b, i: (q0 + b * nq + i, 0))]
    out_shape = [jax.ShapeDtypeStruct((M_ALL, D_ATT), BF16)]
    keys = nsb * (seq + n_cache)
    scratch = [pltpu.VMEM((keys, D_KV), BF16), pltpu.VMEM((keys, 2 * D_KV), BF16)]
    if emit_kv:
        out_specs += [pl.BlockSpec(memory_space=pl.ANY)] * 2
        out_shape += [jax.ShapeDtypeStruct((n_batch, DEPTH, seq, N_KV_HEADS, HEAD_DIM), F32)] * 2
        scratch += [pltpu.VMEM((kv_rows, D_KV), F32), pltpu.SemaphoreType.DMA((nsb, 2 * N_KV_HEADS))]
    return pl.pallas_call(
        functools.partial(_attn_kernel, rope_tabs is not None, n_cache, layer if emit_kv else None, len(aliased),
                          nsb),
        grid=(n_batch // nsb, nq),
        in_specs=in_specs,
        out_specs=out_specs,
        out_shape=out_shape,
        input_output_aliases=aliases,
        scratch_shapes=scratch,
        compiler_params=_params("arbitrary", "arbitrary"),
        name=f"attention_{seq}",
    )(*args)


def _dft_matrix(L):
    k = np.arange(L, dtype=np.int64)[:, None]
    t = np.arange(L, dtype=np.int64)[None, :]
    ang = 2.0 * np.pi * ((k * t) % (2 * L)).astype(np.float64) / (2 * L)
    top = np.cos(ang)
    bot = np.sin(ang)
    bot[0, :] = np.where(np.arange(L) % 2 == 0, 1.0, -1.0)
    return np.concatenate([top, bot], axis=0).astype(np.float32)


def _filter_kernel(L, z_ref, w1_ref, b1_ref, w2_ref, b2_ref, fr_ref, w3f_ref, w3b_ref, dec_ref,
                   f_ref, o_ref, hdn_scr):
    hp = lax.Precision.HIGHEST

    @pl.when(pl.program_id(1) == 0)
    def _():
        fr = fr_ref[0]
        h1 = jnp.sin(fr * (jnp.dot(z_ref[...], w1_ref[0], precision=hp, preferred_element_type=F32) + b1_ref[0]))
        hdn_scr[...] = jnp.sin(fr * (jnp.dot(h1, w2_ref[0], precision=hp, preferred_element_type=F32) + b2_ref[0]))

    hdn = hdn_scr[...]
    dec = dec_ref[...]
    hdn_bf = hdn.astype(BF16)
    h_f = jnp.dot(hdn_bf, w3f_ref[0].astype(BF16), preferred_element_type=F32) * dec
    h_b = jnp.dot(hdn_bf, w3b_ref[0].astype(BF16), preferred_element_type=F32) * dec
    hs = h_f + h_b
    ha = jnp.dot(f_ref[0:L, :], hs.astype(BF16), preferred_element_type=F32)
    hb = jnp.dot(f_ref[L:2 * L, :], (h_f - h_b).astype(BF16), preferred_element_type=F32)
    row = lax.broadcasted_iota(jnp.int32, (L, 1), 0)
    first = row == 0
    nyquist = jnp.sum(jnp.where(row % 2 == 0, hs, -hs), axis=0, keepdims=True)
    wk = jnp.where(first, 1.0 / (2 * L), 2.0 / (2 * L))
    o_ref[0, 0] = ha * wk
    o_ref[0, 1] = jnp.where(first, 0.0, hb * wk)
    o_ref[0, 2] = jnp.where(first, nyquist, ha) * wk


FILT_CB = 512


def _filter_spectra(L, w1, b1, w2, b2, freq, w3, f_bf):
    tpos = np.arange(L, dtype=np.float64)
    t_norm = tpos / max(L - 1, 1)
    w = 2.0 * math.pi * tpos / L
    bands = np.linspace(1e-4, POS_BANDS - 1, POS_BANDS)
    z = np.concatenate([t_norm[:, None], np.cos(w[:, None] * bands), -np.sin(w[:, None] * bands)], axis=-1)
    z = np.pad(z, ((0, 0), (0, FEAT_PAD - POS_EMB))).astype(np.float32)
    max_decay = math.log(DECAY_TARGET) / FAST_DECAY_PCT
    min_decay = math.log(DECAY_TARGET) / SLOW_DECAY_PCT
    deltas = np.abs(np.linspace(min_decay, max_decay, D_HY))
    dec = (np.exp(-t_norm[:, None] * deltas) + DECAY_SHIFT).astype(np.float32)

    pad_h = FEAT_PAD - FILT_HID
    w1p = jnp.pad(w1, ((0, 0), (0, FEAT_PAD - POS_EMB), (0, pad_h)))
    w2p = jnp.pad(w2, ((0, 0), (0, pad_h), (0, pad_h)))
    w3p = jnp.pad(w3, ((0, 0), (0, pad_h), (0, 0)))
    vec = lambda v: jnp.pad(v, ((0, 0), (0, pad_h))).reshape(DEPTH, 1, FEAT_PAD)
    ncb = D_HY // FILT_CB
    small = lambda shape: pl.BlockSpec((1,) + shape, lambda l, c: (l, 0, 0))
    return pl.pallas_call(
        functools.partial(_filter_kernel, L),
        grid=(DEPTH, ncb),
        in_specs=[
            pl.BlockSpec((L, FEAT_PAD), lambda l, c: (0, 0)),
            small((FEAT_PAD, FEAT_PAD)), small((1, FEAT_PAD)),
            small((FEAT_PAD, FEAT_PAD)), small((1, FEAT_PAD)), small((1, FEAT_PAD)),
            pl.BlockSpec((1, FEAT_PAD, FILT_CB), lambda l, c: (l, 0, c)),
            pl.BlockSpec((1, FEAT_PAD, FILT_CB), lambda l, c: (l, 0, ncb + c)),
            pl.BlockSpec((L, FILT_CB), lambda l, c: (0, c)),
            pl.BlockSpec((2 * L, L), lambda l, c: (0, 0)),
        ],
        out_specs=pl.BlockSpec((1, 3, L, FILT_CB), lambda l, c: (l, 0, 0, c)),
        out_shape=jax.ShapeDtypeStruct((DEPTH, 3, L, D_HY), F32),
        scratch_shapes=[pltpu.VMEM((L, FEAT_PAD), F32)],
        compiler_params=_params("arbitrary", "arbitrary"),
        name=f"hyena_filter_{L}",
    )(jnp.asarray(z), w1p, vec(b1), w2p, vec(b2), vec(freq), w3p, w3p, jnp.asarray(dec), f_bf)


def _hyena_kernel(L, nseq, x0_ref, x1_ref, vv_ref, g_ref, cw0_ref, cw1_ref, cwv_ref, cb0_ref, cb1_ref,
                  cbv_ref, bias_ref, coef_ref, f_ref, ft_ref, *rest):
    o_ref = rest[-1]
    rows, cb = o_ref.shape
    t = lax.broadcasted_iota(jnp.int32, (rows, 1), 0) % L
    is_first = t == 0
    is_last = t == L - 1

    def sconv(x_ref, w_ref, b_ref):
        x = x_ref[...]
        prev = jnp.where(is_first, 0.0, pltpu.roll(x, 1, axis=0))
        nxt = jnp.where(is_last, 0.0, pltpu.roll(x, rows - 1, axis=0))
        return w_ref[0, 0:1, :] * prev + w_ref[0, 1:2, :] * x + w_ref[0, 2:3, :] * nxt + b_ref[0]

    def on_lanes(x):
        return jnp.concatenate([x[i * L:(i + 1) * L] for i in range(nseq)], axis=1)

    def lane_block(x, i):
        return x[:, i * cb:(i + 1) * cb]

    z = sconv(vv_ref, cwv_ref, cbv_ref) * sconv(x1_ref, cw1_ref, cb1_ref)
    ab = jnp.dot(f_ref[...], on_lanes(z).astype(BF16), preferred_element_type=F32)
    g1, g2, g3 = coef_ref[0, 0], coef_ref[0, 1], coef_ref[0, 2]
    pq = []
    for i in range(nseq):
        a, b = lane_block(ab[0:L], i), lane_block(ab[L:2 * L], i)
        pq.append(jnp.concatenate([a * g1 - b * g2, a * g2 + b * g3], axis=0))
    pq = jnp.concatenate(pq, axis=1).astype(BF16)
    y = jnp.dot(ft_ref[...], pq, preferred_element_type=F32)
    y = jnp.concatenate([lane_block(y, i) for i in range(nseq)], axis=0) + bias_ref[0] * z
    o_ref[...] = (sconv(x0_ref, cw0_ref, cb0_ref) * y * _silu(g_ref[...])).astype(BF16)


def _hyena(proj, L, nb, nseq, row0, cb, layer, conv_w, conv_b, hy_bias, coef, f_bf, ft_bf, hy_prev=None):
    ncb = D_HY // cb
    rows = nseq * L
    blk0 = row0 // rows
    extra_specs = [] if hy_prev is None else [pl.BlockSpec(memory_space=pl.ANY)]
    extra_args = [] if hy_prev is None else [hy_prev]
    aliases = {} if hy_prev is None else {14: 0}
    col = lambda off: (lambda c, b: (blk0 + b, off // cb + c))
    cw = lambda part: pl.BlockSpec((1, 3, cb), lambda c, b: (layer, 0, part * ncb + c))
    cbias = lambda part: pl.BlockSpec((1, 1, cb), lambda c, b: (layer, 0, part * ncb + c))
    once = pl.Buffered(1)
    return pl.pallas_call(
        functools.partial(_hyena_kernel, L, nseq),
        grid=(ncb, nb // nseq),
        in_specs=[
            pl.BlockSpec((rows, cb), col(COL_X0)),
            pl.BlockSpec((rows, cb), col(COL_X1)),
            pl.BlockSpec((rows, cb), col(COL_VV)),
            pl.BlockSpec((rows, cb), col(COL_GH)),
            cw(0), cw(1), cw(2), cbias(0), cbias(1), cbias(2),
            pl.BlockSpec((1, 1, cb), lambda c, b: (layer, 0, c)),
            pl.BlockSpec((1, 3, L, cb), lambda c, b: (layer, 0, 0, c)),
            pl.BlockSpec((2 * L, L), lambda c, b: (0, 0), pipeline_mode=once),
            pl.BlockSpec((L, 2 * L), lambda c, b: (0, 0), pipeline_mode=once),
            *extra_specs,
        ],
        out_specs=pl.BlockSpec((rows, cb), lambda c, b: (blk0 + b, c)),
        out_shape=jax.ShapeDtypeStruct((M_ALL, D_HY), BF16),
        input_output_aliases=aliases,
        compiler_params=_params("arbitrary", "arbitrary"),
        name=f"hyena_{L}",
    )(proj, proj, proj, proj, conv_w, conv_w, conv_w, conv_b, conv_b, conv_b, hy_bias, coef, f_bf, ft_bf,
      *extra_args)


OUTPROJ_TM = 256
OUTPROJ_TM_FINAL = 512


def _outproj_kernel(split_in, final, n_ctx_tiles, *refs):
    it = iter(refs)
    att_ref, hy_ref, w_ref = next(it), next(it), next(it)
    x_refs = (next(it), next(it)) if split_in else (next(it),)
    gate_ref, g_ref = next(it), next(it)
    shift_ref, scale_ref = (None, None) if final else (next(it), next(it))
    out_refs = (next(it),) if final else (next(it), next(it))
    w_scr = next(it)
    i = pl.program_id(0)

    @pl.when(i == 0)
    def _():
        w_scr[...] = w_ref[0].astype(BF16)

    out = (jnp.dot(att_ref[...], w_scr[0:D_ATT, :], preferred_element_type=F32)
           + jnp.dot(hy_ref[...], w_scr[D_ATT:D_ATT + D_HY, :], preferred_element_type=F32))
    if split_in:
        x = jnp.where(i < n_ctx_tiles, x_refs[0][...], x_refs[1][...])
    else:
        x = x_refs[0][...]
    y = x + gate_ref[0, 0] * out
    inv = lax.rsqrt(jnp.mean(y * y, axis=-1, keepdims=True) + EPS)
    if final:
        out_refs[0][...] = y * inv * g_ref[0]
    else:
        out_refs[0][...] = y
        gain = g_ref[0] * (1.0 + scale_ref[0, 0])
        out_refs[1][...] = (y * inv * gain + shift_ref[0, 0]).astype(BF16)


def _out_proj(att, hy, w_out, layer, x_parts, mod4, gains3, final, tm, tile0=0, n_tiles=None):
    n_tiles = M_ALL // tm if n_tiles is None else n_tiles
    split_in = len(x_parts) == 2
    assert not split_in or (tile0 == 0 and n_tiles == M_ALL // tm)
    row_tile = lambda width: pl.BlockSpec((tm, width), lambda i: (tile0 + i, 0))
    out_tile = pl.BlockSpec((tm, D_MODEL), lambda i: (i, 0))
    x_specs = list(_two_stream_specs(tm, D_MODEL)) if split_in else [row_tile(D_MODEL)]
    in_specs = [
        row_tile(D_ATT),
        row_tile(D_HY),
        pl.BlockSpec((1, D_ATT + D_HY, D_MODEL), lambda i: (layer, 0, 0), pipeline_mode=pl.Buffered(1)),
        *x_specs,
        _mod_spec(tm, layer, MOD_GATE, tile0),
    ]
    args = [att, hy, w_out, *x_parts, mod4]
    if final:
        in_specs.append(pl.BlockSpec((1, 1, D_MODEL), lambda i: (0, 0, 0)))
        args.append(gains3)
        out_specs = [out_tile]
        out_shape = [jax.ShapeDtypeStruct((n_tiles * tm, D_MODEL), F32)]
    else:
        in_specs += [pl.BlockSpec((1, 1, D_MODEL), lambda i: (layer + 1, 0, 0)),
                     _mod_spec(tm, layer + 1, MOD_SHIFT, tile0), _mod_spec(tm, layer + 1, MOD_SCALE, tile0)]
        args += [gains3, mod4, mod4]
        out_specs = [out_tile, out_tile]
        out_shape = [jax.ShapeDtypeStruct((n_tiles * tm, D_MODEL), F32),
                     jax.ShapeDtypeStruct((n_tiles * tm, D_MODEL), BF16)]
    return pl.pallas_call(
        functools.partial(_outproj_kernel, split_in, final, M_CTX // tm),
        grid=(n_tiles,),
        in_specs=in_specs,
        out_specs=out_specs,
        out_shape=out_shape,
        scratch_shapes=[pltpu.VMEM((D_ATT + D_HY, D_MODEL), BF16)],
        compiler_params=_params("arbitrary"),
        name="out_proj_final" if final else "out_proj",
    )(*args)


def _rope_tables():
    t = np.arange(DEC_SEQ)
    row = (t // GRID_W).astype(np.float64)
    col = (t % GRID_W).astype(np.float64)
    pairs = HEAD_DIM // 4
    inv_freq = ROPE_THETA ** (-np.arange(pairs, dtype=np.float64) / pairs)
    ang = np.concatenate([row[:, None] * inv_freq, col[:, None] * inv_freq], axis=-1)
    cos = np.repeat(np.cos(ang), 2, axis=-1).astype(np.float32)
    sin = np.repeat(np.sin(ang), 2, axis=-1).astype(np.float32)
    even = (np.arange(HEAD_DIM) % 2 == 0)[None, :]
    sin_a = np.where(even, -sin, 0.0).astype(np.float32)
    sin_b = np.where(even, 0.0, sin).astype(np.float32)
    return jnp.asarray(cos), jnp.asarray(sin_a), jnp.asarray(sin_b)


def kernel(x_prompt, x_sample, cache_k, cache_v, c, c_ctx, norm_g, w_ada, b_ada, w_in, q_norm_g, k_norm_g,
           conv_w, conv_b, filt_w1, filt_b1, filt_w2, filt_b2, filt_w3, filt_freq, hy_bias, w_out, final_norm_g):
    ctx = x_prompt.reshape(BATCH * SEQ, D_MODEL)
    lat = x_sample.reshape(DEC_BATCH * DEC_SEQ, D_MODEL)
    cache_k4 = cache_k.reshape(DEC_BATCH, DEPTH, PAST_LEN, D_KV)
    cache_v4 = cache_v.reshape(DEC_BATCH, DEPTH, PAST_LEN, D_KV)

    cvec = jnp.concatenate([c_ctx[None, :], c, jnp.zeros((MOD_ROWS - 1 - DEC_BATCH, D_MODEL), F32)], axis=0)
    mod = _modulation(cvec, w_ada, b_ada)

    rope_tabs = _rope_tables()
    dft = {}
    coefs = {}
    for L in (SEQ, DEC_SEQ):
        f_np = _dft_matrix(L)
        f_bf = jnp.asarray(f_np).astype(BF16)
        dft[L] = (f_bf, jnp.asarray(np.ascontiguousarray(f_np.T)).astype(BF16))
        coefs[L] = _filter_spectra(L, filt_w1, filt_b1, filt_w2, filt_b2, filt_freq, filt_w3, f_bf)

    mod4 = mod.reshape(DEPTH, MOD_ROWS, 1, 3 * D_MODEL)
    norm_g3 = norm_g.reshape(DEPTH, 1, D_MODEL)
    final_g3 = final_norm_g.reshape(1, 1, D_MODEL)
    conv_b3 = conv_b.reshape(DEPTH, 1, 3 * D_HY)
    hy_bias3 = hy_bias.reshape(DEPTH, 1, D_HY)

    x_parts = (ctx, lat)
    h = _norm_mod(ctx, lat, norm_g3, mod4, 0)
    kv_out = None
    for l in range(DEPTH):
        qg = q_norm_g[l].reshape(1, HEAD_DIM)
        kg = k_norm_g[l].reshape(1, HEAD_DIM)
        final = l == DEPTH - 1

        proj = _in_proj(h, w_in, l)

        att, new_k, new_v = _attention(proj, qg, kg, l, SEQ, CTX_TQ, BATCH, 0, nsb=2, kv_prev=kv_out,
                                       emit_kv=True)
        kv_out = (new_k, new_v)
        (att,) = _attention(proj, qg, kg, l, DEC_SEQ, LAT_TQ, DEC_BATCH, M_CTX, cache=(cache_k4, cache_v4),
                            rope_tabs=rope_tabs, att_prev=att)

        hy = _hyena(proj, SEQ, BATCH, 2, 0, D_HY, l, conv_w, conv_b3, hy_bias3, coefs[SEQ], *dft[SEQ])
        hy = _hyena(proj, DEC_SEQ, DEC_BATCH, 1, M_CTX, 512, l, conv_w, conv_b3, hy_bias3,
                    coefs[DEC_SEQ], *dft[DEC_SEQ], hy_prev=hy)

        if final:
            assert len(x_parts) == 1
            tm = OUTPROJ_TM_FINAL
            nc = M_CTX // tm
            (y_ctx,) = _out_proj(att, hy, w_out, l, x_parts, mod4, final_g3, True, tm, 0, nc)
            (y_lat,) = _out_proj(att, hy, w_out, l, x_parts, mod4, final_g3, True, tm, nc, M_LAT // tm)
        else:
            y, h = _out_proj(att, hy, w_out, l, x_parts, mod4, norm_g3, False, OUTPROJ_TM)
            x_parts = (y,)

    y_prompt = y_ctx.reshape(BATCH, SEQ, D_MODEL)
    y_sample = y_lat.reshape(DEC_BATCH, DEC_SEQ, D_MODEL)
    return (y_prompt, y_sample, kv_out[0], kv_out[1])
```

```python
import functools
import math

import numpy as np
import jax
import jax.numpy as jnp
from jax import lax
from jax.experimental import pallas as pl
from jax.experimental.pallas import tpu as pltpu

D_MODEL = 2048
BATCH = 16
SEQ = 256
DEPTH = 2
DEC_BATCH = 2
DEC_SEQ = 1024
PAST_LEN = 512
GRID_W = 64
D_ATT = 1024
D_HY = 1024
HEAD_DIM = 128
N_HEADS = 8
N_KV_HEADS = 4
GROUP = 2
D_KV = 512
ROPE_THETA = 10000.0
POS_BANDS = 16
POS_EMB = 33
FILT_HID = 64
DECAY_TARGET = 1e-2
FAST_DECAY_PCT = 0.3
SLOW_DECAY_PCT = 1.5
DECAY_SHIFT = 0.05
EPS = 1e-6
D_IN = 7168

COL_Q, COL_K, COL_V, COL_GA, COL_X0, COL_X1, COL_VV, COL_GH = 0, 1024, 1536, 2048, 3072, 4096, 5120, 6144

F32 = jnp.float32
BF16 = jnp.bfloat16

VMEM_LIMIT_BYTES = 56 * 1024 * 1024
FEAT_PAD = 128
def _params(*sem):
    return pltpu.CompilerParams(dimension_semantics=sem, vmem_limit_bytes=VMEM_LIMIT_BYTES)


def _silu(x):
    half = 0.5 * x
    return half + half * jnp.tanh(half)


def _rms(x, g):
    return x * lax.rsqrt(jnp.mean(x * x, axis=-1, keepdims=True) + EPS) * g


MOD_ROWS = 8
MOD_TN = 1024


def _mod_kernel(c_ref, w_ref, b_ref, o_ref):
    s = _silu(c_ref[...]).astype(BF16)
    o_ref[0] = jnp.dot(s, w_ref[0].astype(BF16), preferred_element_type=F32) + b_ref[0]


def _modulation(cvec, w_ada, b_ada):
    n = 3 * D_MODEL
    return pl.pallas_call(
        _mod_kernel,
        grid=(DEPTH, n // MOD_TN),
        in_specs=[
            pl.BlockSpec((MOD_ROWS, D_MODEL), lambda l, j: (0, 0)),
            pl.BlockSpec((1, D_MODEL, MOD_TN), lambda l, j: (l, 0, j)),
            pl.BlockSpec((1, 1, MOD_TN), lambda l, j: (l, 0, j)),
        ],
        out_specs=pl.BlockSpec((1, MOD_ROWS, MOD_TN), lambda l, j: (l, 0, j)),
        out_shape=jax.ShapeDtypeStruct((DEPTH, MOD_ROWS, n), F32),
        compiler_params=_params("arbitrary", "arbitrary"),
        name="adaln_mod",
    )(cvec, w_ada, b_ada.reshape(DEPTH, 1, n))


M_CTX = BATCH * SEQ
M_LAT = DEC_BATCH * DEC_SEQ
M_ALL = M_CTX + M_LAT
NORM_TM = 1024
INPROJ_TM = 3072
INPROJ_TN = 512


ROW_CHUNK = 32


def _modulated_norm_rows(x_ref, h_ref, g, shift, scale):
    gain = g * (1.0 + scale)

    def body(r, _):
        rows = pl.ds(pl.multiple_of(r * ROW_CHUNK, ROW_CHUNK), ROW_CHUNK)
        x = x_ref[rows, :]
        inv = lax.rsqrt(jnp.mean(x * x, axis=-1, keepdims=True) + EPS)
        h_ref[rows, :] = (x * inv * gain + shift).astype(BF16)
        return 0

    lax.fori_loop(0, x_ref.shape[0] // ROW_CHUNK, body, 0, unroll=4)


MOD_SHIFT, MOD_SCALE, MOD_GATE = 0, 1, 2


def _mod_spec(tm, layer, part, tile0=0):
    nc = M_CTX // tm
    per_batch = DEC_SEQ // tm
    row = lambda t: jnp.where(t < nc, 0, 1 + (t - nc) // per_batch)
    return pl.BlockSpec((1, 1, 1, D_MODEL), lambda i: (layer, row(tile0 + i), 0, part))


def _two_stream_specs(tm, width):
    nc = M_CTX // tm
    return (pl.BlockSpec((tm, width), lambda i: (jnp.minimum(i, nc - 1), 0)),
            pl.BlockSpec((tm, width), lambda i: (jnp.maximum(i - nc, 0), 0)))


def _norm_kernel(n_ctx_tiles, xc_ref, xl_ref, g_ref, shift_ref, scale_ref, h_ref):
    def emit(x_ref):
        _modulated_norm_rows(x_ref, h_ref, g_ref[0], shift_ref[0, 0], scale_ref[0, 0])

    i = pl.program_id(0)
    pl.when(i < n_ctx_tiles)(lambda: emit(xc_ref))
    pl.when(i >= n_ctx_tiles)(lambda: emit(xl_ref))


def _norm_mod(x_ctx, x_lat, norm_g3, mod4, layer):
    xc_spec, xl_spec = _two_stream_specs(NORM_TM, D_MODEL)
    return pl.pallas_call(
        functools.partial(_norm_kernel, M_CTX // NORM_TM),
        grid=(M_ALL // NORM_TM,),
        in_specs=[
            xc_spec, xl_spec,
            pl.BlockSpec((1, 1, D_MODEL), lambda i: (layer, 0, 0)),
            _mod_spec(NORM_TM, layer, MOD_SHIFT),
            _mod_spec(NORM_TM, layer, MOD_SCALE),
        ],
        out_specs=pl.BlockSpec((NORM_TM, D_MODEL), lambda i: (i, 0)),
        out_shape=jax.ShapeDtypeStruct((M_ALL, D_MODEL), BF16),
        compiler_params=_params("arbitrary"),
        name="norm_mod",
    )(x_ctx, x_lat, norm_g3, mod4, mod4)


def _inproj_kernel(h_ref, w_ref, o_ref, w_scr):
    i = pl.program_id(1)

    @pl.when(i == 0)
    def _():
        w_scr[...] = w_ref[0].astype(BF16)

    rows = pl.ds(pl.multiple_of(i * INPROJ_TM, INPROJ_TM), INPROJ_TM)
    o_ref[...] = jnp.dot(h_ref[rows, :], w_scr[...], preferred_element_type=F32)


def _in_proj(h, w_in, layer):
    return pl.pallas_call(
        _inproj_kernel,
        grid=(D_IN // INPROJ_TN, M_ALL // INPROJ_TM),
        in_specs=[
            pl.BlockSpec((M_ALL, D_MODEL), lambda j, i: (0, 0), pipeline_mode=pl.Buffered(1)),
            pl.BlockSpec((1, D_MODEL, INPROJ_TN), lambda j, i: (layer, 0, j)),
        ],
        out_specs=pl.BlockSpec((INPROJ_TM, INPROJ_TN), lambda j, i: (i, j)),
        out_shape=jax.ShapeDtypeStruct((M_ALL, D_IN), F32),
        scratch_shapes=[pltpu.VMEM((D_MODEL, INPROJ_TN), BF16)],
        compiler_params=_params("arbitrary", "arbitrary"),
        name="in_proj",
    )(h, w_in)


ATT_SCALE = 1.0 / math.sqrt(HEAD_DIM)


def _head(ref_or_val, h):
    return ref_or_val[:, h * HEAD_DIM:(h + 1) * HEAD_DIM]


Q_PRESCALE = ATT_SCALE * math.log2(math.e)


def _softmax_pv(q2, k_bf, v_ext):
    s = lax.dot_general(q2, k_bf, (((1,), (1,)), ((), ())), preferred_element_type=F32)
    p = jnp.exp2(s - jnp.max(s, axis=-1, keepdims=True)).astype(BF16)
    o = jnp.dot(p, v_ext, preferred_element_type=F32)
    return o[:, 0:HEAD_DIM] / o[:, HEAD_DIM:2 * HEAD_DIM]


def _rope(x, cos2, sin_a, sin_b):
    nxt = pltpu.roll(x, HEAD_DIM - 1, axis=1)
    prv = pltpu.roll(x, 1, axis=1)
    return x * cos2 + nxt * sin_a + prv * sin_b


CTX_TQ = SEQ
LAT_TQ = 128


def _attn_kernel(rope, n_cache, emit_kv_layer, n_aliased, nsb, *refs):
    emit_kv = emit_kv_layer is not None
    it = iter(refs)
    q_all, kv_all, g_all, qg_ref, kg_ref = (next(it) for _ in range(5))
    ck_ref, cv_ref = (next(it), next(it)) if n_cache else (None, None)
    q_tabs = tuple(next(it) for _ in range(3)) if rope else None
    k_tabs = tuple(next(it) for _ in range(3)) if rope else None
    for _ in range(n_aliased):
        next(it)
    att_all = next(it)
    ko_hbm, vo_hbm = (next(it), next(it)) if emit_kv else (None, None)
    k_all, v_all = next(it), next(it)
    kf_all, kv_sems = (next(it), next(it)) if emit_kv else (None, None)
    n_new = kv_all.shape[0] // nsb
    single_tile = q_all.shape[0] == kv_all.shape[0]
    assert single_tile or nsb == 1
    assert not emit_kv or single_tile

    def part(ref, sb, rows_per_seq):
        return ref if nsb == 1 else ref.at[pl.ds(sb * rows_per_seq, rows_per_seq)]

    def kv_copy(sb, h, is_value):
        b = pl.program_id(0) * nsb + sb
        if is_value:
            src = part(kv_all, sb, n_new).at[:, pl.ds((N_KV_HEADS + h) * HEAD_DIM, HEAD_DIM)]
            return pltpu.make_async_copy(src, vo_hbm.at[b, emit_kv_layer, :, h, :],
                                         kv_sems.at[sb, N_KV_HEADS + h])
        src = part(kf_all, sb, n_new).at[:, pl.ds(h * HEAD_DIM, HEAD_DIM)]
        return pltpu.make_async_copy(src, ko_hbm.at[b, emit_kv_layer, :, h, :], kv_sems.at[sb, h])

    def prepare_keys_values(sb):
        kv_ref, k_scr, v_scr = part(kv_all, sb, n_new), part(k_all, sb, n_new + n_cache), part(v_all, sb, n_new + n_cache)
        kf_scr = part(kf_all, sb, n_new) if emit_kv else None
        for h in range(N_KV_HEADS):
            kn = _rms(_head(kv_ref, h), kg_ref[...])
            vh = _head(kv_ref, N_KV_HEADS + h)
            if emit_kv:
                kf_scr[:, h * HEAD_DIM:(h + 1) * HEAD_DIM] = kn
            if rope:
                kn = _rope(kn, *(t[...] for t in k_tabs))
            k_scr[0:n_new, h * HEAD_DIM:(h + 1) * HEAD_DIM] = kn.astype(BF16)
            v0 = 2 * h * HEAD_DIM
            v_scr[0:n_new, v0:v0 + HEAD_DIM] = vh.astype(BF16)
            if n_cache:
                v_scr[n_new:n_new + n_cache, v0:v0 + HEAD_DIM] = _head(cv_ref[0, 0], h).astype(BF16)
            v_scr[:, v0 + HEAD_DIM:v0 + 2 * HEAD_DIM] = jnp.ones((n_new + n_cache, HEAD_DIM), BF16)
        if n_cache:
            k_scr[n_new:n_new + n_cache, :] = ck_ref[0, 0].astype(BF16)

    def attend(sb):
        rows = q_all.shape[0] // nsb
        q_ref, g_ref, att_ref = part(q_all, sb, rows), part(g_all, sb, rows), part(att_all, sb, rows)
        k_scr, v_scr = part(k_all, sb, n_new + n_cache), part(v_all, sb, n_new + n_cache)

        def query(hq):
            x = _rms(_head(q_ref, hq), qg_ref[...])
            if rope:
                x = _rope(x, *(t[...] for t in q_tabs))
            return x * Q_PRESCALE

        for h in range(N_KV_HEADS):
            q2 = jnp.concatenate([query(GROUP * h + g) for g in range(GROUP)], axis=0).astype(BF16)
            o = _softmax_pv(q2, _head(k_scr, h), v_scr[:, 2 * h * HEAD_DIM:2 * (h + 1) * HEAD_DIM])
            for g in range(GROUP):
                hq = GROUP * h + g
                gate = _silu(_head(g_ref, hq))
                att_ref[:, hq * HEAD_DIM:(hq + 1) * HEAD_DIM] = (o[g * rows:(g + 1) * rows] * gate).astype(BF16)

    every = [(sb, h) for sb in range(nsb) for h in range(N_KV_HEADS)]
    if single_tile:
        if emit_kv:
            for sb, h in every:
                kv_copy(sb, h, True).start()
        for sb in range(nsb):
            prepare_keys_values(sb)
        if emit_kv:
            for sb, h in every:
                kv_copy(sb, h, False).start()
    else:
        pl.when(pl.program_id(1) == 0)(lambda: prepare_keys_values(0))

    for sb in range(nsb):
        attend(sb)

    if emit_kv:
        for sb, h in every:
            kv_copy(sb, h, False).wait()
            kv_copy(sb, h, True).wait()


def _attention(proj, q_g, k_g, layer, seq, tq, n_batch, row0, *, nsb=1, cache=None, rope_tabs=None,
               att_prev=None, kv_prev=None, emit_kv=False):
    nq = seq // tq
    assert nsb == 1 or (nq == 1 and cache is None)
    tq, kv_rows = nsb * tq, nsb * seq
    q0, kv0 = row0 // tq, row0 // kv_rows
    n_cache = 0 if cache is None else cache[0].shape[2]
    vec = pl.BlockSpec((1, HEAD_DIM), lambda b, i: (0, 0))
    in_specs = [
        pl.BlockSpec((tq, D_ATT), lambda b, i: (q0 + b * nq + i, COL_Q // D_ATT)),
        pl.BlockSpec((kv_rows, 2 * D_KV), lambda b, i: (kv0 + b, COL_K // (2 * D_KV))),
        pl.BlockSpec((tq, D_ATT), lambda b, i: (q0 + b * nq + i, COL_GA // D_ATT)),
        vec, vec,
    ]
    args = [proj, proj, proj, q_g, k_g]
    if cache is not None:
        in_specs += [pl.BlockSpec((1, 1, n_cache, D_KV), lambda b, i: (b, layer, 0, 0))] * 2
        args += list(cache)
    if rope_tabs is not None:
        in_specs += [pl.BlockSpec((tq, HEAD_DIM), lambda b, i: (i, 0))] * 3
        in_specs += [pl.BlockSpec((seq, HEAD_DIM), lambda b, i: (0, 0))] * 3
        args += list(rope_tabs) * 2
    aliased = ([] if att_prev is None else [att_prev]) + ([] if kv_prev is None else list(kv_prev))
    out_first = 0 if att_prev is not None else 1
    aliases = {len(args) + n: out_first + n for n in range(len(aliased))}
    in_specs += [pl.BlockSpec(memory_space=pl.ANY)] * len(aliased)
    args += aliased
    out_specs = [pl.BlockSpec((tq, D_ATT), lambda b, i: (q0 + b * nq + i, 0))]
    out_shape = [jax.ShapeDtypeStruct((M_ALL, D_ATT), BF16)]
    keys = nsb * (seq + n_cache)
    scratch = [pltpu.VMEM((keys, D_KV), BF16), pltpu.VMEM((keys, 2 * D_KV), BF16)]
    if emit_kv:
        out_specs += [pl.BlockSpec(memory_space=pl.ANY)] * 2
        out_shape += [jax.ShapeDtypeStruct((n_batch, DEPTH, seq, N_KV_HEADS, HEAD_DIM), F32)] * 2
        scratch += [pltpu.VMEM((kv_rows, D_KV), F32), pltpu.SemaphoreType.DMA((nsb, 2 * N_KV_HEADS))]
    return pl.pallas_call(
        functools.partial(_attn_kernel, rope_tabs is not None, n_cache, layer if emit_kv else None, len(aliased),
                          nsb),
        grid=(n_batch // nsb, nq),
        in_specs=in_specs,
        out_specs=out_specs,
        out_shape=out_shape,
        input_output_aliases=aliases,
        scratch_shapes=scratch,
        compiler_params=_params("arbitrary", "arbitrary"),
        name=f"attention_{seq}",
    )(*args)


def _dft_matrix(L):
    k = np.arange(L, dtype=np.int64)[:, None]
    t = np.arange(L, dtype=np.int64)[None, :]
    ang = 2.0 * np.pi * ((k * t) % (2 * L)).astype(np.float64) / (2 * L)
    top = np.cos(ang)
    bot = np.sin(ang)
    bot[0, :] = np.where(np.arange(L) % 2 == 0, 1.0, -1.0)
    return np.concatenate([top, bot], axis=0).astype(np.float32)


def _filter_kernel(L, z_ref, w1_ref, b1_ref, w2_ref, b2_ref, fr_ref, w3f_ref, w3b_ref, dec_ref,
                   f_ref, o_ref, hdn_scr):
    hp = lax.Precision.HIGHEST

    @pl.when(pl.program_id(1) == 0)
    def _():
        fr = fr_ref[0]
        h1 = jnp.sin(fr * (jnp.dot(z_ref[...], w1_ref[0], precision=hp, preferred_element_type=F32) + b1_ref[0]))
        hdn_scr[...] = jnp.sin(fr * (jnp.dot(h1, w2_ref[0], precision=hp, preferred_element_type=F32) + b2_ref[0]))

    hdn = hdn_scr[...]
    dec = dec_ref[...]
    hdn_bf = hdn.astype(BF16)
    h_f = jnp.dot(hdn_bf, w3f_ref[0].astype(BF16), preferred_element_type=F32) * dec
    h_b = jnp.dot(hdn_bf, w3b_ref[0].astype(BF16), preferred_element_type=F32) * dec
    hs = h_f + h_b
    ha = jnp.dot(f_ref[0:L, :], hs.astype(BF16), preferred_element_type=F32)
    hb = jnp.dot(f_ref[L:2 * L, :], (h_f - h_b).astype(BF16), preferred_element_type=F32)
    row = lax.broadcasted_iota(jnp.int32, (L, 1), 0)
    first = row == 0
    nyquist = jnp.sum(jnp.where(row % 2 == 0, hs, -hs), axis=0, keepdims=True)
    wk = jnp.where(first, 1.0 / (2 * L), 2.0 / (2 * L))
    o_ref[0, 0] = ha * wk
    o_ref[0, 1] = jnp.where(first, 0.0, hb * wk)
    o_ref[0, 2] = jnp.where(first, nyquist, ha) * wk


FILT_CB = 512


def _filter_spectra(L, w1, b1, w2, b2, freq, w3, f_bf):
    tpos = np.arange(L, dtype=np.float64)
    t_norm = tpos / max(L - 1, 1)
    w = 2.0 * math.pi * tpos / L
    bands = np.linspace(1e-4, POS_BANDS - 1, POS_BANDS)
    z = np.concatenate([t_norm[:, None], np.cos(w[:, None] * bands), -np.sin(w[:, None] * bands)], axis=-1)
    z = np.pad(z, ((0, 0), (0, FEAT_PAD - POS_EMB))).astype(np.float32)
    max_decay = math.log(DECAY_TARGET) / FAST_DECAY_PCT
    min_decay = math.log(DECAY_TARGET) / SLOW_DECAY_PCT
    deltas = np.abs(np.linspace(min_decay, max_decay, D_HY))
    dec = (np.exp(-t_norm[:, None] * deltas) + DECAY_SHIFT).astype(np.float32)

    pad_h = FEAT_PAD - FILT_HID
    w1p = jnp.pad(w1, ((0, 0), (0, FEAT_PAD - POS_EMB), (0, pad_h)))
    w2p = jnp.pad(w2, ((0, 0), (0, pad_h), (0, pad_h)))
    w3p = jnp.pad(w3, ((0, 0), (0, pad_h), (0, 0)))
    vec = lambda v: jnp.pad(v, ((0, 0), (0, pad_h))).reshape(DEPTH, 1, FEAT_PAD)
    ncb = D_HY // FILT_CB
    small = lambda shape: pl.BlockSpec((1,) + shape, lambda l, c: (l, 0, 0))
    return pl.pallas_call(
        functools.partial(_filter_kernel, L),
        grid=(DEPTH, ncb),
        in_specs=[
            pl.BlockSpec((L, FEAT_PAD), lambda l, c: (0, 0)),
            small((FEAT_PAD, FEAT_PAD)), small((1, FEAT_PAD)),
            small((FEAT_PAD, FEAT_PAD)), small((1, FEAT_PAD)), small((1, FEAT_PAD)),
            pl.BlockSpec((1, FEAT_PAD, FILT_CB), lambda l, c: (l, 0, c)),
            pl.BlockSpec((1, FEAT_PAD, FILT_CB), lambda l, c: (l, 0, ncb + c)),
            pl.BlockSpec((L, FILT_CB), lambda l, c: (0, c)),
            pl.BlockSpec((2 * L, L), lambda l, c: (0, 0)),
        ],
        out_specs=pl.BlockSpec((1, 3, L, FILT_CB), lambda l, c: (l, 0, 0, c)),
        out_shape=jax.ShapeDtypeStruct((DEPTH, 3, L, D_HY), F32),
        scratch_shapes=[pltpu.VMEM((L, FEAT_PAD), F32)],
        compiler_params=_params("arbitrary", "arbitrary"),
        name=f"hyena_filter_{L}",
    )(jnp.asarray(z), w1p, vec(b1), w2p, vec(b2), vec(freq), w3p, w3p, jnp.asarray(dec), f_bf)


def _hyena_kernel(L, x0_ref, x1_ref, vv_ref, g_ref, cw0_ref, cw1_ref, cwv_ref, cb0_ref, cb1_ref,
                  cbv_ref, bias_ref, coef_ref, f_ref, ft_ref, *rest):
    o_ref = rest[-1]
    row = lax.broadcasted_iota(jnp.int32, (L, 1), 0)
    is_first = row == 0
    is_last = row == L - 1

    def sconv(x_ref, w_ref, b_ref):
        x = x_ref[...]
        prev = jnp.where(is_first, 0.0, pltpu.roll(x, 1, axis=0))
        nxt = jnp.where(is_last, 0.0, pltpu.roll(x, L - 1, axis=0))
        return w_ref[0, 0:1, :] * prev + w_ref[0, 1:2, :] * x + w_ref[0, 2:3, :] * nxt + b_ref[0]

    z = sconv(vv_ref, cwv_ref, cbv_ref) * sconv(x1_ref, cw1_ref, cb1_ref)
    ab = jnp.dot(f_ref[...], z.astype(BF16), preferred_element_type=F32)
    a, b = ab[0:L], ab[L:2 * L]
    g1, g2, g3 = coef_ref[0, 0], coef_ref[0, 1], coef_ref[0, 2]
    pq = jnp.concatenate([a * g1 - b * g2, a * g2 + b * g3], axis=0).astype(BF16)
    y = jnp.dot(ft_ref[...], pq, preferred_element_type=F32) + bias_ref[0] * z
    o_ref[...] = (sconv(x0_ref, cw0_ref, cb0_ref) * y * _silu(g_ref[...])).astype(BF16)


def _hyena(proj, L, nb, row0, cb, layer, conv_w, conv_b, hy_bias, coef, f_bf, ft_bf, hy_prev=None):
    ncb = D_HY // cb
    rows = L
    blk0 = row0 // rows
    extra_specs = [] if hy_prev is None else [pl.BlockSpec(memory_space=pl.ANY)]
    extra_args = [] if hy_prev is None else [hy_prev]
    aliases = {} if hy_prev is None else {14: 0}
    col = lambda off: (lambda c, b: (blk0 + b, off // cb + c))
    cw = lambda part: pl.BlockSpec((1, 3, cb), lambda c, b: (layer, 0, part * ncb + c))
    cbias = lambda part: pl.BlockSpec((1, 1, cb), lambda c, b: (layer, 0, part * ncb + c))
    once = pl.Buffered(1)
    return pl.pallas_call(
        functools.partial(_hyena_kernel, L),
        grid=(ncb, nb),
        in_specs=[
            pl.BlockSpec((rows, cb), col(COL_X0)),
            pl.BlockSpec((rows, cb), col(COL_X1)),
            pl.BlockSpec((rows, cb), col(COL_VV)),
            pl.BlockSpec((rows, cb), col(COL_GH)),
            cw(0), cw(1), cw(2), cbias(0), cbias(1), cbias(2),
            pl.BlockSpec((1, 1, cb), lambda c, b: (layer, 0, c)),
            pl.BlockSpec((1, 3, L, cb), lambda c, b: (layer, 0, 0, c)),
            pl.BlockSpec((2 * L, L), lambda c, b: (0, 0), pipeline_mode=once),
            pl.BlockSpec((L, 2 * L), lambda c, b: (0, 0), pipeline_mode=once),
            *extra_specs,
        ],
        out_specs=pl.BlockSpec((rows, cb), lambda c, b: (blk0 + b, c)),
        out_shape=jax.ShapeDtypeStruct((M_ALL, D_HY), BF16),
        input_output_aliases=aliases,
        compiler_params=_params("arbitrary", "arbitrary"),
        name=f"hyena_{L}",
    )(proj, proj, proj, proj, conv_w, conv_w, conv_w, conv_b, conv_b, conv_b, hy_bias, coef, f_bf, ft_bf,
      *extra_args)


OUTPROJ_TM = 256
OUTPROJ_TM_FINAL = 512


def _outproj_kernel(split_in, final, n_ctx_tiles, *refs):
    it = iter(refs)
    att_ref, hy_ref, w_ref = next(it), next(it), next(it)
    x_refs = (next(it), next(it)) if split_in else (next(it),)
    gate_ref, g_ref = next(it), next(it)
    shift_ref, scale_ref = (None, None) if final else (next(it), next(it))
    out_refs = (next(it),) if final else (next(it), next(it))
    w_scr = next(it)
    i = pl.program_id(0)

    @pl.when(i == 0)
    def _():
        w_scr[...] = w_ref[0].astype(BF16)

    out = (jnp.dot(att_ref[...], w_scr[0:D_ATT, :], preferred_element_type=F32)
           + jnp.dot(hy_ref[...], w_scr[D_ATT:D_ATT + D_HY, :], preferred_element_type=F32))
    if split_in:
        x = jnp.where(i < n_ctx_tiles, x_refs[0][...], x_refs[1][...])
    else:
        x = x_refs[0][...]
    y = x + gate_ref[0, 0] * out
    inv = lax.rsqrt(jnp.mean(y * y, axis=-1, keepdims=True) + EPS)
    if final:
        out_refs[0][...] = y * inv * g_ref[0]
    else:
        out_refs[0][...] = y
        gain = g_ref[0] * (1.0 + scale_ref[0, 0])
        out_refs[1][...] = (y * inv * gain + shift_ref[0, 0]).astype(BF16)


def _out_proj(att, hy, w_out, layer, x_parts, mod4, gains3, final, tm, tile0=0, n_tiles=None):
    n_tiles = M_ALL // tm if n_tiles is None else n_tiles
    split_in = len(x_parts) == 2
    assert not split_in or (tile0 == 0 and n_tiles == M_ALL // tm)
    row_tile = lambda width: pl.BlockSpec((tm, width), lambda i: (tile0 + i, 0))
    out_tile = pl.BlockSpec((tm, D_MODEL), lambda i: (i, 0))
    x_specs = list(_two_stream_specs(tm, D_MODEL)) if split_in else [row_tile(D_MODEL)]
    in_specs = [
        row_tile(D_ATT),
        row_tile(D_HY),
        pl.BlockSpec((1, D_ATT + D_HY, D_MODEL), lambda i: (layer, 0, 0), pipeline_mode=pl.Buffered(1)),
        *x_specs,
        _mod_spec(tm, layer, MOD_GATE, tile0),
    ]
    args = [att, hy, w_out, *x_parts, mod4]
    if final:
        in_specs.append(pl.BlockSpec((1, 1, D_MODEL), lambda i: (0, 0, 0)))
        args.append(gains3)
        out_specs = [out_tile]
        out_shape = [jax.ShapeDtypeStruct((n_tiles * tm, D_MODEL), F32)]
    else:
        in_specs += [pl.BlockSpec((1, 1, D_MODEL), lambda i: (layer + 1, 0, 0)),
                     _mod_spec(tm, layer + 1, MOD_SHIFT, tile0), _mod_spec(tm, layer + 1, MOD_SCALE, tile0)]
        args += [gains3, mod4, mod4]
        out_specs = [out_tile, out_tile]
        out_shape = [jax.ShapeDtypeStruct((n_tiles * tm, D_MODEL), F32),
                     jax.ShapeDtypeStruct((n_tiles * tm, D_MODEL), BF16)]
    return pl.pallas_call(
        functools.partial(_outproj_kernel, split_in, final, M_CTX // tm),
        grid=(n_tiles,),
        in_specs=in_specs,
        out_specs=out_specs,
        out_shape=out_shape,
        scratch_shapes=[pltpu.VMEM((D_ATT + D_HY, D_MODEL), BF16)],
        compiler_params=_params("arbitrary"),
        name="out_proj_final" if final else "out_proj",
    )(*args)


def _rope_tables():
    t = np.arange(DEC_SEQ)
    row = (t // GRID_W).astype(np.float64)
    col = (t % GRID_W).astype(np.float64)
    pairs = HEAD_DIM // 4
    inv_freq = ROPE_THETA ** (-np.arange(pairs, dtype=np.float64) / pairs)
    ang = np.concatenate([row[:, None] * inv_freq, col[:, None] * inv_freq], axis=-1)
    cos = np.repeat(np.cos(ang), 2, axis=-1).astype(np.float32)
    sin = np.repeat(np.sin(ang), 2, axis=-1).astype(np.float32)
    even = (np.arange(HEAD_DIM) % 2 == 0)[None, :]
    sin_a = np.where(even, -sin, 0.0).astype(np.float32)
    sin_b = np.where(even, 0.0, sin).astype(np.float32)
    return jnp.asarray(cos), jnp.asarray(sin_a), jnp.asarray(sin_b)


def kernel(x_prompt, x_sample, cache_k, cache_v, c, c_ctx, norm_g, w_ada, b_ada, w_in, q_norm_g, k_norm_g,
           conv_w, conv_b, filt_w1, filt_b1, filt_w2, filt_b2, filt_w3, filt_freq, hy_bias, w_out, final_norm_g):
    ctx = x_prompt.reshape(BATCH * SEQ, D_MODEL)
    lat = x_sample.reshape(DEC_BATCH * DEC_SEQ, D_MODEL)
    cache_k4 = cache_k.reshape(DEC_BATCH, DEPTH, PAST_LEN, D_KV)
    cache_v4 = cache_v.reshape(DEC_BATCH, DEPTH, PAST_LEN, D_KV)

    cvec = jnp.concatenate([c_ctx[None, :], c, jnp.zeros((MOD_ROWS - 1 - DEC_BATCH, D_MODEL), F32)], axis=0)
    mod = _modulation(cvec, w_ada, b_ada)

    rope_tabs = _rope_tables()
    dft = {}
    coefs = {}
    for L in (SEQ, DEC_SEQ):
        f_np = _dft_matrix(L)
        f_bf = jnp.asarray(f_np).astype(BF16)
        dft[L] = (f_bf, jnp.asarray(np.ascontiguousarray(f_np.T)).astype(BF16))
        coefs[L] = _filter_spectra(L, filt_w1, filt_b1, filt_w2, filt_b2, filt_freq, filt_w3, f_bf)

    mod4 = mod.reshape(DEPTH, MOD_ROWS, 1, 3 * D_MODEL)
    norm_g3 = norm_g.reshape(DEPTH, 1, D_MODEL)
    final_g3 = final_norm_g.reshape(1, 1, D_MODEL)
    conv_b3 = conv_b.reshape(DEPTH, 1, 3 * D_HY)
    hy_bias3 = hy_bias.reshape(DEPTH, 1, D_HY)

    x_parts = (ctx, lat)
    h = _norm_mod(ctx, lat, norm_g3, mod4, 0)
    kv_out = None
    for l in range(DEPTH):
        qg = q_norm_g[l].reshape(1, HEAD_DIM)
        kg = k_norm_g[l].reshape(1, HEAD_DIM)
        final = l == DEPTH - 1

        proj = _in_proj(h, w_in, l)

        att, new_k, new_v = _attention(proj, qg, kg, l, SEQ, CTX_TQ, BATCH, 0, nsb=4, kv_prev=kv_out,
                                       emit_kv=True)
        kv_out = (new_k, new_v)
        (att,) = _attention(proj, qg, kg, l, DEC_SEQ, LAT_TQ, DEC_BATCH, M_CTX, cache=(cache_k4, cache_v4),
                            rope_tabs=rope_tabs, att_prev=att)

        hy = _hyena(proj, SEQ, BATCH, 0, D_HY, l, conv_w, conv_b3, hy_bias3, coefs[SEQ], *dft[SEQ])
        hy = _hyena(proj, DEC_SEQ, DEC_BATCH, M_CTX, 512, l, conv_w, conv_b3, hy_bias3,
                    coefs[DEC_SEQ], *dft[DEC_SEQ], hy_prev=hy)

        if final:
            assert len(x_parts) == 1
            tm = OUTPROJ_TM_FINAL
            nc = M_CTX // tm
            (y_ctx,) = _out_proj(att, hy, w_out, l, x_parts, mod4, final_g3, True, tm, 0, nc)
            (y_lat,) = _out_proj(att, hy, w_out, l, x_parts, mod4, final_g3, True, tm, nc, M_LAT // tm)
        else:
            y, h = _out_proj(att, hy, w_out, l, x_parts, mod4, norm_g3, False, OUTPROJ_TM)
            x_parts = (y,)

    y_prompt = y_ctx.reshape(BATCH, SEQ, D_MODEL)
    y_sample = y_lat.reshape(DEC_BATCH, DEC_SEQ, D_MODEL)
    return (y_prompt, y_sample, kv_out[0], kv_out[1])
```

```python
import functools
import math

import numpy as np
import jax
import jax.numpy as jnp
from jax import lax
from jax.experimental import pallas as pl
from jax.experimental.pallas import tpu as pltpu

D_MODEL = 2048
BATCH = 16
SEQ = 256
DEPTH = 2
DEC_BATCH = 2
DEC_SEQ = 1024
PAST_LEN = 512
GRID_W = 64
D_ATT = 1024
D_HY = 1024
HEAD_DIM = 128
N_HEADS = 8
N_KV_HEADS = 4
GROUP = 2
D_KV = 512
ROPE_THETA = 10000.0
POS_BANDS = 16
POS_EMB = 33
FILT_HID = 64
DECAY_TARGET = 1e-2
FAST_DECAY_PCT = 0.3
SLOW_DECAY_PCT = 1.5
DECAY_SHIFT = 0.05
EPS = 1e-6
D_IN = 7168

COL_Q, COL_K, COL_V, COL_GA, COL_X0, COL_X1, COL_VV, COL_GH = 0, 1024, 1536, 2048, 3072, 4096, 5120, 6144

F32 = jnp.float32
BF16 = jnp.bfloat16

VMEM_LIMIT_BYTES = 56 * 1024 * 1024
FEAT_PAD = 128
def _params(*sem):
    return pltpu.CompilerParams(dimension_semantics=sem, vmem_limit_bytes=VMEM_LIMIT_BYTES)


def _silu(x):
    half = 0.5 * x
    return half + half * jnp.tanh(half)


def _rms(x, g):
    return x * lax.rsqrt(jnp.mean(x * x, axis=-1, keepdims=True) + EPS) * g


MOD_ROWS = 8
MOD_TN = 1024


def _mod_kernel(c_ref, w_ref, b_ref, o_ref):
    s = _silu(c_ref[...]).astype(BF16)
    o_ref[0] = jnp.dot(s, w_ref[0].astype(BF16), preferred_element_type=F32) + b_ref[0]


def _modulation(cvec, w_ada, b_ada):
    n = 3 * D_MODEL
    return pl.pallas_call(
        _mod_kernel,
        grid=(DEPTH, n // MOD_TN),
        in_specs=[
            pl.BlockSpec((MOD_ROWS, D_MODEL), lambda l, j: (0, 0)),
            pl.BlockSpec((1, D_MODEL, MOD_TN), lambda l, j: (l, 0, j)),
            pl.BlockSpec((1, 1, MOD_TN), lambda l, j: (l, 0, j)),
        ],
        out_specs=pl.BlockSpec((1, MOD_ROWS, MOD_TN), lambda l, j: (l, 0, j)),
        out_shape=jax.ShapeDtypeStruct((DEPTH, MOD_ROWS, n), F32),
        compiler_params=_params("arbitrary", "arbitrary"),
        name="adaln_mod",
    )(cvec, w_ada, b_ada.reshape(DEPTH, 1, n))


M_CTX = BATCH * SEQ
M_LAT = DEC_BATCH * DEC_SEQ
M_ALL = M_CTX + M_LAT
NORM_TM = 1024
INPROJ_TM = 3072
INPROJ_TN = 512


ROW_CHUNK = 32


def _modulated_norm_rows(x_ref, h_ref, g, shift, scale):
    gain = g * (1.0 + scale)

    def body(r, _):
        rows = pl.ds(pl.multiple_of(r * ROW_CHUNK, ROW_CHUNK), ROW_CHUNK)
        x = x_ref[rows, :]
        inv = lax.rsqrt(jnp.mean(x * x, axis=-1, keepdims=True) + EPS)
        h_ref[rows, :] = (x * inv * gain + shift).astype(BF16)
        return 0

    lax.fori_loop(0, x_ref.shape[0] // ROW_CHUNK, body, 0, unroll=4)


MOD_SHIFT, MOD_SCALE, MOD_GATE = 0, 1, 2


def _mod_spec(tm, layer, part, tile0=0):
    nc = M_CTX // tm
    per_batch = DEC_SEQ // tm
    row = lambda t: jnp.where(t < nc, 0, 1 + (t - nc) // per_batch)
    return pl.BlockSpec((1, 1, 1, D_MODEL), lambda i: (layer, row(tile0 + i), 0, part))


def _two_stream_specs(tm, width):
    nc = M_CTX // tm
    return (pl.BlockSpec((tm, width), lambda i: (jnp.minimum(i, nc - 1), 0)),
            pl.BlockSpec((tm, width), lambda i: (jnp.maximum(i - nc, 0), 0)))


def _norm_kernel(n_ctx_tiles, xc_ref, xl_ref, g_ref, shift_ref, scale_ref, h_ref):
    def emit(x_ref):
        _modulated_norm_rows(x_ref, h_ref, g_ref[0], shift_ref[0, 0], scale_ref[0, 0])

    i = pl.program_id(0)
    pl.when(i < n_ctx_tiles)(lambda: emit(xc_ref))
    pl.when(i >= n_ctx_tiles)(lambda: emit(xl_ref))


def _norm_mod(x_ctx, x_lat, norm_g3, mod4, layer):
    xc_spec, xl_spec = _two_stream_specs(NORM_TM, D_MODEL)
    return pl.pallas_call(
        functools.partial(_norm_kernel, M_CTX // NORM_TM),
        grid=(M_ALL // NORM_TM,),
        in_specs=[
            xc_spec, xl_spec,
            pl.BlockSpec((1, 1, D_MODEL), lambda i: (layer, 0, 0)),
            _mod_spec(NORM_TM, layer, MOD_SHIFT),
            _mod_spec(NORM_TM, layer, MOD_SCALE),
        ],
        out_specs=pl.BlockSpec((NORM_TM, D_MODEL), lambda i: (i, 0)),
        out_shape=jax.ShapeDtypeStruct((M_ALL, D_MODEL), BF16),
        compiler_params=_params("arbitrary"),
        name="norm_mod",
    )(x_ctx, x_lat, norm_g3, mod4, mod4)


def _inproj_kernel(h_ref, w_ref, o_ref, w_scr):
    i = pl.program_id(1)

    @pl.when(i == 0)
    def _():
        w_scr[...] = w_ref[0].astype(BF16)

    rows = pl.ds(pl.multiple_of(i * INPROJ_TM, INPROJ_TM), INPROJ_TM)
    o_ref[...] = jnp.dot(h_ref[rows, :], w_scr[...], preferred_element_type=F32)


def _in_proj(h, w_in, layer):
    return pl.pallas_call(
        _inproj_kernel,
        grid=(D_IN // INPROJ_TN, M_ALL // INPROJ_TM),
        in_specs=[
            pl.BlockSpec((M_ALL, D_MODEL), lambda j, i: (0, 0), pipeline_mode=pl.Buffered(1)),
            pl.BlockSpec((1, D_MODEL, INPROJ_TN), lambda j, i: (layer, 0, j)),
        ],
        out_specs=pl.BlockSpec((INPROJ_TM, INPROJ_TN), lambda j, i: (i, j)),
        out_shape=jax.ShapeDtypeStruct((M_ALL, D_IN), F32),
        scratch_shapes=[pltpu.VMEM((D_MODEL, INPROJ_TN), BF16)],
        compiler_params=_params("arbitrary", "arbitrary"),
        name="in_proj",
    )(h, w_in)


ATT_SCALE = 1.0 / math.sqrt(HEAD_DIM)


def _head(ref_or_val, h):
    return ref_or_val[:, h * HEAD_DIM:(h + 1) * HEAD_DIM]


Q_PRESCALE = ATT_SCALE * math.log2(math.e)


def _softmax_pv(q2, k_bf, v_ext):
    s = lax.dot_general(q2, k_bf, (((1,), (1,)), ((), ())), preferred_element_type=F32)
    p = jnp.exp2(s - jnp.max(s, axis=-1, keepdims=True)).astype(BF16)
    o = jnp.dot(p, v_ext, preferred_element_type=F32)
    return o[:, 0:HEAD_DIM] / o[:, HEAD_DIM:2 * HEAD_DIM]


def _rope(x, cos2, sin_a, sin_b):
    nxt = pltpu.roll(x, HEAD_DIM - 1, axis=1)
    prv = pltpu.roll(x, 1, axis=1)
    return x * cos2 + nxt * sin_a + prv * sin_b


CTX_TQ = SEQ
LAT_TQ = 128


def _attn_kernel(rope, n_cache, emit_kv_layer, n_aliased, nsb, *refs):
    emit_kv = emit_kv_layer is not None
    it = iter(refs)
    q_all, kv_all, g_all, qg_ref, kg_ref = (next(it) for _ in range(5))
    ck_ref, cv_ref = (next(it), next(it)) if n_cache else (None, None)
    q_tabs = tuple(next(it) for _ in range(3)) if rope else None
    k_tabs = tuple(next(it) for _ in range(3)) if rope else None
    for _ in range(n_aliased):
        next(it)
    att_all = next(it)
    ko_hbm, vo_hbm = (next(it), next(it)) if emit_kv else (None, None)
    k_all, v_all = next(it), next(it)
    kf_all, kv_sems = (next(it), next(it)) if emit_kv else (None, None)
    single_tile = q_all.shape[0] == kv_all.shape[0]
    n_new = kv_all.shape[0] // nsb if single_tile else kv_all.shape[0]
    assert not emit_kv or single_tile

    def part(ref, sb, rows_per_seq):
        return ref if nsb == 1 else ref.at[pl.ds(sb * rows_per_seq, rows_per_seq)]

    def kv_copy(sb, h, is_value):
        b = pl.program_id(0) * nsb + sb
        if is_value:
            src = part(kv_all, sb, n_new).at[:, pl.ds((N_KV_HEADS + h) * HEAD_DIM, HEAD_DIM)]
            return pltpu.make_async_copy(src, vo_hbm.at[b, emit_kv_layer, :, h, :],
                                         kv_sems.at[sb, N_KV_HEADS + h])
        src = part(kf_all, sb, n_new).at[:, pl.ds(h * HEAD_DIM, HEAD_DIM)]
        return pltpu.make_async_copy(src, ko_hbm.at[b, emit_kv_layer, :, h, :], kv_sems.at[sb, h])

    def prepare_keys_values(sb):
        kv_ref, k_scr, v_scr = part(kv_all, sb, n_new), part(k_all, sb, n_new + n_cache), part(v_all, sb, n_new + n_cache)
        kf_scr = part(kf_all, sb, n_new) if emit_kv else None
        for h in range(N_KV_HEADS):
            kn = _rms(_head(kv_ref, h), kg_ref[...])
            vh = _head(kv_ref, N_KV_HEADS + h)
            if emit_kv:
                kf_scr[:, h * HEAD_DIM:(h + 1) * HEAD_DIM] = kn
            if rope:
                kn = _rope(kn, *(t[...] for t in k_tabs))
            k_scr[0:n_new, h * HEAD_DIM:(h + 1) * HEAD_DIM] = kn.astype(BF16)
            v0 = 2 * h * HEAD_DIM
            v_scr[0:n_new, v0:v0 + HEAD_DIM] = vh.astype(BF16)
            if n_cache:
                v_scr[n_new:n_new + n_cache, v0:v0 + HEAD_DIM] = _head(cv_ref[0, 0], h).astype(BF16)
            v_scr[:, v0 + HEAD_DIM:v0 + 2 * HEAD_DIM] = jnp.ones((n_new + n_cache, HEAD_DIM), BF16)
        if n_cache:
            k_scr[n_new:n_new + n_cache, :] = ck_ref[0, 0].astype(BF16)

    def attend(sb):
        rows = q_all.shape[0] // nsb
        q_ref, g_ref, att_ref = part(q_all, sb, rows), part(g_all, sb, rows), part(att_all, sb, rows)
        kv_sb = sb if single_tile else 0
        k_scr, v_scr = part(k_all, kv_sb, n_new + n_cache), part(v_all, kv_sb, n_new + n_cache)

        def query(hq):
            x = _rms(_head(q_ref, hq), qg_ref[...])
            if rope:
                x = _rope(x, *(part(t, sb, rows)[...] for t in q_tabs))
            return x * Q_PRESCALE

        for h in range(N_KV_HEADS):
            q2 = jnp.concatenate([query(GROUP * h + g) for g in range(GROUP)], axis=0).astype(BF16)
            o = _softmax_pv(q2, _head(k_scr, h), v_scr[:, 2 * h * HEAD_DIM:2 * (h + 1) * HEAD_DIM])
            for g in range(GROUP):
                hq = GROUP * h + g
                gate = _silu(_head(g_ref, hq))
                att_ref[:, hq * HEAD_DIM:(hq + 1) * HEAD_DIM] = (o[g * rows:(g + 1) * rows] * gate).astype(BF16)

    every = [(sb, h) for sb in range(nsb) for h in range(N_KV_HEADS)]
    if single_tile:
        if emit_kv:
            for sb, h in every:
                kv_copy(sb, h, True).start()
        for sb in range(nsb):
            prepare_keys_values(sb)
        if emit_kv:
            for sb, h in every:
                kv_copy(sb, h, False).start()
    else:
        pl.when(pl.program_id(1) == 0)(lambda: prepare_keys_values(0))

    for sb in range(nsb):
        attend(sb)

    if emit_kv:
        for sb, h in every:
            kv_copy(sb, h, False).wait()
            kv_copy(sb, h, True).wait()


def _attention(proj, q_g, k_g, layer, seq, tq, n_batch, row0, *, nsb=1, cache=None, rope_tabs=None,
               att_prev=None, kv_prev=None, emit_kv=False):
    whole = tq == seq
    assert not whole or cache is None
    tq = nsb * tq
    kv_rows = nsb * seq if whole else seq
    nq = kv_rows // tq
    q0, kv0 = row0 // tq, row0 // kv_rows
    n_cache = 0 if cache is None else cache[0].shape[2]
    vec = pl.BlockSpec((1, HEAD_DIM), lambda b, i: (0, 0))
    in_specs = [
        pl.BlockSpec((tq, D_ATT), lambda b, i: (q0 + b * nq + i, COL_Q // D_ATT)),
        pl.BlockSpec((kv_rows, 2 * D_KV), lambda b, i: (kv0 + b, COL_K // (2 * D_KV))),
        pl.BlockSpec((tq, D_ATT), lambda b, i: (q0 + b * nq + i, COL_GA // D_ATT)),
        vec, vec,
    ]
    args = [proj, proj, proj, q_g, k_g]
    if cache is not None:
        in_specs += [pl.BlockSpec((1, 1, n_cache, D_KV), lambda b, i: (b, layer, 0, 0))] * 2
        args += list(cache)
    if rope_tabs is not None:
        in_specs += [pl.BlockSpec((tq, HEAD_DIM), lambda b, i: (i, 0))] * 3
        in_specs += [pl.BlockSpec((seq, HEAD_DIM), lambda b, i: (0, 0))] * 3
        args += list(rope_tabs) * 2
    aliased = ([] if att_prev is None else [att_prev]) + ([] if kv_prev is None else list(kv_prev))
    out_first = 0 if att_prev is not None else 1
    aliases = {len(args) + n: out_first + n for n in range(len(aliased))}
    in_specs += [pl.BlockSpec(memory_space=pl.ANY)] * len(aliased)
    args += aliased
    out_specs = [pl.BlockSpec((tq, D_ATT), lambda b, i: (q0 + b * nq + i, 0))]
    out_shape = [jax.ShapeDtypeStruct((M_ALL, D_ATT), BF16)]
    keys = kv_rows + n_cache
    scratch =[pltpu.VMEM((keys, D_KV), BF16), pltpu.VMEM((keys, 2 * D_KV), BF16)]
    if emit_kv:
        out_specs += [pl.BlockSpec(memory_space=pl.ANY)] * 2
        out_shape += [jax.ShapeDtypeStruct((n_batch, DEPTH, seq, N_KV_HEADS, HEAD_DIM), F32)] * 2
        scratch += [pltpu.VMEM((kv_rows, D_KV), F32), pltpu.SemaphoreType.DMA((nsb, 2 * N_KV_HEADS))]
    return pl.pallas_call(
        functools.partial(_attn_kernel, rope_tabs is not None, n_cache, layer if emit_kv else None, len(aliased),
                          nsb),
        grid=(n_batch * seq // kv_rows, nq),
        in_specs=in_specs,
        out_specs=out_specs,
        out_shape=out_shape,
        input_output_aliases=aliases,
        scratch_shapes=scratch,
        compiler_params=_params("arbitrary", "arbitrary"),
        name=f"attention_{seq}",
    )(*args)


def _dft_matrix(L):
    k = np.arange(L, dtype=np.int64)[:, None]
    t = np.arange(L, dtype=np.int64)[None, :]
    ang = 2.0 * np.pi * ((k * t) % (2 * L)).astype(np.float64) / (2 * L)
    top = np.cos(ang)
    bot = np.sin(ang)
    bot[0, :] = np.where(np.arange(L) % 2 == 0, 1.0, -1.0)
    return np.concatenate([top, bot], axis=0).astype(np.float32)


def _filter_kernel(L, z_ref, w1_ref, b1_ref, w2_ref, b2_ref, fr_ref, w3f_ref, w3b_ref, dec_ref,
                   f_ref, o_ref, hdn_scr):
    hp = lax.Precision.HIGHEST

    @pl.when(pl.program_id(1) == 0)
    def _():
        fr = fr_ref[0]
        h1 = jnp.sin(fr * (jnp.dot(z_ref[...], w1_ref[0], precision=hp, preferred_element_type=F32) + b1_ref[0]))
        hdn_scr[...] = jnp.sin(fr * (jnp.dot(h1, w2_ref[0], precision=hp, preferred_element_type=F32) + b2_ref[0]))

    hdn = hdn_scr[...]
    dec = dec_ref[...]
    hdn_bf = hdn.astype(BF16)
    h_f = jnp.dot(hdn_bf, w3f_ref[0].astype(BF16), preferred_element_type=F32) * dec
    h_b = jnp.dot(hdn_bf, w3b_ref[0].astype(BF16), preferred_element_type=F32) * dec
    hs = h_f + h_b
    ha = jnp.dot(f_ref[0:L, :], hs.astype(BF16), preferred_element_type=F32)
    hb = jnp.dot(f_ref[L:2 * L, :], (h_f - h_b).astype(BF16), preferred_element_type=F32)
    row = lax.broadcasted_iota(jnp.int32, (L, 1), 0)
    first = row == 0
    nyquist = jnp.sum(jnp.where(row % 2 == 0, hs, -hs), axis=0, keepdims=True)
    wk = jnp.where(first, 1.0 / (2 * L), 2.0 / (2 * L))
    o_ref[0, 0] = ha * wk
    o_ref[0, 1] = jnp.where(first, 0.0, hb * wk)
    o_ref[0, 2] = jnp.where(first, nyquist, ha) * wk


FILT_CB = 512


def _filter_spectra(L, w1, b1, w2, b2, freq, w3, f_bf):
    tpos = np.arange(L, dtype=np.float64)
    t_norm = tpos / max(L - 1, 1)
    w = 2.0 * math.pi * tpos / L
    bands = np.linspace(1e-4, POS_BANDS - 1, POS_BANDS)
    z = np.concatenate([t_norm[:, None], np.cos(w[:, None] * bands), -np.sin(w[:, None] * bands)], axis=-1)
    z = np.pad(z, ((0, 0), (0, FEAT_PAD - POS_EMB))).astype(np.float32)
    max_decay = math.log(DECAY_TARGET) / FAST_DECAY_PCT
    min_decay = math.log(DECAY_TARGET) / SLOW_DECAY_PCT
    deltas = np.abs(np.linspace(min_decay, max_decay, D_HY))
    dec = (np.exp(-t_norm[:, None] * deltas) + DECAY_SHIFT).astype(np.float32)

    pad_h = FEAT_PAD - FILT_HID
    w1p = jnp.pad(w1, ((0, 0), (0, FEAT_PAD - POS_EMB), (0, pad_h)))
    w2p = jnp.pad(w2, ((0, 0), (0, pad_h), (0, pad_h)))
    w3p = jnp.pad(w3, ((0, 0), (0, pad_h), (0, 0)))
    vec = lambda v: jnp.pad(v, ((0, 0), (0, pad_h))).reshape(DEPTH, 1, FEAT_PAD)
    ncb = D_HY // FILT_CB
    small = lambda shape: pl.BlockSpec((1,) + shape, lambda l, c: (l, 0, 0))
    return pl.pallas_call(
        functools.partial(_filter_kernel, L),
        grid=(DEPTH, ncb),
        in_specs=[
            pl.BlockSpec((L, FEAT_PAD), lambda l, c: (0, 0)),
            small((FEAT_PAD, FEAT_PAD)), small((1, FEAT_PAD)),
            small((FEAT_PAD, FEAT_PAD)), small((1, FEAT_PAD)), small((1, FEAT_PAD)),
            pl.BlockSpec((1, FEAT_PAD, FILT_CB), lambda l, c: (l, 0, c)),
            pl.BlockSpec((1, FEAT_PAD, FILT_CB), lambda l, c: (l, 0, ncb + c)),
            pl.BlockSpec((L, FILT_CB), lambda l, c: (0, c)),
            pl.BlockSpec((2 * L, L), lambda l, c: (0, 0)),
        ],
        out_specs=pl.BlockSpec((1, 3, L, FILT_CB), lambda l, c: (l, 0, 0, c)),
        out_shape=jax.ShapeDtypeStruct((DEPTH, 3, L, D_HY), F32),
        scratch_shapes=[pltpu.VMEM((L, FEAT_PAD), F32)],
        compiler_params=_params("arbitrary", "arbitrary"),
        name=f"hyena_filter_{L}",
    )(jnp.asarray(z), w1p, vec(b1), w2p, vec(b2), vec(freq), w3p, w3p, jnp.asarray(dec), f_bf)


def _hyena_kernel(L, x0_ref, x1_ref, vv_ref, g_ref, cw0_ref, cw1_ref, cwv_ref, cb0_ref, cb1_ref,
                  cbv_ref, bias_ref, coef_ref, f_ref, ft_ref, *rest):
    o_ref = rest[-1]
    row = lax.broadcasted_iota(jnp.int32, (L, 1), 0)
    is_first = row == 0
    is_last = row == L - 1

    def sconv(x_ref, w_ref, b_ref):
        x = x_ref[...]
        prev = jnp.where(is_first, 0.0, pltpu.roll(x, 1, axis=0))
        nxt = jnp.where(is_last, 0.0, pltpu.roll(x, L - 1, axis=0))
        return w_ref[0, 0:1, :] * prev + w_ref[0, 1:2, :] * x + w_ref[0, 2:3, :] * nxt + b_ref[0]

    z = sconv(vv_ref, cwv_ref, cbv_ref) * sconv(x1_ref, cw1_ref, cb1_ref)
    ab = jnp.dot(f_ref[...], z.astype(BF16), preferred_element_type=F32)
    a, b = ab[0:L], ab[L:2 * L]
    g1, g2, g3 = coef_ref[0, 0], coef_ref[0, 1], coef_ref[0, 2]
    pq = jnp.concatenate([a * g1 - b * g2, a * g2 + b * g3], axis=0).astype(BF16)
    y = jnp.dot(ft_ref[...], pq, preferred_element_type=F32) + bias_ref[0] * z
    o_ref[...] = (sconv(x0_ref, cw0_ref, cb0_ref) * y * _silu(g_ref[...])).astype(BF16)


def _hyena(proj, L, nb, row0, cb, layer, conv_w, conv_b, hy_bias, coef, f_bf, ft_bf, hy_prev=None):
    ncb = D_HY // cb
    rows = L
    blk0 = row0 // rows
    extra_specs = [] if hy_prev is None else [pl.BlockSpec(memory_space=pl.ANY)]
    extra_args = [] if hy_prev is None else [hy_prev]
    aliases = {} if hy_prev is None else {14: 0}
    col = lambda off: (lambda c, b: (blk0 + b, off // cb + c))
    cw = lambda part: pl.BlockSpec((1, 3, cb), lambda c, b: (layer, 0, part * ncb + c))
    cbias = lambda part: pl.BlockSpec((1, 1, cb), lambda c, b: (layer, 0, part * ncb + c))
    once = pl.Buffered(1)
    return pl.pallas_call(
        functools.partial(_hyena_kernel, L),
        grid=(ncb, nb),
        in_specs=[
            pl.BlockSpec((rows, cb), col(COL_X0)),
            pl.BlockSpec((rows, cb), col(COL_X1)),
            pl.BlockSpec((rows, cb), col(COL_VV)),
            pl.BlockSpec((rows, cb), col(COL_GH)),
            cw(0), cw(1), cw(2), cbias(0), cbias(1), cbias(2),
            pl.BlockSpec((1, 1, cb), lambda c, b: (layer, 0, c)),
            pl.BlockSpec((1, 3, L, cb), lambda c, b: (layer, 0, 0, c)),
            pl.BlockSpec((2 * L, L), lambda c, b: (0, 0), pipeline_mode=once),
            pl.BlockSpec((L, 2 * L), lambda c, b: (0, 0), pipeline_mode=once),
            *extra_specs,
        ],
        out_specs=pl.BlockSpec((rows, cb), lambda c, b: (blk0 + b, c)),
        out_shape=jax.ShapeDtypeStruct((M_ALL, D_HY), BF16),
        input_output_aliases=aliases,
        compiler_params=_params("arbitrary", "arbitrary"),
        name=f"hyena_{L}",
    )(proj, proj, proj, proj, conv_w, conv_w, conv_w, conv_b, conv_b, conv_b, hy_bias, coef, f_bf, ft_bf,
      *extra_args)


OUTPROJ_TM = 256
OUTPROJ_TM_FINAL = 512


def _outproj_kernel(split_in, final, n_ctx_tiles, *refs):
    it = iter(refs)
    att_ref, hy_ref, w_ref = next(it), next(it), next(it)
    x_refs = (next(it), next(it)) if split_in else (next(it),)
    gate_ref, g_ref = next(it), next(it)
    shift_ref, scale_ref = (None, None) if final else (next(it), next(it))
    out_refs = (next(it),) if final else (next(it), next(it))
    w_scr = next(it)
    i = pl.program_id(0)

    @pl.when(i == 0)
    def _():
        w_scr[...] = w_ref[0].astype(BF16)

    out = (jnp.dot(att_ref[...], w_scr[0:D_ATT, :], preferred_element_type=F32)
           + jnp.dot(hy_ref[...], w_scr[D_ATT:D_ATT + D_HY, :], preferred_element_type=F32))
    if split_in:
        x = jnp.where(i < n_ctx_tiles, x_refs[0][...], x_refs[1][...])
    else:
        x = x_refs[0][...]
    y = x + gate_ref[0, 0] * out
    inv = lax.rsqrt(jnp.mean(y * y, axis=-1, keepdims=True) + EPS)
    if final:
        out_refs[0][...] = y * inv * g_ref[0]
    else:
        out_refs[0][...] = y
        gain = g_ref[0] * (1.0 + scale_ref[0, 0])
        out_refs[1][...] = (y * inv * gain + shift_ref[0, 0]).astype(BF16)


def _out_proj(att, hy, w_out, layer, x_parts, mod4, gains3, final, tm, tile0=0, n_tiles=None):
    n_tiles = M_ALL // tm if n_tiles is None else n_tiles
    split_in = len(x_parts) == 2
    assert not split_in or (tile0 == 0 and n_tiles == M_ALL // tm)
    row_tile = lambda width: pl.BlockSpec((tm, width), lambda i: (tile0 + i, 0))
    out_tile = pl.BlockSpec((tm, D_MODEL), lambda i: (i, 0))
    x_specs = list(_two_stream_specs(tm, D_MODEL)) if split_in else [row_tile(D_MODEL)]
    in_specs = [
        row_tile(D_ATT),
        row_tile(D_HY),
        pl.BlockSpec((1, D_ATT + D_HY, D_MODEL), lambda i: (layer, 0, 0), pipeline_mode=pl.Buffered(1)),
        *x_specs,
        _mod_spec(tm, layer, MOD_GATE, tile0),
    ]
    args = [att, hy, w_out, *x_parts, mod4]
    if final:
        in_specs.append(pl.BlockSpec((1, 1, D_MODEL), lambda i: (0, 0, 0)))
        args.append(gains3)
        out_specs = [out_tile]
        out_shape = [jax.ShapeDtypeStruct((n_tiles * tm, D_MODEL), F32)]
    else:
        in_specs += [pl.BlockSpec((1, 1, D_MODEL), lambda i: (layer + 1, 0, 0)),
                     _mod_spec(tm, layer + 1, MOD_SHIFT, tile0), _mod_spec(tm, layer + 1, MOD_SCALE, tile0)]
        args += [gains3, mod4, mod4]
        out_specs = [out_tile, out_tile]
        out_shape = [jax.ShapeDtypeStruct((n_tiles * tm, D_MODEL), F32),
                     jax.ShapeDtypeStruct((n_tiles * tm, D_MODEL), BF16)]
    return pl.pallas_call(
        functools.partial(_outproj_kernel, split_in, final, M_CTX // tm),
        grid=(n_tiles,),
        in_specs=in_specs,
        out_specs=out_specs,
        out_shape=out_shape,
        scratch_shapes=[pltpu.VMEM((D_ATT + D_HY, D_MODEL), BF16)],
        compiler_params=_params("arbitrary"),
        name="out_proj_final" if final else "out_proj",
    )(*args)


def _rope_tables():
    t = np.arange(DEC_SEQ)
    row = (t // GRID_W).astype(np.float64)
    col = (t % GRID_W).astype(np.float64)
    pairs = HEAD_DIM // 4
    inv_freq = ROPE_THETA ** (-np.arange(pairs, dtype=np.float64) / pairs)
    ang = np.concatenate([row[:, None] * inv_freq, col[:, None] * inv_freq], axis=-1)
    cos = np.repeat(np.cos(ang), 2, axis=-1).astype(np.float32)
    sin = np.repeat(np.sin(ang), 2, axis=-1).astype(np.float32)
    even = (np.arange(HEAD_DIM) % 2 == 0)[None, :]
    sin_a = np.where(even, -sin, 0.0).astype(np.float32)
    sin_b = np.where(even, 0.0, sin).astype(np.float32)
    return jnp.asarray(cos), jnp.asarray(sin_a), jnp.asarray(sin_b)


def kernel(x_prompt, x_sample, cache_k, cache_v, c, c_ctx, norm_g, w_ada, b_ada, w_in, q_norm_g, k_norm_g,
           conv_w, conv_b, filt_w1, filt_b1, filt_w2, filt_b2, filt_w3, filt_freq, hy_bias, w_out, final_norm_g):
    ctx = x_prompt.reshape(BATCH * SEQ, D_MODEL)
    lat = x_sample.reshape(DEC_BATCH * DEC_SEQ, D_MODEL)
    cache_k4 = cache_k.reshape(DEC_BATCH, DEPTH, PAST_LEN, D_KV)
    cache_v4 = cache_v.reshape(DEC_BATCH, DEPTH, PAST_LEN, D_KV)

    cvec = jnp.concatenate([c_ctx[None, :], c, jnp.zeros((MOD_ROWS - 1 - DEC_BATCH, D_MODEL), F32)], axis=0)
    mod = _modulation(cvec, w_ada, b_ada)

    rope_tabs = _rope_tables()
    dft = {}
    coefs = {}
    for L in (SEQ, DEC_SEQ):
        f_np = _dft_matrix(L)
        f_bf = jnp.asarray(f_np).astype(BF16)
        dft[L] = (f_bf, jnp.asarray(np.ascontiguousarray(f_np.T)).astype(BF16))
        coefs[L] = _filter_spectra(L, filt_w1, filt_b1, filt_w2, filt_b2, filt_freq, filt_w3, f_bf)

    mod4 = mod.reshape(DEPTH, MOD_ROWS, 1, 3 * D_MODEL)
    norm_g3 = norm_g.reshape(DEPTH, 1, D_MODEL)
    final_g3 = final_norm_g.reshape(1, 1, D_MODEL)
    conv_b3 = conv_b.reshape(DEPTH, 1, 3 * D_HY)
    hy_bias3 = hy_bias.reshape(DEPTH, 1, D_HY)

    x_parts = (ctx, lat)
    h = _norm_mod(ctx, lat, norm_g3, mod4, 0)
    kv_out = None
    for l in range(DEPTH):
        qg = q_norm_g[l].reshape(1, HEAD_DIM)
        kg = k_norm_g[l].reshape(1, HEAD_DIM)
        final = l == DEPTH - 1

        proj = _in_proj(h, w_in, l)

        att, new_k, new_v = _attention(proj, qg, kg, l, SEQ, CTX_TQ, BATCH, 0, nsb=4, kv_prev=kv_out,
                                       emit_kv=True)
        kv_out = (new_k, new_v)
        (att,) = _attention(proj, qg, kg, l, DEC_SEQ, LAT_TQ, DEC_BATCH, M_CTX, nsb=4, cache=(cache_k4, cache_v4),
                            rope_tabs=rope_tabs, att_prev=att)

        hy = _hyena(proj, SEQ, BATCH, 0, D_HY, l, conv_w, conv_b3, hy_bias3, coefs[SEQ], *dft[SEQ])
        hy = _hyena(proj, DEC_SEQ, DEC_BATCH, M_CTX, 512, l, conv_w, conv_b3, hy_bias3,
                    coefs[DEC_SEQ], *dft[DEC_SEQ], hy_prev=hy)

        if final:
            assert len(x_parts) == 1
            tm = OUTPROJ_TM_FINAL
            nc = M_CTX // tm
            (y_ctx,) = _out_proj(att, hy, w_out, l, x_parts, mod4, final_g3, True, tm, 0, nc)
            (y_lat,) = _out_proj(att, hy, w_out, l, x_parts, mod4, final_g3, True, tm, nc, M_LAT // tm)
        else:
            y, h = _out_proj(att, hy, w_out, l, x_parts, mod4, norm_g3, False, OUTPROJ_TM)
            x_parts = (y,)

    y_prompt = y_ctx.reshape(BATCH, SEQ, D_MODEL)
    y_sample = y_lat.reshape(DEC_BATCH, DEC_SEQ, D_MODEL)
    return (y_prompt, y_sample, kv_out[0], kv_out[1])
```

```python
import functools
import math

import numpy as np
import jax
import jax.numpy as jnp
from jax import lax
from jax.experimental import pallas as pl
from jax.experimental.pallas import tpu as pltpu

D_MODEL = 2048
BATCH = 16
SEQ = 256
DEPTH = 2
DEC_BATCH = 2
DEC_SEQ = 1024
PAST_LEN = 512
GRID_W = 64
D_ATT = 1024
D_HY = 1024
HEAD_DIM = 128
N_HEADS = 8
N_KV_HEADS = 4
GROUP = 2
D_KV = 512
ROPE_THETA = 10000.0
POS_BANDS = 16
POS_EMB = 33
FILT_HID = 64
DECAY_TARGET = 1e-2
FAST_DECAY_PCT = 0.3
SLOW_DECAY_PCT = 1.5
DECAY_SHIFT = 0.05
EPS = 1e-6
D_IN = 7168

COL_Q, COL_K, COL_V, COL_GA, COL_X0, COL_X1, COL_VV, COL_GH = 0, 1024, 1536, 2048, 3072, 4096, 5120, 6144

F32 = jnp.float32
BF16 = jnp.bfloat16

VMEM_LIMIT_BYTES = 56 * 1024 * 1024
FEAT_PAD = 128
def _params(*sem):
    return pltpu.CompilerParams(dimension_semantics=sem, vmem_limit_bytes=VMEM_LIMIT_BYTES)


def _silu(x):
    half = 0.5 * x
    return half + half * jnp.tanh(half)


def _rms(x, g):
    return x * lax.rsqrt(jnp.mean(x * x, axis=-1, keepdims=True) + EPS) * g


MOD_ROWS = 8
MOD_TN = 1024


def _mod_kernel(cctx_ref, c_ref, w_ref, b_ref, o_ref, c_scr):
    c_scr[...] = jnp.zeros(c_scr.shape, F32)
    c_scr[0:1, :] = cctx_ref[...]
    c_scr[1:1 + DEC_BATCH, :] = c_ref[...]
    s = _silu(c_scr[...]).astype(BF16)
    layer = pl.program_id(0)
    o_ref[0] = jnp.dot(s, w_ref[0].astype(BF16), preferred_element_type=F32) + b_ref[pl.ds(layer, 1), :]


def _modulation(c_ctx, c, w_ada, b_ada):
    n = 3 * D_MODEL
    return pl.pallas_call(
        _mod_kernel,
        grid=(DEPTH, n // MOD_TN),
        in_specs=[
            pl.BlockSpec((1, D_MODEL), lambda l, j: (0, 0)),
            pl.BlockSpec((DEC_BATCH, D_MODEL), lambda l, j: (0, 0)),
            pl.BlockSpec((1, D_MODEL, MOD_TN), lambda l, j: (l, 0, j)),
            pl.BlockSpec((DEPTH, MOD_TN), lambda l, j: (0, j)),
        ],
        out_specs=pl.BlockSpec((1, MOD_ROWS, MOD_TN), lambda l, j: (l, 0, j)),
        out_shape=jax.ShapeDtypeStruct((DEPTH, MOD_ROWS, n), F32),
        scratch_shapes=[pltpu.VMEM((MOD_ROWS, D_MODEL), F32)],
        compiler_params=_params("arbitrary", "arbitrary"),
        name="adaln_mod",
    )(c_ctx.reshape(1, D_MODEL), c, w_ada, b_ada)


M_CTX = BATCH * SEQ
M_LAT = DEC_BATCH * DEC_SEQ
M_ALL = M_CTX + M_LAT
NORM_TM = 1024
INPROJ_TM = 3072
INPROJ_TN = 512


ROW_CHUNK = 32


def _modulated_norm_rows(x_ref, h_ref, g, shift, scale):
    gain = g * (1.0 + scale)

    def body(r, _):
        rows = pl.ds(pl.multiple_of(r * ROW_CHUNK, ROW_CHUNK), ROW_CHUNK)
        x = x_ref[rows, :]
        inv = lax.rsqrt(jnp.mean(x * x, axis=-1, keepdims=True) + EPS)
        h_ref[rows, :] = (x * inv * gain + shift).astype(BF16)
        return 0

    lax.fori_loop(0, x_ref.shape[0] // ROW_CHUNK, body, 0, unroll=4)


MOD_SHIFT, MOD_SCALE, MOD_GATE = 0, 1, 2


def _mod_spec(layer, part):
    return pl.BlockSpec((1, MOD_ROWS, D_MODEL), lambda i: (layer, 0, part))


def _mod_row(ref, tile, tm):
    nc = M_CTX // tm
    per_batch = DEC_SEQ // tm
    row = jnp.where(tile < nc, 0, 1 + (tile - nc) // per_batch)
    return ref[0, pl.ds(row, 1), :]


def _two_stream_specs(tm, width):
    nc = M_CTX // tm
    return (pl.BlockSpec((tm, width), lambda i: (jnp.minimum(i, nc - 1), 0)),
            pl.BlockSpec((tm, width), lambda i: (jnp.maximum(i - nc, 0), 0)))


def _norm_kernel(layer, xc_ref, xl_ref, g_ref, shift_ref, scale_ref, h_ref):
    i = pl.program_id(0)

    def emit(x_ref):
        _modulated_norm_rows(x_ref, h_ref, g_ref[layer:layer + 1, :], _mod_row(shift_ref, i, NORM_TM),
                             _mod_row(scale_ref, i, NORM_TM))

    pl.when(i < M_CTX // NORM_TM)(lambda: emit(xc_ref))
    pl.when(i >= M_CTX // NORM_TM)(lambda: emit(xl_ref))


def _norm_mod(x_ctx, x_lat, norm_g, mod, layer):
    xc_spec, xl_spec = _two_stream_specs(NORM_TM, D_MODEL)
    return pl.pallas_call(
        functools.partial(_norm_kernel, layer),
        grid=(M_ALL // NORM_TM,),
        in_specs=[
            xc_spec, xl_spec,
            pl.BlockSpec((DEPTH, D_MODEL), lambda i: (0, 0)),
            _mod_spec(layer, MOD_SHIFT),
            _mod_spec(layer, MOD_SCALE),
        ],
        out_specs=pl.BlockSpec((NORM_TM, D_MODEL), lambda i: (i, 0)),
        out_shape=jax.ShapeDtypeStruct((M_ALL, D_MODEL), BF16),
        compiler_params=_params("arbitrary"),
        name="norm_mod",
    )(x_ctx, x_lat, norm_g, mod, mod)


def _inproj_kernel(h_ref, w_ref, o_ref, w_scr):
    i = pl.program_id(1)

    @pl.when(i == 0)
    def _():
        w_scr[...] = w_ref[0].astype(BF16)

    rows = pl.ds(pl.multiple_of(i * INPROJ_TM, INPROJ_TM), INPROJ_TM)
    o_ref[...] = jnp.dot(h_ref[rows, :], w_scr[...], preferred_element_type=F32)


def _in_proj(h, w_in, layer):
    return pl.pallas_call(
        _inproj_kernel,
        grid=(D_IN // INPROJ_TN, M_ALL // INPROJ_TM),
        in_specs=[
            pl.BlockSpec((M_ALL, D_MODEL), lambda j, i: (0, 0), pipeline_mode=pl.Buffered(1)),
            pl.BlockSpec((1, D_MODEL, INPROJ_TN), lambda j, i: (layer, 0, j)),
        ],
        out_specs=pl.BlockSpec((INPROJ_TM, INPROJ_TN), lambda j, i: (i, j)),
        out_shape=jax.ShapeDtypeStruct((M_ALL, D_IN), F32),
        scratch_shapes=[pltpu.VMEM((D_MODEL, INPROJ_TN), BF16)],
        compiler_params=_params("arbitrary", "arbitrary"),
        name="in_proj",
    )(h, w_in)


ATT_SCALE = 1.0 / math.sqrt(HEAD_DIM)


def _head(ref_or_val, h):
    return ref_or_val[:, h * HEAD_DIM:(h + 1) * HEAD_DIM]


Q_PRESCALE = ATT_SCALE * math.log2(math.e)


def _softmax_pv(q2, k_bf, v_ext):
    s = lax.dot_general(q2, k_bf, (((1,), (1,)), ((), ())), preferred_element_type=F32)
    p = jnp.exp2(s - jnp.max(s, axis=-1, keepdims=True)).astype(BF16)
    o = jnp.dot(p, v_ext, preferred_element_type=F32)
    return o[:, 0:HEAD_DIM] / o[:, HEAD_DIM:2 * HEAD_DIM]


def _rope(x, cos2, sin_a, sin_b):
    nxt = pltpu.roll(x, HEAD_DIM - 1, axis=1)
    prv = pltpu.roll(x, 1, axis=1)
    return x * cos2 + nxt * sin_a + prv * sin_b


CTX_TQ = SEQ
LAT_TQ = 128


def _attn_kernel(rope, n_cache, layer, emit_kv, n_aliased, nsb, *refs):
    it = iter(refs)
    q_all, kv_all, g_all, qg_ref, kg_ref = (next(it) for _ in range(5))
    ck_ref, cv_ref = (next(it), next(it)) if n_cache else (None, None)
    q_tabs = tuple(next(it) for _ in range(3)) if rope else None
    k_tabs = tuple(next(it) for _ in range(3)) if rope else None
    for _ in range(n_aliased):
        next(it)
    att_all = next(it)
    ko_hbm, vo_hbm = (next(it), next(it)) if emit_kv else (None, None)
    k_all, v_all = next(it), next(it)
    kf_all, kv_sems = (next(it), next(it)) if emit_kv else (None, None)
    single_tile = q_all.shape[0] == kv_all.shape[0]
    n_new = kv_all.shape[0] // nsb if single_tile else kv_all.shape[0]
    assert not emit_kv or single_tile

    def part(ref, sb, rows_per_seq):
        return ref if nsb == 1 else ref.at[pl.ds(sb * rows_per_seq, rows_per_seq)]

    def kv_copy(sb, h, is_value):
        b = pl.program_id(0) * nsb + sb
        if is_value:
            src = part(kv_all, sb, n_new).at[:, pl.ds((N_KV_HEADS + h) * HEAD_DIM, HEAD_DIM)]
            return pltpu.make_async_copy(src, vo_hbm.at[b, layer, :, h, :],
                                         kv_sems.at[sb, N_KV_HEADS + h])
        src = part(kf_all, sb, n_new).at[:, pl.ds(h * HEAD_DIM, HEAD_DIM)]
        return pltpu.make_async_copy(src, ko_hbm.at[b, layer, :, h, :], kv_sems.at[sb, h])

    def prepare_keys_values(sb):
        kv_ref, k_scr, v_scr = part(kv_all, sb, n_new), part(k_all, sb, n_new + n_cache), part(v_all, sb, n_new + n_cache)
        kf_scr = part(kf_all, sb, n_new) if emit_kv else None
        for h in range(N_KV_HEADS):
            kn = _rms(_head(kv_ref, h), kg_ref[layer:layer + 1, :])
            vh = _head(kv_ref, N_KV_HEADS + h)
            if emit_kv:
                kf_scr[:, h * HEAD_DIM:(h + 1) * HEAD_DIM] = kn
            if rope:
                kn = _rope(kn, *(t[...] for t in k_tabs))
            k_scr[0:n_new, h * HEAD_DIM:(h + 1) * HEAD_DIM] = kn.astype(BF16)
            v0 = 2 * h * HEAD_DIM
            v_scr[0:n_new, v0:v0 + HEAD_DIM] = vh.astype(BF16)
            if n_cache:
                v_scr[n_new:n_new + n_cache, v0:v0 + HEAD_DIM] = _head(cv_ref[0, 0], h).astype(BF16)
            v_scr[:, v0 + HEAD_DIM:v0 + 2 * HEAD_DIM] = jnp.ones((n_new + n_cache, HEAD_DIM), BF16)
        if n_cache:
            k_scr[n_new:n_new + n_cache, :] = ck_ref[0, 0].astype(BF16)

    def attend(sb):
        rows = q_all.shape[0] // nsb
        q_ref, g_ref, att_ref = part(q_all, sb, rows), part(g_all, sb, rows), part(att_all, sb, rows)
        kv_sb = sb if single_tile else 0
        k_scr, v_scr = part(k_all, kv_sb, n_new + n_cache), part(v_all, kv_sb, n_new + n_cache)

        def query(hq):
            x = _rms(_head(q_ref, hq), qg_ref[layer:layer + 1, :])
            if rope:
                x = _rope(x, *(part(t, sb, rows)[...] for t in q_tabs))
            return x * Q_PRESCALE

        for h in range(N_KV_HEADS):
            q2 = jnp.concatenate([query(GROUP * h + g) for g in range(GROUP)], axis=0).astype(BF16)
            o = _softmax_pv(q2, _head(k_scr, h), v_scr[:, 2 * h * HEAD_DIM:2 * (h + 1) * HEAD_DIM])
            for g in range(GROUP):
                hq = GROUP * h + g
                gate = _silu(_head(g_ref, hq))
                att_ref[:, hq * HEAD_DIM:(hq + 1) * HEAD_DIM] = (o[g * rows:(g + 1) * rows] * gate).astype(BF16)

    every = [(sb, h) for sb in range(nsb) for h in range(N_KV_HEADS)]
    if single_tile:
        if emit_kv:
            for sb, h in every:
                kv_copy(sb, h, True).start()
        for sb in range(nsb):
            prepare_keys_values(sb)
        if emit_kv:
            for sb, h in every:
                kv_copy(sb, h, False).start()
    else:
        pl.when(pl.program_id(1) == 0)(lambda: prepare_keys_values(0))

    for sb in range(nsb):
        attend(sb)

    if emit_kv:
        for sb, h in every:
            kv_copy(sb, h, False).wait()
            kv_copy(sb, h, True).wait()


def _attention(proj, q_g, k_g, layer, seq, tq, n_batch, row0, *, nsb=1, cache=None, rope_tabs=None,
               att_prev=None, kv_prev=None, emit_kv=False):
    whole = tq == seq
    assert not whole or cache is None
    tq = nsb * tq
    kv_rows = nsb * seq if whole else seq
    nq = kv_rows // tq
    q0, kv0 = row0 // tq, row0 // kv_rows
    n_cache = 0 if cache is None else cache[0].shape[2]
    vec = pl.BlockSpec((DEPTH, HEAD_DIM), lambda b, i: (0, 0))
    in_specs = [
        pl.BlockSpec((tq, D_ATT), lambda b, i: (q0 + b * nq + i, COL_Q // D_ATT)),
        pl.BlockSpec((kv_rows, 2 * D_KV), lambda b, i: (kv0 + b, COL_K // (2 * D_KV))),
        pl.BlockSpec((tq, D_ATT), lambda b, i: (q0 + b * nq + i, COL_GA // D_ATT)),
        vec, vec,
    ]
    args = [proj, proj, proj, q_g, k_g]
    if cache is not None:
        in_specs += [pl.BlockSpec((1, 1, n_cache, D_KV), lambda b, i: (b, layer, 0, 0))] * 2
        args += list(cache)
    if rope_tabs is not None:
        in_specs += [pl.BlockSpec((tq, HEAD_DIM), lambda b, i: (i, 0))] * 3
        in_specs += [pl.BlockSpec((seq, HEAD_DIM), lambda b, i: (0, 0))] * 3
        args += list(rope_tabs) * 2
    aliased = ([] if att_prev is None else [att_prev]) + ([] if kv_prev is None else list(kv_prev))
    out_first = 0 if att_prev is not None else 1
    aliases = {len(args) + n: out_first + n for n in range(len(aliased))}
    in_specs += [pl.BlockSpec(memory_space=pl.ANY)] * len(aliased)
    args += aliased
    out_specs = [pl.BlockSpec((tq, D_ATT), lambda b, i: (q0 + b * nq + i, 0))]
    out_shape = [jax.ShapeDtypeStruct((M_ALL, D_ATT), BF16)]
    keys = kv_rows + n_cache
    scratch =[pltpu.VMEM((keys, D_KV), BF16), pltpu.VMEM((keys, 2 * D_KV), BF16)]
    if emit_kv:
        out_specs += [pl.BlockSpec(memory_space=pl.ANY)] * 2
        out_shape += [jax.ShapeDtypeStruct((n_batch, DEPTH, seq, N_KV_HEADS, HEAD_DIM), F32)] * 2
        scratch += [pltpu.VMEM((kv_rows, D_KV), F32), pltpu.SemaphoreType.DMA((nsb, 2 * N_KV_HEADS))]
    return pl.pallas_call(
        functools.partial(_attn_kernel, rope_tabs is not None, n_cache, layer, emit_kv, len(aliased), nsb),
        grid=(n_batch * seq // kv_rows, nq),
        in_specs=in_specs,
        out_specs=out_specs,
        out_shape=out_shape,
        input_output_aliases=aliases,
        scratch_shapes=scratch,
        compiler_params=_params("arbitrary", "arbitrary"),
        name=f"attention_{seq}",
    )(*args)


def _dft_matrix(L):
    k = np.arange(L, dtype=np.int64)[:, None]
    t = np.arange(L, dtype=np.int64)[None, :]
    ang = 2.0 * np.pi * ((k * t) % (2 * L)).astype(np.float64) / (2 * L)
    top = np.cos(ang)
    bot = np.sin(ang)
    bot[0, :] = np.where(np.arange(L) % 2 == 0, 1.0, -1.0)
    return np.concatenate([top, bot], axis=0).astype(np.float32)


def _filter_kernel(L, z_ref, w1_ref, b1_ref, w2_ref, b2_ref, fr_ref, w3f_ref, w3b_ref, dec_ref,
                   f_ref, o_ref, hdn_scr, w1_scr, w2_scr, vec_scr, w3_scr):
    hp = lax.Precision.HIGHEST
    layer = pl.program_id(0)

    @pl.when(pl.program_id(1) == 0)
    def _():
        w1_scr[...] = jnp.zeros(w1_scr.shape, F32)
        w1_scr[0:POS_EMB, 0:FILT_HID] = w1_ref[0]
        w2_scr[...] = jnp.zeros(w2_scr.shape, F32)
        w2_scr[0:FILT_HID, 0:FILT_HID] = w2_ref[0]
        vec_scr[...] = jnp.zeros(vec_scr.shape, F32)
        for r, ref in enumerate((b1_ref, b2_ref, fr_ref)):
            vec_scr[r:r + 1, 0:FILT_HID] = ref[pl.ds(layer, 1), :]
        w3_scr[...] = jnp.zeros(w3_scr.shape, BF16)
        b1, b2, fr = vec_scr[0:1, :], vec_scr[1:2, :], vec_scr[2:3, :]
        h1 = jnp.sin(fr * (jnp.dot(z_ref[...], w1_scr[...], precision=hp, preferred_element_type=F32) + b1))
        hdn_scr[...] = jnp.sin(fr * (jnp.dot(h1, w2_scr[...], precision=hp, preferred_element_type=F32) + b2))

    hdn = hdn_scr[...]
    dec = dec_ref[...]
    hdn_bf = hdn.astype(BF16)
    w3_scr[0, 0:FILT_HID, :] = w3f_ref[0].astype(BF16)
    w3_scr[1, 0:FILT_HID, :] = w3b_ref[0].astype(BF16)
    h_f = jnp.dot(hdn_bf, w3_scr[0], preferred_element_type=F32) * dec
    h_b = jnp.dot(hdn_bf, w3_scr[1], preferred_element_type=F32) * dec
    hs = h_f + h_b
    ha = jnp.dot(f_ref[0:L, :], hs.astype(BF16), preferred_element_type=F32)
    hb = jnp.dot(f_ref[L:2 * L, :], (h_f - h_b).astype(BF16), preferred_element_type=F32)
    row = lax.broadcasted_iota(jnp.int32, (L, 1), 0)
    first = row == 0
    nyquist = jnp.sum(jnp.where(row % 2 == 0, hs, -hs), axis=0, keepdims=True)
    wk = jnp.where(first, 1.0 / (2 * L), 2.0 / (2 * L))
    o_ref[0, 0] = ha * wk
    o_ref[0, 1] = jnp.where(first, 0.0, hb * wk)
    o_ref[0, 2] = jnp.where(first, nyquist, ha) * wk


FILT_CB = 512


def _filter_spectra(L, w1, b1, w2, b2, freq, w3, f_bf):
    tpos = np.arange(L, dtype=np.float64)
    t_norm = tpos / max(L - 1, 1)
    w = 2.0 * math.pi * tpos / L
    bands = np.linspace(1e-4, POS_BANDS - 1, POS_BANDS)
    z = np.concatenate([t_norm[:, None], np.cos(w[:, None] * bands), -np.sin(w[:, None] * bands)], axis=-1)
    z = np.pad(z, ((0, 0), (0, FEAT_PAD - POS_EMB))).astype(np.float32)
    max_decay = math.log(DECAY_TARGET) / FAST_DECAY_PCT
    min_decay = math.log(DECAY_TARGET) / SLOW_DECAY_PCT
    deltas = np.abs(np.linspace(min_decay, max_decay, D_HY))
    dec = (np.exp(-t_norm[:, None] * deltas) + DECAY_SHIFT).astype(np.float32)

    ncb = D_HY // FILT_CB
    matrix = lambda rows, cols: pl.BlockSpec((1, rows, cols), lambda l, c: (l, 0, 0))
    per_layer_vec = pl.BlockSpec((DEPTH, FILT_HID), lambda l, c: (0, 0))
    return pl.pallas_call(
        functools.partial(_filter_kernel, L),
        grid=(DEPTH, ncb),
        in_specs=[
            pl.BlockSpec((L, FEAT_PAD), lambda l, c: (0, 0)),
            matrix(POS_EMB, FILT_HID), per_layer_vec,
            matrix(FILT_HID, FILT_HID), per_layer_vec, per_layer_vec,
            pl.BlockSpec((1, FILT_HID, FILT_CB), lambda l, c: (l, 0, c)),
            pl.BlockSpec((1, FILT_HID, FILT_CB), lambda l, c: (l, 0, ncb + c)),
            pl.BlockSpec((L, FILT_CB), lambda l, c: (0, c)),
            pl.BlockSpec((2 * L, L), lambda l, c: (0, 0)),
        ],
        out_specs=pl.BlockSpec((1, 3, L, FILT_CB), lambda l, c: (l, 0, 0, c)),
        out_shape=jax.ShapeDtypeStruct((DEPTH, 3, L, D_HY), F32),
        scratch_shapes=[pltpu.VMEM((L, FEAT_PAD), F32), pltpu.VMEM((FEAT_PAD, FEAT_PAD), F32),
                        pltpu.VMEM((FEAT_PAD, FEAT_PAD), F32), pltpu.VMEM((8, FEAT_PAD), F32),
                        pltpu.VMEM((2, FEAT_PAD, FILT_CB), BF16)],
        compiler_params=_params("arbitrary", "arbitrary"),
        name=f"hyena_filter_{L}",
    )(jnp.asarray(z), w1, b1, w2, b2, freq, w3, w3, jnp.asarray(dec), f_bf)


def _hyena_kernel(L, layer, x0_ref, x1_ref, vv_ref, g_ref, cw0_ref, cw1_ref, cwv_ref, cb0_ref, cb1_ref,
                  cbv_ref, bias_ref, coef_ref, f_ref, ft_ref, *rest):
    o_ref = rest[-1]
    this_layer = slice(layer, layer + 1)
    row = lax.broadcasted_iota(jnp.int32, (L, 1), 0)
    is_first = row == 0
    is_last = row == L - 1

    def sconv(x_ref, w_ref, b_ref):
        x = x_ref[...]
        prev = jnp.where(is_first, 0.0, pltpu.roll(x, 1, axis=0))
        nxt = jnp.where(is_last, 0.0, pltpu.roll(x, L - 1, axis=0))
        return w_ref[0, 0:1, :] * prev + w_ref[0, 1:2, :] * x + w_ref[0, 2:3, :] * nxt + b_ref[this_layer, :]

    z = sconv(vv_ref, cwv_ref, cbv_ref) * sconv(x1_ref, cw1_ref, cb1_ref)
    ab = jnp.dot(f_ref[...], z.astype(BF16), preferred_element_type=F32)
    a, b = ab[0:L], ab[L:2 * L]
    g1, g2, g3 = coef_ref[0, 0], coef_ref[0, 1], coef_ref[0, 2]
    pq = jnp.concatenate([a * g1 - b * g2, a * g2 + b * g3], axis=0).astype(BF16)
    y = jnp.dot(ft_ref[...], pq, preferred_element_type=F32) + bias_ref[this_layer, :] * z
    o_ref[...] = (sconv(x0_ref, cw0_ref, cb0_ref) * y * _silu(g_ref[...])).astype(BF16)


def _hyena(proj, L, nb, row0, cb, layer, conv_w, conv_b, hy_bias, coef, f_bf, ft_bf, hy_prev=None):
    ncb = D_HY // cb
    rows = L
    blk0 = row0 // rows
    extra_specs = [] if hy_prev is None else [pl.BlockSpec(memory_space=pl.ANY)]
    extra_args = [] if hy_prev is None else [hy_prev]
    aliases = {} if hy_prev is None else {14: 0}
    col = lambda off: (lambda c, b: (blk0 + b, off // cb + c))
    cw = lambda part: pl.BlockSpec((1, 3, cb), lambda c, b: (layer, 0, part * ncb + c))
    cbias = lambda part: pl.BlockSpec((DEPTH, cb), lambda c, b: (0, part * ncb + c))
    once = pl.Buffered(1)
    return pl.pallas_call(
        functools.partial(_hyena_kernel, L, layer),
        grid=(ncb, nb),
        in_specs=[
            pl.BlockSpec((rows, cb), col(COL_X0)),
            pl.BlockSpec((rows, cb), col(COL_X1)),
            pl.BlockSpec((rows, cb), col(COL_VV)),
            pl.BlockSpec((rows, cb), col(COL_GH)),
            cw(0), cw(1), cw(2), cbias(0), cbias(1), cbias(2),
            pl.BlockSpec((DEPTH, cb), lambda c, b: (0, c)),
            pl.BlockSpec((1, 3, L, cb), lambda c, b: (layer, 0, 0, c)),
            pl.BlockSpec((2 * L, L), lambda c, b: (0, 0), pipeline_mode=once),
            pl.BlockSpec((L, 2 * L), lambda c, b: (0, 0), pipeline_mode=once),
            *extra_specs,
        ],
        out_specs=pl.BlockSpec((rows, cb), lambda c, b: (blk0 + b, c)),
        out_shape=jax.ShapeDtypeStruct((M_ALL, D_HY), BF16),
        input_output_aliases=aliases,
        compiler_params=_params("arbitrary", "arbitrary"),
        name=f"hyena_{L}",
    )(proj, proj, proj, proj, conv_w, conv_w, conv_w, conv_b, conv_b, conv_b, hy_bias, coef, f_bf, ft_bf,
      *extra_args)


OUTPROJ_TM = 256
OUTPROJ_TM_FINAL = 512


def _outproj_kernel(split_in, final, tm, tile0, gain_row, *refs):
    it = iter(refs)
    att_ref, hy_ref, w_ref = next(it), next(it), next(it)
    x_refs = (next(it), next(it)) if split_in else (next(it),)
    gate_ref, g_ref = next(it), next(it)
    shift_ref, scale_ref = (None, None) if final else (next(it), next(it))
    out_refs = (next(it),) if final else (next(it), next(it))
    w_scr = next(it)
    i = pl.program_id(0)

    @pl.when(i == 0)
    def _():
        w_scr[...] = w_ref[0].astype(BF16)

    out = (jnp.dot(att_ref[...], w_scr[0:D_ATT, :], preferred_element_type=F32)
           + jnp.dot(hy_ref[...], w_scr[D_ATT:D_ATT + D_HY, :], preferred_element_type=F32))
    tile = tile0 + i
    if split_in:
        x = jnp.where(tile < M_CTX // tm, x_refs[0][...], x_refs[1][...])
    else:
        x = x_refs[0][...]
    y = x + _mod_row(gate_ref, tile, tm) * out
    inv = lax.rsqrt(jnp.mean(y * y, axis=-1, keepdims=True) + EPS)
    g = g_ref[gain_row:gain_row + 1, :]
    if final:
        out_refs[0][...] = y * inv * g
    else:
        out_refs[0][...] = y
        gain = g * (1.0 + _mod_row(scale_ref, tile, tm))
        out_refs[1][...] = (y * inv * gain + _mod_row(shift_ref, tile, tm)).astype(BF16)


def _out_proj(att, hy, w_out, layer, x_parts, mod, gains, final, tm, tile0=0, n_tiles=None):
    n_tiles = M_ALL // tm if n_tiles is None else n_tiles
    split_in = len(x_parts) == 2
    assert not split_in or (tile0 == 0 and n_tiles == M_ALL // tm)
    row_tile = lambda width: pl.BlockSpec((tm, width), lambda i: (tile0 + i, 0))
    out_tile = pl.BlockSpec((tm, D_MODEL), lambda i: (i, 0))
    x_specs = list(_two_stream_specs(tm, D_MODEL)) if split_in else [row_tile(D_MODEL)]
    in_specs = [
        row_tile(D_ATT),
        row_tile(D_HY),
        pl.BlockSpec((1, D_ATT + D_HY, D_MODEL), lambda i: (layer, 0, 0), pipeline_mode=pl.Buffered(1)),
        *x_specs,
        _mod_spec(layer, MOD_GATE),
        pl.BlockSpec(gains.shape, lambda i: (0, 0)),
    ]
    args = [att, hy, w_out, *x_parts, mod, gains]
    if final:
        out_specs = [out_tile]
        out_shape = [jax.ShapeDtypeStruct((n_tiles * tm, D_MODEL), F32)]
    else:
        in_specs += [_mod_spec(layer + 1, MOD_SHIFT), _mod_spec(layer + 1, MOD_SCALE)]
        args += [mod, mod]
        out_specs = [out_tile, out_tile]
        out_shape = [jax.ShapeDtypeStruct((n_tiles * tm, D_MODEL), F32),
                     jax.ShapeDtypeStruct((n_tiles * tm, D_MODEL), BF16)]
    return pl.pallas_call(
        functools.partial(_outproj_kernel, split_in, final, tm, tile0, 0 if final else layer + 1),
        grid=(n_tiles,),
        in_specs=in_specs,
        out_specs=out_specs,
        out_shape=out_shape,
        scratch_shapes=[pltpu.VMEM((D_ATT + D_HY, D_MODEL), BF16)],
        compiler_params=_params("arbitrary"),
        name="out_proj_final" if final else "out_proj",
    )(*args)


def _rope_tables():
    t = np.arange(DEC_SEQ)
    row = (t // GRID_W).astype(np.float64)
    col = (t % GRID_W).astype(np.float64)
    pairs = HEAD_DIM // 4
    inv_freq = ROPE_THETA ** (-np.arange(pairs, dtype=np.float64) / pairs)
    ang = np.concatenate([row[:, None] * inv_freq, col[:, None] * inv_freq], axis=-1)
    cos = np.repeat(np.cos(ang), 2, axis=-1).astype(np.float32)
    sin = np.repeat(np.sin(ang), 2, axis=-1).astype(np.float32)
    even = (np.arange(HEAD_DIM) % 2 == 0)[None, :]
    sin_a = np.where(even, -sin, 0.0).astype(np.float32)
    sin_b = np.where(even, 0.0, sin).astype(np.float32)
    return jnp.asarray(cos), jnp.asarray(sin_a), jnp.asarray(sin_b)


def kernel(x_prompt, x_sample, cache_k, cache_v, c, c_ctx, norm_g, w_ada, b_ada, w_in, q_norm_g, k_norm_g,
           conv_w, conv_b, filt_w1, filt_b1, filt_w2, filt_b2, filt_w3, filt_freq, hy_bias, w_out, final_norm_g):
    ctx = x_prompt.reshape(BATCH * SEQ, D_MODEL)
    lat = x_sample.reshape(DEC_BATCH * DEC_SEQ, D_MODEL)
    cache_k4 = cache_k.reshape(DEC_BATCH, DEPTH, PAST_LEN, D_KV)
    cache_v4 = cache_v.reshape(DEC_BATCH, DEPTH, PAST_LEN, D_KV)

    mod = _modulation(c_ctx, c, w_ada, b_ada)

    rope_tabs = _rope_tables()
    dft = {}
    coefs = {}
    for L in (SEQ, DEC_SEQ):
        f_np = _dft_matrix(L)
        f_bf = jnp.asarray(f_np).astype(BF16)
        dft[L] = (f_bf, jnp.asarray(np.ascontiguousarray(f_np.T)).astype(BF16))
        coefs[L] = _filter_spectra(L, filt_w1, filt_b1, filt_w2, filt_b2, filt_freq, filt_w3, f_bf)

    final_g = final_norm_g.reshape(1, D_MODEL)

    x_parts = (ctx, lat)
    h = _norm_mod(ctx, lat, norm_g, mod, 0)
    kv_out = None
    for l in range(DEPTH):
        final = l == DEPTH - 1

        proj = _in_proj(h, w_in, l)

        att, new_k, new_v = _attention(proj, q_norm_g, k_norm_g, l, SEQ, CTX_TQ, BATCH, 0, nsb=4, kv_prev=kv_out,
                                       emit_kv=True)
        kv_out = (new_k, new_v)
        (att,) = _attention(proj, q_norm_g, k_norm_g, l, DEC_SEQ, LAT_TQ, DEC_BATCH, M_CTX, nsb=4,
                            cache=(cache_k4, cache_v4), rope_tabs=rope_tabs, att_prev=att)

        hy = _hyena(proj, SEQ, BATCH, 0, D_HY, l, conv_w, conv_b, hy_bias, coefs[SEQ], *dft[SEQ])
        hy = _hyena(proj, DEC_SEQ, DEC_BATCH, M_CTX, 512, l, conv_w, conv_b, hy_bias,
                    coefs[DEC_SEQ], *dft[DEC_SEQ], hy_prev=hy)

        if final:
            assert len(x_parts) == 1
            tm = OUTPROJ_TM_FINAL
            nc = M_CTX // tm
            (y_ctx,) = _out_proj(att, hy, w_out, l, x_parts, mod, final_g, True, tm, 0, nc)
            (y_lat,) = _out_proj(att, hy, w_out, l, x_parts, mod, final_g, True, tm, nc, M_LAT // tm)
        else:
            y, h = _out_proj(att, hy, w_out, l, x_parts, mod, norm_g, False, OUTPROJ_TM)
            x_parts = (y,)

    y_prompt = y_ctx.reshape(BATCH, SEQ, D_MODEL)
    y_sample = y_lat.reshape(DEC_BATCH, DEC_SEQ, D_MODEL)
    return (y_prompt, y_sample, kv_out[0], kv_out[1])
```

```python
import functools
import math

import numpy as np
import jax
import jax.numpy as jnp
from jax import lax
from jax.experimental import pallas as pl
from jax.experimental.pallas import tpu as pltpu

D_MODEL = 2048
BATCH = 16
SEQ = 256
DEPTH = 2
DEC_BATCH = 2
DEC_SEQ = 1024
PAST_LEN = 512
GRID_W = 64
D_ATT = 1024
D_HY = 1024
HEAD_DIM = 128
N_HEADS = 8
N_KV_HEADS = 4
GROUP = 2
D_KV = 512
ROPE_THETA = 10000.0
POS_BANDS = 16
POS_EMB = 33
FILT_HID = 64
DECAY_TARGET = 1e-2
FAST_DECAY_PCT = 0.3
SLOW_DECAY_PCT = 1.5
DECAY_SHIFT = 0.05
EPS = 1e-6
D_IN = 7168

COL_Q, COL_K, COL_V, COL_GA, COL_X0, COL_X1, COL_VV, COL_GH = 0, 1024, 1536, 2048, 3072, 4096, 5120, 6144

F32 = jnp.float32
BF16 = jnp.bfloat16

VMEM_LIMIT_BYTES = 56 * 1024 * 1024
FEAT_PAD = 128
def _params(*sem):
    return pltpu.CompilerParams(dimension_semantics=sem, vmem_limit_bytes=VMEM_LIMIT_BYTES)


def _silu(x):
    half = 0.5 * x
    return half + half * jnp.tanh(half)


def _rms(x, g):
    return x * lax.rsqrt(jnp.mean(x * x, axis=-1, keepdims=True) + EPS) * g


MOD_ROWS = 8
MOD_TN = 1024


def _mod_kernel(cctx_ref, c_ref, w_ref, b_ref, o_ref, c_scr):
    c_scr[...] = jnp.zeros(c_scr.shape, F32)
    c_scr[0:1, :] = cctx_ref[...]
    c_scr[1:1 + DEC_BATCH, :] = c_ref[...]
    s = _silu(c_scr[...]).astype(BF16)
    layer = pl.program_id(0)
    o_ref[0] = jnp.dot(s, w_ref[0].astype(BF16), preferred_element_type=F32) + b_ref[pl.ds(layer, 1), :]


def _modulation(c_ctx, c, w_ada, b_ada):
    n = 3 * D_MODEL
    return pl.pallas_call(
        _mod_kernel,
        grid=(DEPTH, n // MOD_TN),
        in_specs=[
            pl.BlockSpec((1, D_MODEL), lambda l, j: (0, 0)),
            pl.BlockSpec((DEC_BATCH, D_MODEL), lambda l, j: (0, 0)),
            pl.BlockSpec((1, D_MODEL, MOD_TN), lambda l, j: (l, 0, j)),
            pl.BlockSpec((DEPTH, MOD_TN), lambda l, j: (0, j)),
        ],
        out_specs=pl.BlockSpec((1, MOD_ROWS, MOD_TN), lambda l, j: (l, 0, j)),
        out_shape=jax.ShapeDtypeStruct((DEPTH, MOD_ROWS, n), F32),
        scratch_shapes=[pltpu.VMEM((MOD_ROWS, D_MODEL), F32)],
        compiler_params=_params("arbitrary", "arbitrary"),
        name="adaln_mod",
    )(c_ctx.reshape(1, D_MODEL), c, w_ada, b_ada)


M_CTX = BATCH * SEQ
M_LAT = DEC_BATCH * DEC_SEQ
M_ALL = M_CTX + M_LAT
NORM_TM = 1024
INPROJ_TM = 3072
INPROJ_TN = 512


ROW_CHUNK = 32


def _modulated_norm_rows(x_ref, h_ref, g, shift, scale):
    gain = g * (1.0 + scale)

    def body(r, _):
        rows = pl.ds(pl.multiple_of(r * ROW_CHUNK, ROW_CHUNK), ROW_CHUNK)
        x = x_ref[rows, :]
        inv = lax.rsqrt(jnp.mean(x * x, axis=-1, keepdims=True) + EPS)
        h_ref[rows, :] = (x * inv * gain + shift).astype(BF16)
        return 0

    lax.fori_loop(0, x_ref.shape[0] // ROW_CHUNK, body, 0, unroll=4)


MOD_SHIFT, MOD_SCALE, MOD_GATE = 0, 1, 2


def _mod_spec(layer, part):
    return pl.BlockSpec((1, MOD_ROWS, D_MODEL), lambda i: (layer, 0, part))


def _mod_row(ref, tile, tm):
    nc = M_CTX // tm
    per_batch = DEC_SEQ // tm
    row = jnp.where(tile < nc, 0, 1 + (tile - nc) // per_batch)
    return ref[0, pl.ds(row, 1), :]


def _two_stream_specs(tm, width):
    nc = M_CTX // tm
    return (pl.BlockSpec((tm, width), lambda i: (jnp.minimum(i, nc - 1), 0)),
            pl.BlockSpec((tm, width), lambda i: (jnp.maximum(i - nc, 0), 0)))


def _norm_kernel(layer, xc_ref, xl_ref, g_ref, shift_ref, scale_ref, h_ref):
    i = pl.program_id(0)

    def emit(x_ref):
        _modulated_norm_rows(x_ref, h_ref, g_ref[layer:layer + 1, :], _mod_row(shift_ref, i, NORM_TM),
                             _mod_row(scale_ref, i, NORM_TM))

    pl.when(i < M_CTX // NORM_TM)(lambda: emit(xc_ref))
    pl.when(i >= M_CTX // NORM_TM)(lambda: emit(xl_ref))


def _norm_mod(x_ctx, x_lat, norm_g, mod, layer):
    xc_spec, xl_spec = _two_stream_specs(NORM_TM, D_MODEL)
    return pl.pallas_call(
        functools.partial(_norm_kernel, layer),
        grid=(M_ALL // NORM_TM,),
        in_specs=[
            xc_spec, xl_spec,
            pl.BlockSpec((DEPTH, D_MODEL), lambda i: (0, 0)),
            _mod_spec(layer, MOD_SHIFT),
            _mod_spec(layer, MOD_SCALE),
        ],
        out_specs=pl.BlockSpec((NORM_TM, D_MODEL), lambda i: (i, 0)),
        out_shape=jax.ShapeDtypeStruct((M_ALL, D_MODEL), BF16),
        compiler_params=_params("arbitrary"),
        name="norm_mod",
    )(x_ctx, x_lat, norm_g, mod, mod)


def _inproj_kernel(h_hbm, w_ref, o_ref, w_scr, h_scr, h_sems):
    j, i = pl.program_id(0), pl.program_id(1)
    n_chunks = M_ALL // INPROJ_TM

    def h_copy(c):
        rows = pl.ds(c * INPROJ_TM, INPROJ_TM)
        return pltpu.make_async_copy(h_hbm.at[rows], h_scr.at[rows], h_sems.at[c])

    @pl.when((j == 0) & (i == 0))
    def _():
        for c in range(n_chunks):
            h_copy(c).start()

    @pl.when(i == 0)
    def _():
        w_scr[...] = w_ref[0].astype(BF16)

    for c in range(n_chunks):
        @pl.when((j == 0) & (i == c))
        def _():
            h_copy(c).wait()

    rows = pl.ds(pl.multiple_of(i * INPROJ_TM, INPROJ_TM), INPROJ_TM)
    o_ref[...] = jnp.dot(h_scr[rows, :], w_scr[...], preferred_element_type=F32)


def _in_proj(h, w_in, layer):
    return pl.pallas_call(
        _inproj_kernel,
        grid=(D_IN // INPROJ_TN, M_ALL // INPROJ_TM),
        in_specs=[
            pl.BlockSpec(memory_space=pl.ANY),
            pl.BlockSpec((1, D_MODEL, INPROJ_TN), lambda j, i: (layer, 0, j)),
        ],
        out_specs=pl.BlockSpec((INPROJ_TM, INPROJ_TN), lambda j, i: (i, j)),
        out_shape=jax.ShapeDtypeStruct((M_ALL, D_IN), F32),
        scratch_shapes=[pltpu.VMEM((D_MODEL, INPROJ_TN), BF16), pltpu.VMEM((M_ALL, D_MODEL), BF16),
                        pltpu.SemaphoreType.DMA((M_ALL // INPROJ_TM,))],
        compiler_params=_params("arbitrary", "arbitrary"),
        name="in_proj",
    )(h, w_in)


ATT_SCALE = 1.0 / math.sqrt(HEAD_DIM)


def _head(ref_or_val, h):
    return ref_or_val[:, h * HEAD_DIM:(h + 1) * HEAD_DIM]


Q_PRESCALE = ATT_SCALE * math.log2(math.e)


def _softmax_pv(q2, k_bf, v_ext):
    s = lax.dot_general(q2, k_bf, (((1,), (1,)), ((), ())), preferred_element_type=F32)
    p = jnp.exp2(s - jnp.max(s, axis=-1, keepdims=True)).astype(BF16)
    o = jnp.dot(p, v_ext, preferred_element_type=F32)
    return o[:, 0:HEAD_DIM] / o[:, HEAD_DIM:2 * HEAD_DIM]


def _rope(x, cos2, sin_a, sin_b):
    nxt = pltpu.roll(x, HEAD_DIM - 1, axis=1)
    prv = pltpu.roll(x, 1, axis=1)
    return x * cos2 + nxt * sin_a + prv * sin_b


CTX_TQ = SEQ
LAT_TQ = 128


def _attn_kernel(rope, n_cache, layer, emit_kv, n_aliased, nsb, *refs):
    it = iter(refs)
    q_all, kv_all, g_all, qg_ref, kg_ref = (next(it) for _ in range(5))
    ck_ref, cv_ref = (next(it), next(it)) if n_cache else (None, None)
    q_tabs = tuple(next(it) for _ in range(3)) if rope else None
    k_tabs = tuple(next(it) for _ in range(3)) if rope else None
    for _ in range(n_aliased):
        next(it)
    att_all = next(it)
    ko_hbm, vo_hbm = (next(it), next(it)) if emit_kv else (None, None)
    k_all, v_all = next(it), next(it)
    kf_all, kv_sems = (next(it), next(it)) if emit_kv else (None, None)
    single_tile = q_all.shape[0] == kv_all.shape[0]
    n_new = kv_all.shape[0] // nsb if single_tile else kv_all.shape[0]
    assert not emit_kv or single_tile

    def part(ref, sb, rows_per_seq):
        return ref if nsb == 1 else ref.at[pl.ds(sb * rows_per_seq, rows_per_seq)]

    def kv_copy(sb, h, is_value):
        b = pl.program_id(0) * nsb + sb
        if is_value:
            src = part(kv_all, sb, n_new).at[:, pl.ds((N_KV_HEADS + h) * HEAD_DIM, HEAD_DIM)]
            return pltpu.make_async_copy(src, vo_hbm.at[b, layer, :, h, :],
                                         kv_sems.at[sb, N_KV_HEADS + h])
        src = part(kf_all, sb, n_new).at[:, pl.ds(h * HEAD_DIM, HEAD_DIM)]
        return pltpu.make_async_copy(src, ko_hbm.at[b, layer, :, h, :], kv_sems.at[sb, h])

    def prepare_keys_values(sb):
        kv_ref, k_scr, v_scr = part(kv_all, sb, n_new), part(k_all, sb, n_new + n_cache), part(v_all, sb, n_new + n_cache)
        kf_scr = part(kf_all, sb, n_new) if emit_kv else None
        for h in range(N_KV_HEADS):
            kn = _rms(_head(kv_ref, h), kg_ref[layer:layer + 1, :])
            vh = _head(kv_ref, N_KV_HEADS + h)
            if emit_kv:
                kf_scr[:, h * HEAD_DIM:(h + 1) * HEAD_DIM] = kn
            if rope:
                kn = _rope(kn, *(t[...] for t in k_tabs))
            k_scr[0:n_new, h * HEAD_DIM:(h + 1) * HEAD_DIM] = kn.astype(BF16)
            v0 = 2 * h * HEAD_DIM
            v_scr[0:n_new, v0:v0 + HEAD_DIM] = vh.astype(BF16)
            if n_cache:
                v_scr[n_new:n_new + n_cache, v0:v0 + HEAD_DIM] = _head(cv_ref[0, 0], h).astype(BF16)
            v_scr[:, v0 + HEAD_DIM:v0 + 2 * HEAD_DIM] = jnp.ones((n_new + n_cache, HEAD_DIM), BF16)
        if n_cache:
            k_scr[n_new:n_new + n_cache, :] = ck_ref[0, 0].astype(BF16)

    def attend(sb):
        rows = q_all.shape[0] // nsb
        q_ref, g_ref, att_ref = part(q_all, sb, rows), part(g_all, sb, rows), part(att_all, sb, rows)
        kv_sb = sb if single_tile else 0
        k_scr, v_scr = part(k_all, kv_sb, n_new + n_cache), part(v_all, kv_sb, n_new + n_cache)

        def query(hq):
            x = _rms(_head(q_ref, hq), qg_ref[layer:layer + 1, :])
            if rope:
                x = _rope(x, *(part(t, sb, rows)[...] for t in q_tabs))
            return x * Q_PRESCALE

        for h in range(N_KV_HEADS):
            q2 = jnp.concatenate([query(GROUP * h + g) for g in range(GROUP)], axis=0).astype(BF16)
            o = _softmax_pv(q2, _head(k_scr, h), v_scr[:, 2 * h * HEAD_DIM:2 * (h + 1) * HEAD_DIM])
            for g in range(GROUP):
                hq = GROUP * h + g
                gate = _silu(_head(g_ref, hq))
                att_ref[:, hq * HEAD_DIM:(hq + 1) * HEAD_DIM] = (o[g * rows:(g + 1) * rows] * gate).astype(BF16)

    every = [(sb, h) for sb in range(nsb) for h in range(N_KV_HEADS)]
    if single_tile:
        if emit_kv:
            for sb, h in every:
                kv_copy(sb, h, True).start()
        for sb in range(nsb):
            prepare_keys_values(sb)
        if emit_kv:
            for sb, h in every:
                kv_copy(sb, h, False).start()
    else:
        pl.when(pl.program_id(1) == 0)(lambda: prepare_keys_values(0))

    for sb in range(nsb):
        attend(sb)

    if emit_kv:
        for sb, h in every:
            kv_copy(sb, h, False).wait()
            kv_copy(sb, h, True).wait()


def _attention(proj, q_g, k_g, layer, seq, tq, n_batch, row0, *, nsb=1, cache=None, rope_tabs=None,
               att_prev=None, kv_prev=None, emit_kv=False):
    whole = tq == seq
    assert not whole or cache is None
    tq = nsb * tq
    kv_rows = nsb * seq if whole else seq
    nq = kv_rows // tq
    q0, kv0 = row0 // tq, row0 // kv_rows
    n_cache = 0 if cache is None else cache[0].shape[2]
    vec = pl.BlockSpec((DEPTH, HEAD_DIM), lambda b, i: (0, 0))
    in_specs = [
        pl.BlockSpec((tq, D_ATT), lambda b, i: (q0 + b * nq + i, COL_Q // D_ATT)),
        pl.BlockSpec((kv_rows, 2 * D_KV), lambda b, i: (kv0 + b, COL_K // (2 * D_KV))),
        pl.BlockSpec((tq, D_ATT), lambda b, i: (q0 + b * nq + i, COL_GA // D_ATT)),
        vec, vec,
    ]
    args = [proj, proj, proj, q_g, k_g]
    if cache is not None:
        in_specs += [pl.BlockSpec((1, 1, n_cache, D_KV), lambda b, i: (b, layer, 0, 0))] * 2
        args += list(cache)
    if rope_tabs is not None:
        in_specs += [pl.BlockSpec((tq, HEAD_DIM), lambda b, i: (i, 0))] * 3
        in_specs += [pl.BlockSpec((seq, HEAD_DIM), lambda b, i: (0, 0))] * 3
        args += list(rope_tabs) * 2
    aliased = ([] if att_prev is None else [att_prev]) + ([] if kv_prev is None else list(kv_prev))
    out_first = 0 if att_prev is not None else 1
    aliases = {len(args) + n: out_first + n for n in range(len(aliased))}
    in_specs += [pl.BlockSpec(memory_space=pl.ANY)] * len(aliased)
    args += aliased
    out_specs = [pl.BlockSpec((tq, D_ATT), lambda b, i: (q0 + b * nq + i, 0))]
    out_shape = [jax.ShapeDtypeStruct((M_ALL, D_ATT), BF16)]
    keys = kv_rows + n_cache
    scratch =[pltpu.VMEM((keys, D_KV), BF16), pltpu.VMEM((keys, 2 * D_KV), BF16)]
    if emit_kv:
        out_specs += [pl.BlockSpec(memory_space=pl.ANY)] * 2
        out_shape += [jax.ShapeDtypeStruct((n_batch, DEPTH, seq, N_KV_HEADS, HEAD_DIM), F32)] * 2
        scratch += [pltpu.VMEM((kv_rows, D_KV), F32), pltpu.SemaphoreType.DMA((nsb, 2 * N_KV_HEADS))]
    return pl.pallas_call(
        functools.partial(_attn_kernel, rope_tabs is not None, n_cache, layer, emit_kv, len(aliased), nsb),
        grid=(n_batch * seq // kv_rows, nq),
        in_specs=in_specs,
        out_specs=out_specs,
        out_shape=out_shape,
        input_output_aliases=aliases,
        scratch_shapes=scratch,
        compiler_params=_params("arbitrary", "arbitrary"),
        name=f"attention_{seq}",
    )(*args)


def _dft_matrix(L):
    k = np.arange(L, dtype=np.int64)[:, None]
    t = np.arange(L, dtype=np.int64)[None, :]
    ang = 2.0 * np.pi * ((k * t) % (2 * L)).astype(np.float64) / (2 * L)
    top = np.cos(ang)
    bot = np.sin(ang)
    bot[0, :] = np.where(np.arange(L) % 2 == 0, 1.0, -1.0)
    return np.concatenate([top, bot], axis=0).astype(np.float32)


def _filter_kernel(L, z_ref, w1_ref, b1_ref, w2_ref, b2_ref, fr_ref, w3f_ref, w3b_ref, dec_ref,
                   f_ref, o_ref, hdn_scr, w1_scr, w2_scr, vec_scr, w3_scr):
    hp = lax.Precision.HIGHEST
    layer = pl.program_id(0)

    halves = ((0, 0), (FEAT_PAD, FILT_HID))

    @pl.when(pl.program_id(1) == 0)
    def _():
        w1_scr[...] = jnp.zeros(w1_scr.shape, F32)
        w2_scr[...] = jnp.zeros(w2_scr.shape, F32)
        vec_scr[...] = jnp.zeros(vec_scr.shape, F32)
        w3_scr[...] = jnp.zeros(w3_scr.shape, BF16)
        for feat0, hid0 in halves:
            w1_scr[feat0:feat0 + POS_EMB, hid0:hid0 + FILT_HID] = w1_ref[0]
            w2_scr[hid0:hid0 + FILT_HID, hid0:hid0 + FILT_HID] = w2_ref[0]
            for r, ref in enumerate((b1_ref, b2_ref, fr_ref)):
                vec_scr[r:r + 1, hid0:hid0 + FILT_HID] = ref[pl.ds(layer, 1), :]
        b1, b2, fr = vec_scr[0:1, :], vec_scr[1:2, :], vec_scr[2:3, :]
        h1 = jnp.sin(fr * (jnp.dot(z_ref[...], w1_scr[...], precision=hp, preferred_element_type=F32) + b1))
        hdn_scr[...] = jnp.sin(fr * (jnp.dot(h1, w2_scr[...], precision=hp, preferred_element_type=F32) + b2))

    hdn_bf = hdn_scr[...].astype(BF16)
    dec = dec_ref[...]

    def project(w3_ref, slot):
        parts = []
        for k, (_, hid0) in enumerate(halves):
            w3_scr[slot + k, hid0:hid0 + FILT_HID, :] = w3_ref[0].astype(BF16)
            parts.append(jnp.dot(hdn_bf, w3_scr[slot + k], preferred_element_type=F32))
        return jnp.concatenate(parts, axis=0) * dec

    h_f = project(w3f_ref, 0)
    h_b = project(w3b_ref, 2)
    hs = h_f + h_b
    ha = jnp.dot(f_ref[0:L, :], hs.astype(BF16), preferred_element_type=F32)
    hb = jnp.dot(f_ref[L:2 * L, :], (h_f - h_b).astype(BF16), preferred_element_type=F32)
    row = lax.broadcasted_iota(jnp.int32, (L, 1), 0)
    first = row == 0
    nyquist = jnp.sum(jnp.where(row % 2 == 0, hs, -hs), axis=0, keepdims=True)
    wk = jnp.where(first, 1.0 / (2 * L), 2.0 / (2 * L))
    o_ref[0, 0] = ha * wk
    o_ref[0, 1] = jnp.where(first, 0.0, hb * wk)
    o_ref[0, 2] = jnp.where(first, nyquist, ha) * wk


FILT_CB = 512


def _filter_spectra(L, w1, b1, w2, b2, freq, w3, f_bf):
    tpos = np.arange(L, dtype=np.float64)
    t_norm = tpos / max(L - 1, 1)
    w = 2.0 * math.pi * tpos / L
    bands = np.linspace(1e-4, POS_BANDS - 1, POS_BANDS)
    z = np.concatenate([t_norm[:, None], np.cos(w[:, None] * bands), -np.sin(w[:, None] * bands)], axis=-1)
    z = np.pad(z, ((0, 0), (0, FEAT_PAD - POS_EMB))).astype(np.float32)
    z = np.concatenate([z[:L // 2], z[L // 2:]], axis=1)
    max_decay = math.log(DECAY_TARGET) / FAST_DECAY_PCT
    min_decay = math.log(DECAY_TARGET) / SLOW_DECAY_PCT
    deltas = np.abs(np.linspace(min_decay, max_decay, D_HY))
    dec = (np.exp(-t_norm[:, None] * deltas) + DECAY_SHIFT).astype(np.float32)

    ncb = D_HY // FILT_CB
    matrix = lambda rows, cols: pl.BlockSpec((1, rows, cols), lambda l, c: (l, 0, 0))
    per_layer_vec = pl.BlockSpec((DEPTH, FILT_HID), lambda l, c: (0, 0))
    return pl.pallas_call(
        functools.partial(_filter_kernel, L),
        grid=(DEPTH, ncb),
        in_specs=[
            pl.BlockSpec((L // 2, 2 * FEAT_PAD), lambda l, c: (0, 0)),
            matrix(POS_EMB, FILT_HID), per_layer_vec,
            matrix(FILT_HID, FILT_HID), per_layer_vec, per_layer_vec,
            pl.BlockSpec((1, FILT_HID, FILT_CB), lambda l, c: (l, 0, c)),
            pl.BlockSpec((1, FILT_HID, FILT_CB), lambda l, c: (l, 0, ncb + c)),
            pl.BlockSpec((L, FILT_CB), lambda l, c: (0, c)),
            pl.BlockSpec((2 * L, L), lambda l, c: (0, 0)),
        ],
        out_specs=pl.BlockSpec((1, 3, L, FILT_CB), lambda l, c: (l, 0, 0, c)),
        out_shape=jax.ShapeDtypeStruct((DEPTH, 3, L, D_HY), F32),
        scratch_shapes=[pltpu.VMEM((L // 2, FEAT_PAD), F32), pltpu.VMEM((2 * FEAT_PAD, FEAT_PAD), F32),
                        pltpu.VMEM((FEAT_PAD, FEAT_PAD), F32), pltpu.VMEM((8, FEAT_PAD), F32),
                        pltpu.VMEM((4, FEAT_PAD, FILT_CB), BF16)],
        compiler_params=_params("arbitrary", "arbitrary"),
        name=f"hyena_filter_{L}",
    )(jnp.asarray(z), w1, b1, w2, b2, freq, w3, w3, jnp.asarray(dec), f_bf)


def _hyena_kernel(L, layer, x0_ref, x1_ref, vv_ref, g_ref, cw0_ref, cw1_ref, cwv_ref, cb0_ref, cb1_ref,
                  cbv_ref, bias_ref, coef_ref, f_ref, ft_ref, *rest):
    o_ref = rest[-1]
    this_layer = slice(layer, layer + 1)
    row = lax.broadcasted_iota(jnp.int32, (L, 1), 0)
    is_first = row == 0
    is_last = row == L - 1

    def sconv(x_ref, w_ref, b_ref):
        x = x_ref[...]
        prev = jnp.where(is_first, 0.0, pltpu.roll(x, 1, axis=0))
        nxt = jnp.where(is_last, 0.0, pltpu.roll(x, L - 1, axis=0))
        return w_ref[0, 0:1, :] * prev + w_ref[0, 1:2, :] * x + w_ref[0, 2:3, :] * nxt + b_ref[this_layer, :]

    z = sconv(vv_ref, cwv_ref, cbv_ref) * sconv(x1_ref, cw1_ref, cb1_ref)
    ab = jnp.dot(f_ref[...], z.astype(BF16), preferred_element_type=F32)
    a, b = ab[0:L], ab[L:2 * L]
    g1, g2, g3 = coef_ref[0, 0], coef_ref[0, 1], coef_ref[0, 2]
    pq = jnp.concatenate([a * g1 - b * g2, a * g2 + b * g3], axis=0).astype(BF16)
    y = jnp.dot(ft_ref[...], pq, preferred_element_type=F32) + bias_ref[this_layer, :] * z
    o_ref[...] = (sconv(x0_ref, cw0_ref, cb0_ref) * y * _silu(g_ref[...])).astype(BF16)


def _hyena(proj, L, nb, row0, cb, layer, conv_w, conv_b, hy_bias, coef, f_bf, ft_bf, hy_prev=None):
    ncb = D_HY // cb
    rows = L
    blk0 = row0 // rows
    extra_specs = [] if hy_prev is None else [pl.BlockSpec(memory_space=pl.ANY)]
    extra_args = [] if hy_prev is None else [hy_prev]
    aliases = {} if hy_prev is None else {14: 0}
    col = lambda off: (lambda c, b: (blk0 + b, off // cb + c))
    cw = lambda part: pl.BlockSpec((1, 3, cb), lambda c, b: (layer, 0, part * ncb + c))
    cbias = lambda part: pl.BlockSpec((DEPTH, cb), lambda c, b: (0, part * ncb + c))
    once = pl.Buffered(1)
    return pl.pallas_call(
        functools.partial(_hyena_kernel, L, layer),
        grid=(ncb, nb),
        in_specs=[
            pl.BlockSpec((rows, cb), col(COL_X0)),
            pl.BlockSpec((rows, cb), col(COL_X1)),
            pl.BlockSpec((rows, cb), col(COL_VV)),
            pl.BlockSpec((rows, cb), col(COL_GH)),
            cw(0), cw(1), cw(2), cbias(0), cbias(1), cbias(2),
            pl.BlockSpec((DEPTH, cb), lambda c, b: (0, c)),
            pl.BlockSpec((1, 3, L, cb), lambda c, b: (layer, 0, 0, c)),
            pl.BlockSpec((2 * L, L), lambda c, b: (0, 0), pipeline_mode=once),
            pl.BlockSpec((L, 2 * L), lambda c, b: (0, 0), pipeline_mode=once),
            *extra_specs,
        ],
        out_specs=pl.BlockSpec((rows, cb), lambda c, b: (blk0 + b, c)),
        out_shape=jax.ShapeDtypeStruct((M_ALL, D_HY), BF16),
        input_output_aliases=aliases,
        compiler_params=_params("arbitrary", "arbitrary"),
        name=f"hyena_{L}",
    )(proj, proj, proj, proj, conv_w, conv_w, conv_w, conv_b, conv_b, conv_b, hy_bias, coef, f_bf, ft_bf,
      *extra_args)


OUTPROJ_TM = 256
OUTPROJ_TM_FINAL = 512


def _outproj_kernel(split_in, final, tm, tile0, gain_row, *refs):
    it = iter(refs)
    att_ref, hy_ref, w_ref = next(it), next(it), next(it)
    x_refs = (next(it), next(it)) if split_in else (next(it),)
    gate_ref, g_ref = next(it), next(it)
    shift_ref, scale_ref = (None, None) if final else (next(it), next(it))
    out_refs = (next(it),) if final else (next(it), next(it))
    w_scr = next(it)
    i = pl.program_id(0)

    @pl.when(i == 0)
    def _():
        w_scr[...] = w_ref[0].astype(BF16)

    out = (jnp.dot(att_ref[...], w_scr[0:D_ATT, :], preferred_element_type=F32)
           + jnp.dot(hy_ref[...], w_scr[D_ATT:D_ATT + D_HY, :], preferred_element_type=F32))
    tile = tile0 + i
    if split_in:
        x = jnp.where(tile < M_CTX // tm, x_refs[0][...], x_refs[1][...])
    else:
        x = x_refs[0][...]
    y = x + _mod_row(gate_ref, tile, tm) * out
    inv = lax.rsqrt(jnp.mean(y * y, axis=-1, keepdims=True) + EPS)
    g = g_ref[gain_row:gain_row + 1, :]
    if final:
        out_refs[0][...] = y * inv * g
    else:
        out_refs[0][...] = y
        gain = g * (1.0 + _mod_row(scale_ref, tile, tm))
        out_refs[1][...] = (y * inv * gain + _mod_row(shift_ref, tile, tm)).astype(BF16)


def _out_proj(att, hy, w_out, layer, x_parts, mod, gains, final, tm, tile0=0, n_tiles=None):
    n_tiles = M_ALL // tm if n_tiles is None else n_tiles
    split_in = len(x_parts) == 2
    assert not split_in or (tile0 == 0 and n_tiles == M_ALL // tm)
    row_tile = lambda width: pl.BlockSpec((tm, width), lambda i: (tile0 + i, 0))
    out_tile = pl.BlockSpec((tm, D_MODEL), lambda i: (i, 0))
    x_specs = list(_two_stream_specs(tm, D_MODEL)) if split_in else [row_tile(D_MODEL)]
    in_specs = [
        row_tile(D_ATT),
        row_tile(D_HY),
        pl.BlockSpec((1, D_ATT + D_HY, D_MODEL), lambda i: (layer, 0, 0), pipeline_mode=pl.Buffered(1)),
        *x_specs,
        _mod_spec(layer, MOD_GATE),
        pl.BlockSpec(gains.shape, lambda i: (0, 0)),
    ]
    args = [att, hy, w_out, *x_parts, mod, gains]
    if final:
        out_specs = [out_tile]
        out_shape = [jax.ShapeDtypeStruct((n_tiles * tm, D_MODEL), F32)]
    else:
        in_specs += [_mod_spec(layer + 1, MOD_SHIFT), _mod_spec(layer + 1, MOD_SCALE)]
        args += [mod, mod]
        out_specs = [out_tile, out_tile]
        out_shape = [jax.ShapeDtypeStruct((n_tiles * tm, D_MODEL), F32),
                     jax.ShapeDtypeStruct((n_tiles * tm, D_MODEL), BF16)]
    return pl.pallas_call(
        functools.partial(_outproj_kernel, split_in, final, tm, tile0, 0 if final else layer + 1),
        grid=(n_tiles,),
        in_specs=in_specs,
        out_specs=out_specs,
        out_shape=out_shape,
        scratch_shapes=[pltpu.VMEM((D_ATT + D_HY, D_MODEL), BF16)],
        compiler_params=_params("arbitrary"),
        name="out_proj_final" if final else "out_proj",
    )(*args)


def _rope_tables():
    t = np.arange(DEC_SEQ)
    row = (t // GRID_W).astype(np.float64)
    col = (t % GRID_W).astype(np.float64)
    pairs = HEAD_DIM // 4
    inv_freq = ROPE_THETA ** (-np.arange(pairs, dtype=np.float64) / pairs)
    ang = np.concatenate([row[:, None] * inv_freq, col[:, None] * inv_freq], axis=-1)
    cos = np.repeat(np.cos(ang), 2, axis=-1).astype(np.float32)
    sin = np.repeat(np.sin(ang), 2, axis=-1).astype(np.float32)
    even = (np.arange(HEAD_DIM) % 2 == 0)[None, :]
    sin_a = np.where(even, -sin, 0.0).astype(np.float32)
    sin_b = np.where(even, 0.0, sin).astype(np.float32)
    return jnp.asarray(cos), jnp.asarray(sin_a), jnp.asarray(sin_b)


def kernel(x_prompt, x_sample, cache_k, cache_v, c, c_ctx, norm_g, w_ada, b_ada, w_in, q_norm_g, k_norm_g,
           conv_w, conv_b, filt_w1, filt_b1, filt_w2, filt_b2, filt_w3, filt_freq, hy_bias, w_out, final_norm_g):
    ctx = x_prompt.reshape(BATCH * SEQ, D_MODEL)
    lat = x_sample.reshape(DEC_BATCH * DEC_SEQ, D_MODEL)
    cache_k4 = cache_k.reshape(DEC_BATCH, DEPTH, PAST_LEN, D_KV)
    cache_v4 = cache_v.reshape(DEC_BATCH, DEPTH, PAST_LEN, D_KV)

    mod = _modulation(c_ctx, c, w_ada, b_ada)

    rope_tabs = _rope_tables()
    dft = {}
    coefs = {}
    for L in (SEQ, DEC_SEQ):
        f_np = _dft_matrix(L)
        f_bf = jnp.asarray(f_np).astype(BF16)
        dft[L] = (f_bf, jnp.asarray(np.ascontiguousarray(f_np.T)).astype(BF16))
        coefs[L] = _filter_spectra(L, filt_w1, filt_b1, filt_w2, filt_b2, filt_freq, filt_w3, f_bf)

    final_g = final_norm_g.reshape(1, D_MODEL)

    x_parts = (ctx, lat)
    h = _norm_mod(ctx, lat, norm_g, mod, 0)
    kv_out = None
    for l in range(DEPTH):
        final = l == DEPTH - 1

        proj = _in_proj(h, w_in, l)

        att, new_k, new_v = _attention(proj, q_norm_g, k_norm_g, l, SEQ, CTX_TQ, BATCH, 0, nsb=4, kv_prev=kv_out,
                                       emit_kv=True)
        kv_out = (new_k, new_v)
        (att,) = _attention(proj, q_norm_g, k_norm_g, l, DEC_SEQ, LAT_TQ, DEC_BATCH, M_CTX, nsb=4,
                            cache=(cache_k4, cache_v4), rope_tabs=rope_tabs, att_prev=att)

        hy = _hyena(proj, SEQ, BATCH, 0, D_HY, l, conv_w, conv_b, hy_bias, coefs[SEQ], *dft[SEQ])
        hy = _hyena(proj, DEC_SEQ, DEC_BATCH, M_CTX, 512, l, conv_w, conv_b, hy_bias,
                    coefs[DEC_SEQ], *dft[DEC_SEQ], hy_prev=hy)

        if final:
            assert len(x_parts) == 1
            tm = OUTPROJ_TM_FINAL
            nc = M_CTX // tm
            (y_ctx,) = _out_proj(att, hy, w_out, l, x_parts, mod, final_g, True, tm, 0, nc)
            (y_lat,) = _out_proj(att, hy, w_out, l, x_parts, mod, final_g, True, tm, nc, M_LAT // tm)
        else:
            y, h = _out_proj(att, hy, w_out, l, x_parts, mod, norm_g, False, OUTPROJ_TM)
            x_parts = (y,)

    y_prompt = y_ctx.reshape(BATCH, SEQ, D_MODEL)
    y_sample = y_lat.reshape(DEC_BATCH, DEC_SEQ, D_MODEL)
    return (y_prompt, y_sample, kv_out[0], kv_out[1])
```

```python
import functools
import math

import numpy as np
import jax
import jax.numpy as jnp
from jax import lax
from jax.experimental import pallas as pl
from jax.experimental.pallas import tpu as pltpu

D_MODEL = 2048
BATCH = 16
SEQ = 256
DEPTH = 2
DEC_BATCH = 2
DEC_SEQ = 1024
PAST_LEN = 512
GRID_W = 64
D_ATT = 1024
D_HY = 1024
HEAD_DIM = 128
N_HEADS = 8
N_KV_HEADS = 4
GROUP = 2
D_KV = 512
ROPE_THETA = 10000.0
POS_BANDS = 16
POS_EMB = 33
FILT_HID = 64
DECAY_TARGET = 1e-2
FAST_DECAY_PCT = 0.3
SLOW_DECAY_PCT = 1.5
DECAY_SHIFT = 0.05
EPS = 1e-6
D_IN = 7168

COL_Q, COL_K, COL_V, COL_GA, COL_X0, COL_X1, COL_VV, COL_GH = 0, 1024, 1536, 2048, 3072, 4096, 5120, 6144

F32 = jnp.float32
BF16 = jnp.bfloat16

VMEM_LIMIT_BYTES = 56 * 1024 * 1024
FEAT_PAD = 128


def _params(*sem):
    return pltpu.CompilerParams(dimension_semantics=sem, vmem_limit_bytes=VMEM_LIMIT_BYTES)


def _silu(x):
    half = 0.5 * x
    return half + half * jnp.tanh(half)


def _rms(x, g):
    return x * lax.rsqrt(jnp.mean(x * x, axis=-1, keepdims=True) + EPS) * g


MOD_ROWS = 8
MOD_TN = 1024


def _mod_kernel(cctx_ref, c_ref, w_ref, b_ref, o_ref, c_scr):
    c_scr[...] = jnp.zeros(c_scr.shape, F32)
    c_scr[0:1, :] = cctx_ref[...]
    c_scr[1:1 + DEC_BATCH, :] = c_ref[...]
    s = _silu(c_scr[...]).astype(BF16)
    layer = pl.program_id(0)
    o_ref[0] = jnp.dot(s, w_ref[0].astype(BF16), preferred_element_type=F32) + b_ref[pl.ds(layer, 1), :]


def _modulation(c_ctx, c, w_ada, b_ada):
    n = 3 * D_MODEL
    return pl.pallas_call(
        _mod_kernel,
        grid=(DEPTH, n // MOD_TN),
        in_specs=[
            pl.BlockSpec((1, D_MODEL), lambda l, j: (0, 0)),
            pl.BlockSpec((DEC_BATCH, D_MODEL), lambda l, j: (0, 0)),
            pl.BlockSpec((1, D_MODEL, MOD_TN), lambda l, j: (l, 0, j)),
            pl.BlockSpec((DEPTH, MOD_TN), lambda l, j: (0, j)),
        ],
        out_specs=pl.BlockSpec((1, MOD_ROWS, MOD_TN), lambda l, j: (l, 0, j)),
        out_shape=jax.ShapeDtypeStruct((DEPTH, MOD_ROWS, n), F32),
        scratch_shapes=[pltpu.VMEM((MOD_ROWS, D_MODEL), F32)],
        compiler_params=_params("arbitrary", "arbitrary"),
        name="adaln_mod",
    )(c_ctx.reshape(1, D_MODEL), c, w_ada, b_ada)


M_CTX = BATCH * SEQ
M_LAT = DEC_BATCH * DEC_SEQ
M_ALL = M_CTX + M_LAT
NORM_TM = 1024
INPROJ_TM = 3072
INPROJ_TN = 512


ROW_CHUNK = 32


def _modulated_norm_rows(x_ref, h_ref, g, shift, scale):
    gain = g * (1.0 + scale)

    def body(r, _):
        rows = pl.ds(pl.multiple_of(r * ROW_CHUNK, ROW_CHUNK), ROW_CHUNK)
        x = x_ref[rows, :]
        inv = lax.rsqrt(jnp.mean(x * x, axis=-1, keepdims=True) + EPS)
        h_ref[rows, :] = (x * inv * gain + shift).astype(BF16)
        return 0

    lax.fori_loop(0, x_ref.shape[0] // ROW_CHUNK, body, 0, unroll=4)


MOD_SHIFT, MOD_SCALE, MOD_GATE = 0, 1, 2


def _mod_spec(layer, part):
    return pl.BlockSpec((1, MOD_ROWS, D_MODEL), lambda i: (layer, 0, part))


def _mod_row(ref, tile, tm):
    nc = M_CTX // tm
    per_batch = DEC_SEQ // tm
    row = jnp.where(tile < nc, 0, 1 + (tile - nc) // per_batch)
    return ref[0, pl.ds(row, 1), :]


def _two_stream_specs(tm, width):
    nc = M_CTX // tm
    return (pl.BlockSpec((tm, width), lambda i: (jnp.minimum(i, nc - 1), 0)),
            pl.BlockSpec((tm, width), lambda i: (jnp.maximum(i - nc, 0), 0)))


def _norm_kernel(layer, xc_ref, xl_ref, g_ref, shift_ref, scale_ref, h_ref):
    i = pl.program_id(0)

    def emit(x_ref):
        _modulated_norm_rows(x_ref, h_ref, g_ref[layer:layer + 1, :], _mod_row(shift_ref, i, NORM_TM),
                             _mod_row(scale_ref, i, NORM_TM))

    pl.when(i < M_CTX // NORM_TM)(lambda: emit(xc_ref))
    pl.when(i >= M_CTX // NORM_TM)(lambda: emit(xl_ref))


def _norm_mod(x_ctx, x_lat, norm_g, mod, layer):
    xc_spec, xl_spec = _two_stream_specs(NORM_TM, D_MODEL)
    return pl.pallas_call(
        functools.partial(_norm_kernel, layer),
        grid=(M_ALL // NORM_TM,),
        in_specs=[
            xc_spec, xl_spec,
            pl.BlockSpec((DEPTH, D_MODEL), lambda i: (0, 0)),
            _mod_spec(layer, MOD_SHIFT),
            _mod_spec(layer, MOD_SCALE),
        ],
        out_specs=pl.BlockSpec((NORM_TM, D_MODEL), lambda i: (i, 0)),
        out_shape=jax.ShapeDtypeStruct((M_ALL, D_MODEL), BF16),
        compiler_params=_params("arbitrary"),
        name="norm_mod",
    )(x_ctx, x_lat, norm_g, mod, mod)


def _inproj_kernel(h_hbm, w_ref, o_ref, w_scr, h_scr, h_sems):
    j, i = pl.program_id(0), pl.program_id(1)
    n_chunks = M_ALL // INPROJ_TM

    def h_copy(c):
        rows = pl.ds(c * INPROJ_TM, INPROJ_TM)
        return pltpu.make_async_copy(h_hbm.at[rows], h_scr.at[rows], h_sems.at[c])

    @pl.when((j == 0) & (i == 0))
    def _():
        for c in range(n_chunks):
            h_copy(c).start()

    @pl.when(i == 0)
    def _():
        w_scr[...] = w_ref[0].astype(BF16)

    for c in range(n_chunks):
        @pl.when((j == 0) & (i == c))
        def _():
            h_copy(c).wait()

    rows = pl.ds(pl.multiple_of(i * INPROJ_TM, INPROJ_TM), INPROJ_TM)
    o_ref[...] = jnp.dot(h_scr[rows, :], w_scr[...], preferred_element_type=F32)


def _in_proj(h, w_in, layer):
    return pl.pallas_call(
        _inproj_kernel,
        grid=(D_IN // INPROJ_TN, M_ALL // INPROJ_TM),
        in_specs=[
            pl.BlockSpec(memory_space=pl.ANY),
            pl.BlockSpec((1, D_MODEL, INPROJ_TN), lambda j, i: (layer, 0, j)),
        ],
        out_specs=pl.BlockSpec((INPROJ_TM, INPROJ_TN), lambda j, i: (i, j)),
        out_shape=jax.ShapeDtypeStruct((M_ALL, D_IN), F32),
        scratch_shapes=[pltpu.VMEM((D_MODEL, INPROJ_TN), BF16), pltpu.VMEM((M_ALL, D_MODEL), BF16),
                        pltpu.SemaphoreType.DMA((M_ALL // INPROJ_TM,))],
        compiler_params=_params("arbitrary", "arbitrary"),
        name="in_proj",
    )(h, w_in)


ATT_SCALE = 1.0 / math.sqrt(HEAD_DIM)


def _head(ref_or_val, h):
    return ref_or_val[:, h * HEAD_DIM:(h + 1) * HEAD_DIM]


Q_PRESCALE = ATT_SCALE * math.log2(math.e)


def _softmax_pv(q2, k_bf, v_ext):
    s = lax.dot_general(q2, k_bf, (((1,), (1,)), ((), ())), preferred_element_type=F32)
    p = jnp.exp2(s - jnp.max(s, axis=-1, keepdims=True)).astype(BF16)
    o = jnp.dot(p, v_ext, preferred_element_type=F32)
    return o[:, 0:HEAD_DIM] / o[:, HEAD_DIM:2 * HEAD_DIM]


def _rope(x, cos2, sin_a, sin_b):
    nxt = pltpu.roll(x, HEAD_DIM - 1, axis=1)
    prv = pltpu.roll(x, 1, axis=1)
    return x * cos2 + nxt * sin_a + prv * sin_b


CTX_TQ = SEQ
LAT_TQ = 128


def _attn_kernel(rope, n_cache, layer, emit_kv, n_aliased, nsb, *refs):
    it = iter(refs)
    q_all, kv_all, g_all, qg_ref, kg_ref = (next(it) for _ in range(5))
    ck_ref, cv_ref = (next(it), next(it)) if n_cache else (None, None)
    q_tabs = tuple(next(it) for _ in range(3)) if rope else None
    k_tabs = tuple(next(it) for _ in range(3)) if rope else None
    for _ in range(n_aliased):
        next(it)
    att_all = next(it)
    ko_hbm, vo_hbm = (next(it), next(it)) if emit_kv else (None, None)
    k_all, v_all = next(it), next(it)
    kf_all, kv_sems = (next(it), next(it)) if emit_kv else (None, None)
    single_tile = q_all.shape[0] == kv_all.shape[0]
    n_new = kv_all.shape[0] // nsb if single_tile else kv_all.shape[0]
    assert not emit_kv or single_tile

    def part(ref, sb, rows_per_seq):
        return ref if nsb == 1 else ref.at[pl.ds(sb * rows_per_seq, rows_per_seq)]

    def kv_copy(sb, h, is_value):
        b = pl.program_id(0) * nsb + sb
        if is_value:
            src = part(kv_all, sb, n_new).at[:, pl.ds((N_KV_HEADS + h) * HEAD_DIM, HEAD_DIM)]
            return pltpu.make_async_copy(src, vo_hbm.at[b, layer, :, h, :],
                                         kv_sems.at[sb, N_KV_HEADS + h])
        src = part(kf_all, sb, n_new).at[:, pl.ds(h * HEAD_DIM, HEAD_DIM)]
        return pltpu.make_async_copy(src, ko_hbm.at[b, layer, :, h, :], kv_sems.at[sb, h])

    def prepare_keys_values(sb):
        kv_ref, k_scr, v_scr = part(kv_all, sb, n_new), part(k_all, sb, n_new + n_cache), part(v_all, sb, n_new + n_cache)
        kf_scr = part(kf_all, sb, n_new) if emit_kv else None
        for h in range(N_KV_HEADS):
            kn = _rms(_head(kv_ref, h), kg_ref[layer:layer + 1, :])
            vh = _head(kv_ref, N_KV_HEADS + h)
            if emit_kv:
                kf_scr[:, h * HEAD_DIM:(h + 1) * HEAD_DIM] = kn
            if rope:
                kn = _rope(kn, *(t[...] for t in k_tabs))
            k_scr[0:n_new, h * HEAD_DIM:(h + 1) * HEAD_DIM] = kn.astype(BF16)
            v0 = 2 * h * HEAD_DIM
            v_scr[0:n_new, v0:v0 + HEAD_DIM] = vh.astype(BF16)
            if n_cache:
                v_scr[n_new:n_new + n_cache, v0:v0 + HEAD_DIM] = _head(cv_ref[0, 0], h).astype(BF16)
            v_scr[:, v0 + HEAD_DIM:v0 + 2 * HEAD_DIM] = jnp.ones((n_new + n_cache, HEAD_DIM), BF16)
        if n_cache:
            k_scr[n_new:n_new + n_cache, :] = ck_ref[0, 0].astype(BF16)

    def attend(sb):
        rows = q_all.shape[0] // nsb
        q_ref, g_ref, att_ref = part(q_all, sb, rows), part(g_all, sb, rows), part(att_all, sb, rows)
        kv_sb = sb if single_tile else 0
        k_scr, v_scr = part(k_all, kv_sb, n_new + n_cache), part(v_all, kv_sb, n_new + n_cache)

        def query(hq):
            x = _rms(_head(q_ref, hq), qg_ref[layer:layer + 1, :])
            if rope:
                x = _rope(x, *(part(t, sb, rows)[...] for t in q_tabs))
            return x * Q_PRESCALE

        for h in range(N_KV_HEADS):
            q2 = jnp.concatenate([query(GROUP * h + g) for g in range(GROUP)], axis=0).astype(BF16)
            o = _softmax_pv(q2, _head(k_scr, h), v_scr[:, 2 * h * HEAD_DIM:2 * (h + 1) * HEAD_DIM])
            for g in range(GROUP):
                hq = GROUP * h + g
                gate = _silu(_head(g_ref, hq))
                att_ref[:, hq * HEAD_DIM:(hq + 1) * HEAD_DIM] = (o[g * rows:(g + 1) * rows] * gate).astype(BF16)

    every = [(sb, h) for sb in range(nsb) for h in range(N_KV_HEADS)]
    if single_tile:
        if emit_kv:
            for sb, h in every:
                kv_copy(sb, h, True).start()
        for sb in range(nsb):
            prepare_keys_values(sb)
        if emit_kv:
            for sb, h in every:
                kv_copy(sb, h, False).start()
    else:
        pl.when(pl.program_id(1) == 0)(lambda: prepare_keys_values(0))

    for sb in range(nsb):
        attend(sb)

    if emit_kv:
        for sb, h in every:
            kv_copy(sb, h, False).wait()
            kv_copy(sb, h, True).wait()


def _attention(proj, q_g, k_g, layer, seq, tq, n_batch, row0, *, nsb=1, cache=None, rope_tabs=None,
               att_prev=None, kv_prev=None, emit_kv=False):
    whole = tq == seq
    assert not whole or cache is None
    tq = nsb * tq
    kv_rows = nsb * seq if whole else seq
    nq = kv_rows // tq
    q0, kv0 = row0 // tq, row0 // kv_rows
    n_cache = 0 if cache is None else cache[0].shape[2]
    vec = pl.BlockSpec((DEPTH, HEAD_DIM), lambda b, i: (0, 0))
    in_specs = [
        pl.BlockSpec((tq, D_ATT), lambda b, i: (q0 + b * nq + i, COL_Q // D_ATT)),
        pl.BlockSpec((kv_rows, 2 * D_KV), lambda b, i: (kv0 + b, COL_K // (2 * D_KV))),
        pl.BlockSpec((tq, D_ATT), lambda b, i: (q0 + b * nq + i, COL_GA // D_ATT)),
        vec, vec,
    ]
    args = [proj, proj, proj, q_g, k_g]
    if cache is not None:
        in_specs += [pl.BlockSpec((1, 1, n_cache, D_KV), lambda b, i: (b, layer, 0, 0))] * 2
        args += list(cache)
    if rope_tabs is not None:
        in_specs += [pl.BlockSpec((tq, HEAD_DIM), lambda b, i: (i, 0))] * 3
        in_specs += [pl.BlockSpec((seq, HEAD_DIM), lambda b, i: (0, 0))] * 3
        args += list(rope_tabs) * 2
    aliased = ([] if att_prev is None else [att_prev]) + ([] if kv_prev is None else list(kv_prev))
    out_first = 0 if att_prev is not None else 1
    aliases = {len(args) + n: out_first + n for n in range(len(aliased))}
    in_specs += [pl.BlockSpec(memory_space=pl.ANY)] * len(aliased)
    args += aliased
    out_specs = [pl.BlockSpec((tq, D_ATT), lambda b, i: (q0 + b * nq + i, 0))]
    out_shape = [jax.ShapeDtypeStruct((M_ALL, D_ATT), BF16)]
    keys = kv_rows + n_cache
    scratch =[pltpu.VMEM((keys, D_KV), BF16), pltpu.VMEM((keys, 2 * D_KV), BF16)]
    if emit_kv:
        out_specs += [pl.BlockSpec(memory_space=pl.ANY)] * 2
        out_shape += [jax.ShapeDtypeStruct((n_batch, DEPTH, seq, N_KV_HEADS, HEAD_DIM), F32)] * 2
        scratch += [pltpu.VMEM((kv_rows, D_KV), F32), pltpu.SemaphoreType.DMA((nsb, 2 * N_KV_HEADS))]
    return pl.pallas_call(
        functools.partial(_attn_kernel, rope_tabs is not None, n_cache, layer, emit_kv, len(aliased), nsb),
        grid=(n_batch * seq // kv_rows, nq),
        in_specs=in_specs,
        out_specs=out_specs,
        out_shape=out_shape,
        input_output_aliases=aliases,
        scratch_shapes=scratch,
        compiler_params=_params("arbitrary", "arbitrary"),
        name=f"attention_{seq}",
    )(*args)


def _dft_matrix(L):
    k = np.arange(L, dtype=np.int64)[:, None]
    t = np.arange(L, dtype=np.int64)[None, :]
    ang = 2.0 * np.pi * ((k * t) % (2 * L)).astype(np.float64) / (2 * L)
    top = np.cos(ang)
    bot = np.sin(ang)
    bot[0, :] = np.where(np.arange(L) % 2 == 0, 1.0, -1.0)
    return np.concatenate([top, bot], axis=0).astype(np.float32)


def _filter_kernel(L, z_ref, w1_ref, b1_ref, w2_ref, b2_ref, fr_ref, w3f_ref, w3b_ref, dec_ref,
                   f_ref, o_ref, hdn_scr, w1_scr, w2_scr, vec_scr, w3_scr):
    hp = lax.Precision.HIGHEST
    layer = pl.program_id(0)

    halves = ((0, 0), (FEAT_PAD, FILT_HID))

    @pl.when(pl.program_id(1) == 0)
    def _():
        w1_scr[...] = jnp.zeros(w1_scr.shape, F32)
        w2_scr[...] = jnp.zeros(w2_scr.shape, F32)
        vec_scr[...] = jnp.zeros(vec_scr.shape, F32)
        w3_scr[...] = jnp.zeros(w3_scr.shape, BF16)
        for feat0, hid0 in halves:
            w1_scr[feat0:feat0 + POS_EMB, hid0:hid0 + FILT_HID] = w1_ref[0]
            w2_scr[hid0:hid0 + FILT_HID, hid0:hid0 + FILT_HID] = w2_ref[0]
            for r, ref in enumerate((b1_ref, b2_ref, fr_ref)):
                vec_scr[r:r + 1, hid0:hid0 + FILT_HID] = ref[pl.ds(layer, 1), :]
        b1, b2, fr = vec_scr[0:1, :], vec_scr[1:2, :], vec_scr[2:3, :]
        h1 = jnp.sin(fr * (jnp.dot(z_ref[...], w1_scr[...], precision=hp, preferred_element_type=F32) + b1))
        hdn_scr[...] = jnp.sin(fr * (jnp.dot(h1, w2_scr[...], precision=hp, preferred_element_type=F32) + b2))

    hdn_bf = hdn_scr[...].astype(BF16)
    dec = dec_ref[...]

    def project(w3_ref, slot):
        parts = []
        for k, (_, hid0) in enumerate(halves):
            w3_scr[slot + k, hid0:hid0 + FILT_HID, :] = w3_ref[0].astype(BF16)
            parts.append(jnp.dot(hdn_bf, w3_scr[slot + k], preferred_element_type=F32))
        return jnp.concatenate(parts, axis=0) * dec

    h_f = project(w3f_ref, 0)
    h_b = project(w3b_ref, 2)
    hs = h_f + h_b
    ha = jnp.dot(f_ref[0:L, :], hs.astype(BF16), preferred_element_type=F32)
    hb = jnp.dot(f_ref[L:2 * L, :], (h_f - h_b).astype(BF16), preferred_element_type=F32)
    row = lax.broadcasted_iota(jnp.int32, (L, 1), 0)
    first = row == 0
    nyquist = jnp.sum(jnp.where(row % 2 == 0, hs, -hs), axis=0, keepdims=True)
    wk = jnp.where(first, 1.0 / (2 * L), 2.0 / (2 * L))
    o_ref[0, 0] = ha * wk
    o_ref[0, 1] = jnp.where(first, 0.0, hb * wk)
    o_ref[0, 2] = jnp.where(first, nyquist, ha) * wk


FILT_CB = 512


def _filter_spectra(L, w1, b1, w2, b2, freq, w3, f_bf):
    tpos = np.arange(L, dtype=np.float64)
    t_norm = tpos / max(L - 1, 1)
    w = 2.0 * math.pi * tpos / L
    bands = np.linspace(1e-4, POS_BANDS - 1, POS_BANDS)
    z = np.concatenate([t_norm[:, None], np.cos(w[:, None] * bands), -np.sin(w[:, None] * bands)], axis=-1)
    z = np.pad(z, ((0, 0), (0, FEAT_PAD - POS_EMB))).astype(np.float32)
    z = np.concatenate([z[:L // 2], z[L // 2:]], axis=1)
    max_decay = math.log(DECAY_TARGET) / FAST_DECAY_PCT
    min_decay = math.log(DECAY_TARGET) / SLOW_DECAY_PCT
    deltas = np.abs(np.linspace(min_decay, max_decay, D_HY))
    dec = (np.exp(-t_norm[:, None] * deltas) + DECAY_SHIFT).astype(np.float32)

    ncb = D_HY // FILT_CB
    matrix = lambda rows, cols: pl.BlockSpec((1, rows, cols), lambda l, c: (l, 0, 0))
    per_layer_vec = pl.BlockSpec((DEPTH, FILT_HID), lambda l, c: (0, 0))
    return pl.pallas_call(
        functools.partial(_filter_kernel, L),
        grid=(DEPTH, ncb),
        in_specs=[
            pl.BlockSpec((L // 2, 2 * FEAT_PAD), lambda l, c: (0, 0)),
            matrix(POS_EMB, FILT_HID), per_layer_vec,
            matrix(FILT_HID, FILT_HID), per_layer_vec, per_layer_vec,
            pl.BlockSpec((1, FILT_HID, FILT_CB), lambda l, c: (l, 0, c)),
            pl.BlockSpec((1, FILT_HID, FILT_CB), lambda l, c: (l, 0, ncb + c)),
            pl.BlockSpec((L, FILT_CB), lambda l, c: (0, c)),
            pl.BlockSpec((2 * L, L), lambda l, c: (0, 0)),
        ],
        out_specs=pl.BlockSpec((1, 3, L, FILT_CB), lambda l, c: (l, 0, 0, c)),
        out_shape=jax.ShapeDtypeStruct((DEPTH, 3, L, D_HY), F32),
        scratch_shapes=[pltpu.VMEM((L // 2, FEAT_PAD), F32), pltpu.VMEM((2 * FEAT_PAD, FEAT_PAD), F32),
                        pltpu.VMEM((FEAT_PAD, FEAT_PAD), F32), pltpu.VMEM((8, FEAT_PAD), F32),
                        pltpu.VMEM((4, FEAT_PAD, FILT_CB), BF16)],
        compiler_params=_params("arbitrary", "arbitrary"),
        name=f"hyena_filter_{L}",
    )(jnp.asarray(z), w1, b1, w2, b2, freq, w3, w3, jnp.asarray(dec), f_bf)


def _hyena_kernel(L, layer, x0_ref, x1_ref, vv_ref, g_ref, cw0_ref, cw1_ref, cwv_ref, cb0_ref, cb1_ref,
                  cbv_ref, bias_ref, coef_ref, f_ref, ft_ref, *rest):
    o_ref = rest[-1]
    this_layer = slice(layer, layer + 1)
    row = lax.broadcasted_iota(jnp.int32, (L, 1), 0)
    is_first = row == 0
    is_last = row == L - 1

    def sconv(x_ref, w_ref, b_ref):
        x = x_ref[...]
        prev = jnp.where(is_first, 0.0, pltpu.roll(x, 1, axis=0))
        nxt = jnp.where(is_last, 0.0, pltpu.roll(x, L - 1, axis=0))
        return w_ref[0, 0:1, :] * prev + w_ref[0, 1:2, :] * x + w_ref[0, 2:3, :] * nxt + b_ref[this_layer, :]

    z = sconv(vv_ref, cwv_ref, cbv_ref) * sconv(x1_ref, cw1_ref, cb1_ref)
    ab = jnp.dot(f_ref[...], z.astype(BF16), preferred_element_type=F32)
    a, b = ab[0:L], ab[L:2 * L]
    g1, g2, g3 = coef_ref[0, 0], coef_ref[0, 1], coef_ref[0, 2]
    pq = jnp.concatenate([a * g1 - b * g2, a * g2 + b * g3], axis=0).astype(BF16)
    y = jnp.dot(ft_ref[...], pq, preferred_element_type=F32) + bias_ref[this_layer, :] * z
    o_ref[...] = (sconv(x0_ref, cw0_ref, cb0_ref) * y * _silu(g_ref[...])).astype(BF16)


def _hyena(proj, L, nb, row0, cb, layer, conv_w, conv_b, hy_bias, coef, f_bf, ft_bf, hy_prev=None):
    ncb = D_HY // cb
    rows = L
    blk0 = row0 // rows
    extra_specs = [] if hy_prev is None else [pl.BlockSpec(memory_space=pl.ANY)]
    extra_args = [] if hy_prev is None else [hy_prev]
    aliases = {} if hy_prev is None else {14: 0}
    col = lambda off: (lambda c, b: (blk0 + b, off // cb + c))
    cw = lambda part: pl.BlockSpec((1, 3, cb), lambda c, b: (layer, 0, part * ncb + c))
    cbias = lambda part: pl.BlockSpec((DEPTH, cb), lambda c, b: (0, part * ncb + c))
    once = pl.Buffered(1)
    return pl.pallas_call(
        functools.partial(_hyena_kernel, L, layer),
        grid=(ncb, nb),
        in_specs=[
            pl.BlockSpec((rows, cb), col(COL_X0)),
            pl.BlockSpec((rows, cb), col(COL_X1)),
            pl.BlockSpec((rows, cb), col(COL_VV)),
            pl.BlockSpec((rows, cb), col(COL_GH)),
            cw(0), cw(1), cw(2), cbias(0), cbias(1), cbias(2),
            pl.BlockSpec((DEPTH, cb), lambda c, b: (0, c)),
            pl.BlockSpec((1, 3, L, cb), lambda c, b: (layer, 0, 0, c)),
            pl.BlockSpec((2 * L, L), lambda c, b: (0, 0), pipeline_mode=once),
            pl.BlockSpec((L, 2 * L), lambda c, b: (0, 0), pipeline_mode=once),
            *extra_specs,
        ],
        out_specs=pl.BlockSpec((rows, cb), lambda c, b: (blk0 + b, c)),
        out_shape=jax.ShapeDtypeStruct((M_ALL, D_HY), BF16),
        input_output_aliases=aliases,
        compiler_params=_params("arbitrary", "arbitrary"),
        name=f"hyena_{L}",
    )(proj, proj, proj, proj, conv_w, conv_w, conv_w, conv_b, conv_b, conv_b, hy_bias, coef, f_bf, ft_bf,
      *extra_args)


OUTPROJ_TM = 512
W_STAGE_ROWS = 512
W_STAGES = (D_ATT + D_HY) // W_STAGE_ROWS


def _outproj_kernel(layer, split_in, final, tm, tile0, gain_row, *refs):
    it = iter(refs)
    att_ref, hy_ref, w_hbm = next(it), next(it), next(it)
    x_refs = (next(it), next(it)) if split_in else (next(it),)
    gate_ref, g_ref = next(it), next(it)
    shift_ref, scale_ref = (None, None) if final else (next(it), next(it))
    out_refs = (next(it),) if final else (next(it), next(it))
    w_scr, stage, stage_sems = next(it), next(it), next(it)
    i = pl.program_id(0)

    def w_copy(k):
        rows = pl.ds(k * W_STAGE_ROWS, W_STAGE_ROWS)
        return pltpu.make_async_copy(w_hbm.at[layer, rows], stage.at[k % 2], stage_sems.at[k % 2])

    @pl.when(i == 0)
    def _():
        w_copy(0).start()
        for k in range(W_STAGES):
            if k + 1 < W_STAGES:
                w_copy(k + 1).start()
            w_copy(k).wait()
            w_scr[k * W_STAGE_ROWS:(k + 1) * W_STAGE_ROWS, :] = stage[k % 2].astype(BF16)

    out = (jnp.dot(att_ref[...], w_scr[0:D_ATT, :], preferred_element_type=F32)
           + jnp.dot(hy_ref[...], w_scr[D_ATT:D_ATT + D_HY, :], preferred_element_type=F32))
    tile = tile0 + i
    if split_in:
        x = jnp.where(tile < M_CTX // tm, x_refs[0][...], x_refs[1][...])
    else:
        x = x_refs[0][...]
    y = x + _mod_row(gate_ref, tile, tm) * out
    inv = lax.rsqrt(jnp.mean(y * y, axis=-1, keepdims=True) + EPS)
    g = g_ref[gain_row:gain_row + 1, :]
    if final:
        out_refs[0][...] = y * inv * g
    else:
        out_refs[0][...] = y
        gain = g * (1.0 + _mod_row(scale_ref, tile, tm))
        out_refs[1][...] = (y * inv * gain + _mod_row(shift_ref, tile, tm)).astype(BF16)


def _out_proj(att, hy, w_out, layer, x_parts, mod, gains, final, tm, tile0=0, n_tiles=None):
    n_tiles = M_ALL // tm if n_tiles is None else n_tiles
    split_in = len(x_parts) == 2
    assert not split_in or (tile0 == 0 and n_tiles == M_ALL // tm)
    row_tile = lambda width: pl.BlockSpec((tm, width), lambda i: (tile0 + i, 0))
    out_tile = pl.BlockSpec((tm, D_MODEL), lambda i: (i, 0))
    x_specs = list(_two_stream_specs(tm, D_MODEL)) if split_in else [row_tile(D_MODEL)]
    in_specs = [
        row_tile(D_ATT),
        row_tile(D_HY),
        pl.BlockSpec(memory_space=pl.ANY),
        *x_specs,
        _mod_spec(layer, MOD_GATE),
        pl.BlockSpec(gains.shape, lambda i: (0, 0)),
    ]
    args = [att, hy, w_out, *x_parts, mod, gains]
    if final:
        out_specs = [out_tile]
        out_shape = [jax.ShapeDtypeStruct((n_tiles * tm, D_MODEL), F32)]
    else:
        in_specs += [_mod_spec(layer + 1, MOD_SHIFT), _mod_spec(layer + 1, MOD_SCALE)]
        args += [mod, mod]
        out_specs = [out_tile, out_tile]
        out_shape = [jax.ShapeDtypeStruct((n_tiles * tm, D_MODEL), F32),
                     jax.ShapeDtypeStruct((n_tiles * tm, D_MODEL), BF16)]
    return pl.pallas_call(
        functools.partial(_outproj_kernel, layer, split_in, final, tm, tile0, 0 if final else layer + 1),
        grid=(n_tiles,),
        in_specs=in_specs,
        out_specs=out_specs,
        out_shape=out_shape,
        scratch_shapes=[pltpu.VMEM((D_ATT + D_HY, D_MODEL), BF16), pltpu.VMEM((2, W_STAGE_ROWS, D_MODEL), F32),
                        pltpu.SemaphoreType.DMA((2,))],
        compiler_params=_params("arbitrary"),
        name="out_proj_final" if final else "out_proj",
    )(*args)


def _rope_tables():
    t = np.arange(DEC_SEQ)
    row = (t // GRID_W).astype(np.float64)
    col = (t % GRID_W).astype(np.float64)
    pairs = HEAD_DIM // 4
    inv_freq = ROPE_THETA ** (-np.arange(pairs, dtype=np.float64) / pairs)
    ang = np.concatenate([row[:, None] * inv_freq, col[:, None] * inv_freq], axis=-1)
    cos = np.repeat(np.cos(ang), 2, axis=-1).astype(np.float32)
    sin = np.repeat(np.sin(ang), 2, axis=-1).astype(np.float32)
    even = (np.arange(HEAD_DIM) % 2 == 0)[None, :]
    sin_a = np.where(even, -sin, 0.0).astype(np.float32)
    sin_b = np.where(even, 0.0, sin).astype(np.float32)
    return jnp.asarray(cos), jnp.asarray(sin_a), jnp.asarray(sin_b)


def kernel(x_prompt, x_sample, cache_k, cache_v, c, c_ctx, norm_g, w_ada, b_ada, w_in, q_norm_g, k_norm_g,
           conv_w, conv_b, filt_w1, filt_b1, filt_w2, filt_b2, filt_w3, filt_freq, hy_bias, w_out, final_norm_g):
    ctx = x_prompt.reshape(BATCH * SEQ, D_MODEL)
    lat = x_sample.reshape(DEC_BATCH * DEC_SEQ, D_MODEL)
    cache_k4 = cache_k.reshape(DEC_BATCH, DEPTH, PAST_LEN, D_KV)
    cache_v4 = cache_v.reshape(DEC_BATCH, DEPTH, PAST_LEN, D_KV)

    mod = _modulation(c_ctx, c, w_ada, b_ada)

    rope_tabs = _rope_tables()
    dft = {}
    coefs = {}
    for L in (SEQ, DEC_SEQ):
        f_np = _dft_matrix(L)
        f_bf = jnp.asarray(f_np).astype(BF16)
        dft[L] = (f_bf, jnp.asarray(np.ascontiguousarray(f_np.T)).astype(BF16))
        coefs[L] = _filter_spectra(L, filt_w1, filt_b1, filt_w2, filt_b2, filt_freq, filt_w3, f_bf)

    final_g = final_norm_g.reshape(1, D_MODEL)

    x_parts = (ctx, lat)
    h = _norm_mod(ctx, lat, norm_g, mod, 0)
    kv_out = None
    for l in range(DEPTH):
        final = l == DEPTH - 1

        proj = _in_proj(h, w_in, l)

        att, new_k, new_v = _attention(proj, q_norm_g, k_norm_g, l, SEQ, CTX_TQ, BATCH, 0, nsb=4, kv_prev=kv_out,
                                       emit_kv=True)
        kv_out = (new_k, new_v)
        (att,) = _attention(proj, q_norm_g, k_norm_g, l, DEC_SEQ, LAT_TQ, DEC_BATCH, M_CTX, nsb=4,
                            cache=(cache_k4, cache_v4), rope_tabs=rope_tabs, att_prev=att)

        hy = _hyena(proj, SEQ, BATCH, 0, D_HY, l, conv_w, conv_b, hy_bias, coefs[SEQ], *dft[SEQ])
        hy = _hyena(proj, DEC_SEQ, DEC_BATCH, M_CTX, 512, l, conv_w, conv_b, hy_bias,
                    coefs[DEC_SEQ], *dft[DEC_SEQ], hy_prev=hy)

        if final:
            assert len(x_parts) == 1
            tm = OUTPROJ_TM
            nc = M_CTX // tm
            (y_ctx,) = _out_proj(att, hy, w_out, l, x_parts, mod, final_g, True, tm, 0, nc)
            (y_lat,) = _out_proj(att, hy, w_out, l, x_parts, mod, final_g, True, tm, nc, M_LAT // tm)
        else:
            y, h = _out_proj(att, hy, w_out, l, x_parts, mod, norm_g, False, OUTPROJ_TM)
            x_parts = (y,)

    y_prompt = y_ctx.reshape(BATCH, SEQ, D_MODEL)
    y_sample = y_lat.reshape(DEC_BATCH, DEC_SEQ, D_MODEL)
    return (y_prompt, y_sample, kv_out[0], kv_out[1])
```

```python
import functools
import math

import numpy as np
import jax
import jax.numpy as jnp
from jax import lax
from jax.experimental import pallas as pl
from jax.experimental.pallas import tpu as pltpu

D_MODEL = 2048
BATCH = 16
SEQ = 256
DEPTH = 2
DEC_BATCH = 2
DEC_SEQ = 1024
PAST_LEN = 512
GRID_W = 64
D_ATT = 1024
D_HY = 1024
HEAD_DIM = 128
N_HEADS = 8
N_KV_HEADS = 4
GROUP = 2
D_KV = 512
ROPE_THETA = 10000.0
POS_BANDS = 16
POS_EMB = 33
FILT_HID = 64
DECAY_TARGET = 1e-2
FAST_DECAY_PCT = 0.3
SLOW_DECAY_PCT = 1.5
DECAY_SHIFT = 0.05
EPS = 1e-6
D_IN = 7168

COL_Q, COL_K, COL_V, COL_GA, COL_X0, COL_X1, COL_VV, COL_GH = 0, 1024, 1536, 2048, 3072, 4096, 5120, 6144

F32 = jnp.float32
BF16 = jnp.bfloat16

VMEM_LIMIT_BYTES = 56 * 1024 * 1024
FEAT_PAD = 128


def _params(*sem):
    return pltpu.CompilerParams(dimension_semantics=sem, vmem_limit_bytes=VMEM_LIMIT_BYTES)


def _silu(x):
    half = 0.5 * x
    return half + half * jnp.tanh(half)


def _rms(x, g):
    return x * lax.rsqrt(jnp.mean(x * x, axis=-1, keepdims=True) + EPS) * g


MOD_ROWS = 8
MOD_TN = 2048


def _mod_kernel(cctx_ref, c_ref, w_ref, b_ref, o_ref, c_scr):
    c_scr[...] = jnp.zeros(c_scr.shape, F32)
    c_scr[0:1, :] = cctx_ref[...]
    c_scr[1:1 + DEC_BATCH, :] = c_ref[...]
    s = _silu(c_scr[...]).astype(BF16)
    layer = pl.program_id(0)
    o_ref[0] = jnp.dot(s, w_ref[0].astype(BF16), preferred_element_type=F32) + b_ref[pl.ds(layer, 1), :]


def _modulation(c_ctx, c, w_ada, b_ada):
    n = 3 * D_MODEL
    return pl.pallas_call(
        _mod_kernel,
        grid=(DEPTH, n // MOD_TN),
        in_specs=[
            pl.BlockSpec((1, D_MODEL), lambda l, j: (0, 0)),
            pl.BlockSpec((DEC_BATCH, D_MODEL), lambda l, j: (0, 0)),
            pl.BlockSpec((1, D_MODEL, MOD_TN), lambda l, j: (l, 0, j)),
            pl.BlockSpec((DEPTH, MOD_TN), lambda l, j: (0, j)),
        ],
        out_specs=pl.BlockSpec((1, MOD_ROWS, MOD_TN), lambda l, j: (l, 0, j)),
        out_shape=jax.ShapeDtypeStruct((DEPTH, MOD_ROWS, n), F32),
        scratch_shapes=[pltpu.VMEM((MOD_ROWS, D_MODEL), F32)],
        compiler_params=_params("arbitrary", "arbitrary"),
        name="adaln_mod",
    )(c_ctx.reshape(1, D_MODEL), c, w_ada, b_ada)


M_CTX = BATCH * SEQ
M_LAT = DEC_BATCH * DEC_SEQ
M_ALL = M_CTX + M_LAT
NORM_TM = 1024
INPROJ_TM = 3072
INPROJ_TN = 512


ROW_CHUNK = 32


def _modulated_norm_rows(x_ref, h_ref, g, shift, scale):
    gain = g * (1.0 + scale)

    def body(r, _):
        rows = pl.ds(pl.multiple_of(r * ROW_CHUNK, ROW_CHUNK), ROW_CHUNK)
        x = x_ref[rows, :]
        inv = lax.rsqrt(jnp.mean(x * x, axis=-1, keepdims=True) + EPS)
        h_ref[rows, :] = (x * inv * gain + shift).astype(BF16)
        return 0

    lax.fori_loop(0, x_ref.shape[0] // ROW_CHUNK, body, 0, unroll=4)


MOD_SHIFT, MOD_SCALE, MOD_GATE = 0, 1, 2


def _mod_spec(layer, part):
    return pl.BlockSpec((1, MOD_ROWS, D_MODEL), lambda i: (layer, 0, part))


def _mod_row(ref, tile, tm):
    nc = M_CTX // tm
    per_batch = DEC_SEQ // tm
    row = jnp.where(tile < nc, 0, 1 + (tile - nc) // per_batch)
    return ref[0, pl.ds(row, 1), :]


def _two_stream_specs(tm, width):
    nc = M_CTX // tm
    return (pl.BlockSpec((tm, width), lambda i: (jnp.minimum(i, nc - 1), 0)),
            pl.BlockSpec((tm, width), lambda i: (jnp.maximum(i - nc, 0), 0)))


def _norm_kernel(layer, xc_ref, xl_ref, g_ref, shift_ref, scale_ref, h_ref):
    i = pl.program_id(0)

    def emit(x_ref):
        _modulated_norm_rows(x_ref, h_ref, g_ref[layer:layer + 1, :], _mod_row(shift_ref, i, NORM_TM),
                             _mod_row(scale_ref, i, NORM_TM))

    pl.when(i < M_CTX // NORM_TM)(lambda: emit(xc_ref))
    pl.when(i >= M_CTX // NORM_TM)(lambda: emit(xl_ref))


def _norm_mod(x_ctx, x_lat, norm_g, mod, layer):
    xc_spec, xl_spec = _two_stream_specs(NORM_TM, D_MODEL)
    return pl.pallas_call(
        functools.partial(_norm_kernel, layer),
        grid=(M_ALL // NORM_TM,),
        in_specs=[
            xc_spec, xl_spec,
            pl.BlockSpec((DEPTH, D_MODEL), lambda i: (0, 0)),
            _mod_spec(layer, MOD_SHIFT),
            _mod_spec(layer, MOD_SCALE),
        ],
        out_specs=pl.BlockSpec((NORM_TM, D_MODEL), lambda i: (i, 0)),
        out_shape=jax.ShapeDtypeStruct((M_ALL, D_MODEL), BF16),
        compiler_params=_params("arbitrary"),
        name="norm_mod",
    )(x_ctx, x_lat, norm_g, mod, mod)


def _inproj_kernel(h_hbm, w_ref, o_ref, w_scr, h_scr, h_sems):
    j, i = pl.program_id(0), pl.program_id(1)
    n_chunks = M_ALL // INPROJ_TM

    def h_copy(c):
        rows = pl.ds(c * INPROJ_TM, INPROJ_TM)
        return pltpu.make_async_copy(h_hbm.at[rows], h_scr.at[rows], h_sems.at[c])

    @pl.when((j == 0) & (i == 0))
    def _():
        for c in range(n_chunks):
            h_copy(c).start()

    @pl.when(i == 0)
    def _():
        w_scr[...] = w_ref[0].astype(BF16)

    for c in range(n_chunks):
        @pl.when((j == 0) & (i == c))
        def _():
            h_copy(c).wait()

    rows = pl.ds(pl.multiple_of(i * INPROJ_TM, INPROJ_TM), INPROJ_TM)
    o_ref[...] = jnp.dot(h_scr[rows, :], w_scr[...], preferred_element_type=F32)


def _in_proj(h, w_in, layer):
    return pl.pallas_call(
        _inproj_kernel,
        grid=(D_IN // INPROJ_TN, M_ALL // INPROJ_TM),
        in_specs=[
            pl.BlockSpec(memory_space=pl.ANY),
            pl.BlockSpec((1, D_MODEL, INPROJ_TN), lambda j, i: (layer, 0, j)),
        ],
        out_specs=pl.BlockSpec((INPROJ_TM, INPROJ_TN), lambda j, i: (i, j)),
        out_shape=jax.ShapeDtypeStruct((M_ALL, D_IN), F32),
        scratch_shapes=[pltpu.VMEM((D_MODEL, INPROJ_TN), BF16), pltpu.VMEM((M_ALL, D_MODEL), BF16),
                        pltpu.SemaphoreType.DMA((M_ALL // INPROJ_TM,))],
        compiler_params=_params("arbitrary", "arbitrary"),
        name="in_proj",
    )(h, w_in)


ATT_SCALE = 1.0 / math.sqrt(HEAD_DIM)


def _head(ref_or_val, h):
    return ref_or_val[:, h * HEAD_DIM:(h + 1) * HEAD_DIM]


Q_PRESCALE = ATT_SCALE * math.log2(math.e)


def _softmax_pv(q2, k_bf, v_ext):
    s = lax.dot_general(q2, k_bf, (((1,), (1,)), ((), ())), preferred_element_type=F32)
    p = jnp.exp2(s - jnp.max(s, axis=-1, keepdims=True)).astype(BF16)
    o = jnp.dot(p, v_ext, preferred_element_type=F32)
    return o[:, 0:HEAD_DIM] / o[:, HEAD_DIM:2 * HEAD_DIM]


def _rope(x, cos2, sin_a, sin_b):
    nxt = pltpu.roll(x, HEAD_DIM - 1, axis=1)
    prv = pltpu.roll(x, 1, axis=1)
    return x * cos2 + nxt * sin_a + prv * sin_b


CTX_TQ = SEQ
LAT_TQ = 128


def _attn_kernel(rope, n_cache, layer, emit_kv, n_aliased, nsb, *refs):
    it = iter(refs)
    q_all, kv_all, g_all, qg_ref, kg_ref = (next(it) for _ in range(5))
    ck_ref, cv_ref = (next(it), next(it)) if n_cache else (None, None)
    q_tabs = tuple(next(it) for _ in range(3)) if rope else None
    k_tabs = tuple(next(it) for _ in range(3)) if rope else None
    for _ in range(n_aliased):
        next(it)
    att_all = next(it)
    ko_hbm, vo_hbm = (next(it), next(it)) if emit_kv else (None, None)
    k_all, v_all = next(it), next(it)
    kf_all, kv_sems = (next(it), next(it)) if emit_kv else (None, None)
    single_tile = q_all.shape[0] == kv_all.shape[0]
    n_new = kv_all.shape[0] // nsb if single_tile else kv_all.shape[0]
    assert not emit_kv or single_tile

    def part(ref, sb, rows_per_seq):
        return ref if nsb == 1 else ref.at[pl.ds(sb * rows_per_seq, rows_per_seq)]

    def kv_copy(sb, h, is_value):
        b = pl.program_id(0) * nsb + sb
        if is_value:
            src = part(kv_all, sb, n_new).at[:, pl.ds((N_KV_HEADS + h) * HEAD_DIM, HEAD_DIM)]
            return pltpu.make_async_copy(src, vo_hbm.at[b, layer, :, h, :],
                                         kv_sems.at[sb, N_KV_HEADS + h])
        src = part(kf_all, sb, n_new).at[:, pl.ds(h * HEAD_DIM, HEAD_DIM)]
        return pltpu.make_async_copy(src, ko_hbm.at[b, layer, :, h, :], kv_sems.at[sb, h])

    def prepare_keys_values(sb):
        kv_ref, k_scr, v_scr = part(kv_all, sb, n_new), part(k_all, sb, n_new + n_cache), part(v_all, sb, n_new + n_cache)
        kf_scr = part(kf_all, sb, n_new) if emit_kv else None
        for h in range(N_KV_HEADS):
            kn = _rms(_head(kv_ref, h), kg_ref[layer:layer + 1, :])
            vh = _head(kv_ref, N_KV_HEADS + h)
            if emit_kv:
                kf_scr[:, h * HEAD_DIM:(h + 1) * HEAD_DIM] = kn
            if rope:
                kn = _rope(kn, *(t[...] for t in k_tabs))
            k_scr[0:n_new, h * HEAD_DIM:(h + 1) * HEAD_DIM] = kn.astype(BF16)
            v0 = 2 * h * HEAD_DIM
            v_scr[0:n_new, v0:v0 + HEAD_DIM] = vh.astype(BF16)
            if n_cache:
                v_scr[n_new:n_new + n_cache, v0:v0 + HEAD_DIM] = _head(cv_ref[0, 0], h).astype(BF16)
            v_scr[:, v0 + HEAD_DIM:v0 + 2 * HEAD_DIM] = jnp.ones((n_new + n_cache, HEAD_DIM), BF16)
        if n_cache:
            k_scr[n_new:n_new + n_cache, :] = ck_ref[0, 0].astype(BF16)

    def attend(sb):
        rows = q_all.shape[0] // nsb
        q_ref, g_ref, att_ref = part(q_all, sb, rows), part(g_all, sb, rows), part(att_all, sb, rows)
        kv_sb = sb if single_tile else 0
        k_scr, v_scr = part(k_all, kv_sb, n_new + n_cache), part(v_all, kv_sb, n_new + n_cache)

        def query(hq):
            x = _rms(_head(q_ref, hq), qg_ref[layer:layer + 1, :])
            if rope:
                x = _rope(x, *(part(t, sb, rows)[...] for t in q_tabs))
            return x * Q_PRESCALE

        for h in range(N_KV_HEADS):
            q2 = jnp.concatenate([query(GROUP * h + g) for g in range(GROUP)], axis=0).astype(BF16)
            o = _softmax_pv(q2, _head(k_scr, h), v_scr[:, 2 * h * HEAD_DIM:2 * (h + 1) * HEAD_DIM])
            for g in range(GROUP):
                hq = GROUP * h + g
                gate = _silu(_head(g_ref, hq))
                att_ref[:, hq * HEAD_DIM:(hq + 1) * HEAD_DIM] = (o[g * rows:(g + 1) * rows] * gate).astype(BF16)

    every = [(sb, h) for sb in range(nsb) for h in range(N_KV_HEADS)]
    if single_tile:
        if emit_kv:
            for sb, h in every:
                kv_copy(sb, h, True).start()
        for sb in range(nsb):
            prepare_keys_values(sb)
        if emit_kv:
            for sb, h in every:
                kv_copy(sb, h, False).start()
    else:
        pl.when(pl.program_id(1) == 0)(lambda: prepare_keys_values(0))

    for sb in range(nsb):
        attend(sb)

    if emit_kv:
        for sb, h in every:
            kv_copy(sb, h, False).wait()
            kv_copy(sb, h, True).wait()


def _attention(proj, q_g, k_g, layer, seq, tq, n_batch, row0, *, nsb=1, cache=None, rope_tabs=None,
               att_prev=None, kv_prev=None, emit_kv=False):
    whole = tq == seq
    assert not whole or cache is None
    tq = nsb * tq
    kv_rows = nsb * seq if whole else seq
    nq = kv_rows // tq
    q0, kv0 = row0 // tq, row0 // kv_rows
    n_cache = 0 if cache is None else cache[0].shape[2]
    vec = pl.BlockSpec((DEPTH, HEAD_DIM), lambda b, i: (0, 0))
    in_specs = [
        pl.BlockSpec((tq, D_ATT), lambda b, i: (q0 + b * nq + i, COL_Q // D_ATT)),
        pl.BlockSpec((kv_rows, 2 * D_KV), lambda b, i: (kv0 + b, COL_K // (2 * D_KV))),
        pl.BlockSpec((tq, D_ATT), lambda b, i: (q0 + b * nq + i, COL_GA // D_ATT)),
        vec, vec,
    ]
    args = [proj, proj, proj, q_g, k_g]
    if cache is not None:
        in_specs += [pl.BlockSpec((1, 1, n_cache, D_KV), lambda b, i: (b, layer, 0, 0))] * 2
        args += list(cache)
    if rope_tabs is not None:
        in_specs += [pl.BlockSpec((tq, HEAD_DIM), lambda b, i: (i, 0))] * 3
        in_specs += [pl.BlockSpec((seq, HEAD_DIM), lambda b, i: (0, 0))] * 3
        args += list(rope_tabs) * 2
    aliased = ([] if att_prev is None else [att_prev]) + ([] if kv_prev is None else list(kv_prev))
    out_first = 0 if att_prev is not None else 1
    aliases = {len(args) + n: out_first + n for n in range(len(aliased))}
    in_specs += [pl.BlockSpec(memory_space=pl.ANY)] * len(aliased)
    args += aliased
    out_specs = [pl.BlockSpec((tq, D_ATT), lambda b, i: (q0 + b * nq + i, 0))]
    out_shape = [jax.ShapeDtypeStruct((M_ALL, D_ATT), BF16)]
    keys = kv_rows + n_cache
    scratch =[pltpu.VMEM((keys, D_KV), BF16), pltpu.VMEM((keys, 2 * D_KV), BF16)]
    if emit_kv:
        out_specs += [pl.BlockSpec(memory_space=pl.ANY)] * 2
        out_shape += [jax.ShapeDtypeStruct((n_batch, DEPTH, seq, N_KV_HEADS, HEAD_DIM), F32)] * 2
        scratch += [pltpu.VMEM((kv_rows, D_KV), F32), pltpu.SemaphoreType.DMA((nsb, 2 * N_KV_HEADS))]
    return pl.pallas_call(
        functools.partial(_attn_kernel, rope_tabs is not None, n_cache, layer, emit_kv, len(aliased), nsb),
        grid=(n_batch * seq // kv_rows, nq),
        in_specs=in_specs,
        out_specs=out_specs,
        out_shape=out_shape,
        input_output_aliases=aliases,
        scratch_shapes=scratch,
        compiler_params=_params("arbitrary", "arbitrary"),
        name=f"attention_{seq}",
    )(*args)


def _dft_matrix(L):
    k = np.arange(L, dtype=np.int64)[:, None]
    t = np.arange(L, dtype=np.int64)[None, :]
    ang = 2.0 * np.pi * ((k * t) % (2 * L)).astype(np.float64) / (2 * L)
    top = np.cos(ang)
    bot = np.sin(ang)
    bot[0, :] = np.where(np.arange(L) % 2 == 0, 1.0, -1.0)
    return np.concatenate([top, bot], axis=0).astype(np.float32)


def _filter_kernel(L, z_ref, w1_ref, b1_ref, w2_ref, b2_ref, fr_ref, w3f_ref, w3b_ref, dec_ref,
                   f_ref, o_ref, hdn_scr, w1_scr, w2_scr, vec_scr, w3_scr):
    hp = lax.Precision.HIGHEST
    layer = pl.program_id(0)

    halves = ((0, 0), (FEAT_PAD, FILT_HID))

    @pl.when(pl.program_id(1) == 0)
    def _():
        w1_scr[...] = jnp.zeros(w1_scr.shape, F32)
        w2_scr[...] = jnp.zeros(w2_scr.shape, F32)
        vec_scr[...] = jnp.zeros(vec_scr.shape, F32)
        w3_scr[...] = jnp.zeros(w3_scr.shape, BF16)
        for feat0, hid0 in halves:
            w1_scr[feat0:feat0 + POS_EMB, hid0:hid0 + FILT_HID] = w1_ref[0]
            w2_scr[hid0:hid0 + FILT_HID, hid0:hid0 + FILT_HID] = w2_ref[0]
            for r, ref in enumerate((b1_ref, b2_ref, fr_ref)):
                vec_scr[r:r + 1, hid0:hid0 + FILT_HID] = ref[pl.ds(layer, 1), :]
        b1, b2, fr = vec_scr[0:1, :], vec_scr[1:2, :], vec_scr[2:3, :]
        h1 = jnp.sin(fr * (jnp.dot(z_ref[...], w1_scr[...], precision=hp, preferred_element_type=F32) + b1))
        hdn_scr[...] = jnp.sin(fr * (jnp.dot(h1, w2_scr[...], precision=hp, preferred_element_type=F32) + b2))

    hdn_bf = hdn_scr[...].astype(BF16)
    dec = dec_ref[...]

    def project(w3_ref, slot):
        parts = []
        for k, (_, hid0) in enumerate(halves):
            w3_scr[slot + k, hid0:hid0 + FILT_HID, :] = w3_ref[0].astype(BF16)
            parts.append(jnp.dot(hdn_bf, w3_scr[slot + k], preferred_element_type=F32))
        return jnp.concatenate(parts, axis=0) * dec

    h_f = project(w3f_ref, 0)
    h_b = project(w3b_ref, 2)
    hs = h_f + h_b
    ha = jnp.dot(f_ref[0:L, :], hs.astype(BF16), preferred_element_type=F32)
    hb = jnp.dot(f_ref[L:2 * L, :], (h_f - h_b).astype(BF16), preferred_element_type=F32)
    row = lax.broadcasted_iota(jnp.int32, (L, 1), 0)
    first = row == 0
    nyquist = jnp.sum(jnp.where(row % 2 == 0, hs, -hs), axis=0, keepdims=True)
    wk = jnp.where(first, 1.0 / (2 * L), 2.0 / (2 * L))
    o_ref[0, 0] = ha * wk
    o_ref[0, 1] = jnp.where(first, 0.0, hb * wk)
    o_ref[0, 2] = jnp.where(first, nyquist, ha) * wk


FILT_CB = 512


def _filter_spectra(L, w1, b1, w2, b2, freq, w3, f_bf):
    tpos = np.arange(L, dtype=np.float64)
    t_norm = tpos / max(L - 1, 1)
    w = 2.0 * math.pi * tpos / L
    bands = np.linspace(1e-4, POS_BANDS - 1, POS_BANDS)
    z = np.concatenate([t_norm[:, None], np.cos(w[:, None] * bands), -np.sin(w[:, None] * bands)], axis=-1)
    z = np.pad(z, ((0, 0), (0, FEAT_PAD - POS_EMB))).astype(np.float32)
    z = np.concatenate([z[:L // 2], z[L // 2:]], axis=1)
    max_decay = math.log(DECAY_TARGET) / FAST_DECAY_PCT
    min_decay = math.log(DECAY_TARGET) / SLOW_DECAY_PCT
    deltas = np.abs(np.linspace(min_decay, max_decay, D_HY))
    dec = (np.exp(-t_norm[:, None] * deltas) + DECAY_SHIFT).astype(np.float32)

    ncb = D_HY // FILT_CB
    matrix = lambda rows, cols: pl.BlockSpec((1, rows, cols), lambda l, c: (l, 0, 0))
    per_layer_vec = pl.BlockSpec((DEPTH, FILT_HID), lambda l, c: (0, 0))
    return pl.pallas_call(
        functools.partial(_filter_kernel, L),
        grid=(DEPTH, ncb),
        in_specs=[
            pl.BlockSpec((L // 2, 2 * FEAT_PAD), lambda l, c: (0, 0)),
            matrix(POS_EMB, FILT_HID), per_layer_vec,
            matrix(FILT_HID, FILT_HID), per_layer_vec, per_layer_vec,
            pl.BlockSpec((1, FILT_HID, FILT_CB), lambda l, c: (l, 0, c)),
            pl.BlockSpec((1, FILT_HID, FILT_CB), lambda l, c: (l, 0, ncb + c)),
            pl.BlockSpec((L, FILT_CB), lambda l, c: (0, c)),
            pl.BlockSpec((2 * L, L), lambda l, c: (0, 0)),
        ],
        out_specs=pl.BlockSpec((1, 3, L, FILT_CB), lambda l, c: (l, 0, 0, c)),
        out_shape=jax.ShapeDtypeStruct((DEPTH, 3, L, D_HY), F32),
        scratch_shapes=[pltpu.VMEM((L // 2, FEAT_PAD), F32), pltpu.VMEM((2 * FEAT_PAD, FEAT_PAD), F32),
                        pltpu.VMEM((FEAT_PAD, FEAT_PAD), F32), pltpu.VMEM((8, FEAT_PAD), F32),
                        pltpu.VMEM((4, FEAT_PAD, FILT_CB), BF16)],
        compiler_params=_params("arbitrary", "arbitrary"),
        name=f"hyena_filter_{L}",
    )(jnp.asarray(z), w1, b1, w2, b2, freq, w3, w3, jnp.asarray(dec), f_bf)


def _hyena_kernel(L, layer, x0_all, x1_all, vv_all, g_all, cw0_ref, cw1_ref, cwv_ref, cb0_ref, cb1_ref,
                  cbv_ref, bias_ref, coef_ref, f_ref, ft_ref, *rest):
    o_all = rest[-1]
    this_layer = slice(layer, layer + 1)
    row = lax.broadcasted_iota(jnp.int32, (L, 1), 0)
    is_first = row == 0
    is_last = row == L - 1

    def sconv(x_ref, w_ref, b_ref):
        x = x_ref[...]
        prev = jnp.where(is_first, 0.0, pltpu.roll(x, 1, axis=0))
        nxt = jnp.where(is_last, 0.0, pltpu.roll(x, L - 1, axis=0))
        return w_ref[0, 0:1, :] * prev + w_ref[0, 1:2, :] * x + w_ref[0, 2:3, :] * nxt + b_ref[this_layer, :]

    for s in range(o_all.shape[0] // L):
        x0_ref, x1_ref, vv_ref, g_ref, o_ref = (r.at[pl.ds(s * L, L)] for r in (x0_all, x1_all, vv_all, g_all, o_all))
        z = sconv(vv_ref, cwv_ref, cbv_ref) * sconv(x1_ref, cw1_ref, cb1_ref)
        ab = jnp.dot(f_ref[...], z.astype(BF16), preferred_element_type=F32)
        a, b = ab[0:L], ab[L:2 * L]
        g1, g2, g3 = coef_ref[0, 0], coef_ref[0, 1], coef_ref[0, 2]
        pq = jnp.concatenate([a * g1 - b * g2, a * g2 + b * g3], axis=0).astype(BF16)
        y = jnp.dot(ft_ref[...], pq, preferred_element_type=F32) + bias_ref[this_layer, :] * z
        o_ref[...] = (sconv(x0_ref, cw0_ref, cb0_ref) * y * _silu(g_ref[...])).astype(BF16)


def _hyena(proj, L, nb, nseq, row0, cb, layer, conv_w, conv_b, hy_bias, coef, f_bf, ft_bf, hy_prev=None):
    ncb = D_HY // cb
    rows = nseq * L
    blk0 = row0 // rows
    extra_specs = [] if hy_prev is None else [pl.BlockSpec(memory_space=pl.ANY)]
    extra_args = [] if hy_prev is None else [hy_prev]
    aliases = {} if hy_prev is None else {14: 0}
    col = lambda off: (lambda c, b: (blk0 + b, off // cb + c))
    cw = lambda part: pl.BlockSpec((1, 3, cb), lambda c, b: (layer, 0, part * ncb + c))
    cbias = lambda part: pl.BlockSpec((DEPTH, cb), lambda c, b: (0, part * ncb + c))
    once = pl.Buffered(1)
    return pl.pallas_call(
        functools.partial(_hyena_kernel, L, layer),
        grid=(ncb, nb // nseq),
        in_specs=[
            pl.BlockSpec((rows, cb), col(COL_X0)),
            pl.BlockSpec((rows, cb), col(COL_X1)),
            pl.BlockSpec((rows, cb), col(COL_VV)),
            pl.BlockSpec((rows, cb), col(COL_GH)),
            cw(0), cw(1), cw(2), cbias(0), cbias(1), cbias(2),
            pl.BlockSpec((DEPTH, cb), lambda c, b: (0, c)),
            pl.BlockSpec((1, 3, L, cb), lambda c, b: (layer, 0, 0, c)),
            pl.BlockSpec((2 * L, L), lambda c, b: (0, 0), pipeline_mode=once),
            pl.BlockSpec((L, 2 * L), lambda c, b: (0, 0), pipeline_mode=once),
            *extra_specs,
        ],
        out_specs=pl.BlockSpec((rows, cb), lambda c, b: (blk0 + b, c)),
        out_shape=jax.ShapeDtypeStruct((M_ALL, D_HY), BF16),
        input_output_aliases=aliases,
        compiler_params=_params("arbitrary", "arbitrary"),
        name=f"hyena_{L}",
    )(proj, proj, proj, proj, conv_w, conv_w, conv_w, conv_b, conv_b, conv_b, hy_bias, coef, f_bf, ft_bf,
      *extra_args)


OUTPROJ_TM = 256
OUTPROJ_TM_FINAL = 512


def _outproj_kernel(split_in, final, tm, tile0, gain_row, *refs):
    it = iter(refs)
    att_ref, hy_ref, w_ref = next(it), next(it), next(it)
    x_refs = (next(it), next(it)) if split_in else (next(it),)
    gate_ref, g_ref = next(it), next(it)
    shift_ref, scale_ref = (None, None) if final else (next(it), next(it))
    out_refs = (next(it),) if final else (next(it), next(it))
    w_scr = next(it)
    i = pl.program_id(0)

    @pl.when(i == 0)
    def _():
        w_scr[...] = w_ref[0].astype(BF16)

    out = (jnp.dot(att_ref[...], w_scr[0:D_ATT, :], preferred_element_type=F32)
           + jnp.dot(hy_ref[...], w_scr[D_ATT:D_ATT + D_HY, :], preferred_element_type=F32))
    tile = tile0 + i
    if split_in:
        x = jnp.where(tile < M_CTX // tm, x_refs[0][...], x_refs[1][...])
    else:
        x = x_refs[0][...]
    y = x + _mod_row(gate_ref, tile, tm) * out
    inv = lax.rsqrt(jnp.mean(y * y, axis=-1, keepdims=True) + EPS)
    g = g_ref[gain_row:gain_row + 1, :]
    if final:
        out_refs[0][...] = y * inv * g
    else:
        out_refs[0][...] = y
        gain = g * (1.0 + _mod_row(scale_ref, tile, tm))
        out_refs[1][...] = (y * inv * gain + _mod_row(shift_ref, tile, tm)).astype(BF16)


def _out_proj(att, hy, w_out, layer, x_parts, mod, gains, final, tm, tile0=0, n_tiles=None):
    n_tiles = M_ALL // tm if n_tiles is None else n_tiles
    split_in = len(x_parts) == 2
    assert not split_in or (tile0 == 0 and n_tiles == M_ALL // tm)
    row_tile = lambda width: pl.BlockSpec((tm, width), lambda i: (tile0 + i, 0))
    out_tile = pl.BlockSpec((tm, D_MODEL), lambda i: (i, 0))
    x_specs = list(_two_stream_specs(tm, D_MODEL)) if split_in else [row_tile(D_MODEL)]
    in_specs = [
        row_tile(D_ATT),
        row_tile(D_HY),
        pl.BlockSpec((1, D_ATT + D_HY, D_MODEL), lambda i: (layer, 0, 0), pipeline_mode=pl.Buffered(1)),
        *x_specs,
        _mod_spec(layer, MOD_GATE),
        pl.BlockSpec(gains.shape, lambda i: (0, 0)),
    ]
    args = [att, hy, w_out, *x_parts, mod, gains]
    if final:
        out_specs = [out_tile]
        out_shape = [jax.ShapeDtypeStruct((n_tiles * tm, D_MODEL), F32)]
    else:
        in_specs += [_mod_spec(layer + 1, MOD_SHIFT), _mod_spec(layer + 1, MOD_SCALE)]
        args += [mod, mod]
        out_specs = [out_tile, out_tile]
        out_shape = [jax.ShapeDtypeStruct((n_tiles * tm, D_MODEL), F32),
                     jax.ShapeDtypeStruct((n_tiles * tm, D_MODEL), BF16)]
    return pl.pallas_call(
        functools.partial(_outproj_kernel, split_in, final, tm, tile0, 0 if final else layer + 1),
        grid=(n_tiles,),
        in_specs=in_specs,
        out_specs=out_specs,
        out_shape=out_shape,
        scratch_shapes=[pltpu.VMEM((D_ATT + D_HY, D_MODEL), BF16)],
        compiler_params=_params("arbitrary"),
        name="out_proj_final" if final else "out_proj",
    )(*args)


def _rope_tables():
    t = np.arange(DEC_SEQ)
    row = (t // GRID_W).astype(np.float64)
    col = (t % GRID_W).astype(np.float64)
    pairs = HEAD_DIM // 4
    inv_freq = ROPE_THETA ** (-np.arange(pairs, dtype=np.float64) / pairs)
    ang = np.concatenate([row[:, None] * inv_freq, col[:, None] * inv_freq], axis=-1)
    cos = np.repeat(np.cos(ang), 2, axis=-1).astype(np.float32)
    sin = np.repeat(np.sin(ang), 2, axis=-1).astype(np.float32)
    even = (np.arange(HEAD_DIM) % 2 == 0)[None, :]
    sin_a = np.where(even, -sin, 0.0).astype(np.float32)
    sin_b = np.where(even, 0.0, sin).astype(np.float32)
    return jnp.asarray(cos), jnp.asarray(sin_a), jnp.asarray(sin_b)


def kernel(x_prompt, x_sample, cache_k, cache_v, c, c_ctx, norm_g, w_ada, b_ada, w_in, q_norm_g, k_norm_g,
           conv_w, conv_b, filt_w1, filt_b1, filt_w2, filt_b2, filt_w3, filt_freq, hy_bias, w_out, final_norm_g):
    ctx = x_prompt.reshape(BATCH * SEQ, D_MODEL)
    lat = x_sample.reshape(DEC_BATCH * DEC_SEQ, D_MODEL)
    cache_k4 = cache_k.reshape(DEC_BATCH, DEPTH, PAST_LEN, D_KV)
    cache_v4 = cache_v.reshape(DEC_BATCH, DEPTH, PAST_LEN, D_KV)

    mod = _modulation(c_ctx, c, w_ada, b_ada)

    rope_tabs = _rope_tables()
    dft = {}
    coefs = {}
    for L in (SEQ, DEC_SEQ):
        f_np = _dft_matrix(L)
        f_bf = jnp.asarray(f_np).astype(BF16)
        dft[L] = (f_bf, jnp.asarray(np.ascontiguousarray(f_np.T)).astype(BF16))
        coefs[L] = _filter_spectra(L, filt_w1, filt_b1, filt_w2, filt_b2, filt_freq, filt_w3, f_bf)

    final_g = final_norm_g.reshape(1, D_MODEL)

    x_parts = (ctx, lat)
    h = _norm_mod(ctx, lat, norm_g, mod, 0)
    kv_out = None
    for l in range(DEPTH):
        final = l == DEPTH - 1

        proj = _in_proj(h, w_in, l)

        att, new_k, new_v = _attention(proj, q_norm_g, k_norm_g, l, SEQ, CTX_TQ, BATCH, 0, nsb=4, kv_prev=kv_out,
                                       emit_kv=True)
        kv_out = (new_k, new_v)
        (att,) = _attention(proj, q_norm_g, k_norm_g, l, DEC_SEQ, LAT_TQ, DEC_BATCH, M_CTX, nsb=4,
                            cache=(cache_k4, cache_v4), rope_tabs=rope_tabs, att_prev=att)

        hy = _hyena(proj, SEQ, BATCH, 2, 0, D_HY, l, conv_w, conv_b, hy_bias, coefs[SEQ], *dft[SEQ])
        hy = _hyena(proj, DEC_SEQ, DEC_BATCH, 1, M_CTX, 512, l, conv_w, conv_b, hy_bias,
                    coefs[DEC_SEQ], *dft[DEC_SEQ], hy_prev=hy)

        if final:
            assert len(x_parts) == 1
            tm = OUTPROJ_TM_FINAL
            nc = M_CTX // tm
            (y_ctx,) = _out_proj(att, hy, w_out, l, x_parts, mod, final_g, True, tm, 0, nc)
            (y_lat,) = _out_proj(att, hy, w_out, l, x_parts, mod, final_g, True, tm, nc, M_LAT // tm)
        else:
            y, h = _out_proj(att, hy, w_out, l, x_parts, mod, norm_g, False, OUTPROJ_TM)
            x_parts = (y,)

    y_prompt = y_ctx.reshape(BATCH, SEQ, D_MODEL)
    y_sample = y_lat.reshape(DEC_BATCH, DEC_SEQ, D_MODEL)
    return (y_prompt, y_sample, kv_out[0], kv_out[1])
```

```python
import functools
import math

import numpy as np
import jax
import jax.numpy as jnp
from jax import lax
from jax.experimental import pallas as pl
from jax.experimental.pallas import tpu as pltpu

D_MODEL = 2048
BATCH = 16
SEQ = 256
DEPTH = 2
DEC_BATCH = 2
DEC_SEQ = 1024
PAST_LEN = 512
GRID_W = 64
D_ATT = 1024
D_HY = 1024
HEAD_DIM = 128
N_HEADS = 8
N_KV_HEADS = 4
GROUP = 2
D_KV = 512
ROPE_THETA = 10000.0
POS_BANDS = 16
POS_EMB = 33
FILT_HID = 64
DECAY_TARGET = 1e-2
FAST_DECAY_PCT = 0.3
SLOW_DECAY_PCT = 1.5
DECAY_SHIFT = 0.05
EPS = 1e-6
D_IN = 7168

COL_Q, COL_K, COL_V, COL_GA, COL_X0, COL_X1, COL_VV, COL_GH = 0, 1024, 1536, 2048, 3072, 4096, 5120, 6144

F32 = jnp.float32
BF16 = jnp.bfloat16

VMEM_LIMIT_BYTES = 56 * 1024 * 1024
FEAT_PAD = 128


def _params(*sem):
    return pltpu.CompilerParams(dimension_semantics=sem, vmem_limit_bytes=VMEM_LIMIT_BYTES)


def _silu(x):
    half = 0.5 * x
    return half + half * jnp.tanh(half)


def _rms(x, g):
    return x * lax.rsqrt(jnp.mean(x * x, axis=-1, keepdims=True) + EPS) * g


MOD_ROWS = 8
MOD_TN = 1024


def _mod_kernel(cctx_ref, c_ref, w_ref, b_ref, o_ref, c_scr):
    c_scr[...] = jnp.zeros(c_scr.shape, F32)
    c_scr[0:1, :] = cctx_ref[...]
    c_scr[1:1 + DEC_BATCH, :] = c_ref[...]
    s = _silu(c_scr[...]).astype(BF16)
    layer = pl.program_id(0)
    o_ref[0] = jnp.dot(s, w_ref[0].astype(BF16), preferred_element_type=F32) + b_ref[pl.ds(layer, 1), :]


def _modulation(c_ctx, c, w_ada, b_ada):
    n = 3 * D_MODEL
    return pl.pallas_call(
        _mod_kernel,
        grid=(DEPTH, n // MOD_TN),
        in_specs=[
            pl.BlockSpec((1, D_MODEL), lambda l, j: (0, 0)),
            pl.BlockSpec((DEC_BATCH, D_MODEL), lambda l, j: (0, 0)),
            pl.BlockSpec((1, D_MODEL, MOD_TN), lambda l, j: (l, 0, j)),
            pl.BlockSpec((DEPTH, MOD_TN), lambda l, j: (0, j)),
        ],
        out_specs=pl.BlockSpec((1, MOD_ROWS, MOD_TN), lambda l, j: (l, 0, j)),
        out_shape=jax.ShapeDtypeStruct((DEPTH, MOD_ROWS, n), F32),
        scratch_shapes=[pltpu.VMEM((MOD_ROWS, D_MODEL), F32)],
        compiler_params=_params("arbitrary", "arbitrary"),
        name="adaln_mod",
    )(c_ctx.reshape(1, D_MODEL), c, w_ada, b_ada)


M_CTX = BATCH * SEQ
M_LAT = DEC_BATCH * DEC_SEQ
M_ALL = M_CTX + M_LAT
NORM_TM = 1024
INPROJ_TM = 3072
INPROJ_TN = 512


ROW_CHUNK = 32


def _modulated_norm_rows(x_ref, h_ref, g, shift, scale):
    gain = g * (1.0 + scale)

    def body(r, _):
        rows = pl.ds(pl.multiple_of(r * ROW_CHUNK, ROW_CHUNK), ROW_CHUNK)
        x = x_ref[rows, :]
        inv = lax.rsqrt(jnp.mean(x * x, axis=-1, keepdims=True) + EPS)
        h_ref[rows, :] = (x * inv * gain + shift).astype(BF16)
        return 0

    lax.fori_loop(0, x_ref.shape[0] // ROW_CHUNK, body, 0, unroll=4)


MOD_SHIFT, MOD_SCALE, MOD_GATE = 0, 1, 2


def _mod_spec(layer, part):
    return pl.BlockSpec((1, MOD_ROWS, D_MODEL), lambda i: (layer, 0, part))


def _mod_row(ref, tile, tm):
    nc = M_CTX // tm
    per_batch = DEC_SEQ // tm
    row = jnp.where(tile < nc, 0, 1 + (tile - nc) // per_batch)
    return ref[0, pl.ds(row, 1), :]


def _two_stream_specs(tm, width):
    nc = M_CTX // tm
    return (pl.BlockSpec((tm, width), lambda i: (jnp.minimum(i, nc - 1), 0)),
            pl.BlockSpec((tm, width), lambda i: (jnp.maximum(i - nc, 0), 0)))


def _norm_kernel(layer, xc_ref, xl_ref, g_ref, shift_ref, scale_ref, h_ref):
    i = pl.program_id(0)

    def emit(x_ref):
        _modulated_norm_rows(x_ref, h_ref, g_ref[layer:layer + 1, :], _mod_row(shift_ref, i, NORM_TM),
                             _mod_row(scale_ref, i, NORM_TM))

    pl.when(i < M_CTX // NORM_TM)(lambda: emit(xc_ref))
    pl.when(i >= M_CTX // NORM_TM)(lambda: emit(xl_ref))


def _norm_mod(x_ctx, x_lat, norm_g, mod, layer):
    xc_spec, xl_spec = _two_stream_specs(NORM_TM, D_MODEL)
    return pl.pallas_call(
        functools.partial(_norm_kernel, layer),
        grid=(M_ALL // NORM_TM,),
        in_specs=[
            xc_spec, xl_spec,
            pl.BlockSpec((DEPTH, D_MODEL), lambda i: (0, 0)),
            _mod_spec(layer, MOD_SHIFT),
            _mod_spec(layer, MOD_SCALE),
        ],
        out_specs=pl.BlockSpec((NORM_TM, D_MODEL), lambda i: (i, 0)),
        out_shape=jax.ShapeDtypeStruct((M_ALL, D_MODEL), BF16),
        compiler_params=_params("arbitrary"),
        name="norm_mod",
    )(x_ctx, x_lat, norm_g, mod, mod)


def _inproj_kernel(h_hbm, w_ref, o_ref, w_scr, h_scr, h_sems):
    j, i = pl.program_id(0), pl.program_id(1)
    n_chunks = M_ALL // INPROJ_TM

    def h_copy(c):
        rows = pl.ds(c * INPROJ_TM, INPROJ_TM)
        return pltpu.make_async_copy(h_hbm.at[rows], h_scr.at[rows], h_sems.at[c])

    @pl.when((j == 0) & (i == 0))
    def _():
        for c in range(n_chunks):
            h_copy(c).start()

    @pl.when(i == 0)
    def _():
        w_scr[...] = w_ref[0].astype(BF16)

    for c in range(n_chunks):
        @pl.when((j == 0) & (i == c))
        def _():
            h_copy(c).wait()

    rows = pl.ds(pl.multiple_of(i * INPROJ_TM, INPROJ_TM), INPROJ_TM)
    o_ref[...] = jnp.dot(h_scr[rows, :], w_scr[...], preferred_element_type=F32)


def _in_proj(h, w_in, layer):
    return pl.pallas_call(
        _inproj_kernel,
        grid=(D_IN // INPROJ_TN, M_ALL // INPROJ_TM),
        in_specs=[
            pl.BlockSpec(memory_space=pl.ANY),
            pl.BlockSpec((1, D_MODEL, INPROJ_TN), lambda j, i: (layer, 0, j)),
        ],
        out_specs=pl.BlockSpec((INPROJ_TM, INPROJ_TN), lambda j, i: (i, j)),
        out_shape=jax.ShapeDtypeStruct((M_ALL, D_IN), F32),
        scratch_shapes=[pltpu.VMEM((D_MODEL, INPROJ_TN), BF16), pltpu.VMEM((M_ALL, D_MODEL), BF16),
                        pltpu.SemaphoreType.DMA((M_ALL // INPROJ_TM,))],
        compiler_params=_params("arbitrary", "arbitrary"),
        name="in_proj",
    )(h, w_in)


ATT_SCALE = 1.0 / math.sqrt(HEAD_DIM)


def _head(ref_or_val, h):
    return ref_or_val[:, h * HEAD_DIM:(h + 1) * HEAD_DIM]


Q_PRESCALE = ATT_SCALE * math.log2(math.e)


def _softmax_pv(q2, k_bf, v_ext):
    s = lax.dot_general(q2, k_bf, (((1,), (1,)), ((), ())), preferred_element_type=F32)
    p = jnp.exp2(s - jnp.max(s, axis=-1, keepdims=True)).astype(BF16)
    o = jnp.dot(p, v_ext, preferred_element_type=F32)
    return o[:, 0:HEAD_DIM] / o[:, HEAD_DIM:2 * HEAD_DIM]


def _rope(x, cos2, sin_a, sin_b):
    nxt = pltpu.roll(x, HEAD_DIM - 1, axis=1)
    prv = pltpu.roll(x, 1, axis=1)
    return x * cos2 + nxt * sin_a + prv * sin_b


CTX_TQ = SEQ
LAT_TQ = 128


def _attn_kernel(rope, n_cache, layer, emit_kv, n_aliased, nsb, *refs):
    it = iter(refs)
    q_all, kv_all, g_all, qg_ref, kg_ref = (next(it) for _ in range(5))
    ck_ref, cv_ref = (next(it), next(it)) if n_cache else (None, None)
    q_tabs = tuple(next(it) for _ in range(3)) if rope else None
    k_tabs = tuple(next(it) for _ in range(3)) if rope else None
    for _ in range(n_aliased):
        next(it)
    att_all = next(it)
    ko_hbm, vo_hbm = (next(it), next(it)) if emit_kv else (None, None)
    k_all, v_all = next(it), next(it)
    kf_all, kv_sems = (next(it), next(it)) if emit_kv else (None, None)
    single_tile = q_all.shape[0] == kv_all.shape[0]
    n_new = kv_all.shape[0] // nsb if single_tile else kv_all.shape[0]
    assert not emit_kv or single_tile

    def part(ref, sb, rows_per_seq):
        return ref if nsb == 1 else ref.at[pl.ds(sb * rows_per_seq, rows_per_seq)]

    def kv_copy(sb, h, is_value):
        b = pl.program_id(0) * nsb + sb
        if is_value:
            src = part(kv_all, sb, n_new).at[:, pl.ds((N_KV_HEADS + h) * HEAD_DIM, HEAD_DIM)]
            return pltpu.make_async_copy(src, vo_hbm.at[b, layer, :, h, :],
                                         kv_sems.at[sb, N_KV_HEADS + h])
        src = part(kf_all, sb, n_new).at[:, pl.ds(h * HEAD_DIM, HEAD_DIM)]
        return pltpu.make_async_copy(src, ko_hbm.at[b, layer, :, h, :], kv_sems.at[sb, h])

    def prepare_keys_values(sb):
        kv_ref, k_scr, v_scr = part(kv_all, sb, n_new), part(k_all, sb, n_new + n_cache), part(v_all, sb, n_new + n_cache)
        kf_scr = part(kf_all, sb, n_new) if emit_kv else None
        for h in range(N_KV_HEADS):
            kn = _rms(_head(kv_ref, h), kg_ref[layer:layer + 1, :])
            vh = _head(kv_ref, N_KV_HEADS + h)
            if emit_kv:
                kf_scr[:, h * HEAD_DIM:(h + 1) * HEAD_DIM] = kn
            if rope:
                kn = _rope(kn, *(t[...] for t in k_tabs))
            k_scr[0:n_new, h * HEAD_DIM:(h + 1) * HEAD_DIM] = kn.astype(BF16)
            v0 = 2 * h * HEAD_DIM
            v_scr[0:n_new, v0:v0 + HEAD_DIM] = vh.astype(BF16)
            if n_cache:
                v_scr[n_new:n_new + n_cache, v0:v0 + HEAD_DIM] = _head(cv_ref[0, 0], h).astype(BF16)
            v_scr[:, v0 + HEAD_DIM:v0 + 2 * HEAD_DIM] = jnp.ones((n_new + n_cache, HEAD_DIM), BF16)
        if n_cache:
            k_scr[n_new:n_new + n_cache, :] = ck_ref[0, 0].astype(BF16)

    def attend(sb):
        rows = q_all.shape[0] // nsb
        q_ref, g_ref, att_ref = part(q_all, sb, rows), part(g_all, sb, rows), part(att_all, sb, rows)
        kv_sb = sb if single_tile else 0
        k_scr, v_scr = part(k_all, kv_sb, n_new + n_cache), part(v_all, kv_sb, n_new + n_cache)

        def query(hq):
            x = _rms(_head(q_ref, hq), qg_ref[layer:layer + 1, :])
            if rope:
                x = _rope(x, *(part(t, sb, rows)[...] for t in q_tabs))
            return x * Q_PRESCALE

        for h in range(N_KV_HEADS):
            q2 = jnp.concatenate([query(GROUP * h + g) for g in range(GROUP)], axis=0).astype(BF16)
            o = _softmax_pv(q2, _head(k_scr, h), v_scr[:, 2 * h * HEAD_DIM:2 * (h + 1) * HEAD_DIM])
            for g in range(GROUP):
                hq = GROUP * h + g
                gate = _silu(_head(g_ref, hq))
                att_ref[:, hq * HEAD_DIM:(hq + 1) * HEAD_DIM] = (o[g * rows:(g + 1) * rows] * gate).astype(BF16)

    every = [(sb, h) for sb in range(nsb) for h in range(N_KV_HEADS)]
    if single_tile:
        if emit_kv:
            for sb, h in every:
                kv_copy(sb, h, True).start()
        for sb in range(nsb):
            prepare_keys_values(sb)
        if emit_kv:
            for sb, h in every:
                kv_copy(sb, h, False).start()
    else:
        pl.when(pl.program_id(1) == 0)(lambda: prepare_keys_values(0))

    for sb in range(nsb):
        attend(sb)

    if emit_kv:
        for sb, h in every:
            kv_copy(sb, h, False).wait()
            kv_copy(sb, h, True).wait()


def _attention(proj, q_g, k_g, layer, seq, tq, n_batch, row0, *, nsb=1, cache=None, rope_tabs=None,
               att_prev=None, kv_prev=None, emit_kv=False):
    whole = tq == seq
    assert not whole or cache is None
    tq = nsb * tq
    kv_rows = nsb * seq if whole else seq
    nq = kv_rows // tq
    q0, kv0 = row0 // tq, row0 // kv_rows
    n_cache = 0 if cache is None else cache[0].shape[2]
    vec = pl.BlockSpec((DEPTH, HEAD_DIM), lambda b, i: (0, 0))
    in_specs = [
        pl.BlockSpec((tq, D_ATT), lambda b, i: (q0 + b * nq + i, COL_Q // D_ATT)),
        pl.BlockSpec((kv_rows, 2 * D_KV), lambda b, i: (kv0 + b, COL_K // (2 * D_KV))),
        pl.BlockSpec((tq, D_ATT), lambda b, i: (q0 + b * nq + i, COL_GA // D_ATT)),
        vec, vec,
    ]
    args = [proj, proj, proj, q_g, k_g]
    if cache is not None:
        in_specs += [pl.BlockSpec((1, 1, n_cache, D_KV), lambda b, i: (b, layer, 0, 0))] * 2
        args += list(cache)
    if rope_tabs is not None:
        in_specs += [pl.BlockSpec((tq, HEAD_DIM), lambda b, i: (i, 0))] * 3
        in_specs += [pl.BlockSpec((seq, HEAD_DIM), lambda b, i: (0, 0))] * 3
        args += list(rope_tabs) * 2
    aliased = ([] if att_prev is None else [att_prev]) + ([] if kv_prev is None else list(kv_prev))
    out_first = 0 if att_prev is not None else 1
    aliases = {len(args) + n: out_first + n for n in range(len(aliased))}
    in_specs += [pl.BlockSpec(memory_space=pl.ANY)] * len(aliased)
    args += aliased
    out_specs = [pl.BlockSpec((tq, D_ATT), lambda b, i: (q0 + b * nq + i, 0))]
    out_shape = [jax.ShapeDtypeStruct((M_ALL, D_ATT), BF16)]
    keys = kv_rows + n_cache
    scratch =[pltpu.VMEM((keys, D_KV), BF16), pltpu.VMEM((keys, 2 * D_KV), BF16)]
    if emit_kv:
        out_specs += [pl.BlockSpec(memory_space=pl.ANY)] * 2
        out_shape += [jax.ShapeDtypeStruct((n_batch, DEPTH, seq, N_KV_HEADS, HEAD_DIM), F32)] * 2
        scratch += [pltpu.VMEM((kv_rows, D_KV), F32), pltpu.SemaphoreType.DMA((nsb, 2 * N_KV_HEADS))]
    return pl.pallas_call(
        functools.partial(_attn_kernel, rope_tabs is not None, n_cache, layer, emit_kv, len(aliased), nsb),
        grid=(n_batch * seq // kv_rows, nq),
        in_specs=in_specs,
        out_specs=out_specs,
        out_shape=out_shape,
        input_output_aliases=aliases,
        scratch_shapes=scratch,
        compiler_params=_params("arbitrary", "arbitrary"),
        name=f"attention_{seq}",
    )(*args)


def _dft_matrix(L):
    k = np.arange(L, dtype=np.int64)[:, None]
    t = np.arange(L, dtype=np.int64)[None, :]
    ang = 2.0 * np.pi * ((k * t) % (2 * L)).astype(np.float64) / (2 * L)
    top = np.cos(ang)
    bot = np.sin(ang)
    bot[0, :] = np.where(np.arange(L) % 2 == 0, 1.0, -1.0)
    return np.concatenate([top, bot], axis=0).astype(np.float32)


def _filter_kernel(L, z_ref, w1_ref, b1_ref, w2_ref, b2_ref, fr_ref, w3f_ref, w3b_ref, dec_ref,
                   f_ref, o_ref, hdn_scr, w1_scr, w2_scr, vec_scr, w3_scr):
    hp = lax.Precision.HIGHEST
    layer = pl.program_id(0)

    halves = ((0, 0), (FEAT_PAD, FILT_HID))

    @pl.when(pl.program_id(1) == 0)
    def _():
        w1_scr[...] = jnp.zeros(w1_scr.shape, F32)
        w2_scr[...] = jnp.zeros(w2_scr.shape, F32)
        vec_scr[...] = jnp.zeros(vec_scr.shape, F32)
        w3_scr[...] = jnp.zeros(w3_scr.shape, BF16)
        for feat0, hid0 in halves:
            w1_scr[feat0:feat0 + POS_EMB, hid0:hid0 + FILT_HID] = w1_ref[0]
            w2_scr[hid0:hid0 + FILT_HID, hid0:hid0 + FILT_HID] = w2_ref[0]
            for r, ref in enumerate((b1_ref, b2_ref, fr_ref)):
                vec_scr[r:r + 1, hid0:hid0 + FILT_HID] = ref[pl.ds(layer, 1), :]
        b1, b2, fr = vec_scr[0:1, :], vec_scr[1:2, :], vec_scr[2:3, :]
        h1 = jnp.sin(fr * (jnp.dot(z_ref[...], w1_scr[...], precision=hp, preferred_element_type=F32) + b1))
        hdn_scr[...] = jnp.sin(fr * (jnp.dot(h1, w2_scr[...], precision=hp, preferred_element_type=F32) + b2))

    hdn_bf = hdn_scr[...].astype(BF16)
    dec = dec_ref[...]

    def project(w3_ref, slot):
        parts = []
        for k, (_, hid0) in enumerate(halves):
            w3_scr[slot + k, hid0:hid0 + FILT_HID, :] = w3_ref[0].astype(BF16)
            parts.append(jnp.dot(hdn_bf, w3_scr[slot + k], preferred_element_type=F32))
        return jnp.concatenate(parts, axis=0) * dec

    h_f = project(w3f_ref, 0)
    h_b = project(w3b_ref, 2)
    hs = h_f + h_b
    ha = jnp.dot(f_ref[0:L, :], hs.astype(BF16), preferred_element_type=F32)
    hb = jnp.dot(f_ref[L:2 * L, :], (h_f - h_b).astype(BF16), preferred_element_type=F32)
    row = lax.broadcasted_iota(jnp.int32, (L, 1), 0)
    first = row == 0
    nyquist = jnp.sum(jnp.where(row % 2 == 0, hs, -hs), axis=0, keepdims=True)
    wk = jnp.where(first, 1.0 / (2 * L), 2.0 / (2 * L))
    o_ref[0, 0] = ha * wk
    o_ref[0, 1] = jnp.where(first, 0.0, hb * wk)
    o_ref[0, 2] = jnp.where(first, nyquist, ha) * wk


FILT_CB = 512


def _filter_spectra(L, w1, b1, w2, b2, freq, w3, f_bf):
    tpos = np.arange(L, dtype=np.float64)
    t_norm = tpos / max(L - 1, 1)
    w = 2.0 * math.pi * tpos / L
    bands = np.linspace(1e-4, POS_BANDS - 1, POS_BANDS)
    z = np.concatenate([t_norm[:, None], np.cos(w[:, None] * bands), -np.sin(w[:, None] * bands)], axis=-1)
    z = np.pad(z, ((0, 0), (0, FEAT_PAD - POS_EMB))).astype(np.float32)
    z = np.concatenate([z[:L // 2], z[L // 2:]], axis=1)
    max_decay = math.log(DECAY_TARGET) / FAST_DECAY_PCT
    min_decay = math.log(DECAY_TARGET) / SLOW_DECAY_PCT
    deltas = np.abs(np.linspace(min_decay, max_decay, D_HY))
    dec = (np.exp(-t_norm[:, None] * deltas) + DECAY_SHIFT).astype(np.float32)

    ncb = D_HY // FILT_CB
    matrix = lambda rows, cols: pl.BlockSpec((1, rows, cols), lambda l, c: (l, 0, 0))
    per_layer_vec = pl.BlockSpec((DEPTH, FILT_HID), lambda l, c: (0, 0))
    return pl.pallas_call(
        functools.partial(_filter_kernel, L),
        grid=(DEPTH, ncb),
        in_specs=[
            pl.BlockSpec((L // 2, 2 * FEAT_PAD), lambda l, c: (0, 0)),
            matrix(POS_EMB, FILT_HID), per_layer_vec,
            matrix(FILT_HID, FILT_HID), per_layer_vec, per_layer_vec,
            pl.BlockSpec((1, FILT_HID, FILT_CB), lambda l, c: (l, 0, c)),
            pl.BlockSpec((1, FILT_HID, FILT_CB), lambda l, c: (l, 0, ncb + c)),
            pl.BlockSpec((L, FILT_CB), lambda l, c: (0, c)),
            pl.BlockSpec((2 * L, L), lambda l, c: (0, 0)),
        ],
        out_specs=pl.BlockSpec((1, 3, L, FILT_CB), lambda l, c: (l, 0, 0, c)),
        out_shape=jax.ShapeDtypeStruct((DEPTH, 3, L, D_HY), F32),
        scratch_shapes=[pltpu.VMEM((L // 2, FEAT_PAD), F32), pltpu.VMEM((2 * FEAT_PAD, FEAT_PAD), F32),
                        pltpu.VMEM((FEAT_PAD, FEAT_PAD), F32), pltpu.VMEM((8, FEAT_PAD), F32),
                        pltpu.VMEM((4, FEAT_PAD, FILT_CB), BF16)],
        compiler_params=_params("arbitrary", "arbitrary"),
        name=f"hyena_filter_{L}",
    )(jnp.asarray(z), w1, b1, w2, b2, freq, w3, w3, jnp.asarray(dec), f_bf)


def _hyena_kernel(L, layer, x0_ref, x1_ref, vv_ref, g_ref, cw0_ref, cw1_ref, cwv_ref, cb0_ref, cb1_ref,
                  cbv_ref, bias_ref, coef_ref, f_ref, ft_ref, *rest):
    o_ref = rest[-1]
    this_layer = slice(layer, layer + 1)
    row = lax.broadcasted_iota(jnp.int32, (L, 1), 0)
    is_first = row == 0
    is_last = row == L - 1

    def sconv(x_ref, w_ref, b_ref):
        x = x_ref[...]
        prev = jnp.where(is_first, 0.0, pltpu.roll(x, 1, axis=0))
        nxt = jnp.where(is_last, 0.0, pltpu.roll(x, L - 1, axis=0))
        return w_ref[0, 0:1, :] * prev + w_ref[0, 1:2, :] * x + w_ref[0, 2:3, :] * nxt + b_ref[this_layer, :]

    z = sconv(vv_ref, cwv_ref, cbv_ref) * sconv(x1_ref, cw1_ref, cb1_ref)
    ab = jnp.dot(f_ref[...], z.astype(BF16), preferred_element_type=F32)
    a, b = ab[0:L], ab[L:2 * L]
    g1, g2, g3 = coef_ref[0, 0], coef_ref[0, 1], coef_ref[0, 2]
    pq = jnp.concatenate([a * g1 - b * g2, a * g2 + b * g3], axis=0).astype(BF16)
    y = jnp.dot(ft_ref[...], pq, preferred_element_type=F32) + bias_ref[this_layer, :] * z
    o_ref[...] = (sconv(x0_ref, cw0_ref, cb0_ref) * y * _silu(g_ref[...])).astype(BF16)


def _hyena(proj, L, nb, row0, cb, layer, conv_w, conv_b, hy_bias, coef, f_bf, ft_bf, hy_prev=None):
    ncb = D_HY // cb
    rows = L
    blk0 = row0 // rows
    extra_specs = [] if hy_prev is None else [pl.BlockSpec(memory_space=pl.ANY)]
    extra_args = [] if hy_prev is None else [hy_prev]
    aliases = {} if hy_prev is None else {14: 0}
    col = lambda off: (lambda c, b: (blk0 + b, off // cb + c))
    cw = lambda part: pl.BlockSpec((1, 3, cb), lambda c, b: (layer, 0, part * ncb + c))
    cbias = lambda part: pl.BlockSpec((DEPTH, cb), lambda c, b: (0, part * ncb + c))
    once = pl.Buffered(1)
    return pl.pallas_call(
        functools.partial(_hyena_kernel, L, layer),
        grid=(ncb, nb),
        in_specs=[
            pl.BlockSpec((rows, cb), col(COL_X0)),
            pl.BlockSpec((rows, cb), col(COL_X1)),
            pl.BlockSpec((rows, cb), col(COL_VV)),
            pl.BlockSpec((rows, cb), col(COL_GH)),
            cw(0), cw(1), cw(2), cbias(0), cbias(1), cbias(2),
            pl.BlockSpec((DEPTH, cb), lambda c, b: (0, c)),
            pl.BlockSpec((1, 3, L, cb), lambda c, b: (layer, 0, 0, c)),
            pl.BlockSpec((2 * L, L), lambda c, b: (0, 0), pipeline_mode=once),
            pl.BlockSpec((L, 2 * L), lambda c, b: (0, 0), pipeline_mode=once),
            *extra_specs,
        ],
        out_specs=pl.BlockSpec((rows, cb), lambda c, b: (blk0 + b, c)),
        out_shape=jax.ShapeDtypeStruct((M_ALL, D_HY), BF16),
        input_output_aliases=aliases,
        compiler_params=_params("arbitrary", "arbitrary"),
        name=f"hyena_{L}",
    )(proj, proj, proj, proj, conv_w, conv_w, conv_w, conv_b, conv_b, conv_b, hy_bias, coef, f_bf, ft_bf,
      *extra_args)


OUTPROJ_TM = 256
OUTPROJ_TM_FINAL = 512


def _outproj_kernel(split_in, final, tm, tile0, gain_row, *refs):
    it = iter(refs)
    att_ref, hy_ref, w_ref = next(it), next(it), next(it)
    x_refs = (next(it), next(it)) if split_in else (next(it),)
    gate_ref, g_ref = next(it), next(it)
    shift_ref, scale_ref = (None, None) if final else (next(it), next(it))
    out_refs = (next(it),) if final else (next(it), next(it))
    w_scr = next(it)
    i = pl.program_id(0)

    @pl.when(i == 0)
    def _():
        w_scr[...] = w_ref[0].astype(BF16)

    out = (jnp.dot(att_ref[...], w_scr[0:D_ATT, :], preferred_element_type=F32)
           + jnp.dot(hy_ref[...], w_scr[D_ATT:D_ATT + D_HY, :], preferred_element_type=F32))
    tile = tile0 + i
    if split_in:
        x = jnp.where(tile < M_CTX // tm, x_refs[0][...], x_refs[1][...])
    else:
        x = x_refs[0][...]
    y = x + _mod_row(gate_ref, tile, tm) * out
    inv = lax.rsqrt(jnp.mean(y * y, axis=-1, keepdims=True) + EPS)
    g = g_ref[gain_row:gain_row + 1, :]
    if final:
        out_refs[0][...] = y * inv * g
    else:
        out_refs[0][...] = y
        gain = g * (1.0 + _mod_row(scale_ref, tile, tm))
        out_refs[1][...] = (y * inv * gain + _mod_row(shift_ref, tile, tm)).astype(BF16)


def _out_proj(att, hy, w_out, layer, x_parts, mod, gains, final, tm, tile0=0, n_tiles=None):
    n_tiles = M_ALL // tm if n_tiles is None else n_tiles
    split_in = len(x_parts) == 2
    assert not split_in or (tile0 == 0 and n_tiles == M_ALL // tm)
    row_tile = lambda width: pl.BlockSpec((tm, width), lambda i: (tile0 + i, 0))
    out_tile = pl.BlockSpec((tm, D_MODEL), lambda i: (i, 0))
    x_specs = list(_two_stream_specs(tm, D_MODEL)) if split_in else [row_tile(D_MODEL)]
    in_specs = [
        row_tile(D_ATT),
        row_tile(D_HY),
        pl.BlockSpec((1, D_ATT + D_HY, D_MODEL), lambda i: (layer, 0, 0), pipeline_mode=pl.Buffered(1)),
        *x_specs,
        _mod_spec(layer, MOD_GATE),
        pl.BlockSpec(gains.shape, lambda i: (0, 0)),
    ]
    args = [att, hy, w_out, *x_parts, mod, gains]
    if final:
        out_specs = [out_tile]
        out_shape = [jax.ShapeDtypeStruct((n_tiles * tm, D_MODEL), F32)]
    else:
        in_specs += [_mod_spec(layer + 1, MOD_SHIFT), _mod_spec(layer + 1, MOD_SCALE)]
        args += [mod, mod]
        out_specs = [out_tile, out_tile]
        out_shape = [jax.ShapeDtypeStruct((n_tiles * tm, D_MODEL), F32),
                     jax.ShapeDtypeStruct((n_tiles * tm, D_MODEL), BF16)]
    return pl.pallas_call(
        functools.partial(_outproj_kernel, split_in, final, tm, tile0, 0 if final else layer + 1),
        grid=(n_tiles,),
        in_specs=in_specs,
        out_specs=out_specs,
        out_shape=out_shape,
        scratch_shapes=[pltpu.VMEM((D_ATT + D_HY, D_MODEL), BF16)],
        compiler_params=_params("arbitrary"),
        name="out_proj_final" if final else "out_proj",
    )(*args)


def _rope_tables():
    t = np.arange(DEC_SEQ)
    row = (t // GRID_W).astype(np.float64)
    col = (t % GRID_W).astype(np.float64)
    pairs = HEAD_DIM // 4
    inv_freq = ROPE_THETA ** (-np.arange(pairs, dtype=np.float64) / pairs)
    ang = np.concatenate([row[:, None] * inv_freq, col[:, None] * inv_freq], axis=-1)
    cos = np.repeat(np.cos(ang), 2, axis=-1).astype(np.float32)
    sin = np.repeat(np.sin(ang), 2, axis=-1).astype(np.float32)
    even = (np.arange(HEAD_DIM) % 2 == 0)[None, :]
    sin_a = np.where(even, -sin, 0.0).astype(np.float32)
    sin_b = np.where(even, 0.0, sin).astype(np.float32)
    return jnp.asarray(cos), jnp.asarray(sin_a), jnp.asarray(sin_b)


def kernel(x_prompt, x_sample, cache_k, cache_v, c, c_ctx, norm_g, w_ada, b_ada, w_in, q_norm_g, k_norm_g,
           conv_w, conv_b, filt_w1, filt_b1, filt_w2, filt_b2, filt_w3, filt_freq, hy_bias, w_out, final_norm_g):
    ctx = x_prompt.reshape(BATCH * SEQ, D_MODEL)
    lat = x_sample.reshape(DEC_BATCH * DEC_SEQ, D_MODEL)
    cache_k4 = cache_k.reshape(DEC_BATCH, DEPTH, PAST_LEN, D_KV)
    cache_v4 = cache_v.reshape(DEC_BATCH, DEPTH, PAST_LEN, D_KV)

    mod = _modulation(c_ctx, c, w_ada, b_ada)

    rope_tabs = _rope_tables()
    dft = {}
    coefs = {}
    for L in (SEQ, DEC_SEQ):
        f_np = _dft_matrix(L)
        f_bf = jnp.asarray(f_np).astype(BF16)
        dft[L] = (f_bf, jnp.asarray(np.ascontiguousarray(f_np.T)).astype(BF16))
        coefs[L] = _filter_spectra(L, filt_w1, filt_b1, filt_w2, filt_b2, filt_freq, filt_w3, f_bf)

    final_g = final_norm_g.reshape(1, D_MODEL)

    x_parts = (ctx, lat)
    h = _norm_mod(ctx, lat, norm_g, mod, 0)
    kv_out = None
    for l in range(DEPTH):
        final = l == DEPTH - 1

        proj = _in_proj(h, w_in, l)

        att, new_k, new_v = _attention(proj, q_norm_g, k_norm_g, l, SEQ, CTX_TQ, BATCH, 0, nsb=4, kv_prev=kv_out,
                                       emit_kv=True)
        kv_out = (new_k, new_v)
        (att,) = _attention(proj, q_norm_g, k_norm_g, l, DEC_SEQ, LAT_TQ, DEC_BATCH, M_CTX, nsb=4,
                            cache=(cache_k4, cache_v4), rope_tabs=rope_tabs, att_prev=att)

        hy = _hyena(proj, SEQ, BATCH, 0, D_HY, l, conv_w, conv_b, hy_bias, coefs[SEQ], *dft[SEQ])
        hy = _hyena(proj, DEC_SEQ, DEC_BATCH, M_CTX, 512, l, conv_w, conv_b, hy_bias,
                    coefs[DEC_SEQ], *dft[DEC_SEQ], hy_prev=hy)

        if final:
            assert len(x_parts) == 1
            tm = OUTPROJ_TM_FINAL
            nc = M_CTX // tm
            (y_ctx,) = _out_proj(att, hy, w_out, l, x_parts, mod, final_g, True, tm, 0, nc)
            (y_lat,) = _out_proj(att, hy, w_out, l, x_parts, mod, final_g, True, tm, nc, M_LAT // tm)
        else:
            y, h = _out_proj(att, hy, w_out, l, x_parts, mod, norm_g, False, OUTPROJ_TM)
            x_parts = (y,)

    y_prompt = y_ctx.reshape(BATCH, SEQ, D_MODEL)
    y_sample = y_lat.reshape(DEC_BATCH, DEC_SEQ, D_MODEL)
    return (y_prompt, y_sample, kv_out[0], kv_out[1])
```

```python
import functools
import math

import numpy as np
import jax
import jax.numpy as jnp
from jax import lax
from jax.experimental import pallas as pl
from jax.experimental.pallas import tpu as pltpu

D_MODEL = 2048
BATCH = 16
SEQ = 256
DEPTH = 2
DEC_BATCH = 2
DEC_SEQ = 1024
PAST_LEN = 512
GRID_W = 64
D_ATT = 1024
D_HY = 1024
HEAD_DIM = 128
N_HEADS = 8
N_KV_HEADS = 4
GROUP = 2
D_KV = 512
ROPE_THETA = 10000.0
POS_BANDS = 16
POS_EMB = 33
FILT_HID = 64
DECAY_TARGET = 1e-2
FAST_DECAY_PCT = 0.3
SLOW_DECAY_PCT = 1.5
DECAY_SHIFT = 0.05
EPS = 1e-6
D_IN = 7168

COL_Q, COL_K, COL_V, COL_GA, COL_X0, COL_X1, COL_VV, COL_GH = 0, 1024, 1536, 2048, 3072, 4096, 5120, 6144

F32 = jnp.float32
BF16 = jnp.bfloat16

VMEM_LIMIT_BYTES = 56 * 1024 * 1024
FEAT_PAD = 128


def _params(*sem):
    return pltpu.CompilerParams(dimension_semantics=sem, vmem_limit_bytes=VMEM_LIMIT_BYTES)


def _silu(x):
    half = 0.5 * x
    return half + half * jnp.tanh(half)


def _rms(x, g):
    return x * lax.rsqrt(jnp.mean(x * x, axis=-1, keepdims=True) + EPS) * g


MOD_ROWS = 8
MOD_TN = 1024


def _mod_kernel(cctx_ref, c_ref, w_ref, b_ref, o_ref, c_scr):
    c_scr[...] = jnp.zeros(c_scr.shape, F32)
    c_scr[0:1, :] = cctx_ref[...]
    c_scr[1:1 + DEC_BATCH, :] = c_ref[...]
    s = _silu(c_scr[...]).astype(BF16)
    layer = pl.program_id(0)
    o_ref[0] = jnp.dot(s, w_ref[0].astype(BF16), preferred_element_type=F32) + b_ref[pl.ds(layer, 1), :]


def _modulation(c_ctx, c, w_ada, b_ada):
    n = 3 * D_MODEL
    return pl.pallas_call(
        _mod_kernel,
        grid=(DEPTH, n // MOD_TN),
        in_specs=[
            pl.BlockSpec((1, D_MODEL), lambda l, j: (0, 0)),
            pl.BlockSpec((DEC_BATCH, D_MODEL), lambda l, j: (0, 0)),
            pl.BlockSpec((1, D_MODEL, MOD_TN), lambda l, j: (l, 0, j)),
            pl.BlockSpec((DEPTH, MOD_TN), lambda l, j: (0, j)),
        ],
        out_specs=pl.BlockSpec((1, MOD_ROWS, MOD_TN), lambda l, j: (l, 0, j)),
        out_shape=jax.ShapeDtypeStruct((DEPTH, MOD_ROWS, n), F32),
        scratch_shapes=[pltpu.VMEM((MOD_ROWS, D_MODEL), F32)],
        compiler_params=_params("arbitrary", "arbitrary"),
        name="adaln_mod",
    )(c_ctx.reshape(1, D_MODEL), c, w_ada, b_ada)


M_CTX = BATCH * SEQ
M_LAT = DEC_BATCH * DEC_SEQ
M_ALL = M_CTX + M_LAT
NORM_TM = 1024
INPROJ_TM = 3072
INPROJ_TN = 512


ROW_CHUNK = 32


def _modulated_norm_rows(x_ref, h_ref, g, shift, scale):
    gain = g * (1.0 + scale)

    def body(r, _):
        rows = pl.ds(pl.multiple_of(r * ROW_CHUNK, ROW_CHUNK), ROW_CHUNK)
        x = x_ref[rows, :]
        inv = lax.rsqrt(jnp.mean(x * x, axis=-1, keepdims=True) + EPS)
        h_ref[rows, :] = (x * inv * gain + shift).astype(BF16)
        return 0

    lax.fori_loop(0, x_ref.shape[0] // ROW_CHUNK, body, 0, unroll=4)


MOD_SHIFT, MOD_SCALE, MOD_GATE = 0, 1, 2


def _mod_spec(layer, part):
    return pl.BlockSpec((1, MOD_ROWS, D_MODEL), lambda i: (layer, 0, part))


def _mod_row(ref, tile, tm):
    nc = M_CTX // tm
    per_batch = DEC_SEQ // tm
    row = jnp.where(tile < nc, 0, 1 + (tile - nc) // per_batch)
    return ref[0, pl.ds(row, 1), :]


def _two_stream_specs(tm, width):
    nc = M_CTX // tm
    return (pl.BlockSpec((tm, width), lambda i: (jnp.minimum(i, nc - 1), 0)),
            pl.BlockSpec((tm, width), lambda i: (jnp.maximum(i - nc, 0), 0)))


def _norm_kernel(layer, xc_ref, xl_ref, g_ref, shift_ref, scale_ref, h_ref):
    i = pl.program_id(0)

    def emit(x_ref):
        _modulated_norm_rows(x_ref, h_ref, g_ref[layer:layer + 1, :], _mod_row(shift_ref, i, NORM_TM),
                             _mod_row(scale_ref, i, NORM_TM))

    pl.when(i < M_CTX // NORM_TM)(lambda: emit(xc_ref))
    pl.when(i >= M_CTX // NORM_TM)(lambda: emit(xl_ref))


def _norm_mod(x_ctx, x_lat, norm_g, mod, layer):
    xc_spec, xl_spec = _two_stream_specs(NORM_TM, D_MODEL)
    return pl.pallas_call(
        functools.partial(_norm_kernel, layer),
        grid=(M_ALL // NORM_TM,),
        in_specs=[
            xc_spec, xl_spec,
            pl.BlockSpec((DEPTH, D_MODEL), lambda i: (0, 0)),
            _mod_spec(layer, MOD_SHIFT),
            _mod_spec(layer, MOD_SCALE),
        ],
        out_specs=pl.BlockSpec((NORM_TM, D_MODEL), lambda i: (i, 0)),
        out_shape=jax.ShapeDtypeStruct((M_ALL, D_MODEL), BF16),
        compiler_params=_params("arbitrary"),
        name="norm_mod",
    )(x_ctx, x_lat, norm_g, mod, mod)


def _inproj_kernel(h_hbm, w_ref, o_ref, w_scr, h_scr, h_sems):
    j, i = pl.program_id(0), pl.program_id(1)
    n_chunks = M_ALL // INPROJ_TM

    def h_copy(c):
        rows = pl.ds(c * INPROJ_TM, INPROJ_TM)
        return pltpu.make_async_copy(h_hbm.at[rows], h_scr.at[rows], h_sems.at[c])

    @pl.when((j == 0) & (i == 0))
    def _():
        for c in range(n_chunks):
            h_copy(c).start()

    @pl.when(i == 0)
    def _():
        w_scr[...] = w_ref[0].astype(BF16)

    for c in range(n_chunks):
        @pl.when((j == 0) & (i == c))
        def _():
            h_copy(c).wait()

    rows = pl.ds(pl.multiple_of(i * INPROJ_TM, INPROJ_TM), INPROJ_TM)
    o_ref[...] = jnp.dot(h_scr[rows, :], w_scr[...], preferred_element_type=F32)


def _in_proj(h, w_in, layer):
    return pl.pallas_call(
        _inproj_kernel,
        grid=(D_IN // INPROJ_TN, M_ALL // INPROJ_TM),
        in_specs=[
            pl.BlockSpec(memory_space=pl.ANY),
            pl.BlockSpec((1, D_MODEL, INPROJ_TN), lambda j, i: (layer, 0, j)),
        ],
        out_specs=pl.BlockSpec((INPROJ_TM, INPROJ_TN), lambda j, i: (i, j)),
        out_shape=jax.ShapeDtypeStruct((M_ALL, D_IN), F32),
        scratch_shapes=[pltpu.VMEM((D_MODEL, INPROJ_TN), BF16), pltpu.VMEM((M_ALL, D_MODEL), BF16),
                        pltpu.SemaphoreType.DMA((M_ALL // INPROJ_TM,))],
        compiler_params=_params("arbitrary", "arbitrary"),
        name="in_proj",
    )(h, w_in)


ATT_SCALE = 1.0 / math.sqrt(HEAD_DIM)


def _head(ref_or_val, h):
    return ref_or_val[:, h * HEAD_DIM:(h + 1) * HEAD_DIM]


Q_PRESCALE = ATT_SCALE * math.log2(math.e)


def _softmax_pv(q2, k_bf, v_ext):
    s = lax.dot_general(q2, k_bf, (((1,), (1,)), ((), ())), preferred_element_type=F32)
    p = jnp.exp2(s - jnp.max(s, axis=-1, keepdims=True)).astype(BF16)
    o = jnp.dot(p, v_ext, preferred_element_type=F32)
    return o[:, 0:HEAD_DIM] / o[:, HEAD_DIM:2 * HEAD_DIM]


def _rope(x, cos2, sin_a, sin_b):
    nxt = pltpu.roll(x, HEAD_DIM - 1, axis=1)
    prv = pltpu.roll(x, 1, axis=1)
    return x * cos2 + nxt * sin_a + prv * sin_b


CTX_TQ = SEQ
LAT_TQ = 128
KV_COPY_PRIORITY = 1


def _attn_kernel(rope, n_cache, layer, emit_kv, n_aliased, nsb, *refs):
    it = iter(refs)
    q_all, kv_all, g_all, qg_ref, kg_ref = (next(it) for _ in range(5))
    ck_ref, cv_ref = (next(it), next(it)) if n_cache else (None, None)
    q_tabs = tuple(next(it) for _ in range(3)) if rope else None
    k_tabs = tuple(next(it) for _ in range(3)) if rope else None
    for _ in range(n_aliased):
        next(it)
    att_all = next(it)
    ko_hbm, vo_hbm = (next(it), next(it)) if emit_kv else (None, None)
    k_all, v_all = next(it), next(it)
    kf_all, kv_sems = (next(it), next(it)) if emit_kv else (None, None)
    single_tile = q_all.shape[0] == kv_all.shape[0]
    n_new = kv_all.shape[0] // nsb if single_tile else kv_all.shape[0]
    assert not emit_kv or single_tile

    def part(ref, sb, rows_per_seq):
        return ref if nsb == 1 else ref.at[pl.ds(sb * rows_per_seq, rows_per_seq)]

    def kv_copy(sb, h, is_value):
        b = pl.program_id(0) * nsb + sb
        if is_value:
            src = part(kv_all, sb, n_new).at[:, pl.ds((N_KV_HEADS + h) * HEAD_DIM, HEAD_DIM)]
            return pltpu.make_async_copy(src, vo_hbm.at[b, layer, :, h, :],
                                         kv_sems.at[sb, N_KV_HEADS + h])
        src = part(kf_all, sb, n_new).at[:, pl.ds(h * HEAD_DIM, HEAD_DIM)]
        return pltpu.make_async_copy(src, ko_hbm.at[b, layer, :, h, :], kv_sems.at[sb, h])

    def prepare_keys_values(sb):
        kv_ref, k_scr, v_scr = part(kv_all, sb, n_new), part(k_all, sb, n_new + n_cache), part(v_all, sb, n_new + n_cache)
        kf_scr = part(kf_all, sb, n_new) if emit_kv else None
        for h in range(N_KV_HEADS):
            kn = _rms(_head(kv_ref, h), kg_ref[layer:layer + 1, :])
            vh = _head(kv_ref, N_KV_HEADS + h)
            if emit_kv:
                kf_scr[:, h * HEAD_DIM:(h + 1) * HEAD_DIM] = kn
            if rope:
                kn = _rope(kn, *(t[...] for t in k_tabs))
            k_scr[0:n_new, h * HEAD_DIM:(h + 1) * HEAD_DIM] = kn.astype(BF16)
            v0 = 2 * h * HEAD_DIM
            v_scr[0:n_new, v0:v0 + HEAD_DIM] = vh.astype(BF16)
            if n_cache:
                v_scr[n_new:n_new + n_cache, v0:v0 + HEAD_DIM] = _head(cv_ref[0, 0], h).astype(BF16)
            v_scr[:, v0 + HEAD_DIM:v0 + 2 * HEAD_DIM] = jnp.ones((n_new + n_cache, HEAD_DIM), BF16)
        if n_cache:
            k_scr[n_new:n_new + n_cache, :] = ck_ref[0, 0].astype(BF16)

    def attend(sb):
        rows = q_all.shape[0] // nsb
        q_ref, g_ref, att_ref = part(q_all, sb, rows), part(g_all, sb, rows), part(att_all, sb, rows)
        kv_sb = sb if single_tile else 0
        k_scr, v_scr = part(k_all, kv_sb, n_new + n_cache), part(v_all, kv_sb, n_new + n_cache)

        def query(hq):
            x = _rms(_head(q_ref, hq), qg_ref[layer:layer + 1, :])
            if rope:
                x = _rope(x, *(part(t, sb, rows)[...] for t in q_tabs))
            return x * Q_PRESCALE

        for h in range(N_KV_HEADS):
            q2 = jnp.concatenate([query(GROUP * h + g) for g in range(GROUP)], axis=0).astype(BF16)
            o = _softmax_pv(q2, _head(k_scr, h), v_scr[:, 2 * h * HEAD_DIM:2 * (h + 1) * HEAD_DIM])
            for g in range(GROUP):
                hq = GROUP * h + g
                gate = _silu(_head(g_ref, hq))
                att_ref[:, hq * HEAD_DIM:(hq + 1) * HEAD_DIM] = (o[g * rows:(g + 1) * rows] * gate).astype(BF16)

    every = [(sb, h) for sb in range(nsb) for h in range(N_KV_HEADS)]
    if single_tile:
        if emit_kv:
            for sb, h in every:
                kv_copy(sb, h, True).start(priority=KV_COPY_PRIORITY)
        for sb in range(nsb):
            prepare_keys_values(sb)
        if emit_kv:
            for sb, h in every:
                kv_copy(sb, h, False).start(priority=KV_COPY_PRIORITY)
    else:
        pl.when(pl.program_id(1) == 0)(lambda: prepare_keys_values(0))

    for sb in range(nsb):
        attend(sb)

    if emit_kv:
        for sb, h in every:
            kv_copy(sb, h, False).wait()
            kv_copy(sb, h, True).wait()


def _attention(proj, q_g, k_g, layer, seq, tq, n_batch, row0, *, nsb=1, cache=None, rope_tabs=None,
               att_prev=None, kv_prev=None, emit_kv=False):
    whole = tq == seq
    assert not whole or cache is None
    tq = nsb * tq
    kv_rows = nsb * seq if whole else seq
    nq = kv_rows // tq
    q0, kv0 = row0 // tq, row0 // kv_rows
    n_cache = 0 if cache is None else cache[0].shape[2]
    vec = pl.BlockSpec((DEPTH, HEAD_DIM), lambda b, i: (0, 0))
    in_specs = [
        pl.BlockSpec((tq, D_ATT), lambda b, i: (q0 + b * nq + i, COL_Q // D_ATT)),
        pl.BlockSpec((kv_rows, 2 * D_KV), lambda b, i: (kv0 + b, COL_K // (2 * D_KV))),
        pl.BlockSpec((tq, D_ATT), lambda b, i: (q0 + b * nq + i, COL_GA // D_ATT)),
        vec, vec,
    ]
    args = [proj, proj, proj, q_g, k_g]
    if cache is not None:
        in_specs += [pl.BlockSpec((1, 1, n_cache, D_KV), lambda b, i: (b, layer, 0, 0))] * 2
        args += list(cache)
    if rope_tabs is not None:
        in_specs += [pl.BlockSpec((tq, HEAD_DIM), lambda b, i: (i, 0))] * 3
        in_specs += [pl.BlockSpec((seq, HEAD_DIM), lambda b, i: (0, 0))] * 3
        args += list(rope_tabs) * 2
    aliased = ([] if att_prev is None else [att_prev]) + ([] if kv_prev is None else list(kv_prev))
    out_first = 0 if att_prev is not None else 1
    aliases = {len(args) + n: out_first + n for n in range(len(aliased))}
    in_specs += [pl.BlockSpec(memory_space=pl.ANY)] * len(aliased)
    args += aliased
    out_specs = [pl.BlockSpec((tq, D_ATT), lambda b, i: (q0 + b * nq + i, 0))]
    out_shape = [jax.ShapeDtypeStruct((M_ALL, D_ATT), BF16)]
    keys = kv_rows + n_cache
    scratch =[pltpu.VMEM((keys, D_KV), BF16), pltpu.VMEM((keys, 2 * D_KV), BF16)]
    if emit_kv:
        out_specs += [pl.BlockSpec(memory_space=pl.ANY)] * 2
        out_shape += [jax.ShapeDtypeStruct((n_batch, DEPTH, seq, N_KV_HEADS, HEAD_DIM), F32)] * 2
        scratch += [pltpu.VMEM((kv_rows, D_KV), F32), pltpu.SemaphoreType.DMA((nsb, 2 * N_KV_HEADS))]
    return pl.pallas_call(
        functools.partial(_attn_kernel, rope_tabs is not None, n_cache, layer, emit_kv, len(aliased), nsb),
        grid=(n_batch * seq // kv_rows, nq),
        in_specs=in_specs,
        out_specs=out_specs,
        out_shape=out_shape,
        input_output_aliases=aliases,
        scratch_shapes=scratch,
        compiler_params=_params("arbitrary", "arbitrary"),
        name=f"attention_{seq}",
    )(*args)


def _dft_matrix(L):
    k = np.arange(L, dtype=np.int64)[:, None]
    t = np.arange(L, dtype=np.int64)[None, :]
    ang = 2.0 * np.pi * ((k * t) % (2 * L)).astype(np.float64) / (2 * L)
    top = np.cos(ang)
    bot = np.sin(ang)
    bot[0, :] = np.where(np.arange(L) % 2 == 0, 1.0, -1.0)
    return np.concatenate([top, bot], axis=0).astype(np.float32)


def _filter_kernel(L, z_ref, w1_ref, b1_ref, w2_ref, b2_ref, fr_ref, w3f_ref, w3b_ref, dec_ref,
                   f_ref, o_ref, hdn_scr, w1_scr, w2_scr, vec_scr, w3_scr):
    hp = lax.Precision.HIGHEST
    layer = pl.program_id(0)

    halves = ((0, 0), (FEAT_PAD, FILT_HID))

    @pl.when(pl.program_id(1) == 0)
    def _():
        w1_scr[...] = jnp.zeros(w1_scr.shape, F32)
        w2_scr[...] = jnp.zeros(w2_scr.shape, F32)
        vec_scr[...] = jnp.zeros(vec_scr.shape, F32)
        w3_scr[...] = jnp.zeros(w3_scr.shape, BF16)
        for feat0, hid0 in halves:
            w1_scr[feat0:feat0 + POS_EMB, hid0:hid0 + FILT_HID] = w1_ref[0]
            w2_scr[hid0:hid0 + FILT_HID, hid0:hid0 + FILT_HID] = w2_ref[0]
            for r, ref in enumerate((b1_ref, b2_ref, fr_ref)):
                vec_scr[r:r + 1, hid0:hid0 + FILT_HID] = ref[pl.ds(layer, 1), :]
        b1, b2, fr = vec_scr[0:1, :], vec_scr[1:2, :], vec_scr[2:3, :]
        h1 = jnp.sin(fr * (jnp.dot(z_ref[...], w1_scr[...], precision=hp, preferred_element_type=F32) + b1))
        hdn_scr[...] = jnp.sin(fr * (jnp.dot(h1, w2_scr[...], precision=hp, preferred_element_type=F32) + b2))

    hdn_bf = hdn_scr[...].astype(BF16)
    dec = dec_ref[...]

    def project(w3_ref, slot):
        parts = []
        for k, (_, hid0) in enumerate(halves):
            w3_scr[slot + k, hid0:hid0 + FILT_HID, :] = w3_ref[0].astype(BF16)
            parts.append(jnp.dot(hdn_bf, w3_scr[slot + k], preferred_element_type=F32))
        return jnp.concatenate(parts, axis=0) * dec

    h_f = project(w3f_ref, 0)
    h_b = project(w3b_ref, 2)
    hs = h_f + h_b
    ha = jnp.dot(f_ref[0:L, :], hs.astype(BF16), preferred_element_type=F32)
    hb = jnp.dot(f_ref[L:2 * L, :], (h_f - h_b).astype(BF16), preferred_element_type=F32)
    row = lax.broadcasted_iota(jnp.int32, (L, 1), 0)
    first = row == 0
    nyquist = jnp.sum(jnp.where(row % 2 == 0, hs, -hs), axis=0, keepdims=True)
    wk = jnp.where(first, 1.0 / (2 * L), 2.0 / (2 * L))
    o_ref[0, 0] = ha * wk
    o_ref[0, 1] = jnp.where(first, 0.0, hb * wk)
    o_ref[0, 2] = jnp.where(first, nyquist, ha) * wk


FILT_CB = 512


def _filter_spectra(L, w1, b1, w2, b2, freq, w3, f_bf):
    tpos = np.arange(L, dtype=np.float64)
    t_norm = tpos / max(L - 1, 1)
    w = 2.0 * math.pi * tpos / L
    bands = np.linspace(1e-4, POS_BANDS - 1, POS_BANDS)
    z = np.concatenate([t_norm[:, None], np.cos(w[:, None] * bands), -np.sin(w[:, None] * bands)], axis=-1)
    z = np.pad(z, ((0, 0), (0, FEAT_PAD - POS_EMB))).astype(np.float32)
    z = np.concatenate([z[:L // 2], z[L // 2:]], axis=1)
    max_decay = math.log(DECAY_TARGET) / FAST_DECAY_PCT
    min_decay = math.log(DECAY_TARGET) / SLOW_DECAY_PCT
    deltas = np.abs(np.linspace(min_decay, max_decay, D_HY))
    dec = (np.exp(-t_norm[:, None] * deltas) + DECAY_SHIFT).astype(np.float32)

    ncb = D_HY // FILT_CB
    matrix = lambda rows, cols: pl.BlockSpec((1, rows, cols), lambda l, c: (l, 0, 0))
    per_layer_vec = pl.BlockSpec((DEPTH, FILT_HID), lambda l, c: (0, 0))
    return pl.pallas_call(
        functools.partial(_filter_kernel, L),
        grid=(DEPTH, ncb),
        in_specs=[
            pl.BlockSpec((L // 2, 2 * FEAT_PAD), lambda l, c: (0, 0)),
            matrix(POS_EMB, FILT_HID), per_layer_vec,
            matrix(FILT_HID, FILT_HID), per_layer_vec, per_layer_vec,
            pl.BlockSpec((1, FILT_HID, FILT_CB), lambda l, c: (l, 0, c)),
            pl.BlockSpec((1, FILT_HID, FILT_CB), lambda l, c: (l, 0, ncb + c)),
            pl.BlockSpec((L, FILT_CB), lambda l, c: (0, c)),
            pl.BlockSpec((2 * L, L), lambda l, c: (0, 0)),
        ],
        out_specs=pl.BlockSpec((1, 3, L, FILT_CB), lambda l, c: (l, 0, 0, c)),
        out_shape=jax.ShapeDtypeStruct((DEPTH, 3, L, D_HY), F32),
        scratch_shapes=[pltpu.VMEM((L // 2, FEAT_PAD), F32), pltpu.VMEM((2 * FEAT_PAD, FEAT_PAD), F32),
                        pltpu.VMEM((FEAT_PAD, FEAT_PAD), F32), pltpu.VMEM((8, FEAT_PAD), F32),
                        pltpu.VMEM((4, FEAT_PAD, FILT_CB), BF16)],
        compiler_params=_params("arbitrary", "arbitrary"),
        name=f"hyena_filter_{L}",
    )(jnp.asarray(z), w1, b1, w2, b2, freq, w3, w3, jnp.asarray(dec), f_bf)


def _hyena_kernel(L, layer, x0_ref, x1_ref, vv_ref, g_ref, cw0_ref, cw1_ref, cwv_ref, cb0_ref, cb1_ref,
                  cbv_ref, bias_ref, coef_ref, f_ref, ft_ref, *rest):
    o_ref = rest[-1]
    this_layer = slice(layer, layer + 1)
    row = lax.broadcasted_iota(jnp.int32, (L, 1), 0)
    is_first = row == 0
    is_last = row == L - 1

    def sconv(x_ref, w_ref, b_ref):
        x = x_ref[...]
        prev = jnp.where(is_first, 0.0, pltpu.roll(x, 1, axis=0))
        nxt = jnp.where(is_last, 0.0, pltpu.roll(x, L - 1, axis=0))
        return w_ref[0, 0:1, :] * prev + w_ref[0, 1:2, :] * x + w_ref[0, 2:3, :] * nxt + b_ref[this_layer, :]

    z = sconv(vv_ref, cwv_ref, cbv_ref) * sconv(x1_ref, cw1_ref, cb1_ref)
    ab = jnp.dot(f_ref[...], z.astype(BF16), preferred_element_type=F32)
    a, b = ab[0:L], ab[L:2 * L]
    g1, g2, g3 = coef_ref[0, 0], coef_ref[0, 1], coef_ref[0, 2]
    pq = jnp.concatenate([a * g1 - b * g2, a * g2 + b * g3], axis=0).astype(BF16)
    y = jnp.dot(ft_ref[...], pq, preferred_element_type=F32) + bias_ref[this_layer, :] * z
    o_ref[...] = (sconv(x0_ref, cw0_ref, cb0_ref) * y * _silu(g_ref[...])).astype(BF16)


def _hyena(proj, L, nb, row0, cb, layer, conv_w, conv_b, hy_bias, coef, f_bf, ft_bf, hy_prev=None):
    ncb = D_HY // cb
    rows = L
    blk0 = row0 // rows
    extra_specs = [] if hy_prev is None else [pl.BlockSpec(memory_space=pl.ANY)]
    extra_args = [] if hy_prev is None else [hy_prev]
    aliases = {} if hy_prev is None else {14: 0}
    col = lambda off: (lambda c, b: (blk0 + b, off // cb + c))
    cw = lambda part: pl.BlockSpec((1, 3, cb), lambda c, b: (layer, 0, part * ncb + c))
    cbias = lambda part: pl.BlockSpec((DEPTH, cb), lambda c, b: (0, part * ncb + c))
    once = pl.Buffered(1)
    return pl.pallas_call(
        functools.partial(_hyena_kernel, L, layer),
        grid=(ncb, nb),
        in_specs=[
            pl.BlockSpec((rows, cb), col(COL_X0)),
            pl.BlockSpec((rows, cb), col(COL_X1)),
            pl.BlockSpec((rows, cb), col(COL_VV)),
            pl.BlockSpec((rows, cb), col(COL_GH)),
            cw(0), cw(1), cw(2), cbias(0), cbias(1), cbias(2),
            pl.BlockSpec((DEPTH, cb), lambda c, b: (0, c)),
            pl.BlockSpec((1, 3, L, cb), lambda c, b: (layer, 0, 0, c)),
            pl.BlockSpec((2 * L, L), lambda c, b: (0, 0), pipeline_mode=once),
            pl.BlockSpec((L, 2 * L), lambda c, b: (0, 0), pipeline_mode=once),
            *extra_specs,
        ],
        out_specs=pl.BlockSpec((rows, cb), lambda c, b: (blk0 + b, c)),
        out_shape=jax.ShapeDtypeStruct((M_ALL, D_HY), BF16),
        input_output_aliases=aliases,
        compiler_params=_params("arbitrary", "arbitrary"),
        name=f"hyena_{L}",
    )(proj, proj, proj, proj, conv_w, conv_w, conv_w, conv_b, conv_b, conv_b, hy_bias, coef, f_bf, ft_bf,
      *extra_args)


OUTPROJ_TM = 256
OUTPROJ_TM_FINAL = 512


def _outproj_kernel(split_in, final, tm, tile0, gain_row, *refs):
    it = iter(refs)
    att_ref, hy_ref, w_ref = next(it), next(it), next(it)
    x_refs = (next(it), next(it)) if split_in else (next(it),)
    gate_ref, g_ref = next(it), next(it)
    shift_ref, scale_ref = (None, None) if final else (next(it), next(it))
    out_refs = (next(it),) if final else (next(it), next(it))
    w_scr = next(it)
    i = pl.program_id(0)

    @pl.when(i == 0)
    def _():
        w_scr[...] = w_ref[0].astype(BF16)

    out = (jnp.dot(att_ref[...], w_scr[0:D_ATT, :], preferred_element_type=F32)
           + jnp.dot(hy_ref[...], w_scr[D_ATT:D_ATT + D_HY, :], preferred_element_type=F32))
    tile = tile0 + i
    if split_in:
        x = jnp.where(tile < M_CTX // tm, x_refs[0][...], x_refs[1][...])
    else:
        x = x_refs[0][...]
    y = x + _mod_row(gate_ref, tile, tm) * out
    inv = lax.rsqrt(jnp.mean(y * y, axis=-1, keepdims=True) + EPS)
    g = g_ref[gain_row:gain_row + 1, :]
    if final:
        out_refs[0][...] = y * inv * g
    else:
        out_refs[0][...] = y
        gain = g * (1.0 + _mod_row(scale_ref, tile, tm))
        out_refs[1][...] = (y * inv * gain + _mod_row(shift_ref, tile, tm)).astype(BF16)


def _out_proj(att, hy, w_out, layer, x_parts, mod, gains, final, tm, tile0=0, n_tiles=None):
    n_tiles = M_ALL // tm if n_tiles is None else n_tiles
    split_in = len(x_parts) == 2
    assert not split_in or (tile0 == 0 and n_tiles == M_ALL // tm)
    row_tile = lambda width: pl.BlockSpec((tm, width), lambda i: (tile0 + i, 0))
    out_tile = pl.BlockSpec((tm, D_MODEL), lambda i: (i, 0))
    x_specs = list(_two_stream_specs(tm, D_MODEL)) if split_in else [row_tile(D_MODEL)]
    in_specs = [
        row_tile(D_ATT),
        row_tile(D_HY),
        pl.BlockSpec((1, D_ATT + D_HY, D_MODEL), lambda i: (layer, 0, 0), pipeline_mode=pl.Buffered(1)),
        *x_specs,
        _mod_spec(layer, MOD_GATE),
        pl.BlockSpec(gains.shape, lambda i: (0, 0)),
    ]
    args = [att, hy, w_out, *x_parts, mod, gains]
    if final:
        out_specs = [out_tile]
        out_shape = [jax.ShapeDtypeStruct((n_tiles * tm, D_MODEL), F32)]
    else:
        in_specs += [_mod_spec(layer + 1, MOD_SHIFT), _mod_spec(layer + 1, MOD_SCALE)]
        args += [mod, mod]
        out_specs = [out_tile, out_tile]
        out_shape = [jax.ShapeDtypeStruct((n_tiles * tm, D_MODEL), F32),
                     jax.ShapeDtypeStruct((n_tiles * tm, D_MODEL), BF16)]
    return pl.pallas_call(
        functools.partial(_outproj_kernel, split_in, final, tm, tile0, 0 if final else layer + 1),
        grid=(n_tiles,),
        in_specs=in_specs,
        out_specs=out_specs,
        out_shape=out_shape,
        scratch_shapes=[pltpu.VMEM((D_ATT + D_HY, D_MODEL), BF16)],
        compiler_params=_params("arbitrary"),
        name="out_proj_final" if final else "out_proj",
    )(*args)


def _rope_tables():
    t = np.arange(DEC_SEQ)
    row = (t // GRID_W).astype(np.float64)
    col = (t % GRID_W).astype(np.float64)
    pairs = HEAD_DIM // 4
    inv_freq = ROPE_THETA ** (-np.arange(pairs, dtype=np.float64) / pairs)
    ang = np.concatenate([row[:, None] * inv_freq, col[:, None] * inv_freq], axis=-1)
    cos = np.repeat(np.cos(ang), 2, axis=-1).astype(np.float32)
    sin = np.repeat(np.sin(ang), 2, axis=-1).astype(np.float32)
    even = (np.arange(HEAD_DIM) % 2 == 0)[None, :]
    sin_a = np.where(even, -sin, 0.0).astype(np.float32)
    sin_b = np.where(even, 0.0, sin).astype(np.float32)
    return jnp.asarray(cos), jnp.asarray(sin_a), jnp.asarray(sin_b)


def kernel(x_prompt, x_sample, cache_k, cache_v, c, c_ctx, norm_g, w_ada, b_ada, w_in, q_norm_g, k_norm_g,
           conv_w, conv_b, filt_w1, filt_b1, filt_w2, filt_b2, filt_w3, filt_freq, hy_bias, w_out, final_norm_g):
    ctx = x_prompt.reshape(BATCH * SEQ, D_MODEL)
    lat = x_sample.reshape(DEC_BATCH * DEC_SEQ, D_MODEL)
    cache_k4 = cache_k.reshape(DEC_BATCH, DEPTH, PAST_LEN, D_KV)
    cache_v4 = cache_v.reshape(DEC_BATCH, DEPTH, PAST_LEN, D_KV)

    mod = _modulation(c_ctx, c, w_ada, b_ada)

    rope_tabs = _rope_tables()
    dft = {}
    coefs = {}
    for L in (SEQ, DEC_SEQ):
        f_np = _dft_matrix(L)
        f_bf = jnp.asarray(f_np).astype(BF16)
        dft[L] = (f_bf, jnp.asarray(np.ascontiguousarray(f_np.T)).astype(BF16))
        coefs[L] = _filter_spectra(L, filt_w1, filt_b1, filt_w2, filt_b2, filt_freq, filt_w3, f_bf)

    final_g = final_norm_g.reshape(1, D_MODEL)

    x_parts = (ctx, lat)
    h = _norm_mod(ctx, lat, norm_g, mod, 0)
    kv_out = None
    for l in range(DEPTH):
        final = l == DEPTH - 1

        proj = _in_proj(h, w_in, l)

        att, new_k, new_v = _attention(proj, q_norm_g, k_norm_g, l, SEQ, CTX_TQ, BATCH, 0, nsb=4, kv_prev=kv_out,
                                       emit_kv=True)
        kv_out = (new_k, new_v)
        (att,) = _attention(proj, q_norm_g, k_norm_g, l, DEC_SEQ, LAT_TQ, DEC_BATCH, M_CTX, nsb=4,
                            cache=(cache_k4, cache_v4), rope_tabs=rope_tabs, att_prev=att)

        hy = _hyena(proj, SEQ, BATCH, 0, D_HY, l, conv_w, conv_b, hy_bias, coefs[SEQ], *dft[SEQ])
        hy = _hyena(proj, DEC_SEQ, DEC_BATCH, M_CTX, 512, l, conv_w, conv_b, hy_bias,
                    coefs[DEC_SEQ], *dft[DEC_SEQ], hy_prev=hy)

        if final:
            assert len(x_parts) == 1
            tm = OUTPROJ_TM_FINAL
            nc = M_CTX // tm
            (y_ctx,) = _out_proj(att, hy, w_out, l, x_parts, mod, final_g, True, tm, 0, nc)
            (y_lat,) = _out_proj(att, hy, w_out, l, x_parts, mod, final_g, True, tm, nc, M_LAT // tm)
        else:
            y, h = _out_proj(att, hy, w_out, l, x_parts, mod, norm_g, False, OUTPROJ_TM)
            x_parts = (y,)

    y_prompt = y_ctx.reshape(BATCH, SEQ, D_MODEL)
    y_sample = y_lat.reshape(DEC_BATCH, DEC_SEQ, D_MODEL)
    return (y_prompt, y_sample, kv_out[0], kv_out[1])
```

```python
import functools
import math

import numpy as np
import jax
import jax.numpy as jnp
from jax import lax
from jax.experimental import pallas as pl
from jax.experimental.pallas import tpu as pltpu

D_MODEL = 2048
BATCH = 16
SEQ = 256
DEPTH = 2
DEC_BATCH = 2
DEC_SEQ = 1024
PAST_LEN = 512
GRID_W = 64
D_ATT = 1024
D_HY = 1024
HEAD_DIM = 128
N_HEADS = 8
N_KV_HEADS = 4
GROUP = 2
D_KV = 512
ROPE_THETA = 10000.0
POS_BANDS = 16
POS_EMB = 33
FILT_HID = 64
DECAY_TARGET = 1e-2
FAST_DECAY_PCT = 0.3
SLOW_DECAY_PCT = 1.5
DECAY_SHIFT = 0.05
EPS = 1e-6
D_IN = 7168

COL_Q, COL_K, COL_V, COL_GA, COL_X0, COL_X1, COL_VV, COL_GH = 0, 1024, 1536, 2048, 3072, 4096, 5120, 6144

F32 = jnp.float32
BF16 = jnp.bfloat16

VMEM_LIMIT_BYTES = 56 * 1024 * 1024
FEAT_PAD = 128


def _params(*sem):
    return pltpu.CompilerParams(dimension_semantics=sem, vmem_limit_bytes=VMEM_LIMIT_BYTES)


def _silu(x):
    half = 0.5 * x
    return half + half * jnp.tanh(half)


def _rms(x, g):
    return x * lax.rsqrt(jnp.mean(x * x, axis=-1, keepdims=True) + EPS) * g


MOD_ROWS = 8
MOD_TN = 1024


def _mod_kernel(cctx_ref, c_ref, w_ref, b_ref, o_ref, c_scr):
    c_scr[...] = jnp.zeros(c_scr.shape, F32)
    c_scr[0:1, :] = cctx_ref[...]
    c_scr[1:1 + DEC_BATCH, :] = c_ref[...]
    s = _silu(c_scr[...]).astype(BF16)
    layer = pl.program_id(0)
    o_ref[0] = jnp.dot(s, w_ref[0].astype(BF16), preferred_element_type=F32) + b_ref[pl.ds(layer, 1), :]


def _modulation(c_ctx, c, w_ada, b_ada):
    n = 3 * D_MODEL
    return pl.pallas_call(
        _mod_kernel,
        grid=(DEPTH, n // MOD_TN),
        in_specs=[
            pl.BlockSpec((1, D_MODEL), lambda l, j: (0, 0)),
            pl.BlockSpec((DEC_BATCH, D_MODEL), lambda l, j: (0, 0)),
            pl.BlockSpec((1, D_MODEL, MOD_TN), lambda l, j: (l, 0, j)),
            pl.BlockSpec((DEPTH, MOD_TN), lambda l, j: (0, j)),
        ],
        out_specs=pl.BlockSpec((1, MOD_ROWS, MOD_TN), lambda l, j: (l, 0, j)),
        out_shape=jax.ShapeDtypeStruct((DEPTH, MOD_ROWS, n), F32),
        scratch_shapes=[pltpu.VMEM((MOD_ROWS, D_MODEL), F32)],
        compiler_params=_params("arbitrary", "arbitrary"),
        name="adaln_mod",
    )(c_ctx.reshape(1, D_MODEL), c, w_ada, b_ada)


M_CTX = BATCH * SEQ
M_LAT = DEC_BATCH * DEC_SEQ
M_ALL = M_CTX + M_LAT
NORM_TM = 1024
INPROJ_TM = 3072
INPROJ_TN = 512


ROW_CHUNK = 32


def _modulated_norm_rows(x_ref, h_ref, g, shift, scale):
    gain = g * (1.0 + scale)

    def body(r, _):
        rows = pl.ds(pl.multiple_of(r * ROW_CHUNK, ROW_CHUNK), ROW_CHUNK)
        x = x_ref[rows, :]
        inv = lax.rsqrt(jnp.mean(x * x, axis=-1, keepdims=True) + EPS)
        h_ref[rows, :] = (x * inv * gain + shift).astype(BF16)
        return 0

    lax.fori_loop(0, x_ref.shape[0] // ROW_CHUNK, body, 0, unroll=4)


MOD_SHIFT, MOD_SCALE, MOD_GATE = 0, 1, 2


def _mod_spec(layer, part):
    return pl.BlockSpec((1, MOD_ROWS, D_MODEL), lambda i: (layer, 0, part))


def _mod_row(ref, tile, tm):
    nc = M_CTX // tm
    per_batch = DEC_SEQ // tm
    row = jnp.where(tile < nc, 0, 1 + (tile - nc) // per_batch)
    return ref[0, pl.ds(row, 1), :]


def _two_stream_specs(tm, width):
    nc = M_CTX // tm
    return (pl.BlockSpec((tm, width), lambda i: (jnp.minimum(i, nc - 1), 0)),
            pl.BlockSpec((tm, width), lambda i: (jnp.maximum(i - nc, 0), 0)))


def _norm_kernel(layer, xc_ref, xl_ref, g_ref, shift_ref, scale_ref, h_ref):
    i = pl.program_id(0)

    def emit(x_ref):
        _modulated_norm_rows(x_ref, h_ref, g_ref[layer:layer + 1, :], _mod_row(shift_ref, i, NORM_TM),
                             _mod_row(scale_ref, i, NORM_TM))

    pl.when(i < M_CTX // NORM_TM)(lambda: emit(xc_ref))
    pl.when(i >= M_CTX // NORM_TM)(lambda: emit(xl_ref))


def _norm_mod(x_ctx, x_lat, norm_g, mod, layer):
    xc_spec, xl_spec = _two_stream_specs(NORM_TM, D_MODEL)
    return pl.pallas_call(
        functools.partial(_norm_kernel, layer),
        grid=(M_ALL // NORM_TM,),
        in_specs=[
            xc_spec, xl_spec,
            pl.BlockSpec((DEPTH, D_MODEL), lambda i: (0, 0)),
            _mod_spec(layer, MOD_SHIFT),
            _mod_spec(layer, MOD_SCALE),
        ],
        out_specs=pl.BlockSpec((NORM_TM, D_MODEL), lambda i: (i, 0)),
        out_shape=jax.ShapeDtypeStruct((M_ALL, D_MODEL), BF16),
        compiler_params=_params("arbitrary"),
        name="norm_mod",
    )(x_ctx, x_lat, norm_g, mod, mod)


def _inproj_kernel(h_hbm, w_ref, o_ref, w_scr, h_scr, h_sems):
    j, i = pl.program_id(0), pl.program_id(1)
    n_chunks = M_ALL // INPROJ_TM

    def h_copy(c):
        rows = pl.ds(c * INPROJ_TM, INPROJ_TM)
        return pltpu.make_async_copy(h_hbm.at[rows], h_scr.at[rows], h_sems.at[c])

    @pl.when((j == 0) & (i == 0))
    def _():
        for c in range(n_chunks):
            h_copy(c).start()

    @pl.when(i == 0)
    def _():
        w_scr[...] = w_ref[0].astype(BF16)

    for c in range(n_chunks):
        @pl.when((j == 0) & (i == c))
        def _():
            h_copy(c).wait()

    rows = pl.ds(pl.multiple_of(i * INPROJ_TM, INPROJ_TM), INPROJ_TM)
    o_ref[...] = jnp.dot(h_scr[rows, :], w_scr[...], preferred_element_type=F32)


def _in_proj(h, w_in, layer):
    return pl.pallas_call(
        _inproj_kernel,
        grid=(D_IN // INPROJ_TN, M_ALL // INPROJ_TM),
        in_specs=[
            pl.BlockSpec(memory_space=pl.ANY),
            pl.BlockSpec((1, D_MODEL, INPROJ_TN), lambda j, i: (layer, 0, j)),
        ],
        out_specs=pl.BlockSpec((INPROJ_TM, INPROJ_TN), lambda j, i: (i, j)),
        out_shape=jax.ShapeDtypeStruct((M_ALL, D_IN), F32),
        scratch_shapes=[pltpu.VMEM((D_MODEL, INPROJ_TN), BF16), pltpu.VMEM((M_ALL, D_MODEL), BF16),
                        pltpu.SemaphoreType.DMA((M_ALL // INPROJ_TM,))],
        compiler_params=_params("arbitrary", "arbitrary"),
        name="in_proj",
    )(h, w_in)


ATT_SCALE = 1.0 / math.sqrt(HEAD_DIM)


def _head(ref_or_val, h):
    return ref_or_val[:, h * HEAD_DIM:(h + 1) * HEAD_DIM]


Q_PRESCALE = ATT_SCALE * math.log2(math.e)


def _softmax_pv(q2, k_bf, v_ext):
    s = lax.dot_general(q2, k_bf, (((1,), (1,)), ((), ())), preferred_element_type=F32)
    p = jnp.exp2(s - jnp.max(s, axis=-1, keepdims=True)).astype(BF16)
    o = jnp.dot(p, v_ext, preferred_element_type=F32)
    return o[:, 0:HEAD_DIM] / o[:, HEAD_DIM:2 * HEAD_DIM]


def _rope(x, cos2, sin_a, sin_b):
    nxt = pltpu.roll(x, HEAD_DIM - 1, axis=1)
    prv = pltpu.roll(x, 1, axis=1)
    return x * cos2 + nxt * sin_a + prv * sin_b


CTX_TQ = SEQ
LAT_TQ = 128


def _attn_kernel(rope, n_cache, layer, emit_kv, n_aliased, nsb, *refs):
    it = iter(refs)
    q_all, kv_all, g_all, qg_ref, kg_ref = (next(it) for _ in range(5))
    ck_ref, cv_ref = (next(it), next(it)) if n_cache else (None, None)
    q_tabs = tuple(next(it) for _ in range(3)) if rope else None
    k_tabs = tuple(next(it) for _ in range(3)) if rope else None
    for _ in range(n_aliased):
        next(it)
    att_all = next(it)
    ko_hbm, vo_hbm = (next(it), next(it)) if emit_kv else (None, None)
    k_all, v_all = next(it), next(it)
    kf_all, kv_sems = (next(it), next(it)) if emit_kv else (None, None)
    single_tile = q_all.shape[0] == kv_all.shape[0]
    n_new = kv_all.shape[0] // nsb if single_tile else kv_all.shape[0]
    assert not emit_kv or single_tile

    def part(ref, sb, rows_per_seq):
        return ref if nsb == 1 else ref.at[pl.ds(sb * rows_per_seq, rows_per_seq)]

    def kv_copy(sb, h, is_value):
        b = pl.program_id(0) * nsb + sb
        if is_value:
            src = part(kv_all, sb, n_new).at[:, pl.ds((N_KV_HEADS + h) * HEAD_DIM, HEAD_DIM)]
            return pltpu.make_async_copy(src, vo_hbm.at[b, layer, :, h, :],
                                         kv_sems.at[sb, N_KV_HEADS + h])
        src = part(kf_all, sb, n_new).at[:, pl.ds(h * HEAD_DIM, HEAD_DIM)]
        return pltpu.make_async_copy(src, ko_hbm.at[b, layer, :, h, :], kv_sems.at[sb, h])

    def prepare_keys_values(sb):
        kv_ref, k_scr, v_scr = part(kv_all, sb, n_new), part(k_all, sb, n_new + n_cache), part(v_all, sb, n_new + n_cache)
        kf_scr = part(kf_all, sb, n_new) if emit_kv else None
        for h in range(N_KV_HEADS):
            kn = _rms(_head(kv_ref, h), kg_ref[layer:layer + 1, :])
            vh = _head(kv_ref, N_KV_HEADS + h)
            if emit_kv:
                kf_scr[:, h * HEAD_DIM:(h + 1) * HEAD_DIM] = kn
            if rope:
                kn = _rope(kn, *(t[...] for t in k_tabs))
            k_scr[0:n_new, h * HEAD_DIM:(h + 1) * HEAD_DIM] = kn.astype(BF16)
            v0 = 2 * h * HEAD_DIM
            v_scr[0:n_new, v0:v0 + HEAD_DIM] = vh.astype(BF16)
            if n_cache:
                v_scr[n_new:n_new + n_cache, v0:v0 + HEAD_DIM] = _head(cv_ref[0, 0], h).astype(BF16)
            v_scr[:, v0 + HEAD_DIM:v0 + 2 * HEAD_DIM] = jnp.ones((n_new + n_cache, HEAD_DIM), BF16)
        if n_cache:
            k_scr[n_new:n_new + n_cache, :] = ck_ref[0, 0].astype(BF16)

    def attend(sb):
        rows = q_all.shape[0] // nsb
        q_ref, g_ref, att_ref = part(q_all, sb, rows), part(g_all, sb, rows), part(att_all, sb, rows)
        kv_sb = sb if single_tile else 0
        k_scr, v_scr = part(k_all, kv_sb, n_new + n_cache), part(v_all, kv_sb, n_new + n_cache)

        def query(hq):
            x = _rms(_head(q_ref, hq), qg_ref[layer:layer + 1, :])
            if rope:
                x = _rope(x, *(part(t, sb, rows)[...] for t in q_tabs))
            return x * Q_PRESCALE

        for h in range(N_KV_HEADS):
            q2 = jnp.concatenate([query(GROUP * h + g) for g in range(GROUP)], axis=0).astype(BF16)
            o = _softmax_pv(q2, _head(k_scr, h), v_scr[:, 2 * h * HEAD_DIM:2 * (h + 1) * HEAD_DIM])
            for g in range(GROUP):
                hq = GROUP * h + g
                gate = _silu(_head(g_ref, hq))
                att_ref[:, hq * HEAD_DIM:(hq + 1) * HEAD_DIM] = (o[g * rows:(g + 1) * rows] * gate).astype(BF16)

    every = [(sb, h) for sb in range(nsb) for h in range(N_KV_HEADS)]
    if single_tile:
        if emit_kv:
            for sb, h in every:
                kv_copy(sb, h, True).start()
        for sb in range(nsb):
            prepare_keys_values(sb)
        if emit_kv:
            for sb, h in every:
                kv_copy(sb, h, False).start()
    else:
        pl.when(pl.program_id(1) == 0)(lambda: prepare_keys_values(0))

    for sb in range(nsb):
        attend(sb)

    if emit_kv:
        for sb, h in every:
            kv_copy(sb, h, False).wait()
            kv_copy(sb, h, True).wait()


def _attention(proj, q_g, k_g, layer, seq, tq, n_batch, row0, *, nsb=1, cache=None, rope_tabs=None,
               att_prev=None, kv_prev=None, emit_kv=False):
    whole = tq == seq
    assert not whole or cache is None
    tq = nsb * tq
    kv_rows = nsb * seq if whole else seq
    nq = kv_rows // tq
    q0, kv0 = row0 // tq, row0 // kv_rows
    n_cache = 0 if cache is None else cache[0].shape[2]
    vec = pl.BlockSpec((DEPTH, HEAD_DIM), lambda b, i: (0, 0))
    in_specs = [
        pl.BlockSpec((tq, D_ATT), lambda b, i: (q0 + b * nq + i, COL_Q // D_ATT)),
        pl.BlockSpec((kv_rows, 2 * D_KV), lambda b, i: (kv0 + b, COL_K // (2 * D_KV))),
        pl.BlockSpec((tq, D_ATT), lambda b, i: (q0 + b * nq + i, COL_GA // D_ATT)),
        vec, vec,
    ]
    args = [proj, proj, proj, q_g, k_g]
    if cache is not None:
        in_specs += [pl.BlockSpec((1, 1, n_cache, D_KV), lambda b, i: (b, layer, 0, 0))] * 2
        args += list(cache)
    if rope_tabs is not None:
        in_specs += [pl.BlockSpec((tq, HEAD_DIM), lambda b, i: (i, 0))] * 3
        in_specs += [pl.BlockSpec((seq, HEAD_DIM), lambda b, i: (0, 0))] * 3
        args += list(rope_tabs) * 2
    aliased = ([] if att_prev is None else [att_prev]) + ([] if kv_prev is None else list(kv_prev))
    out_first = 0 if att_prev is not None else 1
    aliases = {len(args) + n: out_first + n for n in range(len(aliased))}
    in_specs += [pl.BlockSpec(memory_space=pl.ANY)] * len(aliased)
    args += aliased
    out_specs = [pl.BlockSpec((tq, D_ATT), lambda b, i: (q0 + b * nq + i, 0))]
    out_shape = [jax.ShapeDtypeStruct((M_ALL, D_ATT), BF16)]
    keys = kv_rows + n_cache
    scratch =[pltpu.VMEM((keys, D_KV), BF16), pltpu.VMEM((keys, 2 * D_KV), BF16)]
    if emit_kv:
        out_specs += [pl.BlockSpec(memory_space=pl.ANY)] * 2
        out_shape += [jax.ShapeDtypeStruct((n_batch, DEPTH, seq, N_KV_HEADS, HEAD_DIM), F32)] * 2
        scratch += [pltpu.VMEM((kv_rows, D_KV), F32), pltpu.SemaphoreType.DMA((nsb, 2 * N_KV_HEADS))]
    return pl.pallas_call(
        functools.partial(_attn_kernel, rope_tabs is not None, n_cache, layer, emit_kv, len(aliased), nsb),
        grid=(n_batch * seq // kv_rows, nq),
        in_specs=in_specs,
        out_specs=out_specs,
        out_shape=out_shape,
        input_output_aliases=aliases,
        scratch_shapes=scratch,
        compiler_params=_params("arbitrary", "arbitrary"),
        name=f"attention_{seq}",
    )(*args)


def _dft_matrix(L):
    k = np.arange(L, dtype=np.int64)[:, None]
    t = np.arange(L, dtype=np.int64)[None, :]
    ang = 2.0 * np.pi * ((k * t) % (2 * L)).astype(np.float64) / (2 * L)
    top = np.cos(ang)
    bot = np.sin(ang)
    bot[0, :] = np.where(np.arange(L) % 2 == 0, 1.0, -1.0)
    return np.concatenate([top, bot], axis=0).astype(np.float32)


def _filter_kernel(L, z_ref, w1_ref, b1_ref, w2_ref, b2_ref, fr_ref, w3f_ref, w3b_ref, dec_ref,
                   f_ref, o_ref, hdn_scr, w1_scr, w2_scr, vec_scr, w3_scr):
    hp = lax.Precision.HIGHEST
    layer = pl.program_id(0)

    halves = ((0, 0), (FEAT_PAD, FILT_HID))

    @pl.when(pl.program_id(1) == 0)
    def _():
        w1_scr[...] = jnp.zeros(w1_scr.shape, F32)
        w2_scr[...] = jnp.zeros(w2_scr.shape, F32)
        vec_scr[...] = jnp.zeros(vec_scr.shape, F32)
        w3_scr[...] = jnp.zeros(w3_scr.shape, BF16)
        for feat0, hid0 in halves:
            w1_scr[feat0:feat0 + POS_EMB, hid0:hid0 + FILT_HID] = w1_ref[0]
            w2_scr[hid0:hid0 + FILT_HID, hid0:hid0 + FILT_HID] = w2_ref[0]
            for r, ref in enumerate((b1_ref, b2_ref, fr_ref)):
                vec_scr[r:r + 1, hid0:hid0 + FILT_HID] = ref[pl.ds(layer, 1), :]
        b1, b2, fr = vec_scr[0:1, :], vec_scr[1:2, :], vec_scr[2:3, :]
        h1 = jnp.sin(fr * (jnp.dot(z_ref[...], w1_scr[...], precision=hp, preferred_element_type=F32) + b1))
        hdn_scr[...] = jnp.sin(fr * (jnp.dot(h1, w2_scr[...], precision=hp, preferred_element_type=F32) + b2))

    hdn_bf = hdn_scr[...].astype(BF16)
    dec = dec_ref[...]

    def project(w3_ref, slot):
        parts = []
        for k, (_, hid0) in enumerate(halves):
            w3_scr[slot + k, hid0:hid0 + FILT_HID, :] = w3_ref[0].astype(BF16)
            parts.append(jnp.dot(hdn_bf, w3_scr[slot + k], preferred_element_type=F32))
        return jnp.concatenate(parts, axis=0) * dec

    h_f = project(w3f_ref, 0)
    h_b = project(w3b_ref, 2)
    hs = h_f + h_b
    ha = jnp.dot(f_ref[0:L, :], hs.astype(BF16), preferred_element_type=F32)
    hb = jnp.dot(f_ref[L:2 * L, :], (h_f - h_b).astype(BF16), preferred_element_type=F32)
    row = lax.broadcasted_iota(jnp.int32, (L, 1), 0)
    first = row == 0
    nyquist = jnp.sum(jnp.where(row % 2 == 0, hs, -hs), axis=0, keepdims=True)
    wk = jnp.where(first, 1.0 / (2 * L), 2.0 / (2 * L))
    o_ref[0, 0] = ha * wk
    o_ref[0, 1] = jnp.where(first, 0.0, hb * wk)
    o_ref[0, 2] = jnp.where(first, nyquist, ha) * wk


FILT_CB = 512


def _filter_spectra(L, w1, b1, w2, b2, freq, w3, f_bf):
    tpos = np.arange(L, dtype=np.float64)
    t_norm = tpos / max(L - 1, 1)
    w = 2.0 * math.pi * tpos / L
    bands = np.linspace(1e-4, POS_BANDS - 1, POS_BANDS)
    z = np.concatenate([t_norm[:, None], np.cos(w[:, None] * bands), -np.sin(w[:, None] * bands)], axis=-1)
    z = np.pad(z, ((0, 0), (0, FEAT_PAD - POS_EMB))).astype(np.float32)
    z = np.concatenate([z[:L // 2], z[L // 2:]], axis=1)
    max_decay = math.log(DECAY_TARGET) / FAST_DECAY_PCT
    min_decay = math.log(DECAY_TARGET) / SLOW_DECAY_PCT
    deltas = np.abs(np.linspace(min_decay, max_decay, D_HY))
    dec = (np.exp(-t_norm[:, None] * deltas) + DECAY_SHIFT).astype(np.float32)

    ncb = D_HY // FILT_CB
    matrix = lambda rows, cols: pl.BlockSpec((1, rows, cols), lambda l, c: (l, 0, 0))
    per_layer_vec = pl.BlockSpec((DEPTH, FILT_HID), lambda l, c: (0, 0))
    return pl.pallas_call(
        functools.partial(_filter_kernel, L),
        grid=(DEPTH, ncb),
        in_specs=[
            pl.BlockSpec((L // 2, 2 * FEAT_PAD), lambda l, c: (0, 0)),
            matrix(POS_EMB, FILT_HID), per_layer_vec,
            matrix(FILT_HID, FILT_HID), per_layer_vec, per_layer_vec,
            pl.BlockSpec((1, FILT_HID, FILT_CB), lambda l, c: (l, 0, c)),
            pl.BlockSpec((1, FILT_HID, FILT_CB), lambda l, c: (l, 0, ncb + c)),
            pl.BlockSpec((L, FILT_CB), lambda l, c: (0, c)),
            pl.BlockSpec((2 * L, L), lambda l, c: (0, 0)),
        ],
        out_specs=pl.BlockSpec((1, 3, L, FILT_CB), lambda l, c: (l, 0, 0, c)),
        out_shape=jax.ShapeDtypeStruct((DEPTH, 3, L, D_HY), F32),
        scratch_shapes=[pltpu.VMEM((L // 2, FEAT_PAD), F32), pltpu.VMEM((2 * FEAT_PAD, FEAT_PAD), F32),
                        pltpu.VMEM((FEAT_PAD, FEAT_PAD), F32), pltpu.VMEM((8, FEAT_PAD), F32),
                        pltpu.VMEM((4, FEAT_PAD, FILT_CB), BF16)],
        compiler_params=_params("arbitrary", "arbitrary"),
        name=f"hyena_filter_{L}",
    )(jnp.asarray(z), w1, b1, w2, b2, freq, w3, w3, jnp.asarray(dec), f_bf)


def _hyena_kernel(L, layer, sub, x0_ref, x1_ref, vv_ref, g_ref, cw0_ref, cw1_ref, cwv_ref, cb0_ref, cb1_ref,
                  cbv_ref, bias_ref, coef_ref, f_ref, ft_ref, *rest):
    o_ref = rest[-1]
    this_layer = slice(layer, layer + 1)
    row = lax.broadcasted_iota(jnp.int32, (L, 1), 0)
    is_first = row == 0
    is_last = row == L - 1

    for c in range(o_ref.shape[1] // sub):
        cols = slice(c * sub, (c + 1) * sub)

        def sconv(x_ref, w_ref, b_ref):
            x = x_ref[:, cols]
            prev = jnp.where(is_first, 0.0, pltpu.roll(x, 1, axis=0))
            nxt = jnp.where(is_last, 0.0, pltpu.roll(x, L - 1, axis=0))
            return (w_ref[0, 0:1, cols] * prev + w_ref[0, 1:2, cols] * x + w_ref[0, 2:3, cols] * nxt
                    + b_ref[this_layer, cols])

        z = sconv(vv_ref, cwv_ref, cbv_ref) * sconv(x1_ref, cw1_ref, cb1_ref)
        ab = jnp.dot(f_ref[...], z.astype(BF16), preferred_element_type=F32)
        a, b = ab[0:L], ab[L:2 * L]
        g1, g2, g3 = coef_ref[0, 0, :, cols], coef_ref[0, 1, :, cols], coef_ref[0, 2, :, cols]
        pq = jnp.concatenate([a * g1 - b * g2, a * g2 + b * g3], axis=0).astype(BF16)
        y = jnp.dot(ft_ref[...], pq, preferred_element_type=F32) + bias_ref[this_layer, cols] * z
        o_ref[:, cols] = (sconv(x0_ref, cw0_ref, cb0_ref) * y * _silu(g_ref[:, cols])).astype(BF16)


def _hyena(proj, L, nb, row0, cb, sub, layer, conv_w, conv_b, hy_bias, coef, f_bf, ft_bf, hy_prev=None):
    ncb = D_HY // cb
    rows = L
    blk0 = row0 // rows
    extra_specs = [] if hy_prev is None else [pl.BlockSpec(memory_space=pl.ANY)]
    extra_args = [] if hy_prev is None else [hy_prev]
    aliases = {} if hy_prev is None else {14: 0}
    col = lambda off: (lambda c, b: (blk0 + b, off // cb + c))
    cw = lambda part: pl.BlockSpec((1, 3, cb), lambda c, b: (layer, 0, part * ncb + c))
    cbias = lambda part: pl.BlockSpec((DEPTH, cb), lambda c, b: (0, part * ncb + c))
    once = pl.Buffered(1)
    return pl.pallas_call(
        functools.partial(_hyena_kernel, L, layer, sub),
        grid=(ncb, nb),
        in_specs=[
            pl.BlockSpec((rows, cb), col(COL_X0)),
            pl.BlockSpec((rows, cb), col(COL_X1)),
            pl.BlockSpec((rows, cb), col(COL_VV)),
            pl.BlockSpec((rows, cb), col(COL_GH)),
            cw(0), cw(1), cw(2), cbias(0), cbias(1), cbias(2),
            pl.BlockSpec((DEPTH, cb), lambda c, b: (0, c)),
            pl.BlockSpec((1, 3, L, cb), lambda c, b: (layer, 0, 0, c)),
            pl.BlockSpec((2 * L, L), lambda c, b: (0, 0), pipeline_mode=once),
            pl.BlockSpec((L, 2 * L), lambda c, b: (0, 0), pipeline_mode=once),
            *extra_specs,
        ],
        out_specs=pl.BlockSpec((rows, cb), lambda c, b: (blk0 + b, c)),
        out_shape=jax.ShapeDtypeStruct((M_ALL, D_HY), BF16),
        input_output_aliases=aliases,
        compiler_params=_params("arbitrary", "arbitrary"),
        name=f"hyena_{L}",
    )(proj, proj, proj, proj, conv_w, conv_w, conv_w, conv_b, conv_b, conv_b, hy_bias, coef, f_bf, ft_bf,
      *extra_args)


OUTPROJ_TM = 256
OUTPROJ_TM_FINAL = 512


def _outproj_kernel(split_in, final, tm, tile0, gain_row, *refs):
    it = iter(refs)
    att_ref, hy_ref, w_ref = next(it), next(it), next(it)
    x_refs = (next(it), next(it)) if split_in else (next(it),)
    gate_ref, g_ref = next(it), next(it)
    shift_ref, scale_ref = (None, None) if final else (next(it), next(it))
    out_refs = (next(it),) if final else (next(it), next(it))
    w_scr = next(it)
    i = pl.program_id(0)

    @pl.when(i == 0)
    def _():
        w_scr[...] = w_ref[0].astype(BF16)

    out = (jnp.dot(att_ref[...], w_scr[0:D_ATT, :], preferred_element_type=F32)
           + jnp.dot(hy_ref[...], w_scr[D_ATT:D_ATT + D_HY, :], preferred_element_type=F32))
    tile = tile0 + i
    if split_in:
        x = jnp.where(tile < M_CTX // tm, x_refs[0][...], x_refs[1][...])
    else:
        x = x_refs[0][...]
    y = x + _mod_row(gate_ref, tile, tm) * out
    inv = lax.rsqrt(jnp.mean(y * y, axis=-1, keepdims=True) + EPS)
    g = g_ref[gain_row:gain_row + 1, :]
    if final:
        out_refs[0][...] = y * inv * g
    else:
        out_refs[0][...] = y
        gain = g * (1.0 + _mod_row(scale_ref, tile, tm))
        out_refs[1][...] = (y * inv * gain + _mod_row(shift_ref, tile, tm)).astype(BF16)


def _out_proj(att, hy, w_out, layer, x_parts, mod, gains, final, tm, tile0=0, n_tiles=None):
    n_tiles = M_ALL // tm if n_tiles is None else n_tiles
    split_in = len(x_parts) == 2
    assert not split_in or (tile0 == 0 and n_tiles == M_ALL // tm)
    row_tile = lambda width: pl.BlockSpec((tm, width), lambda i: (tile0 + i, 0))
    out_tile = pl.BlockSpec((tm, D_MODEL), lambda i: (i, 0))
    x_specs = list(_two_stream_specs(tm, D_MODEL)) if split_in else [row_tile(D_MODEL)]
    in_specs = [
        row_tile(D_ATT),
        row_tile(D_HY),
        pl.BlockSpec((1, D_ATT + D_HY, D_MODEL), lambda i: (layer, 0, 0), pipeline_mode=pl.Buffered(1)),
        *x_specs,
        _mod_spec(layer, MOD_GATE),
        pl.BlockSpec(gains.shape, lambda i: (0, 0)),
    ]
    args = [att, hy, w_out, *x_parts, mod, gains]
    if final:
        out_specs = [out_tile]
        out_shape = [jax.ShapeDtypeStruct((n_tiles * tm, D_MODEL), F32)]
    else:
        in_specs += [_mod_spec(layer + 1, MOD_SHIFT), _mod_spec(layer + 1, MOD_SCALE)]
        args += [mod, mod]
        out_specs = [out_tile, out_tile]
        out_shape = [jax.ShapeDtypeStruct((n_tiles * tm, D_MODEL), F32),
                     jax.ShapeDtypeStruct((n_tiles * tm, D_MODEL), BF16)]
    return pl.pallas_call(
        functools.partial(_outproj_kernel, split_in, final, tm, tile0, 0 if final else layer + 1),
        grid=(n_tiles,),
        in_specs=in_specs,
        out_specs=out_specs,
        out_shape=out_shape,
        scratch_shapes=[pltpu.VMEM((D_ATT + D_HY, D_MODEL), BF16)],
        compiler_params=_params("arbitrary"),
        name="out_proj_final" if final else "out_proj",
    )(*args)


def _rope_tables():
    t = np.arange(DEC_SEQ)
    row = (t // GRID_W).astype(np.float64)
    col = (t % GRID_W).astype(np.float64)
    pairs = HEAD_DIM // 4
    inv_freq = ROPE_THETA ** (-np.arange(pairs, dtype=np.float64) / pairs)
    ang = np.concatenate([row[:, None] * inv_freq, col[:, None] * inv_freq], axis=-1)
    cos = np.repeat(np.cos(ang), 2, axis=-1).astype(np.float32)
    sin = np.repeat(np.sin(ang), 2, axis=-1).astype(np.float32)
    even = (np.arange(HEAD_DIM) % 2 == 0)[None, :]
    sin_a = np.where(even, -sin, 0.0).astype(np.float32)
    sin_b = np.where(even, 0.0, sin).astype(np.float32)
    return jnp.asarray(cos), jnp.asarray(sin_a), jnp.asarray(sin_b)


def kernel(x_prompt, x_sample, cache_k, cache_v, c, c_ctx, norm_g, w_ada, b_ada, w_in, q_norm_g, k_norm_g,
           conv_w, conv_b, filt_w1, filt_b1, filt_w2, filt_b2, filt_w3, filt_freq, hy_bias, w_out, final_norm_g):
    ctx = x_prompt.reshape(BATCH * SEQ, D_MODEL)
    lat = x_sample.reshape(DEC_BATCH * DEC_SEQ, D_MODEL)
    cache_k4 = cache_k.reshape(DEC_BATCH, DEPTH, PAST_LEN, D_KV)
    cache_v4 = cache_v.reshape(DEC_BATCH, DEPTH, PAST_LEN, D_KV)

    mod = _modulation(c_ctx, c, w_ada, b_ada)

    rope_tabs = _rope_tables()
    dft = {}
    coefs = {}
    for L in (SEQ, DEC_SEQ):
        f_np = _dft_matrix(L)
        f_bf = jnp.asarray(f_np).astype(BF16)
        dft[L] = (f_bf, jnp.asarray(np.ascontiguousarray(f_np.T)).astype(BF16))
        coefs[L] = _filter_spectra(L, filt_w1, filt_b1, filt_w2, filt_b2, filt_freq, filt_w3, f_bf)

    final_g = final_norm_g.reshape(1, D_MODEL)

    x_parts = (ctx, lat)
    h = _norm_mod(ctx, lat, norm_g, mod, 0)
    kv_out = None
    for l in range(DEPTH):
        final = l == DEPTH - 1

        proj = _in_proj(h, w_in, l)

        att, new_k, new_v = _attention(proj, q_norm_g, k_norm_g, l, SEQ, CTX_TQ, BATCH, 0, nsb=4, kv_prev=kv_out,
                                       emit_kv=True)
        kv_out = (new_k, new_v)
        (att,) = _attention(proj, q_norm_g, k_norm_g, l, DEC_SEQ, LAT_TQ, DEC_BATCH, M_CTX, nsb=4,
                            cache=(cache_k4, cache_v4), rope_tabs=rope_tabs, att_prev=att)

        hy = _hyena(proj, SEQ, BATCH, 0, D_HY, 256, l, conv_w, conv_b, hy_bias, coefs[SEQ], *dft[SEQ])
        hy = _hyena(proj, DEC_SEQ, DEC_BATCH, M_CTX, 512, 512, l, conv_w, conv_b, hy_bias,
                    coefs[DEC_SEQ], *dft[DEC_SEQ], hy_prev=hy)

        if final:
            assert len(x_parts) == 1
            tm = OUTPROJ_TM_FINAL
            nc = M_CTX // tm
            (y_ctx,) = _out_proj(att, hy, w_out, l, x_parts, mod, final_g, True, tm, 0, nc)
            (y_lat,) = _out_proj(att, hy, w_out, l, x_parts, mod, final_g, True, tm, nc, M_LAT // tm)
        else:
            y, h = _out_proj(att, hy, w_out, l, x_parts, mod, norm_g, False, OUTPROJ_TM)
            x_parts = (y,)

    y_prompt = y_ctx.reshape(BATCH, SEQ, D_MODEL)
    y_sample = y_lat.reshape(DEC_BATCH, DEC_SEQ, D_MODEL)
    return (y_prompt, y_sample, kv_out[0], kv_out[1])
```

```python
import functools
import math

import numpy as np
import jax
import jax.numpy as jnp
from jax import lax
from jax.experimental import pallas as pl
from jax.experimental.pallas import tpu as pltpu

D_MODEL = 2048
BATCH = 16
SEQ = 256
DEPTH = 2
DEC_BATCH = 2
DEC_SEQ = 1024
PAST_LEN = 512
GRID_W = 64
D_ATT = 1024
D_HY = 1024
HEAD_DIM = 128
N_HEADS = 8
N_KV_HEADS = 4
GROUP = 2
D_KV = 512
ROPE_THETA = 10000.0
POS_BANDS = 16
POS_EMB = 33
FILT_HID = 64
DECAY_TARGET = 1e-2
FAST_DECAY_PCT = 0.3
SLOW_DECAY_PCT = 1.5
DECAY_SHIFT = 0.05
EPS = 1e-6
D_IN = 7168

COL_Q, COL_K, COL_V, COL_GA, COL_X0, COL_X1, COL_VV, COL_GH = 0, 1024, 1536, 2048, 3072, 4096, 5120, 6144

F32 = jnp.float32
BF16 = jnp.bfloat16

VMEM_LIMIT_BYTES = 56 * 1024 * 1024
FEAT_PAD = 128


def _params(*sem):
    return pltpu.CompilerParams(dimension_semantics=sem, vmem_limit_bytes=VMEM_LIMIT_BYTES)


def _silu(x):
    half = 0.5 * x
    return half + half * jnp.tanh(half)


def _rms(x, g):
    return x * lax.rsqrt(jnp.mean(x * x, axis=-1, keepdims=True) + EPS) * g


MOD_ROWS = 8
MOD_TN = 1024
MOD_TILES = 3 * D_MODEL // MOD_TN


def _mod_tile(layer, cctx_ref, c_ref, w_ref, b_ref, o_ref, c_scr):
    c_scr[...] = jnp.zeros(c_scr.shape, F32)
    c_scr[0:1, :] = cctx_ref[...]
    c_scr[1:1 + DEC_BATCH, :] = c_ref[...]
    s = _silu(c_scr[...]).astype(BF16)
    o_ref[0] = jnp.dot(s, w_ref[0].astype(BF16), preferred_element_type=F32) + b_ref[pl.ds(layer, 1), :]


M_CTX = BATCH * SEQ
M_LAT = DEC_BATCH * DEC_SEQ
M_ALL = M_CTX + M_LAT
NORM_TM = 1024
INPROJ_TM = 3072
INPROJ_TN = 512


ROW_CHUNK = 32


def _modulated_norm_rows(x_ref, h_ref, g, shift, scale):
    gain = g * (1.0 + scale)

    def body(r, _):
        rows = pl.ds(pl.multiple_of(r * ROW_CHUNK, ROW_CHUNK), ROW_CHUNK)
        x = x_ref[rows, :]
        inv = lax.rsqrt(jnp.mean(x * x, axis=-1, keepdims=True) + EPS)
        h_ref[rows, :] = (x * inv * gain + shift).astype(BF16)
        return 0

    lax.fori_loop(0, x_ref.shape[0] // ROW_CHUNK, body, 0, unroll=4)


MOD_SHIFT, MOD_SCALE, MOD_GATE = 0, 1, 2


def _mod_spec(layer, part):
    return pl.BlockSpec((1, MOD_ROWS, D_MODEL), lambda i: (layer, 0, part))


def _mod_row(ref, tile, tm):
    nc = M_CTX // tm
    per_batch = DEC_SEQ // tm
    row = jnp.where(tile < nc, 0, 1 + (tile - nc) // per_batch)
    return ref[0, pl.ds(row, 1), :]


def _two_stream_specs(tm, width):
    nc = M_CTX // tm
    return (pl.BlockSpec((tm, width), lambda i: (jnp.minimum(i, nc - 1), 0)),
            pl.BlockSpec((tm, width), lambda i: (jnp.maximum(i - nc, 0), 0)))


def _norm_kernel(layer, xc_ref, xl_ref, g_ref, shift_ref, scale_ref, h_ref):
    i = pl.program_id(0)

    def emit(x_ref):
        _modulated_norm_rows(x_ref, h_ref, g_ref[layer:layer + 1, :], _mod_row(shift_ref, i, NORM_TM),
                             _mod_row(scale_ref, i, NORM_TM))

    pl.when(i < M_CTX // NORM_TM)(lambda: emit(xc_ref))
    pl.when(i >= M_CTX // NORM_TM)(lambda: emit(xl_ref))


def _norm_mod(x_ctx, x_lat, norm_g, mod, layer):
    xc_spec, xl_spec = _two_stream_specs(NORM_TM, D_MODEL)
    return pl.pallas_call(
        functools.partial(_norm_kernel, layer),
        grid=(M_ALL // NORM_TM,),
        in_specs=[
            xc_spec, xl_spec,
            pl.BlockSpec((DEPTH, D_MODEL), lambda i: (0, 0)),
            _mod_spec(layer, MOD_SHIFT),
            _mod_spec(layer, MOD_SCALE),
        ],
        out_specs=pl.BlockSpec((NORM_TM, D_MODEL), lambda i: (i, 0)),
        out_shape=jax.ShapeDtypeStruct((M_ALL, D_MODEL), BF16),
        compiler_params=_params("arbitrary"),
        name="norm_mod",
    )(x_ctx, x_lat, norm_g, mod, mod)


def _inproj_kernel(h_hbm, w_ref, o_ref, w_scr, h_scr, h_sems):
    j, i = pl.program_id(0), pl.program_id(1)
    n_chunks = M_ALL // INPROJ_TM

    def h_copy(c):
        rows = pl.ds(c * INPROJ_TM, INPROJ_TM)
        return pltpu.make_async_copy(h_hbm.at[rows], h_scr.at[rows], h_sems.at[c])

    @pl.when((j == 0) & (i == 0))
    def _():
        for c in range(n_chunks):
            h_copy(c).start()

    @pl.when(i == 0)
    def _():
        w_scr[...] = w_ref[0].astype(BF16)

    for c in range(n_chunks):
        @pl.when((j == 0) & (i == c))
        def _():
            h_copy(c).wait()

    rows = pl.ds(pl.multiple_of(i * INPROJ_TM, INPROJ_TM), INPROJ_TM)
    o_ref[...] = jnp.dot(h_scr[rows, :], w_scr[...], preferred_element_type=F32)


def _in_proj(h, w_in, layer):
    return pl.pallas_call(
        _inproj_kernel,
        grid=(D_IN // INPROJ_TN, M_ALL // INPROJ_TM),
        in_specs=[
            pl.BlockSpec(memory_space=pl.ANY),
            pl.BlockSpec((1, D_MODEL, INPROJ_TN), lambda j, i: (layer, 0, j)),
        ],
        out_specs=pl.BlockSpec((INPROJ_TM, INPROJ_TN), lambda j, i: (i, j)),
        out_shape=jax.ShapeDtypeStruct((M_ALL, D_IN), F32),
        scratch_shapes=[pltpu.VMEM((D_MODEL, INPROJ_TN), BF16), pltpu.VMEM((M_ALL, D_MODEL), BF16),
                        pltpu.SemaphoreType.DMA((M_ALL // INPROJ_TM,))],
        compiler_params=_params("arbitrary", "arbitrary"),
        name="in_proj",
    )(h, w_in)


ATT_SCALE = 1.0 / math.sqrt(HEAD_DIM)


def _head(ref_or_val, h):
    return ref_or_val[:, h * HEAD_DIM:(h + 1) * HEAD_DIM]


Q_PRESCALE = ATT_SCALE * math.log2(math.e)


def _softmax_pv(q2, k_bf, v_ext):
    s = lax.dot_general(q2, k_bf, (((1,), (1,)), ((), ())), preferred_element_type=F32)
    p = jnp.exp2(s - jnp.max(s, axis=-1, keepdims=True)).astype(BF16)
    o = jnp.dot(p, v_ext, preferred_element_type=F32)
    return o[:, 0:HEAD_DIM] / o[:, HEAD_DIM:2 * HEAD_DIM]


def _rope(x, cos2, sin_a, sin_b):
    nxt = pltpu.roll(x, HEAD_DIM - 1, axis=1)
    prv = pltpu.roll(x, 1, axis=1)
    return x * cos2 + nxt * sin_a + prv * sin_b


CTX_TQ = SEQ
LAT_TQ = 128


def _attn_kernel(rope, n_cache, layer, emit_kv, n_aliased, nsb, *refs):
    it = iter(refs)
    q_all, kv_all, g_all, qg_ref, kg_ref = (next(it) for _ in range(5))
    ck_ref, cv_ref = (next(it), next(it)) if n_cache else (None, None)
    q_tabs = tuple(next(it) for _ in range(3)) if rope else None
    k_tabs = tuple(next(it) for _ in range(3)) if rope else None
    for _ in range(n_aliased):
        next(it)
    att_all = next(it)
    ko_hbm, vo_hbm = (next(it), next(it)) if emit_kv else (None, None)
    k_all, v_all = next(it), next(it)
    kf_all, kv_sems = (next(it), next(it)) if emit_kv else (None, None)
    single_tile = q_all.shape[0] == kv_all.shape[0]
    n_new = kv_all.shape[0] // nsb if single_tile else kv_all.shape[0]
    assert not emit_kv or single_tile

    def part(ref, sb, rows_per_seq):
        return ref if nsb == 1 else ref.at[pl.ds(sb * rows_per_seq, rows_per_seq)]

    def kv_copy(sb, h, is_value):
        b = pl.program_id(0) * nsb + sb
        if is_value:
            src = part(kv_all, sb, n_new).at[:, pl.ds((N_KV_HEADS + h) * HEAD_DIM, HEAD_DIM)]
            return pltpu.make_async_copy(src, vo_hbm.at[b, layer, :, h, :],
                                         kv_sems.at[sb, N_KV_HEADS + h])
        src = part(kf_all, sb, n_new).at[:, pl.ds(h * HEAD_DIM, HEAD_DIM)]
        return pltpu.make_async_copy(src, ko_hbm.at[b, layer, :, h, :], kv_sems.at[sb, h])

    def prepare_keys_values(sb):
        kv_ref, k_scr, v_scr = part(kv_all, sb, n_new), part(k_all, sb, n_new + n_cache), part(v_all, sb, n_new + n_cache)
        kf_scr = part(kf_all, sb, n_new) if emit_kv else None
        for h in range(N_KV_HEADS):
            kn = _rms(_head(kv_ref, h), kg_ref[layer:layer + 1, :])
            vh = _head(kv_ref, N_KV_HEADS + h)
            if emit_kv:
                kf_scr[:, h * HEAD_DIM:(h + 1) * HEAD_DIM] = kn
            if rope:
                kn = _rope(kn, *(t[...] for t in k_tabs))
            k_scr[0:n_new, h * HEAD_DIM:(h + 1) * HEAD_DIM] = kn.astype(BF16)
            v0 = 2 * h * HEAD_DIM
            v_scr[0:n_new, v0:v0 + HEAD_DIM] = vh.astype(BF16)
            if n_cache:
                v_scr[n_new:n_new + n_cache, v0:v0 + HEAD_DIM] = _head(cv_ref[0, 0], h).astype(BF16)
            v_scr[:, v0 + HEAD_DIM:v0 + 2 * HEAD_DIM] = jnp.ones((n_new + n_cache, HEAD_DIM), BF16)
        if n_cache:
            k_scr[n_new:n_new + n_cache, :] = ck_ref[0, 0].astype(BF16)

    def attend(sb):
        rows = q_all.shape[0] // nsb
        q_ref, g_ref, att_ref = part(q_all, sb, rows), part(g_all, sb, rows), part(att_all, sb, rows)
        kv_sb = sb if single_tile else 0
        k_scr, v_scr = part(k_all, kv_sb, n_new + n_cache), part(v_all, kv_sb, n_new + n_cache)

        def query(hq):
            x = _rms(_head(q_ref, hq), qg_ref[layer:layer + 1, :])
            if rope:
                x = _rope(x, *(part(t, sb, rows)[...] for t in q_tabs))
            return x * Q_PRESCALE

        for h in range(N_KV_HEADS):
            q2 = jnp.concatenate([query(GROUP * h + g) for g in range(GROUP)], axis=0).astype(BF16)
            o = _softmax_pv(q2, _head(k_scr, h), v_scr[:, 2 * h * HEAD_DIM:2 * (h + 1) * HEAD_DIM])
            for g in range(GROUP):
                hq = GROUP * h + g
                gate = _silu(_head(g_ref, hq))
                att_ref[:, hq * HEAD_DIM:(hq + 1) * HEAD_DIM] = (o[g * rows:(g + 1) * rows] * gate).astype(BF16)

    every = [(sb, h) for sb in range(nsb) for h in range(N_KV_HEADS)]
    if single_tile:
        if emit_kv:
            for sb, h in every:
                kv_copy(sb, h, True).start()
        for sb in range(nsb):
            prepare_keys_values(sb)
        if emit_kv:
            for sb, h in every:
                kv_copy(sb, h, False).start()
    else:
        pl.when(pl.program_id(1) == 0)(lambda: prepare_keys_values(0))

    for sb in range(nsb):
        attend(sb)

    if emit_kv:
        for sb, h in every:
            kv_copy(sb, h, False).wait()
            kv_copy(sb, h, True).wait()


def _attention(proj, q_g, k_g, layer, seq, tq, n_batch, row0, *, nsb=1, cache=None, rope_tabs=None,
               att_prev=None, kv_prev=None, emit_kv=False):
    whole = tq == seq
    assert not whole or cache is None
    tq = nsb * tq
    kv_rows = nsb * seq if whole else seq
    nq = kv_rows // tq
    q0, kv0 = row0 // tq, row0 // kv_rows
    n_cache = 0 if cache is None else cache[0].shape[2]
    vec = pl.BlockSpec((DEPTH, HEAD_DIM), lambda b, i: (0, 0))
    in_specs = [
        pl.BlockSpec((tq, D_ATT), lambda b, i: (q0 + b * nq + i, COL_Q // D_ATT)),
        pl.BlockSpec((kv_rows, 2 * D_KV), lambda b, i: (kv0 + b, COL_K // (2 * D_KV))),
        pl.BlockSpec((tq, D_ATT), lambda b, i: (q0 + b * nq + i, COL_GA // D_ATT)),
        vec, vec,
    ]
    args = [proj, proj, proj, q_g, k_g]
    if cache is not None:
        in_specs += [pl.BlockSpec((1, 1, n_cache, D_KV), lambda b, i: (b, layer, 0, 0))] * 2
        args += list(cache)
    if rope_tabs is not None:
        in_specs += [pl.BlockSpec((tq, HEAD_DIM), lambda b, i: (i, 0))] * 3
        in_specs += [pl.BlockSpec((seq, HEAD_DIM), lambda b, i: (0, 0))] * 3
        args += list(rope_tabs) * 2
    aliased = ([] if att_prev is None else [att_prev]) + ([] if kv_prev is None else list(kv_prev))
    out_first = 0 if att_prev is not None else 1
    aliases = {len(args) + n: out_first + n for n in range(len(aliased))}
    in_specs += [pl.BlockSpec(memory_space=pl.ANY)] * len(aliased)
    args += aliased
    out_specs = [pl.BlockSpec((tq, D_ATT), lambda b, i: (q0 + b * nq + i, 0))]
    out_shape = [jax.ShapeDtypeStruct((M_ALL, D_ATT), BF16)]
    keys = kv_rows + n_cache
    scratch =[pltpu.VMEM((keys, D_KV), BF16), pltpu.VMEM((keys, 2 * D_KV), BF16)]
    if emit_kv:
        out_specs += [pl.BlockSpec(memory_space=pl.ANY)] * 2
        out_shape += [jax.ShapeDtypeStruct((n_batch, DEPTH, seq, N_KV_HEADS, HEAD_DIM), F32)] * 2
        scratch += [pltpu.VMEM((kv_rows, D_KV), F32), pltpu.SemaphoreType.DMA((nsb, 2 * N_KV_HEADS))]
    return pl.pallas_call(
        functools.partial(_attn_kernel, rope_tabs is not None, n_cache, layer, emit_kv, len(aliased), nsb),
        grid=(n_batch * seq // kv_rows, nq),
        in_specs=in_specs,
        out_specs=out_specs,
        out_shape=out_shape,
        input_output_aliases=aliases,
        scratch_shapes=scratch,
        compiler_params=_params("arbitrary", "arbitrary"),
        name=f"attention_{seq}",
    )(*args)


def _dft_matrix(L):
    k = np.arange(L, dtype=np.int64)[:, None]
    t = np.arange(L, dtype=np.int64)[None, :]
    ang = 2.0 * np.pi * ((k * t) % (2 * L)).astype(np.float64) / (2 * L)
    top = np.cos(ang)
    bot = np.sin(ang)
    bot[0, :] = np.where(np.arange(L) % 2 == 0, 1.0, -1.0)
    return np.concatenate([top, bot], axis=0).astype(np.float32)


def _filter_kernel(L, *refs):
    _filter_unit(L, pl.program_id(0), pl.program_id(1) == 0, *refs)


def _filter_unit(L, layer, first_block, z_ref, w1_ref, b1_ref, w2_ref, b2_ref, fr_ref, w3f_ref, w3b_ref, dec_ref,
                 f_ref, o_ref, hdn_scr, w1_scr, w2_scr, vec_scr, w3_scr):
    hp = lax.Precision.HIGHEST

    halves = ((0, 0), (FEAT_PAD, FILT_HID))

    @pl.when(first_block)
    def _():
        w1_scr[...] = jnp.zeros(w1_scr.shape, F32)
        w2_scr[...] = jnp.zeros(w2_scr.shape, F32)
        vec_scr[...] = jnp.zeros(vec_scr.shape, F32)
        w3_scr[...] = jnp.zeros(w3_scr.shape, BF16)
        for feat0, hid0 in halves:
            w1_scr[feat0:feat0 + POS_EMB, hid0:hid0 + FILT_HID] = w1_ref[0]
            w2_scr[hid0:hid0 + FILT_HID, hid0:hid0 + FILT_HID] = w2_ref[0]
            for r, ref in enumerate((b1_ref, b2_ref, fr_ref)):
                vec_scr[r:r + 1, hid0:hid0 + FILT_HID] = ref[pl.ds(layer, 1), :]
        b1, b2, fr = vec_scr[0:1, :], vec_scr[1:2, :], vec_scr[2:3, :]
        h1 = jnp.sin(fr * (jnp.dot(z_ref[...], w1_scr[...], precision=hp, preferred_element_type=F32) + b1))
        hdn_scr[...] = jnp.sin(fr * (jnp.dot(h1, w2_scr[...], precision=hp, preferred_element_type=F32) + b2))

    hdn_bf = hdn_scr[...].astype(BF16)
    dec = dec_ref[...]

    def project(w3_ref, slot):
        parts = []
        for k, (_, hid0) in enumerate(halves):
            w3_scr[slot + k, hid0:hid0 + FILT_HID, :] = w3_ref[0].astype(BF16)
            parts.append(jnp.dot(hdn_bf, w3_scr[slot + k], preferred_element_type=F32))
        return jnp.concatenate(parts, axis=0) * dec

    h_f = project(w3f_ref, 0)
    h_b = project(w3b_ref, 2)
    hs = h_f + h_b
    ha = jnp.dot(f_ref[0:L, :], hs.astype(BF16), preferred_element_type=F32)
    hb = jnp.dot(f_ref[L:2 * L, :], (h_f - h_b).astype(BF16), preferred_element_type=F32)
    row = lax.broadcasted_iota(jnp.int32, (L, 1), 0)
    first = row == 0
    nyquist = jnp.sum(jnp.where(row % 2 == 0, hs, -hs), axis=0, keepdims=True)
    wk = jnp.where(first, 1.0 / (2 * L), 2.0 / (2 * L))
    o_ref[0, 0] = ha * wk
    o_ref[0, 1] = jnp.where(first, 0.0, hb * wk)
    o_ref[0, 2] = jnp.where(first, nyquist, ha) * wk


FILT_CB = 512


def _filter_tables(L):
    tpos = np.arange(L, dtype=np.float64)
    t_norm = tpos / max(L - 1, 1)
    w = 2.0 * math.pi * tpos / L
    bands = np.linspace(1e-4, POS_BANDS - 1, POS_BANDS)
    z = np.concatenate([t_norm[:, None], np.cos(w[:, None] * bands), -np.sin(w[:, None] * bands)], axis=-1)
    z = np.pad(z, ((0, 0), (0, FEAT_PAD - POS_EMB))).astype(np.float32)
    z = np.concatenate([z[:L // 2], z[L // 2:]], axis=1)
    max_decay = math.log(DECAY_TARGET) / FAST_DECAY_PCT
    min_decay = math.log(DECAY_TARGET) / SLOW_DECAY_PCT
    deltas = np.abs(np.linspace(min_decay, max_decay, D_HY))
    dec = (np.exp(-t_norm[:, None] * deltas) + DECAY_SHIFT).astype(np.float32)
    return jnp.asarray(z), jnp.asarray(dec)


FILT_NCB = D_HY // FILT_CB


def _filter_specs(L, unit):
    layer_only = lambda *g: (unit(*g)[0], 0, 0)
    per_layer_vec = pl.BlockSpec((DEPTH, FILT_HID), lambda *g: (0, 0))
    in_specs = [
        pl.BlockSpec((L // 2, 2 * FEAT_PAD), lambda *g: (0, 0)),
        pl.BlockSpec((1, POS_EMB, FILT_HID), layer_only), per_layer_vec,
        pl.BlockSpec((1, FILT_HID, FILT_HID), layer_only), per_layer_vec, per_layer_vec,
        pl.BlockSpec((1, FILT_HID, FILT_CB), lambda *g: (unit(*g)[0], 0, unit(*g)[1])),
        pl.BlockSpec((1, FILT_HID, FILT_CB), lambda *g: (unit(*g)[0], 0, FILT_NCB + unit(*g)[1])),
        pl.BlockSpec((L, FILT_CB), lambda *g: (0, unit(*g)[1])),
        pl.BlockSpec((2 * L, L), lambda *g: (0, 0), pipeline_mode=pl.Buffered(1)),
    ]
    out_spec = pl.BlockSpec((1, 3, L, FILT_CB), lambda *g: (unit(*g)[0], 0, 0, unit(*g)[1]))
    scratch = [pltpu.VMEM((L // 2, FEAT_PAD), F32), pltpu.VMEM((2 * FEAT_PAD, FEAT_PAD), F32),
               pltpu.VMEM((FEAT_PAD, FEAT_PAD), F32), pltpu.VMEM((8, FEAT_PAD), F32),
               pltpu.VMEM((4, FEAT_PAD, FILT_CB), BF16)]
    return in_specs, out_spec, scratch


def _filter_spectra(L, w1, b1, w2, b2, freq, w3, f_bf):
    z, dec = _filter_tables(L)
    in_specs, out_spec, scratch = _filter_specs(L, lambda l, c: (l, c))
    return pl.pallas_call(
        functools.partial(_filter_kernel, L),
        grid=(DEPTH, FILT_NCB),
        in_specs=in_specs,
        out_specs=out_spec,
        out_shape=jax.ShapeDtypeStruct((DEPTH, 3, L, D_HY), F32),
        scratch_shapes=scratch,
        compiler_params=_params("arbitrary", "arbitrary"),
        name=f"hyena_filter_{L}",
    )(z, w1, b1, w2, b2, freq, w3, w3, dec, f_bf)


def _mod_filter_kernel(L, *refs):
    mod_in, filt_in = refs[:4], refs[4:14]
    o_mod, o_coef, c_scr = refs[14], refs[15], refs[16]
    s = pl.program_id(0)
    _mod_tile(s // MOD_TILES, *mod_in, o_mod, c_scr)

    @pl.when(s < DEPTH * FILT_NCB)
    def _():
        _filter_unit(L, s // FILT_NCB, s % FILT_NCB == 0, *filt_in, o_coef, *refs[17:])


def _modulation_and_filter(c_ctx, c, w_ada, b_ada, L, w1, b1, w2, b2, freq, w3, f_bf):
    z, dec = _filter_tables(L)
    last_unit = DEPTH * FILT_NCB - 1
    unit = lambda s: (jnp.minimum(s, last_unit) // FILT_NCB, jnp.minimum(s, last_unit) % FILT_NCB)
    f_in, f_out, f_scratch = _filter_specs(L, unit)
    tile = lambda s: (s // MOD_TILES, s % MOD_TILES)
    return pl.pallas_call(
        functools.partial(_mod_filter_kernel, L),
        grid=(DEPTH * MOD_TILES,),
        in_specs=[
            pl.BlockSpec((1, D_MODEL), lambda s: (0, 0)),
            pl.BlockSpec((DEC_BATCH, D_MODEL), lambda s: (0, 0)),
            pl.BlockSpec((1, D_MODEL, MOD_TN), lambda s: (tile(s)[0], 0, tile(s)[1])),
            pl.BlockSpec((DEPTH, MOD_TN), lambda s: (0, tile(s)[1])),
            *f_in,
        ],
        out_specs=[pl.BlockSpec((1, MOD_ROWS, MOD_TN), lambda s: (tile(s)[0], 0, tile(s)[1])), f_out],
        out_shape=[jax.ShapeDtypeStruct((DEPTH, MOD_ROWS, 3 * D_MODEL), F32),
                   jax.ShapeDtypeStruct((DEPTH, 3, L, D_HY), F32)],
        scratch_shapes=[pltpu.VMEM((MOD_ROWS, D_MODEL), F32), *f_scratch],
        compiler_params=_params("arbitrary"),
        name="adaln_mod_and_filter",
    )(c_ctx.reshape(1, D_MODEL), c, w_ada, b_ada, z, w1, b1, w2, b2, freq, w3, w3, dec, f_bf)


def _hyena_kernel(L, layer, sub, x0_ref, x1_ref, vv_ref, g_ref, cw0_ref, cw1_ref, cwv_ref, cb0_ref, cb1_ref,
                  cbv_ref, bias_ref, coef_ref, f_ref, ft_ref, *rest):
    o_ref = rest[-1]
    this_layer = slice(layer, layer + 1)
    row = lax.broadcasted_iota(jnp.int32, (L, 1), 0)
    is_first = row == 0
    is_last = row == L - 1

    for c in range(o_ref.shape[1] // sub):
        cols = slice(c * sub, (c + 1) * sub)

        def sconv(x_ref, w_ref, b_ref):
            x = x_ref[:, cols]
            prev = jnp.where(is_first, 0.0, pltpu.roll(x, 1, axis=0))
            nxt = jnp.where(is_last, 0.0, pltpu.roll(x, L - 1, axis=0))
            return (w_ref[0, 0:1, cols] * prev + w_ref[0, 1:2, cols] * x + w_ref[0, 2:3, cols] * nxt
                    + b_ref[this_layer, cols])

        z = sconv(vv_ref, cwv_ref, cbv_ref) * sconv(x1_ref, cw1_ref, cb1_ref)
        ab = jnp.dot(f_ref[...], z.astype(BF16), preferred_element_type=F32)
        a, b = ab[0:L], ab[L:2 * L]
        g1, g2, g3 = coef_ref[0, 0, :, cols], coef_ref[0, 1, :, cols], coef_ref[0, 2, :, cols]
        pq = jnp.concatenate([a * g1 - b * g2, a * g2 + b * g3], axis=0).astype(BF16)
        y = jnp.dot(ft_ref[...], pq, preferred_element_type=F32) + bias_ref[this_layer, cols] * z
        o_ref[:, cols] = (sconv(x0_ref, cw0_ref, cb0_ref) * y * _silu(g_ref[:, cols])).astype(BF16)


def _hyena(proj, L, nb, row0, cb, sub, layer, conv_w, conv_b, hy_bias, coef, f_bf, ft_bf, hy_prev=None):
    ncb = D_HY // cb
    rows = L
    blk0 = row0 // rows
    extra_specs = [] if hy_prev is None else [pl.BlockSpec(memory_space=pl.ANY)]
    extra_args = [] if hy_prev is None else [hy_prev]
    aliases = {} if hy_prev is None else {14: 0}
    col = lambda off: (lambda c, b: (blk0 + b, off // cb + c))
    cw = lambda part: pl.BlockSpec((1, 3, cb), lambda c, b: (layer, 0, part * ncb + c))
    cbias = lambda part: pl.BlockSpec((DEPTH, cb), lambda c, b: (0, part * ncb + c))
    once = pl.Buffered(1)
    return pl.pallas_call(
        functools.partial(_hyena_kernel, L, layer, sub),
        grid=(ncb, nb),
        in_specs=[
            pl.BlockSpec((rows, cb), col(COL_X0)),
            pl.BlockSpec((rows, cb), col(COL_X1)),
            pl.BlockSpec((rows, cb), col(COL_VV)),
            pl.BlockSpec((rows, cb), col(COL_GH)),
            cw(0), cw(1), cw(2), cbias(0), cbias(1), cbias(2),
            pl.BlockSpec((DEPTH, cb), lambda c, b: (0, c)),
            pl.BlockSpec((1, 3, L, cb), lambda c, b: (layer, 0, 0, c)),
            pl.BlockSpec((2 * L, L), lambda c, b: (0, 0), pipeline_mode=once),
            pl.BlockSpec((L, 2 * L), lambda c, b: (0, 0), pipeline_mode=once),
            *extra_specs,
        ],
        out_specs=pl.BlockSpec((rows, cb), lambda c, b: (blk0 + b, c)),
        out_shape=jax.ShapeDtypeStruct((M_ALL, D_HY), BF16),
        input_output_aliases=aliases,
        compiler_params=_params("arbitrary", "arbitrary"),
        name=f"hyena_{L}",
    )(proj, proj, proj, proj, conv_w, conv_w, conv_w, conv_b, conv_b, conv_b, hy_bias, coef, f_bf, ft_bf,
      *extra_args)


OUTPROJ_TM = 256
OUTPROJ_TM_FINAL = 512


def _outproj_kernel(split_in, final, tm, tile0, gain_row, *refs):
    it = iter(refs)
    att_ref, hy_ref, w_ref = next(it), next(it), next(it)
    x_refs = (next(it), next(it)) if split_in else (next(it),)
    gate_ref, g_ref = next(it), next(it)
    shift_ref, scale_ref = (None, None) if final else (next(it), next(it))
    out_refs = (next(it),) if final else (next(it), next(it))
    w_scr = next(it)
    i = pl.program_id(0)

    @pl.when(i == 0)
    def _():
        w_scr[...] = w_ref[0].astype(BF16)

    out = (jnp.dot(att_ref[...], w_scr[0:D_ATT, :], preferred_element_type=F32)
           + jnp.dot(hy_ref[...], w_scr[D_ATT:D_ATT + D_HY, :], preferred_element_type=F32))
    tile = tile0 + i
    if split_in:
        x = jnp.where(tile < M_CTX // tm, x_refs[0][...], x_refs[1][...])
    else:
        x = x_refs[0][...]
    y = x + _mod_row(gate_ref, tile, tm) * out
    inv = lax.rsqrt(jnp.mean(y * y, axis=-1, keepdims=True) + EPS)
    g = g_ref[gain_row:gain_row + 1, :]
    if final:
        out_refs[0][...] = y * inv * g
    else:
        out_refs[0][...] = y
        gain = g * (1.0 + _mod_row(scale_ref, tile, tm))
        out_refs[1][...] = (y * inv * gain + _mod_row(shift_ref, tile, tm)).astype(BF16)


def _out_proj(att, hy, w_out, layer, x_parts, mod, gains, final, tm, tile0=0, n_tiles=None):
    n_tiles = M_ALL // tm if n_tiles is None else n_tiles
    split_in = len(x_parts) == 2
    assert not split_in or (tile0 == 0 and n_tiles == M_ALL // tm)
    row_tile = lambda width: pl.BlockSpec((tm, width), lambda i: (tile0 + i, 0))
    out_tile = pl.BlockSpec((tm, D_MODEL), lambda i: (i, 0))
    x_specs = list(_two_stream_specs(tm, D_MODEL)) if split_in else [row_tile(D_MODEL)]
    in_specs = [
        row_tile(D_ATT),
        row_tile(D_HY),
        pl.BlockSpec((1, D_ATT + D_HY, D_MODEL), lambda i: (layer, 0, 0), pipeline_mode=pl.Buffered(1)),
        *x_specs,
        _mod_spec(layer, MOD_GATE),
        pl.BlockSpec(gains.shape, lambda i: (0, 0)),
    ]
    args = [att, hy, w_out, *x_parts, mod, gains]
    if final:
        out_specs = [out_tile]
        out_shape = [jax.ShapeDtypeStruct((n_tiles * tm, D_MODEL), F32)]
    else:
        in_specs += [_mod_spec(layer + 1, MOD_SHIFT), _mod_spec(layer + 1, MOD_SCALE)]
        args += [mod, mod]
        out_specs = [out_tile, out_tile]
        out_shape = [jax.ShapeDtypeStruct((n_tiles * tm, D_MODEL), F32),
                     jax.ShapeDtypeStruct((n_tiles * tm, D_MODEL), BF16)]
    return pl.pallas_call(
        functools.partial(_outproj_kernel, split_in, final, tm, tile0, 0 if final else layer + 1),
        grid=(n_tiles,),
        in_specs=in_specs,
        out_specs=out_specs,
        out_shape=out_shape,
        scratch_shapes=[pltpu.VMEM((D_ATT + D_HY, D_MODEL), BF16)],
        compiler_params=_params("arbitrary"),
        name="out_proj_final" if final else "out_proj",
    )(*args)


def _rope_tables():
    t = np.arange(DEC_SEQ)
    row = (t // GRID_W).astype(np.float64)
    col = (t % GRID_W).astype(np.float64)
    pairs = HEAD_DIM // 4
    inv_freq = ROPE_THETA ** (-np.arange(pairs, dtype=np.float64) / pairs)
    ang = np.concatenate([row[:, None] * inv_freq, col[:, None] * inv_freq], axis=-1)
    cos = np.repeat(np.cos(ang), 2, axis=-1).astype(np.float32)
    sin = np.repeat(np.sin(ang), 2, axis=-1).astype(np.float32)
    even = (np.arange(HEAD_DIM) % 2 == 0)[None, :]
    sin_a = np.where(even, -sin, 0.0).astype(np.float32)
    sin_b = np.where(even, 0.0, sin).astype(np.float32)
    return jnp.asarray(cos), jnp.asarray(sin_a), jnp.asarray(sin_b)


def kernel(x_prompt, x_sample, cache_k, cache_v, c, c_ctx, norm_g, w_ada, b_ada, w_in, q_norm_g, k_norm_g,
           conv_w, conv_b, filt_w1, filt_b1, filt_w2, filt_b2, filt_w3, filt_freq, hy_bias, w_out, final_norm_g):
    ctx = x_prompt.reshape(BATCH * SEQ, D_MODEL)
    lat = x_sample.reshape(DEC_BATCH * DEC_SEQ, D_MODEL)
    cache_k4 = cache_k.reshape(DEC_BATCH, DEPTH, PAST_LEN, D_KV)
    cache_v4 = cache_v.reshape(DEC_BATCH, DEPTH, PAST_LEN, D_KV)

    rope_tabs = _rope_tables()
    dft = {}
    for L in (SEQ, DEC_SEQ):
        f_np = _dft_matrix(L)
        dft[L] = (jnp.asarray(f_np).astype(BF16), jnp.asarray(np.ascontiguousarray(f_np.T)).astype(BF16))
    filt = (filt_w1, filt_b1, filt_w2, filt_b2, filt_freq, filt_w3)
    mod, coef_lat = _modulation_and_filter(c_ctx, c, w_ada, b_ada, DEC_SEQ, *filt, dft[DEC_SEQ][0])
    coefs = {SEQ: _filter_spectra(SEQ, *filt, dft[SEQ][0]), DEC_SEQ: coef_lat}

    final_g = final_norm_g.reshape(1, D_MODEL)

    x_parts = (ctx, lat)
    h = _norm_mod(ctx, lat, norm_g, mod, 0)
    kv_out = None
    for l in range(DEPTH):
        final = l == DEPTH - 1

        proj = _in_proj(h, w_in, l)

        att, new_k, new_v = _attention(proj, q_norm_g, k_norm_g, l, SEQ, CTX_TQ, BATCH, 0, nsb=4, kv_prev=kv_out,
                                       emit_kv=True)
        kv_out = (new_k, new_v)
        (att,) = _attention(proj, q_norm_g, k_norm_g, l, DEC_SEQ, LAT_TQ, DEC_BATCH, M_CTX, nsb=4,
                            cache=(cache_k4, cache_v4), rope_tabs=rope_tabs, att_prev=att)

        hy = _hyena(proj, SEQ, BATCH, 0, D_HY, 256, l, conv_w, conv_b, hy_bias, coefs[SEQ], *dft[SEQ])
        hy = _hyena(proj, DEC_SEQ, DEC_BATCH, M_CTX, 512, 512, l, conv_w, conv_b, hy_bias,
                    coefs[DEC_SEQ], *dft[DEC_SEQ], hy_prev=hy)

        if final:
            assert len(x_parts) == 1
            tm = OUTPROJ_TM_FINAL
            nc = M_CTX // tm
            (y_ctx,) = _out_proj(att, hy, w_out, l, x_parts, mod, final_g, True, tm, 0, nc)
            (y_lat,) = _out_proj(att, hy, w_out, l, x_parts, mod, final_g, True, tm, nc, M_LAT // tm)
        else:
            y, h = _out_proj(att, hy, w_out, l, x_parts, mod, norm_g, False, OUTPROJ_TM)
            x_parts = (y,)

    y_prompt = y_ctx.reshape(BATCH, SEQ, D_MODEL)
    y_sample = y_lat.reshape(DEC_BATCH, DEC_SEQ, D_MODEL)
    return (y_prompt, y_sample, kv_out[0], kv_out[1])
```

```python
import functools
import math

import numpy as np
import jax
import jax.numpy as jnp
from jax import lax
from jax.experimental import pallas as pl
from jax.experimental.pallas import tpu as pltpu

D_MODEL = 2048
BATCH = 16
SEQ = 256
DEPTH = 2
DEC_BATCH = 2
DEC_SEQ = 1024
PAST_LEN = 512
GRID_W = 64
D_ATT = 1024
D_HY = 1024
HEAD_DIM = 128
N_HEADS = 8
N_KV_HEADS = 4
GROUP = 2
D_KV = 512
ROPE_THETA = 10000.0
POS_BANDS = 16
POS_EMB = 33
FILT_HID = 64
DECAY_TARGET = 1e-2
FAST_DECAY_PCT = 0.3
SLOW_DECAY_PCT = 1.5
DECAY_SHIFT = 0.05
EPS = 1e-6
D_IN = 7168

COL_Q, COL_K, COL_V, COL_GA, COL_X0, COL_X1, COL_VV, COL_GH = 0, 1024, 1536, 2048, 3072, 4096, 5120, 6144

F32 = jnp.float32
BF16 = jnp.bfloat16

VMEM_LIMIT_BYTES = 56 * 1024 * 1024
FEAT_PAD = 128


def _params(*sem):
    return pltpu.CompilerParams(dimension_semantics=sem, vmem_limit_bytes=VMEM_LIMIT_BYTES)


def _silu(x):
    half = 0.5 * x
    return half + half * jnp.tanh(half)


def _rms(x, g):
    return x * lax.rsqrt(jnp.mean(x * x, axis=-1, keepdims=True) + EPS) * g


MOD_ROWS = 8
MOD_TN = 1024
MOD_TILES = 3 * D_MODEL // MOD_TN


def _mod_tile(layer, cctx_ref, c_ref, w_ref, b_ref, o_ref, c_scr):
    c_scr[...] = jnp.zeros(c_scr.shape, F32)
    c_scr[0:1, :] = cctx_ref[...]
    c_scr[1:1 + DEC_BATCH, :] = c_ref[...]
    s = _silu(c_scr[...]).astype(BF16)
    o_ref[0] = jnp.dot(s, w_ref[0].astype(BF16), preferred_element_type=F32) + b_ref[pl.ds(layer, 1), :]


M_CTX = BATCH * SEQ
M_LAT = DEC_BATCH * DEC_SEQ
M_ALL = M_CTX + M_LAT
NORM_TM = 1024
INPROJ_TM = 3072
INPROJ_TN = 512


ROW_CHUNK = 32


def _modulated_norm_rows(x_ref, h_ref, g, shift, scale):
    gain = g * (1.0 + scale)

    def body(r, _):
        rows = pl.ds(pl.multiple_of(r * ROW_CHUNK, ROW_CHUNK), ROW_CHUNK)
        x = x_ref[rows, :]
        inv = lax.rsqrt(jnp.mean(x * x, axis=-1, keepdims=True) + EPS)
        h_ref[rows, :] = (x * inv * gain + shift).astype(BF16)
        return 0

    lax.fori_loop(0, x_ref.shape[0] // ROW_CHUNK, body, 0, unroll=4)


MOD_SHIFT, MOD_SCALE, MOD_GATE = 0, 1, 2


def _mod_spec(layer, part):
    return pl.BlockSpec((1, MOD_ROWS, D_MODEL), lambda i: (layer, 0, part))


def _mod_row(ref, tile, tm):
    nc = M_CTX // tm
    per_batch = DEC_SEQ // tm
    row = jnp.where(tile < nc, 0, 1 + (tile - nc) // per_batch)
    return ref[0, pl.ds(row, 1), :]


def _two_stream_specs(tm, width):
    nc = M_CTX // tm
    return (pl.BlockSpec((tm, width), lambda i: (jnp.minimum(i, nc - 1), 0)),
            pl.BlockSpec((tm, width), lambda i: (jnp.maximum(i - nc, 0), 0)))


def _norm_kernel(layer, xc_ref, xl_ref, g_ref, shift_ref, scale_ref, h_ref):
    i = pl.program_id(0)

    def emit(x_ref):
        _modulated_norm_rows(x_ref, h_ref, g_ref[layer:layer + 1, :], _mod_row(shift_ref, i, NORM_TM),
                             _mod_row(scale_ref, i, NORM_TM))

    pl.when(i < M_CTX // NORM_TM)(lambda: emit(xc_ref))
    pl.when(i >= M_CTX // NORM_TM)(lambda: emit(xl_ref))


def _norm_mod(x_ctx, x_lat, norm_g, mod, layer):
    xc_spec, xl_spec = _two_stream_specs(NORM_TM, D_MODEL)
    return pl.pallas_call(
        functools.partial(_norm_kernel, layer),
        grid=(M_ALL // NORM_TM,),
        in_specs=[
            xc_spec, xl_spec,
            pl.BlockSpec((DEPTH, D_MODEL), lambda i: (0, 0)),
            _mod_spec(layer, MOD_SHIFT),
            _mod_spec(layer, MOD_SCALE),
        ],
        out_specs=pl.BlockSpec((NORM_TM, D_MODEL), lambda i: (i, 0)),
        out_shape=jax.ShapeDtypeStruct((M_ALL, D_MODEL), BF16),
        compiler_params=_params("arbitrary"),
        name="norm_mod",
    )(x_ctx, x_lat, norm_g, mod, mod)


def _inproj_kernel(h_hbm, w_ref, o_ref, w_scr, h_scr, h_sems):
    j, i = pl.program_id(0), pl.program_id(1)
    n_chunks = M_ALL // INPROJ_TM

    def h_copy(c):
        rows = pl.ds(c * INPROJ_TM, INPROJ_TM)
        return pltpu.make_async_copy(h_hbm.at[rows], h_scr.at[rows], h_sems.at[c])

    @pl.when((j == 0) & (i == 0))
    def _():
        for c in range(n_chunks):
            h_copy(c).start()

    @pl.when(i == 0)
    def _():
        w_scr[...] = w_ref[0].astype(BF16)

    for c in range(n_chunks):
        @pl.when((j == 0) & (i == c))
        def _():
            h_copy(c).wait()

    rows = pl.ds(pl.multiple_of(i * INPROJ_TM, INPROJ_TM), INPROJ_TM)
    o_ref[...] = jnp.dot(h_scr[rows, :], w_scr[...], preferred_element_type=F32)


def _in_proj(h, w_in, layer):
    return pl.pallas_call(
        _inproj_kernel,
        grid=(D_IN // INPROJ_TN, M_ALL // INPROJ_TM),
        in_specs=[
            pl.BlockSpec(memory_space=pl.ANY),
            pl.BlockSpec((1, D_MODEL, INPROJ_TN), lambda j, i: (layer, 0, j)),
        ],
        out_specs=pl.BlockSpec((INPROJ_TM, INPROJ_TN), lambda j, i: (i, j)),
        out_shape=jax.ShapeDtypeStruct((M_ALL, D_IN), F32),
        scratch_shapes=[pltpu.VMEM((D_MODEL, INPROJ_TN), BF16), pltpu.VMEM((M_ALL, D_MODEL), BF16),
                        pltpu.SemaphoreType.DMA((M_ALL // INPROJ_TM,))],
        compiler_params=_params("arbitrary", "arbitrary"),
        name="in_proj",
    )(h, w_in)


ATT_SCALE = 1.0 / math.sqrt(HEAD_DIM)


def _head(ref_or_val, h):
    return ref_or_val[:, h * HEAD_DIM:(h + 1) * HEAD_DIM]


Q_PRESCALE = ATT_SCALE * math.log2(math.e)


def _softmax_pv(q2, k_bf, v_ext):
    s = lax.dot_general(q2, k_bf, (((1,), (1,)), ((), ())), preferred_element_type=F32)
    p = jnp.exp2(s - jnp.max(s, axis=-1, keepdims=True)).astype(BF16)
    o = jnp.dot(p, v_ext, preferred_element_type=F32)
    return o[:, 0:HEAD_DIM] / o[:, HEAD_DIM:2 * HEAD_DIM]


def _rope(x, cos2, sin_a, sin_b):
    nxt = pltpu.roll(x, HEAD_DIM - 1, axis=1)
    prv = pltpu.roll(x, 1, axis=1)
    return x * cos2 + nxt * sin_a + prv * sin_b


CTX_TQ = SEQ
LAT_TQ = 128


def _attn_kernel(rope, n_cache, layer, emit_kv, n_aliased, nsb, *refs):
    it = iter(refs)
    q_all, kv_all, g_all, qg_ref, kg_ref = (next(it) for _ in range(5))
    ck_ref, cv_ref = (next(it), next(it)) if n_cache else (None, None)
    q_tabs = tuple(next(it) for _ in range(3)) if rope else None
    k_tabs = tuple(next(it) for _ in range(3)) if rope else None
    for _ in range(n_aliased):
        next(it)
    att_all = next(it)
    ko_hbm, vo_hbm = (next(it), next(it)) if emit_kv else (None, None)
    k_all, v_all = next(it), next(it)
    kf_all, kv_sems = (next(it), next(it)) if emit_kv else (None, None)
    single_tile = q_all.shape[0] == kv_all.shape[0]
    n_new = kv_all.shape[0] // nsb if single_tile else kv_all.shape[0]
    assert not emit_kv or single_tile

    def part(ref, sb, rows_per_seq):
        return ref if nsb == 1 else ref.at[pl.ds(sb * rows_per_seq, rows_per_seq)]

    def kv_copy(sb, h, is_value):
        b = pl.program_id(0) * nsb + sb
        if is_value:
            src = part(kv_all, sb, n_new).at[:, pl.ds((N_KV_HEADS + h) * HEAD_DIM, HEAD_DIM)]
            return pltpu.make_async_copy(src, vo_hbm.at[b, layer, :, h, :],
                                         kv_sems.at[sb, N_KV_HEADS + h])
        src = part(kf_all, sb, n_new).at[:, pl.ds(h * HEAD_DIM, HEAD_DIM)]
        return pltpu.make_async_copy(src, ko_hbm.at[b, layer, :, h, :], kv_sems.at[sb, h])

    def prepare_keys_values(sb):
        kv_ref, k_scr, v_scr = part(kv_all, sb, n_new), part(k_all, sb, n_new + n_cache), part(v_all, sb, n_new + n_cache)
        kf_scr = part(kf_all, sb, n_new) if emit_kv else None
        for h in range(N_KV_HEADS):
            kn = _rms(_head(kv_ref, h), kg_ref[layer:layer + 1, :])
            vh = _head(kv_ref, N_KV_HEADS + h)
            if emit_kv:
                kf_scr[:, h * HEAD_DIM:(h + 1) * HEAD_DIM] = kn
            if rope:
                kn = _rope(kn, *(t[...] for t in k_tabs))
            k_scr[0:n_new, h * HEAD_DIM:(h + 1) * HEAD_DIM] = kn.astype(BF16)
            v0 = 2 * h * HEAD_DIM
            v_scr[0:n_new, v0:v0 + HEAD_DIM] = vh.astype(BF16)
            if n_cache:
                v_scr[n_new:n_new + n_cache, v0:v0 + HEAD_DIM] = _head(cv_ref[0, 0], h).astype(BF16)
            v_scr[:, v0 + HEAD_DIM:v0 + 2 * HEAD_DIM] = jnp.ones((n_new + n_cache, HEAD_DIM), BF16)
        if n_cache:
            k_scr[n_new:n_new + n_cache, :] = ck_ref[0, 0].astype(BF16)

    def attend(sb):
        rows = q_all.shape[0] // nsb
        q_ref, g_ref, att_ref = part(q_all, sb, rows), part(g_all, sb, rows), part(att_all, sb, rows)
        kv_sb = sb if single_tile else 0
        k_scr, v_scr = part(k_all, kv_sb, n_new + n_cache), part(v_all, kv_sb, n_new + n_cache)

        def query(hq):
            x = _rms(_head(q_ref, hq), qg_ref[layer:layer + 1, :])
            if rope:
                x = _rope(x, *(part(t, sb, rows)[...] for t in q_tabs))
            return x * Q_PRESCALE

        for h in range(N_KV_HEADS):
            q2 = jnp.concatenate([query(GROUP * h + g) for g in range(GROUP)], axis=0).astype(BF16)
            o = _softmax_pv(q2, _head(k_scr, h), v_scr[:, 2 * h * HEAD_DIM:2 * (h + 1) * HEAD_DIM])
            for g in range(GROUP):
                hq = GROUP * h + g
                gate = _silu(_head(g_ref, hq))
                att_ref[:, hq * HEAD_DIM:(hq + 1) * HEAD_DIM] = (o[g * rows:(g + 1) * rows] * gate).astype(BF16)

    every = [(sb, h) for sb in range(nsb) for h in range(N_KV_HEADS)]
    if single_tile:
        if emit_kv:
            for sb, h in every:
                kv_copy(sb, h, True).start()
        for sb in range(nsb):
            prepare_keys_values(sb)
        if emit_kv:
            for sb, h in every:
                kv_copy(sb, h, False).start()
    else:
        pl.when(pl.program_id(1) == 0)(lambda: prepare_keys_values(0))

    for sb in range(nsb):
        attend(sb)

    if emit_kv:
        for sb, h in every:
            kv_copy(sb, h, False).wait()
            kv_copy(sb, h, True).wait()


def _attention(proj, q_g, k_g, layer, seq, tq, n_batch, row0, *, nsb=1, cache=None, rope_tabs=None,
               att_prev=None, kv_prev=None, emit_kv=False):
    whole = tq == seq
    assert not whole or cache is None
    tq = nsb * tq
    kv_rows = nsb * seq if whole else seq
    nq = kv_rows // tq
    q0, kv0 = row0 // tq, row0 // kv_rows
    n_cache = 0 if cache is None else cache[0].shape[2]
    vec = pl.BlockSpec((DEPTH, HEAD_DIM), lambda b, i: (0, 0))
    in_specs = [
        pl.BlockSpec((tq, D_ATT), lambda b, i: (q0 + b * nq + i, COL_Q // D_ATT)),
        pl.BlockSpec((kv_rows, 2 * D_KV), lambda b, i: (kv0 + b, COL_K // (2 * D_KV))),
        pl.BlockSpec((tq, D_ATT), lambda b, i: (q0 + b * nq + i, COL_GA // D_ATT)),
        vec, vec,
    ]
    args = [proj, proj, proj, q_g, k_g]
    if cache is not None:
        in_specs += [pl.BlockSpec((1, 1, n_cache, D_KV), lambda b, i: (b, layer, 0, 0))] * 2
        args += list(cache)
    if rope_tabs is not None:
        in_specs += [pl.BlockSpec((tq, HEAD_DIM), lambda b, i: (i, 0))] * 3
        in_specs += [pl.BlockSpec((seq, HEAD_DIM), lambda b, i: (0, 0))] * 3
        args += list(rope_tabs) * 2
    aliased = ([] if att_prev is None else [att_prev]) + ([] if kv_prev is None else list(kv_prev))
    out_first = 0 if att_prev is not None else 1
    aliases = {len(args) + n: out_first + n for n in range(len(aliased))}
    in_specs += [pl.BlockSpec(memory_space=pl.ANY)] * len(aliased)
    args += aliased
    out_specs = [pl.BlockSpec((tq, D_ATT), lambda b, i: (q0 + b * nq + i, 0))]
    out_shape = [jax.ShapeDtypeStruct((M_ALL, D_ATT), BF16)]
    keys = kv_rows + n_cache
    scratch =[pltpu.VMEM((keys, D_KV), BF16), pltpu.VMEM((keys, 2 * D_KV), BF16)]
    if emit_kv:
        out_specs += [pl.BlockSpec(memory_space=pl.ANY)] * 2
        out_shape += [jax.ShapeDtypeStruct((n_batch, DEPTH, seq, N_KV_HEADS, HEAD_DIM), F32)] * 2
        scratch += [pltpu.VMEM((kv_rows, D_KV), F32), pltpu.SemaphoreType.DMA((nsb, 2 * N_KV_HEADS))]
    return pl.pallas_call(
        functools.partial(_attn_kernel, rope_tabs is not None, n_cache, layer, emit_kv, len(aliased), nsb),
        grid=(n_batch * seq // kv_rows, nq),
        in_specs=in_specs,
        out_specs=out_specs,
        out_shape=out_shape,
        input_output_aliases=aliases,
        scratch_shapes=scratch,
        compiler_params=_params("arbitrary", "arbitrary"),
        name=f"attention_{seq}",
    )(*args)


def _dft_matrix(L):
    k = np.arange(L, dtype=np.int64)[:, None]
    t = np.arange(L, dtype=np.int64)[None, :]
    ang = 2.0 * np.pi * ((k * t) % (2 * L)).astype(np.float64) / (2 * L)
    top = np.cos(ang)
    bot = np.sin(ang)
    bot[0, :] = np.where(np.arange(L) % 2 == 0, 1.0, -1.0)
    return np.concatenate([top, bot], axis=0).astype(np.float32)


def _filter_kernel(L, *refs):
    _filter_unit(L, pl.program_id(0), pl.program_id(1) == 0, None, *refs)


def _filter_unit(L, layer, run_ffn, run_block, z_ref, w1_ref, b1_ref, w2_ref, b2_ref, fr_ref, w3f_ref, w3b_ref,
                 dec_ref, f_ref, o_ref, hdn_scr, w1_scr, w2_scr, vec_scr, w3_scr):
    hp = lax.Precision.HIGHEST

    halves = ((0, 0), (FEAT_PAD, FILT_HID))

    @pl.when(run_ffn)
    def _():
        w1_scr[...] = jnp.zeros(w1_scr.shape, F32)
        w2_scr[...] = jnp.zeros(w2_scr.shape, F32)
        vec_scr[...] = jnp.zeros(vec_scr.shape, F32)
        w3_scr[...] = jnp.zeros(w3_scr.shape, BF16)
        for feat0, hid0 in halves:
            w1_scr[feat0:feat0 + POS_EMB, hid0:hid0 + FILT_HID] = w1_ref[0]
            w2_scr[hid0:hid0 + FILT_HID, hid0:hid0 + FILT_HID] = w2_ref[0]
            for r, ref in enumerate((b1_ref, b2_ref, fr_ref)):
                vec_scr[r:r + 1, hid0:hid0 + FILT_HID] = ref[pl.ds(layer, 1), :]
        b1, b2, fr = vec_scr[0:1, :], vec_scr[1:2, :], vec_scr[2:3, :]
        h1 = jnp.sin(fr * (jnp.dot(z_ref[...], w1_scr[...], precision=hp, preferred_element_type=F32) + b1))
        hdn_scr[...] = jnp.sin(fr * (jnp.dot(h1, w2_scr[...], precision=hp, preferred_element_type=F32) + b2))

    def block():
        hdn_bf = hdn_scr[...].astype(BF16)
        dec = dec_ref[...]

        def project(w3_ref, slot):
            parts = []
            for k, (_, hid0) in enumerate(halves):
                w3_scr[slot + k, hid0:hid0 + FILT_HID, :] = w3_ref[0].astype(BF16)
                parts.append(jnp.dot(hdn_bf, w3_scr[slot + k], preferred_element_type=F32))
            return jnp.concatenate(parts, axis=0) * dec

        h_f = project(w3f_ref, 0)
        h_b = project(w3b_ref, 2)
        hs = h_f + h_b
        ha = jnp.dot(f_ref[0:L, :], hs.astype(BF16), preferred_element_type=F32)
        hb = jnp.dot(f_ref[L:2 * L, :], (h_f - h_b).astype(BF16), preferred_element_type=F32)
        row = lax.broadcasted_iota(jnp.int32, (L, 1), 0)
        first = row == 0
        nyquist = jnp.sum(jnp.where(row % 2 == 0, hs, -hs), axis=0, keepdims=True)
        wk = jnp.where(first, 1.0 / (2 * L), 2.0 / (2 * L))
        o_ref[0, 0] = ha * wk
        o_ref[0, 1] = jnp.where(first, 0.0, hb * wk)
        o_ref[0, 2] = jnp.where(first, nyquist, ha) * wk

    if run_block is None:
        block()
    else:
        pl.when(run_block)(block)


FILT_CB = 512


def _filter_tables(L):
    tpos = np.arange(L, dtype=np.float64)
    t_norm = tpos / max(L - 1, 1)
    w = 2.0 * math.pi * tpos / L
    bands = np.linspace(1e-4, POS_BANDS - 1, POS_BANDS)
    z = np.concatenate([t_norm[:, None], np.cos(w[:, None] * bands), -np.sin(w[:, None] * bands)], axis=-1)
    z = np.pad(z, ((0, 0), (0, FEAT_PAD - POS_EMB))).astype(np.float32)
    z = np.concatenate([z[:L // 2], z[L // 2:]], axis=1)
    max_decay = math.log(DECAY_TARGET) / FAST_DECAY_PCT
    min_decay = math.log(DECAY_TARGET) / SLOW_DECAY_PCT
    deltas = np.abs(np.linspace(min_decay, max_decay, D_HY))
    dec = (np.exp(-t_norm[:, None] * deltas) + DECAY_SHIFT).astype(np.float32)
    return jnp.asarray(z), jnp.asarray(dec)


FILT_NCB = D_HY // FILT_CB


def _filter_specs(L, unit):
    layer_only = lambda *g: (unit(*g)[0], 0, 0)
    per_layer_vec = pl.BlockSpec((DEPTH, FILT_HID), lambda *g: (0, 0))
    in_specs = [
        pl.BlockSpec((L // 2, 2 * FEAT_PAD), lambda *g: (0, 0)),
        pl.BlockSpec((1, POS_EMB, FILT_HID), layer_only), per_layer_vec,
        pl.BlockSpec((1, FILT_HID, FILT_HID), layer_only), per_layer_vec, per_layer_vec,
        pl.BlockSpec((1, FILT_HID, FILT_CB), lambda *g: (unit(*g)[0], 0, unit(*g)[1])),
        pl.BlockSpec((1, FILT_HID, FILT_CB), lambda *g: (unit(*g)[0], 0, FILT_NCB + unit(*g)[1])),
        pl.BlockSpec((L, FILT_CB), lambda *g: (0, unit(*g)[1])),
        pl.BlockSpec((2 * L, L), lambda *g: (0, 0), pipeline_mode=pl.Buffered(1)),
    ]
    out_spec = pl.BlockSpec((1, 3, L, FILT_CB), lambda *g: (unit(*g)[0], 0, 0, unit(*g)[1]))
    scratch = [pltpu.VMEM((L // 2, FEAT_PAD), F32), pltpu.VMEM((2 * FEAT_PAD, FEAT_PAD), F32),
               pltpu.VMEM((FEAT_PAD, FEAT_PAD), F32), pltpu.VMEM((8, FEAT_PAD), F32),
               pltpu.VMEM((4, FEAT_PAD, FILT_CB), BF16)]
    return in_specs, out_spec, scratch


def _filter_spectra(L, w1, b1, w2, b2, freq, w3, f_bf):
    z, dec = _filter_tables(L)
    in_specs, out_spec, scratch = _filter_specs(L, lambda l, c: (l, c))
    return pl.pallas_call(
        functools.partial(_filter_kernel, L),
        grid=(DEPTH, FILT_NCB),
        in_specs=in_specs,
        out_specs=out_spec,
        out_shape=jax.ShapeDtypeStruct((DEPTH, 3, L, D_HY), F32),
        scratch_shapes=scratch,
        compiler_params=_params("arbitrary", "arbitrary"),
        name=f"hyena_filter_{L}",
    )(z, w1, b1, w2, b2, freq, w3, w3, dec, f_bf)


FILT_PHASES = 1 + FILT_NCB


def _mod_filter_kernel(L, *refs):
    mod_in, filt_in = refs[:4], refs[4:14]
    o_mod, o_coef, c_scr = refs[14], refs[15], refs[16]
    s = pl.program_id(0)
    _mod_tile(s // MOD_TILES, *mod_in, o_mod, c_scr)

    @pl.when(s < DEPTH * FILT_PHASES)
    def _():
        phase = s % FILT_PHASES
        _filter_unit(L, s // FILT_PHASES, phase == 0, phase > 0, *filt_in, o_coef, *refs[17:])


def _modulation_and_filter(c_ctx, c, w_ada, b_ada, L, w1, b1, w2, b2, freq, w3, f_bf):
    z, dec = _filter_tables(L)
    assert DEPTH * FILT_PHASES <= DEPTH * MOD_TILES

    def unit(s):
        u = jnp.minimum(s, DEPTH * FILT_PHASES - 1)
        return u // FILT_PHASES, jnp.maximum(u % FILT_PHASES - 1, 0)

    f_in, f_out, f_scratch = _filter_specs(L, unit)
    tile = lambda s: (s // MOD_TILES, s % MOD_TILES)
    return pl.pallas_call(
        functools.partial(_mod_filter_kernel, L),
        grid=(DEPTH * MOD_TILES,),
        in_specs=[
            pl.BlockSpec((1, D_MODEL), lambda s: (0, 0)),
            pl.BlockSpec((DEC_BATCH, D_MODEL), lambda s: (0, 0)),
            pl.BlockSpec((1, D_MODEL, MOD_TN), lambda s: (tile(s)[0], 0, tile(s)[1])),
            pl.BlockSpec((DEPTH, MOD_TN), lambda s: (0, tile(s)[1])),
            *f_in,
        ],
        out_specs=[pl.BlockSpec((1, MOD_ROWS, MOD_TN), lambda s: (tile(s)[0], 0, tile(s)[1])), f_out],
        out_shape=[jax.ShapeDtypeStruct((DEPTH, MOD_ROWS, 3 * D_MODEL), F32),
                   jax.ShapeDtypeStruct((DEPTH, 3, L, D_HY), F32)],
        scratch_shapes=[pltpu.VMEM((MOD_ROWS, D_MODEL), F32), *f_scratch],
        compiler_params=_params("arbitrary"),
        name="adaln_mod_and_filter",
    )(c_ctx.reshape(1, D_MODEL), c, w_ada, b_ada, z, w1, b1, w2, b2, freq, w3, w3, dec, f_bf)


def _hyena_kernel(L, layer, sub, x0_ref, x1_ref, vv_ref, g_ref, cw0_ref, cw1_ref, cwv_ref, cb0_ref, cb1_ref,
                  cbv_ref, bias_ref, coef_ref, f_ref, ft_ref, *rest):
    o_ref = rest[-1]
    this_layer = slice(layer, layer + 1)
    row = lax.broadcasted_iota(jnp.int32, (L, 1), 0)
    is_first = row == 0
    is_last = row == L - 1

    for c in range(o_ref.shape[1] // sub):
        cols = slice(c * sub, (c + 1) * sub)

        def sconv(x_ref, w_ref, b_ref):
            x = x_ref[:, cols]
            prev = jnp.where(is_first, 0.0, pltpu.roll(x, 1, axis=0))
            nxt = jnp.where(is_last, 0.0, pltpu.roll(x, L - 1, axis=0))
            return (w_ref[0, 0:1, cols] * prev + w_ref[0, 1:2, cols] * x + w_ref[0, 2:3, cols] * nxt
                    + b_ref[this_layer, cols])

        z = sconv(vv_ref, cwv_ref, cbv_ref) * sconv(x1_ref, cw1_ref, cb1_ref)
        ab = jnp.dot(f_ref[...], z.astype(BF16), preferred_element_type=F32)
        a, b = ab[0:L], ab[L:2 * L]
        g1, g2, g3 = coef_ref[0, 0, :, cols], coef_ref[0, 1, :, cols], coef_ref[0, 2, :, cols]
        pq = jnp.concatenate([a * g1 - b * g2, a * g2 + b * g3], axis=0).astype(BF16)
        y = jnp.dot(ft_ref[...], pq, preferred_element_type=F32) + bias_ref[this_layer, cols] * z
        o_ref[:, cols] = (sconv(x0_ref, cw0_ref, cb0_ref) * y * _silu(g_ref[:, cols])).astype(BF16)


def _hyena(proj, L, nb, row0, cb, sub, layer, conv_w, conv_b, hy_bias, coef, f_bf, ft_bf, hy_prev=None):
    ncb = D_HY // cb
    rows = L
    blk0 = row0 // rows
    extra_specs = [] if hy_prev is None else [pl.BlockSpec(memory_space=pl.ANY)]
    extra_args = [] if hy_prev is None else [hy_prev]
    aliases = {} if hy_prev is None else {14: 0}
    col = lambda off: (lambda c, b: (blk0 + b, off // cb + c))
    cw = lambda part: pl.BlockSpec((1, 3, cb), lambda c, b: (layer, 0, part * ncb + c))
    cbias = lambda part: pl.BlockSpec((DEPTH, cb), lambda c, b: (0, part * ncb + c))
    once = pl.Buffered(1)
    return pl.pallas_call(
        functools.partial(_hyena_kernel, L, layer, sub),
        grid=(ncb, nb),
        in_specs=[
            pl.BlockSpec((rows, cb), col(COL_X0)),
            pl.BlockSpec((rows, cb), col(COL_X1)),
            pl.BlockSpec((rows, cb), col(COL_VV)),
            pl.BlockSpec((rows, cb), col(COL_GH)),
            cw(0), cw(1), cw(2), cbias(0), cbias(1), cbias(2),
            pl.BlockSpec((DEPTH, cb), lambda c, b: (0, c)),
            pl.BlockSpec((1, 3, L, cb), lambda c, b: (layer, 0, 0, c)),
            pl.BlockSpec((2 * L, L), lambda c, b: (0, 0), pipeline_mode=once),
            pl.BlockSpec((L, 2 * L), lambda c, b: (0, 0), pipeline_mode=once),
            *extra_specs,
        ],
        out_specs=pl.BlockSpec((rows, cb), lambda c, b: (blk0 + b, c)),
        out_shape=jax.ShapeDtypeStruct((M_ALL, D_HY), BF16),
        input_output_aliases=aliases,
        compiler_params=_params("arbitrary", "arbitrary"),
        name=f"hyena_{L}",
    )(proj, proj, proj, proj, conv_w, conv_w, conv_w, conv_b, conv_b, conv_b, hy_bias, coef, f_bf, ft_bf,
      *extra_args)


OUTPROJ_TM = 256
OUTPROJ_TM_FINAL = 512


def _outproj_kernel(split_in, final, tm, tile0, gain_row, *refs):
    it = iter(refs)
    att_ref, hy_ref, w_ref = next(it), next(it), next(it)
    x_refs = (next(it), next(it)) if split_in else (next(it),)
    gate_ref, g_ref = next(it), next(it)
    shift_ref, scale_ref = (None, None) if final else (next(it), next(it))
    out_refs = (next(it),) if final else (next(it), next(it))
    w_scr = next(it)
    i = pl.program_id(0)

    @pl.when(i == 0)
    def _():
        w_scr[...] = w_ref[0].astype(BF16)

    out = (jnp.dot(att_ref[...], w_scr[0:D_ATT, :], preferred_element_type=F32)
           + jnp.dot(hy_ref[...], w_scr[D_ATT:D_ATT + D_HY, :], preferred_element_type=F32))
    tile = tile0 + i
    if split_in:
        x = jnp.where(tile < M_CTX // tm, x_refs[0][...], x_refs[1][...])
    else:
        x = x_refs[0][...]
    y = x + _mod_row(gate_ref, tile, tm) * out
    inv = lax.rsqrt(jnp.mean(y * y, axis=-1, keepdims=True) + EPS)
    g = g_ref[gain_row:gain_row + 1, :]
    if final:
        out_refs[0][...] = y * inv * g
    else:
        out_refs[0][...] = y
        gain = g * (1.0 + _mod_row(scale_ref, tile, tm))
        out_refs[1][...] = (y * inv * gain + _mod_row(shift_ref, tile, tm)).astype(BF16)


def _out_proj(att, hy, w_out, layer, x_parts, mod, gains, final, tm, tile0=0, n_tiles=None):
    n_tiles = M_ALL // tm if n_tiles is None else n_tiles
    split_in = len(x_parts) == 2
    assert not split_in or (tile0 == 0 and n_tiles == M_ALL // tm)
    row_tile = lambda width: pl.BlockSpec((tm, width), lambda i: (tile0 + i, 0))
    out_tile = pl.BlockSpec((tm, D_MODEL), lambda i: (i, 0))
    x_specs = list(_two_stream_specs(tm, D_MODEL)) if split_in else [row_tile(D_MODEL)]
    in_specs = [
        row_tile(D_ATT),
        row_tile(D_HY),
        pl.BlockSpec((1, D_ATT + D_HY, D_MODEL), lambda i: (layer, 0, 0), pipeline_mode=pl.Buffered(1)),
        *x_specs,
        _mod_spec(layer, MOD_GATE),
        pl.BlockSpec(gains.shape, lambda i: (0, 0)),
    ]
    args = [att, hy, w_out, *x_parts, mod, gains]
    if final:
        out_specs = [out_tile]
        out_shape = [jax.ShapeDtypeStruct((n_tiles * tm, D_MODEL), F32)]
    else:
        in_specs += [_mod_spec(layer + 1, MOD_SHIFT), _mod_spec(layer + 1, MOD_SCALE)]
        args += [mod, mod]
        out_specs = [out_tile, out_tile]
        out_shape = [jax.ShapeDtypeStruct((n_tiles * tm, D_MODEL), F32),
                     jax.ShapeDtypeStruct((n_tiles * tm, D_MODEL), BF16)]
    return pl.pallas_call(
        functools.partial(_outproj_kernel, split_in, final, tm, tile0, 0 if final else layer + 1),
        grid=(n_tiles,),
        in_specs=in_specs,
        out_specs=out_specs,
        out_shape=out_shape,
        scratch_shapes=[pltpu.VMEM((D_ATT + D_HY, D_MODEL), BF16)],
        compiler_params=_params("arbitrary"),
        name="out_proj_final" if final else "out_proj",
    )(*args)


def _rope_tables():
    t = np.arange(DEC_SEQ)
    row = (t // GRID_W).astype(np.float64)
    col = (t % GRID_W).astype(np.float64)
    pairs = HEAD_DIM // 4
    inv_freq = ROPE_THETA ** (-np.arange(pairs, dtype=np.float64) / pairs)
    ang = np.concatenate([row[:, None] * inv_freq, col[:, None] * inv_freq], axis=-1)
    cos = np.repeat(np.cos(ang), 2, axis=-1).astype(np.float32)
    sin = np.repeat(np.sin(ang), 2, axis=-1).astype(np.float32)
    even = (np.arange(HEAD_DIM) % 2 == 0)[None, :]
    sin_a = np.where(even, -sin, 0.0).astype(np.float32)
    sin_b = np.where(even, 0.0, sin).astype(np.float32)
    return jnp.asarray(cos), jnp.asarray(sin_a), jnp.asarray(sin_b)


def kernel(x_prompt, x_sample, cache_k, cache_v, c, c_ctx, norm_g, w_ada, b_ada, w_in, q_norm_g, k_norm_g,
           conv_w, conv_b, filt_w1, filt_b1, filt_w2, filt_b2, filt_w3, filt_freq, hy_bias, w_out, final_norm_g):
    ctx = x_prompt.reshape(BATCH * SEQ, D_MODEL)
    lat = x_sample.reshape(DEC_BATCH * DEC_SEQ, D_MODEL)
    cache_k4 = cache_k.reshape(DEC_BATCH, DEPTH, PAST_LEN, D_KV)
    cache_v4 = cache_v.reshape(DEC_BATCH, DEPTH, PAST_LEN, D_KV)

    rope_tabs = _rope_tables()
    dft = {}
    for L in (SEQ, DEC_SEQ):
        f_np = _dft_matrix(L)
        dft[L] = (jnp.asarray(f_np).astype(BF16), jnp.asarray(np.ascontiguousarray(f_np.T)).astype(BF16))
    filt = (filt_w1, filt_b1, filt_w2, filt_b2, filt_freq, filt_w3)
    mod, coef_lat = _modulation_and_filter(c_ctx, c, w_ada, b_ada, DEC_SEQ, *filt, dft[DEC_SEQ][0])
    coefs = {SEQ: _filter_spectra(SEQ, *filt, dft[SEQ][0]), DEC_SEQ: coef_lat}

    final_g = final_norm_g.reshape(1, D_MODEL)

    x_parts = (ctx, lat)
    h = _norm_mod(ctx, lat, norm_g, mod, 0)
    kv_out = None
    for l in range(DEPTH):
        final = l == DEPTH - 1

        proj = _in_proj(h, w_in, l)

        att, new_k, new_v = _attention(proj, q_norm_g, k_norm_g, l, SEQ, CTX_TQ, BATCH, 0, nsb=4, kv_prev=kv_out,
                                       emit_kv=True)
        kv_out = (new_k, new_v)
        (att,) = _attention(proj, q_norm_g, k_norm_g, l, DEC_SEQ, LAT_TQ, DEC_BATCH, M_CTX, nsb=4,
                            cache=(cache_k4, cache_v4), rope_tabs=rope_tabs, att_prev=att)

        hy = _hyena(proj, SEQ, BATCH, 0, D_HY, 256, l, conv_w, conv_b, hy_bias, coefs[SEQ], *dft[SEQ])
        hy = _hyena(proj, DEC_SEQ, DEC_BATCH, M_CTX, 512, 512, l, conv_w, conv_b, hy_bias,
                    coefs[DEC_SEQ], *dft[DEC_SEQ], hy_prev=hy)

        if final:
            assert len(x_parts) == 1
            tm = OUTPROJ_TM_FINAL
            nc = M_CTX // tm
            (y_ctx,) = _out_proj(att, hy, w_out, l, x_parts, mod, final_g, True, tm, 0, nc)
            (y_lat,) = _out_proj(att, hy, w_out, l, x_parts, mod, final_g, True, tm, nc, M_LAT // tm)
        else:
            y, h = _out_proj(att, hy, w_out, l, x_parts, mod, norm_g, False, OUTPROJ_TM)
            x_parts = (y,)

    y_prompt = y_ctx.reshape(BATCH, SEQ, D_MODEL)
    y_sample = y_lat.reshape(DEC_BATCH, DEC_SEQ, D_MODEL)
    return (y_prompt, y_sample, kv_out[0], kv_out[1])
```

```python
import functools
import math

import numpy as np
import jax
import jax.numpy as jnp
from jax import lax
from jax.experimental import pallas as pl
from jax.experimental.pallas import tpu as pltpu

D_MODEL = 2048
BATCH = 16
SEQ = 256
DEPTH = 2
DEC_BATCH = 2
DEC_SEQ = 1024
PAST_LEN = 512
GRID_W = 64
D_ATT = 1024
D_HY = 1024
HEAD_DIM = 128
N_HEADS = 8
N_KV_HEADS = 4
GROUP = 2
D_KV = 512
ROPE_THETA = 10000.0
POS_BANDS = 16
POS_EMB = 33
FILT_HID = 64
DECAY_TARGET = 1e-2
FAST_DECAY_PCT = 0.3
SLOW_DECAY_PCT = 1.5
DECAY_SHIFT = 0.05
EPS = 1e-6
D_IN = 7168

COL_Q, COL_K, COL_V, COL_GA, COL_X0, COL_X1, COL_VV, COL_GH = 0, 1024, 1536, 2048, 3072, 4096, 5120, 6144

F32 = jnp.float32
BF16 = jnp.bfloat16

VMEM_LIMIT_BYTES = 56 * 1024 * 1024
FEAT_PAD = 128


def _params(*sem):
    return pltpu.CompilerParams(dimension_semantics=sem, vmem_limit_bytes=VMEM_LIMIT_BYTES)


def _silu(x):
    half = 0.5 * x
    return half + half * jnp.tanh(half)


def _rms(x, g):
    return x * lax.rsqrt(jnp.mean(x * x, axis=-1, keepdims=True) + EPS) * g


MOD_ROWS = 8
MOD_TN = 1024
MOD_TILES = 3 * D_MODEL // MOD_TN


def _mod_tile(layer, cctx_ref, c_ref, w_ref, b_ref, o_ref, c_scr):
    c_scr[...] = jnp.zeros(c_scr.shape, F32)
    c_scr[0:1, :] = cctx_ref[...]
    c_scr[1:1 + DEC_BATCH, :] = c_ref[...]
    s = _silu(c_scr[...]).astype(BF16)
    o_ref[0] = jnp.dot(s, w_ref[0].astype(BF16), preferred_element_type=F32) + b_ref[pl.ds(layer, 1), :]


M_CTX = BATCH * SEQ
M_LAT = DEC_BATCH * DEC_SEQ
M_ALL = M_CTX + M_LAT
NORM_TM = 1024
INPROJ_TM = 3072
INPROJ_TN = 512


ROW_CHUNK = 32


def _modulated_norm_rows(x_ref, h_ref, g, shift, scale):
    gain = g * (1.0 + scale)

    def body(r, _):
        rows = pl.ds(pl.multiple_of(r * ROW_CHUNK, ROW_CHUNK), ROW_CHUNK)
        x = x_ref[rows, :]
        inv = lax.rsqrt(jnp.mean(x * x, axis=-1, keepdims=True) + EPS)
        h_ref[rows, :] = (x * inv * gain + shift).astype(BF16)
        return 0

    lax.fori_loop(0, x_ref.shape[0] // ROW_CHUNK, body, 0, unroll=4)


MOD_SHIFT, MOD_SCALE, MOD_GATE = 0, 1, 2


def _mod_spec(layer, part):
    return pl.BlockSpec((1, MOD_ROWS, D_MODEL), lambda i: (layer, 0, part))


def _mod_row(ref, tile, tm):
    nc = M_CTX // tm
    per_batch = DEC_SEQ // tm
    row = jnp.where(tile < nc, 0, 1 + (tile - nc) // per_batch)
    return ref[0, pl.ds(row, 1), :]


def _two_stream_specs(tm, width):
    nc = M_CTX // tm
    return (pl.BlockSpec((tm, width), lambda i: (jnp.minimum(i, nc - 1), 0)),
            pl.BlockSpec((tm, width), lambda i: (jnp.maximum(i - nc, 0), 0)))


def _norm_kernel(layer, xc_ref, xl_ref, g_ref, shift_ref, scale_ref, h_ref):
    i = pl.program_id(0)

    def emit(x_ref):
        _modulated_norm_rows(x_ref, h_ref, g_ref[layer:layer + 1, :], _mod_row(shift_ref, i, NORM_TM),
                             _mod_row(scale_ref, i, NORM_TM))

    pl.when(i < M_CTX // NORM_TM)(lambda: emit(xc_ref))
    pl.when(i >= M_CTX // NORM_TM)(lambda: emit(xl_ref))


def _norm_mod(x_ctx, x_lat, norm_g, mod, layer):
    xc_spec, xl_spec = _two_stream_specs(NORM_TM, D_MODEL)
    return pl.pallas_call(
        functools.partial(_norm_kernel, layer),
        grid=(M_ALL // NORM_TM,),
        in_specs=[
            xc_spec, xl_spec,
            pl.BlockSpec((DEPTH, D_MODEL), lambda i: (0, 0)),
            _mod_spec(layer, MOD_SHIFT),
            _mod_spec(layer, MOD_SCALE),
        ],
        out_specs=pl.BlockSpec((NORM_TM, D_MODEL), lambda i: (i, 0)),
        out_shape=jax.ShapeDtypeStruct((M_ALL, D_MODEL), BF16),
        compiler_params=_params("arbitrary"),
        name="norm_mod",
    )(x_ctx, x_lat, norm_g, mod, mod)


def _inproj_kernel(h_hbm, w_ref, o_ref, w_scr, h_scr, h_sems):
    j, i = pl.program_id(0), pl.program_id(1)
    n_chunks = M_ALL // INPROJ_TM

    def h_copy(c):
        rows = pl.ds(c * INPROJ_TM, INPROJ_TM)
        return pltpu.make_async_copy(h_hbm.at[rows], h_scr.at[rows], h_sems.at[c])

    @pl.when((j == 0) & (i == 0))
    def _():
        for c in range(n_chunks):
            h_copy(c).start()

    @pl.when(i == 0)
    def _():
        w_scr[...] = w_ref[0].astype(BF16)

    for c in range(n_chunks):
        @pl.when((j == 0) & (i == c))
        def _():
            h_copy(c).wait()

    rows = pl.ds(pl.multiple_of(i * INPROJ_TM, INPROJ_TM), INPROJ_TM)
    o_ref[...] = jnp.dot(h_scr[rows, :], w_scr[...], preferred_element_type=F32)


def _in_proj(h, w_in, layer):
    return pl.pallas_call(
        _inproj_kernel,
        grid=(D_IN // INPROJ_TN, M_ALL // INPROJ_TM),
        in_specs=[
            pl.BlockSpec(memory_space=pl.ANY),
            pl.BlockSpec((1, D_MODEL, INPROJ_TN), lambda j, i: (layer, 0, j)),
        ],
        out_specs=pl.BlockSpec((INPROJ_TM, INPROJ_TN), lambda j, i: (i, j)),
        out_shape=jax.ShapeDtypeStruct((M_ALL, D_IN), F32),
        scratch_shapes=[pltpu.VMEM((D_MODEL, INPROJ_TN), BF16), pltpu.VMEM((M_ALL, D_MODEL), BF16),
                        pltpu.SemaphoreType.DMA((M_ALL // INPROJ_TM,))],
        compiler_params=_params("arbitrary", "arbitrary"),
        name="in_proj",
    )(h, w_in)


ATT_SCALE = 1.0 / math.sqrt(HEAD_DIM)


def _head(ref_or_val, h):
    return ref_or_val[:, h * HEAD_DIM:(h + 1) * HEAD_DIM]


Q_PRESCALE = ATT_SCALE * math.log2(math.e)


def _softmax_pv(q2, k_bf, v_ext):
    s = lax.dot_general(q2, k_bf, (((1,), (1,)), ((), ())), preferred_element_type=F32)
    p = jnp.exp2(s - jnp.max(s, axis=-1, keepdims=True)).astype(BF16)
    o = jnp.dot(p, v_ext, preferred_element_type=F32)
    return o[:, 0:HEAD_DIM] / o[:, HEAD_DIM:2 * HEAD_DIM]


def _rope(x, cos2, sin_a, sin_b):
    nxt = pltpu.roll(x, HEAD_DIM - 1, axis=1)
    prv = pltpu.roll(x, 1, axis=1)
    return x * cos2 + nxt * sin_a + prv * sin_b


CTX_TQ = SEQ
LAT_TQ = 128


def _attn_kernel(rope, n_cache, layer, emit_kv, n_aliased, nsb, *refs):
    it = iter(refs)
    q_all, kv_all, g_all, qg_ref, kg_ref = (next(it) for _ in range(5))
    ck_ref, cv_ref = (next(it), next(it)) if n_cache else (None, None)
    q_tabs = tuple(next(it) for _ in range(3)) if rope else None
    k_tabs = tuple(next(it) for _ in range(3)) if rope else None
    for _ in range(n_aliased):
        next(it)
    att_all = next(it)
    ko_hbm, vo_hbm = (next(it), next(it)) if emit_kv else (None, None)
    k_all, v_all = next(it), next(it)
    kf_all, kv_sems = (next(it), next(it)) if emit_kv else (None, None)
    single_tile = q_all.shape[0] == kv_all.shape[0]
    n_new = kv_all.shape[0] // nsb if single_tile else kv_all.shape[0]
    assert not emit_kv or single_tile

    def part(ref, sb, rows_per_seq):
        return ref if nsb == 1 else ref.at[pl.ds(sb * rows_per_seq, rows_per_seq)]

    def kv_copy(sb, h, is_value):
        b = pl.program_id(0) * nsb + sb
        if is_value:
            src = part(kv_all, sb, n_new).at[:, pl.ds((N_KV_HEADS + h) * HEAD_DIM, HEAD_DIM)]
            return pltpu.make_async_copy(src, vo_hbm.at[b, layer, :, h, :],
                                         kv_sems.at[sb, N_KV_HEADS + h])
        src = part(kf_all, sb, n_new).at[:, pl.ds(h * HEAD_DIM, HEAD_DIM)]
        return pltpu.make_async_copy(src, ko_hbm.at[b, layer, :, h, :], kv_sems.at[sb, h])

    def prepare_keys_values(sb):
        kv_ref, k_scr, v_scr = part(kv_all, sb, n_new), part(k_all, sb, n_new + n_cache), part(v_all, sb, n_new + n_cache)
        kf_scr = part(kf_all, sb, n_new) if emit_kv else None
        for h in range(N_KV_HEADS):
            kn = _rms(_head(kv_ref, h), kg_ref[layer:layer + 1, :])
            vh = _head(kv_ref, N_KV_HEADS + h)
            if emit_kv:
                kf_scr[:, h * HEAD_DIM:(h + 1) * HEAD_DIM] = kn
            if rope:
                kn = _rope(kn, *(t[...] for t in k_tabs))
            k_scr[0:n_new, h * HEAD_DIM:(h + 1) * HEAD_DIM] = kn.astype(BF16)
            v0 = 2 * h * HEAD_DIM
            v_scr[0:n_new, v0:v0 + HEAD_DIM] = vh.astype(BF16)
            if n_cache:
                v_scr[n_new:n_new + n_cache, v0:v0 + HEAD_DIM] = _head(cv_ref[0, 0], h).astype(BF16)
            v_scr[:, v0 + HEAD_DIM:v0 + 2 * HEAD_DIM] = jnp.ones((n_new + n_cache, HEAD_DIM), BF16)
        if n_cache:
            k_scr[n_new:n_new + n_cache, :] = ck_ref[0, 0].astype(BF16)

    def attend(sb):
        rows = q_all.shape[0] // nsb
        q_ref, g_ref, att_ref = part(q_all, sb, rows), part(g_all, sb, rows), part(att_all, sb, rows)
        kv_sb = sb if single_tile else 0
        k_scr, v_scr = part(k_all, kv_sb, n_new + n_cache), part(v_all, kv_sb, n_new + n_cache)

        def query(hq):
            x = _rms(_head(q_ref, hq), qg_ref[layer:layer + 1, :])
            if rope:
                x = _rope(x, *(part(t, sb, rows)[...] for t in q_tabs))
            return x * Q_PRESCALE

        for h in range(N_KV_HEADS):
            q2 = jnp.concatenate([query(GROUP * h + g) for g in range(GROUP)], axis=0).astype(BF16)
            o = _softmax_pv(q2, _head(k_scr, h), v_scr[:, 2 * h * HEAD_DIM:2 * (h + 1) * HEAD_DIM])
            for g in range(GROUP):
                hq = GROUP * h + g
                gate = _silu(_head(g_ref, hq))
                att_ref[:, hq * HEAD_DIM:(hq + 1) * HEAD_DIM] = (o[g * rows:(g + 1) * rows] * gate).astype(BF16)

    every = [(sb, h) for sb in range(nsb) for h in range(N_KV_HEADS)]
    if single_tile:
        if emit_kv:
            for sb, h in every:
                kv_copy(sb, h, True).start()
        for sb in range(nsb):
            prepare_keys_values(sb)
        if emit_kv:
            for sb, h in every:
                kv_copy(sb, h, False).start()
    else:
        pl.when(pl.program_id(1) == 0)(lambda: prepare_keys_values(0))

    for sb in range(nsb):
        attend(sb)

    if emit_kv:
        for sb, h in every:
            kv_copy(sb, h, False).wait()
            kv_copy(sb, h, True).wait()


def _attention(proj, q_g, k_g, layer, seq, tq, n_batch, row0, *, nsb=1, cache=None, rope_tabs=None,
               att_prev=None, kv_prev=None, emit_kv=False):
    whole = tq == seq
    assert not whole or cache is None
    tq = nsb * tq
    kv_rows = nsb * seq if whole else seq
    nq = kv_rows // tq
    q0, kv0 = row0 // tq, row0 // kv_rows
    n_cache = 0 if cache is None else cache[0].shape[2]
    vec = pl.BlockSpec((DEPTH, HEAD_DIM), lambda b, i: (0, 0))
    in_specs = [
        pl.BlockSpec((tq, D_ATT), lambda b, i: (q0 + b * nq + i, COL_Q // D_ATT)),
        pl.BlockSpec((kv_rows, 2 * D_KV), lambda b, i: (kv0 + b, COL_K // (2 * D_KV))),
        pl.BlockSpec((tq, D_ATT), lambda b, i: (q0 + b * nq + i, COL_GA // D_ATT)),
        vec, vec,
    ]
    args = [proj, proj, proj, q_g, k_g]
    if cache is not None:
        in_specs += [pl.BlockSpec((1, 1, n_cache, D_KV), lambda b, i: (b, layer, 0, 0))] * 2
        args += list(cache)
    if rope_tabs is not None:
        in_specs += [pl.BlockSpec((tq, HEAD_DIM), lambda b, i: (i, 0))] * 3
        in_specs += [pl.BlockSpec((seq, HEAD_DIM), lambda b, i: (0, 0))] * 3
        args += list(rope_tabs) * 2
    aliased = ([] if att_prev is None else [att_prev]) + ([] if kv_prev is None else list(kv_prev))
    out_first = 0 if att_prev is not None else 1
    aliases = {len(args) + n: out_first + n for n in range(len(aliased))}
    in_specs += [pl.BlockSpec(memory_space=pl.ANY)] * len(aliased)
    args += aliased
    out_specs = [pl.BlockSpec((tq, D_ATT), lambda b, i: (q0 + b * nq + i, 0))]
    out_shape = [jax.ShapeDtypeStruct((M_ALL, D_ATT), BF16)]
    keys = kv_rows + n_cache
    scratch =[pltpu.VMEM((keys, D_KV), BF16), pltpu.VMEM((keys, 2 * D_KV), BF16)]
    if emit_kv:
        out_specs += [pl.BlockSpec(memory_space=pl.ANY)] * 2
        out_shape += [jax.ShapeDtypeStruct((n_batch, DEPTH, seq, N_KV_HEADS, HEAD_DIM), F32)] * 2
        scratch += [pltpu.VMEM((kv_rows, D_KV), F32), pltpu.SemaphoreType.DMA((nsb, 2 * N_KV_HEADS))]
    return pl.pallas_call(
        functools.partial(_attn_kernel, rope_tabs is not None, n_cache, layer, emit_kv, len(aliased), nsb),
        grid=(n_batch * seq // kv_rows, nq),
        in_specs=in_specs,
        out_specs=out_specs,
        out_shape=out_shape,
        input_output_aliases=aliases,
        scratch_shapes=scratch,
        compiler_params=_params("arbitrary", "arbitrary"),
        name=f"attention_{seq}",
    )(*args)


def _dft_matrix(L):
    k = np.arange(L, dtype=np.int64)[:, None]
    t = np.arange(L, dtype=np.int64)[None, :]
    ang = 2.0 * np.pi * ((k * t) % (2 * L)).astype(np.float64) / (2 * L)
    top = np.cos(ang)
    bot = np.sin(ang)
    bot[0, :] = np.where(np.arange(L) % 2 == 0, 1.0, -1.0)
    return np.concatenate([top, bot], axis=0).astype(np.float32)


def _filter_unit(L, layer, run_ffn, run_block, z_ref, w1_ref, b1_ref, w2_ref, b2_ref, fr_ref, w3f_ref, w3b_ref,
                 dec_ref, f_ref, o_ref, hdn_scr, w1_scr, w2_scr, vec_scr, w3_scr):
    hp = lax.Precision.HIGHEST

    halves = ((0, 0), (FEAT_PAD, FILT_HID))

    @pl.when(run_ffn)
    def _():
        w1_scr[...] = jnp.zeros(w1_scr.shape, F32)
        w2_scr[...] = jnp.zeros(w2_scr.shape, F32)
        vec_scr[...] = jnp.zeros(vec_scr.shape, F32)
        w3_scr[...] = jnp.zeros(w3_scr.shape, BF16)
        for feat0, hid0 in halves:
            w1_scr[feat0:feat0 + POS_EMB, hid0:hid0 + FILT_HID] = w1_ref[0]
            w2_scr[hid0:hid0 + FILT_HID, hid0:hid0 + FILT_HID] = w2_ref[0]
            for r, ref in enumerate((b1_ref, b2_ref, fr_ref)):
                vec_scr[r:r + 1, hid0:hid0 + FILT_HID] = ref[pl.ds(layer, 1), :]
        b1, b2, fr = vec_scr[0:1, :], vec_scr[1:2, :], vec_scr[2:3, :]
        h1 = jnp.sin(fr * (jnp.dot(z_ref[...], w1_scr[...], precision=hp, preferred_element_type=F32) + b1))
        hdn_scr[...] = jnp.sin(fr * (jnp.dot(h1, w2_scr[...], precision=hp, preferred_element_type=F32) + b2))

    def block():
        hdn_bf = hdn_scr[...].astype(BF16)
        dec = dec_ref[...]

        def project(w3_ref, slot):
            parts = []
            for k, (_, hid0) in enumerate(halves):
                w3_scr[slot + k, hid0:hid0 + FILT_HID, :] = w3_ref[0].astype(BF16)
                parts.append(jnp.dot(hdn_bf, w3_scr[slot + k], preferred_element_type=F32))
            return jnp.concatenate(parts, axis=0) * dec

        h_f = project(w3f_ref, 0)
        h_b = project(w3b_ref, 2)
        hs = h_f + h_b
        ha = jnp.dot(f_ref[0:L, :], hs.astype(BF16), preferred_element_type=F32)
        hb = jnp.dot(f_ref[L:2 * L, :], (h_f - h_b).astype(BF16), preferred_element_type=F32)
        row = lax.broadcasted_iota(jnp.int32, (L, 1), 0)
        first = row == 0
        nyquist = jnp.sum(jnp.where(row % 2 == 0, hs, -hs), axis=0, keepdims=True)
        wk = jnp.where(first, 1.0 / (2 * L), 2.0 / (2 * L))
        o_ref[0, 0] = ha * wk
        o_ref[0, 1] = jnp.where(first, 0.0, hb * wk)
        o_ref[0, 2] = jnp.where(first, nyquist, ha) * wk

    if run_block is None:
        block()
    else:
        pl.when(run_block)(block)


FILT_CB = 512


def _filter_tables(L):
    tpos = np.arange(L, dtype=np.float64)
    t_norm = tpos / max(L - 1, 1)
    w = 2.0 * math.pi * tpos / L
    bands = np.linspace(1e-4, POS_BANDS - 1, POS_BANDS)
    z = np.concatenate([t_norm[:, None], np.cos(w[:, None] * bands), -np.sin(w[:, None] * bands)], axis=-1)
    z = np.pad(z, ((0, 0), (0, FEAT_PAD - POS_EMB))).astype(np.float32)
    z = np.concatenate([z[:L // 2], z[L // 2:]], axis=1)
    max_decay = math.log(DECAY_TARGET) / FAST_DECAY_PCT
    min_decay = math.log(DECAY_TARGET) / SLOW_DECAY_PCT
    deltas = np.abs(np.linspace(min_decay, max_decay, D_HY))
    dec = (np.exp(-t_norm[:, None] * deltas) + DECAY_SHIFT).astype(np.float32)
    return jnp.asarray(z), jnp.asarray(dec)


FILT_NCB = D_HY // FILT_CB


def _filter_specs(L, unit):
    layer_only = lambda *g: (unit(*g)[0], 0, 0)
    per_layer_vec = pl.BlockSpec((DEPTH, FILT_HID), lambda *g: (0, 0))
    in_specs = [
        pl.BlockSpec((L // 2, 2 * FEAT_PAD), lambda *g: (0, 0)),
        pl.BlockSpec((1, POS_EMB, FILT_HID), layer_only), per_layer_vec,
        pl.BlockSpec((1, FILT_HID, FILT_HID), layer_only), per_layer_vec, per_layer_vec,
        pl.BlockSpec((1, FILT_HID, FILT_CB), lambda *g: (unit(*g)[0], 0, unit(*g)[1])),
        pl.BlockSpec((1, FILT_HID, FILT_CB), lambda *g: (unit(*g)[0], 0, FILT_NCB + unit(*g)[1])),
        pl.BlockSpec((L, FILT_CB), lambda *g: (0, unit(*g)[1])),
        pl.BlockSpec((2 * L, L), lambda *g: (0, 0), pipeline_mode=pl.Buffered(1)),
    ]
    out_spec = pl.BlockSpec((1, 3, L, FILT_CB), lambda *g: (unit(*g)[0], 0, 0, unit(*g)[1]))
    scratch = [pltpu.VMEM((L // 2, FEAT_PAD), F32), pltpu.VMEM((2 * FEAT_PAD, FEAT_PAD), F32),
               pltpu.VMEM((FEAT_PAD, FEAT_PAD), F32), pltpu.VMEM((8, FEAT_PAD), F32),
               pltpu.VMEM((4, FEAT_PAD, FILT_CB), BF16)]
    return in_specs, out_spec, scratch


FILT_PHASES = 1 + FILT_NCB


FILT_SLICES = DEPTH * FILT_PHASES
N_FILT_IN, N_FILT_SCR = 10, 5


def _mod_filter_kernel(lengths, *refs):
    n = len(lengths)
    mod_in = refs[:4]
    filt_in = [refs[4 + N_FILT_IN * k:4 + N_FILT_IN * (k + 1)] for k in range(n)]
    pos = 4 + N_FILT_IN * n
    o_mod, o_coefs, c_scr = refs[pos], refs[pos + 1:pos + 1 + n], refs[pos + 1 + n]
    pos += 2 + n
    filt_scr = [refs[pos + N_FILT_SCR * k:pos + N_FILT_SCR * (k + 1)] for k in range(n)]
    s = pl.program_id(0)
    _mod_tile(s // MOD_TILES, *mod_in, o_mod, c_scr)

    for k, L in enumerate(lengths):
        @pl.when((s >= k * FILT_SLICES) & (s < (k + 1) * FILT_SLICES))
        def _(k=k, L=L):
            t = s - k * FILT_SLICES
            phase = t % FILT_PHASES
            _filter_unit(L, t // FILT_PHASES, phase == 0, phase > 0, *filt_in[k], o_coefs[k], *filt_scr[k])


def _modulation_and_filters(c_ctx, c, w_ada, b_ada, lengths, filt, f_bfs):
    assert len(lengths) * FILT_SLICES <= DEPTH * MOD_TILES
    w1, b1, w2, b2, freq, w3 = filt
    tile = lambda s: (s // MOD_TILES, s % MOD_TILES)
    in_specs = [
        pl.BlockSpec((1, D_MODEL), lambda s: (0, 0)),
        pl.BlockSpec((DEC_BATCH, D_MODEL), lambda s: (0, 0)),
        pl.BlockSpec((1, D_MODEL, MOD_TN), lambda s: (tile(s)[0], 0, tile(s)[1])),
        pl.BlockSpec((DEPTH, MOD_TN), lambda s: (0, tile(s)[1])),
    ]
    args = [c_ctx.reshape(1, D_MODEL), c, w_ada, b_ada]
    out_specs = [pl.BlockSpec((1, MOD_ROWS, MOD_TN), lambda s: (tile(s)[0], 0, tile(s)[1]))]
    out_shape = [jax.ShapeDtypeStruct((DEPTH, MOD_ROWS, 3 * D_MODEL), F32)]
    scratch = [pltpu.VMEM((MOD_ROWS, D_MODEL), F32)]
    for k, (L, f_bf) in enumerate(zip(lengths, f_bfs)):
        def unit(s, k=k):
            u = jnp.clip(s - k * FILT_SLICES, 0, FILT_SLICES - 1)
            return u // FILT_PHASES, jnp.maximum(u % FILT_PHASES - 1, 0)

        f_in, f_out, f_scratch = _filter_specs(L, unit)
        z, dec = _filter_tables(L)
        in_specs += f_in
        args += [z, w1, b1, w2, b2, freq, w3, w3, dec, f_bf]
        out_specs.append(f_out)
        out_shape.append(jax.ShapeDtypeStruct((DEPTH, 3, L, D_HY), F32))
        scratch += f_scratch
    return pl.pallas_call(
        functools.partial(_mod_filter_kernel, tuple(lengths)),
        grid=(DEPTH * MOD_TILES,),
        in_specs=in_specs,
        out_specs=out_specs,
        out_shape=out_shape,
        scratch_shapes=scratch,
        compiler_params=_params("arbitrary"),
        name="adaln_mod_and_filters",
    )(*args)


def _hyena_kernel(L, layer, sub, x0_ref, x1_ref, vv_ref, g_ref, cw0_ref, cw1_ref, cwv_ref, cb0_ref, cb1_ref,
                  cbv_ref, bias_ref, coef_ref, f_ref, ft_ref, *rest):
    o_ref = rest[-1]
    this_layer = slice(layer, layer + 1)
    row = lax.broadcasted_iota(jnp.int32, (L, 1), 0)
    is_first = row == 0
    is_last = row == L - 1

    for c in range(o_ref.shape[1] // sub):
        cols = slice(c * sub, (c + 1) * sub)

        def sconv(x_ref, w_ref, b_ref):
            x = x_ref[:, cols]
            prev = jnp.where(is_first, 0.0, pltpu.roll(x, 1, axis=0))
            nxt = jnp.where(is_last, 0.0, pltpu.roll(x, L - 1, axis=0))
            return (w_ref[0, 0:1, cols] * prev + w_ref[0, 1:2, cols] * x + w_ref[0, 2:3, cols] * nxt
                    + b_ref[this_layer, cols])

        z = sconv(vv_ref, cwv_ref, cbv_ref) * sconv(x1_ref, cw1_ref, cb1_ref)
        ab = jnp.dot(f_ref[...], z.astype(BF16), preferred_element_type=F32)
        a, b = ab[0:L], ab[L:2 * L]
        g1, g2, g3 = coef_ref[0, 0, :, cols], coef_ref[0, 1, :, cols], coef_ref[0, 2, :, cols]
        pq = jnp.concatenate([a * g1 - b * g2, a * g2 + b * g3], axis=0).astype(BF16)
        y = jnp.dot(ft_ref[...], pq, preferred_element_type=F32) + bias_ref[this_layer, cols] * z
        o_ref[:, cols] = (sconv(x0_ref, cw0_ref, cb0_ref) * y * _silu(g_ref[:, cols])).astype(BF16)


def _hyena(proj, L, nb, row0, cb, sub, layer, conv_w, conv_b, hy_bias, coef, f_bf, ft_bf, hy_prev=None):
    ncb = D_HY // cb
    rows = L
    blk0 = row0 // rows
    extra_specs = [] if hy_prev is None else [pl.BlockSpec(memory_space=pl.ANY)]
    extra_args = [] if hy_prev is None else [hy_prev]
    aliases = {} if hy_prev is None else {14: 0}
    col = lambda off: (lambda c, b: (blk0 + b, off // cb + c))
    cw = lambda part: pl.BlockSpec((1, 3, cb), lambda c, b: (layer, 0, part * ncb + c))
    cbias = lambda part: pl.BlockSpec((DEPTH, cb), lambda c, b: (0, part * ncb + c))
    once = pl.Buffered(1)
    return pl.pallas_call(
        functools.partial(_hyena_kernel, L, layer, sub),
        grid=(ncb, nb),
        in_specs=[
            pl.BlockSpec((rows, cb), col(COL_X0)),
            pl.BlockSpec((rows, cb), col(COL_X1)),
            pl.BlockSpec((rows, cb), col(COL_VV)),
            pl.BlockSpec((rows, cb), col(COL_GH)),
            cw(0), cw(1), cw(2), cbias(0), cbias(1), cbias(2),
            pl.BlockSpec((DEPTH, cb), lambda c, b: (0, c)),
            pl.BlockSpec((1, 3, L, cb), lambda c, b: (layer, 0, 0, c)),
            pl.BlockSpec((2 * L, L), lambda c, b: (0, 0), pipeline_mode=once),
            pl.BlockSpec((L, 2 * L), lambda c, b: (0, 0), pipeline_mode=once),
            *extra_specs,
        ],
        out_specs=pl.BlockSpec((rows, cb), lambda c, b: (blk0 + b, c)),
        out_shape=jax.ShapeDtypeStruct((M_ALL, D_HY), BF16),
        input_output_aliases=aliases,
        compiler_params=_params("arbitrary", "arbitrary"),
        name=f"hyena_{L}",
    )(proj, proj, proj, proj, conv_w, conv_w, conv_w, conv_b, conv_b, conv_b, hy_bias, coef, f_bf, ft_bf,
      *extra_args)


OUTPROJ_TM = 256
OUTPROJ_TM_FINAL = 512


def _outproj_kernel(split_in, final, tm, tile0, gain_row, *refs):
    it = iter(refs)
    att_ref, hy_ref, w_ref = next(it), next(it), next(it)
    x_refs = (next(it), next(it)) if split_in else (next(it),)
    gate_ref, g_ref = next(it), next(it)
    shift_ref, scale_ref = (None, None) if final else (next(it), next(it))
    out_refs = (next(it),) if final else (next(it), next(it))
    w_scr = next(it)
    i = pl.program_id(0)

    @pl.when(i == 0)
    def _():
        w_scr[...] = w_ref[0].astype(BF16)

    out = (jnp.dot(att_ref[...], w_scr[0:D_ATT, :], preferred_element_type=F32)
           + jnp.dot(hy_ref[...], w_scr[D_ATT:D_ATT + D_HY, :], preferred_element_type=F32))
    tile = tile0 + i
    if split_in:
        x = jnp.where(tile < M_CTX // tm, x_refs[0][...], x_refs[1][...])
    else:
        x = x_refs[0][...]
    y = x + _mod_row(gate_ref, tile, tm) * out
    inv = lax.rsqrt(jnp.mean(y * y, axis=-1, keepdims=True) + EPS)
    g = g_ref[gain_row:gain_row + 1, :]
    if final:
        out_refs[0][...] = y * inv * g
    else:
        out_refs[0][...] = y
        gain = g * (1.0 + _mod_row(scale_ref, tile, tm))
        out_refs[1][...] = (y * inv * gain + _mod_row(shift_ref, tile, tm)).astype(BF16)


def _out_proj(att, hy, w_out, layer, x_parts, mod, gains, final, tm, tile0=0, n_tiles=None):
    n_tiles = M_ALL // tm if n_tiles is None else n_tiles
    split_in = len(x_parts) == 2
    assert not split_in or (tile0 == 0 and n_tiles == M_ALL // tm)
    row_tile = lambda width: pl.BlockSpec((tm, width), lambda i: (tile0 + i, 0))
    out_tile = pl.BlockSpec((tm, D_MODEL), lambda i: (i, 0))
    x_specs = list(_two_stream_specs(tm, D_MODEL)) if split_in else [row_tile(D_MODEL)]
    in_specs = [
        row_tile(D_ATT),
        row_tile(D_HY),
        pl.BlockSpec((1, D_ATT + D_HY, D_MODEL), lambda i: (layer, 0, 0), pipeline_mode=pl.Buffered(1)),
        *x_specs,
        _mod_spec(layer, MOD_GATE),
        pl.BlockSpec(gains.shape, lambda i: (0, 0)),
    ]
    args = [att, hy, w_out, *x_parts, mod, gains]
    if final:
        out_specs = [out_tile]
        out_shape = [jax.ShapeDtypeStruct((n_tiles * tm, D_MODEL), F32)]
    else:
        in_specs += [_mod_spec(layer + 1, MOD_SHIFT), _mod_spec(layer + 1, MOD_SCALE)]
        args += [mod, mod]
        out_specs = [out_tile, out_tile]
        out_shape = [jax.ShapeDtypeStruct((n_tiles * tm, D_MODEL), F32),
                     jax.ShapeDtypeStruct((n_tiles * tm, D_MODEL), BF16)]
    return pl.pallas_call(
        functools.partial(_outproj_kernel, split_in, final, tm, tile0, 0 if final else layer + 1),
        grid=(n_tiles,),
        in_specs=in_specs,
        out_specs=out_specs,
        out_shape=out_shape,
        scratch_shapes=[pltpu.VMEM((D_ATT + D_HY, D_MODEL), BF16)],
        compiler_params=_params("arbitrary"),
        name="out_proj_final" if final else "out_proj",
    )(*args)


def _rope_tables():
    t = np.arange(DEC_SEQ)
    row = (t // GRID_W).astype(np.float64)
    col = (t % GRID_W).astype(np.float64)
    pairs = HEAD_DIM // 4
    inv_freq = ROPE_THETA ** (-np.arange(pairs, dtype=np.float64) / pairs)
    ang = np.concatenate([row[:, None] * inv_freq, col[:, None] * inv_freq], axis=-1)
    cos = np.repeat(np.cos(ang), 2, axis=-1).astype(np.float32)
    sin = np.repeat(np.sin(ang), 2, axis=-1).astype(np.float32)
    even = (np.arange(HEAD_DIM) % 2 == 0)[None, :]
    sin_a = np.where(even, -sin, 0.0).astype(np.float32)
    sin_b = np.where(even, 0.0, sin).astype(np.float32)
    return jnp.asarray(cos), jnp.asarray(sin_a), jnp.asarray(sin_b)


def kernel(x_prompt, x_sample, cache_k, cache_v, c, c_ctx, norm_g, w_ada, b_ada, w_in, q_norm_g, k_norm_g,
           conv_w, conv_b, filt_w1, filt_b1, filt_w2, filt_b2, filt_w3, filt_freq, hy_bias, w_out, final_norm_g):
    ctx = x_prompt.reshape(BATCH * SEQ, D_MODEL)
    lat = x_sample.reshape(DEC_BATCH * DEC_SEQ, D_MODEL)
    cache_k4 = cache_k.reshape(DEC_BATCH, DEPTH, PAST_LEN, D_KV)
    cache_v4 = cache_v.reshape(DEC_BATCH, DEPTH, PAST_LEN, D_KV)

    rope_tabs = _rope_tables()
    dft = {}
    for L in (SEQ, DEC_SEQ):
        f_np = _dft_matrix(L)
        dft[L] = (jnp.asarray(f_np).astype(BF16), jnp.asarray(np.ascontiguousarray(f_np.T)).astype(BF16))
    filt = (filt_w1, filt_b1, filt_w2, filt_b2, filt_freq, filt_w3)
    mod, coef_lat, coef_ctx = _modulation_and_filters(c_ctx, c, w_ada, b_ada, (DEC_SEQ, SEQ), filt,
                                                      (dft[DEC_SEQ][0], dft[SEQ][0]))
    coefs = {SEQ: coef_ctx, DEC_SEQ: coef_lat}

    final_g = final_norm_g.reshape(1, D_MODEL)

    x_parts = (ctx, lat)
    h = _norm_mod(ctx, lat, norm_g, mod, 0)
    kv_out = None
    for l in range(DEPTH):
        final = l == DEPTH - 1

        proj = _in_proj(h, w_in, l)

        att, new_k, new_v = _attention(proj, q_norm_g, k_norm_g, l, SEQ, CTX_TQ, BATCH, 0, nsb=4, kv_prev=kv_out,
                                       emit_kv=True)
        kv_out = (new_k, new_v)
        (att,) = _attention(proj, q_norm_g, k_norm_g, l, DEC_SEQ, LAT_TQ, DEC_BATCH, M_CTX, nsb=4,
                            cache=(cache_k4, cache_v4), rope_tabs=rope_tabs, att_prev=att)

        hy = _hyena(proj, SEQ, BATCH, 0, D_HY, 256, l, conv_w, conv_b, hy_bias, coefs[SEQ], *dft[SEQ])
        hy = _hyena(proj, DEC_SEQ, DEC_BATCH, M_CTX, 512, 512, l, conv_w, conv_b, hy_bias,
                    coefs[DEC_SEQ], *dft[DEC_SEQ], hy_prev=hy)

        if final:
            assert len(x_parts) == 1
            tm = OUTPROJ_TM_FINAL
            nc = M_CTX // tm
            (y_ctx,) = _out_proj(att, hy, w_out, l, x_parts, mod, final_g, True, tm, 0, nc)
            (y_lat,) = _out_proj(att, hy, w_out, l, x_parts, mod, final_g, True, tm, nc, M_LAT // tm)
        else:
            y, h = _out_proj(att, hy, w_out, l, x_parts, mod, norm_g, False, OUTPROJ_TM)
            x_parts = (y,)

    y_prompt = y_ctx.reshape(BATCH, SEQ, D_MODEL)
    y_sample = y_lat.reshape(DEC_BATCH, DEC_SEQ, D_MODEL)
    return (y_prompt, y_sample, kv_out[0], kv_out[1])
```
